```python
import math
import jax
import jax.numpy as jnp
from jax import lax
import numpy as np

D_MODEL = 1024
BATCH = 8
SEQ = 4096
DEPTH = 2

MEM_LEN = 256
EPS = 1e-6
NEG_INF = -1e30
FORCE_SCORE = 1e4

CONV_CH = 512
CONV_WIDTH = 31

NSA_HEADS = 8
NSA_KV_GROUPS = 2
NSA_HEAD_DIM = 64
CMP_BLOCK = 32
CMP_STRIDE = 16
CMP_HIDDEN = 256
SLC_BLOCK = 64
N_SELECT = 16
WINDOW = 512
NSA_QBLOCK = 64

MLA_HEADS = 4
Q_RANK = 384
KV_RANK = 256
QK_NOPE = 128
QK_ROPE = 64
V_DIM = 128
ROPE_THETA = 10000.0
ATTN_QBLOCK = 128

REL_BUCKETS = 32
REL_MAX_DIST = 128

XATTN_HEADS = 4
XATTN_HEAD_DIM = 128

FFN_HIDDEN = -(-8 * D_MODEL // (3 * 256)) * 256

IN_SIZES = (2 * CONV_CH, NSA_HEADS * NSA_HEAD_DIM, 6 * NSA_KV_GROUPS * NSA_HEAD_DIM, 3 * NSA_HEADS, Q_RANK, KV_RANK, QK_ROPE, 3 * D_MODEL)
IN_COLS = 2 * CONV_CH + NSA_HEADS * NSA_HEAD_DIM + 6 * NSA_KV_GROUPS * NSA_HEAD_DIM + 3 * NSA_HEADS + Q_RANK + KV_RANK + QK_ROPE + 3 * D_MODEL

kernel_name = 'hybrid_conformer_nsa_mla_block'


def rmsnorm(x, g):
    xf = x.astype(jnp.float32)
    y = xf * lax.rsqrt(jnp.mean(xf * xf, axis=-1, keepdims=True) + EPS)
    return (y * g.astype(jnp.float32)).astype(x.dtype)


def layernorm(x, g, b):
    xf = x.astype(jnp.float32)
    mu = jnp.mean(xf, axis=-1, keepdims=True)
    var = jnp.mean(jnp.square(xf - mu), axis=-1, keepdims=True)
    y = (xf - mu) * lax.rsqrt(var + EPS) * g.astype(jnp.float32) + b.astype(jnp.float32)
    return y.astype(x.dtype)


def t5_bucket(dist):
    exact = REL_BUCKETS // 2
    d = jnp.maximum(dist, 0)
    log_ratio = jnp.log(jnp.maximum(d, 1).astype(jnp.float32) / exact) / math.log(REL_MAX_DIST / exact)
    large = jnp.minimum(exact + (log_ratio * (REL_BUCKETS - exact)).astype(jnp.int32), REL_BUCKETS - 1)
    return jnp.where(d < exact, d, large)


def apply_rope(x, cos, sin):
    xf = x.astype(jnp.float32)
    x1, x2 = jnp.split(xf, 2, axis=-1)
    return jnp.concatenate([x1 * cos - x2 * sin, x2 * cos + x1 * sin], axis=-1).astype(x.dtype)


def conformer_conv(u_glu, conv_w, conv_b, ln_g, ln_b, w_proj):
    a, b = jnp.split(u_glu, 2, axis=-1)
    u = a * jax.nn.sigmoid(b)
    u = lax.conv_general_dilated(u, conv_w[:, None, :], window_strides=(1,), padding=[(CONV_WIDTH - 1, 0)],
                                 dimension_numbers=('NWC', 'WIO', 'NWC'), feature_group_count=CONV_CH) + conv_b
    u = jax.nn.silu(layernorm(u, ln_g, ln_b))
    return u @ w_proj


def compress_blocks(tok, pos, w1, b1, w2, n_cmp):
    bsz = tok.shape[0]
    idx = CMP_STRIDE * jnp.arange(n_cmp)[:, None] + jnp.arange(CMP_BLOCK)[None, :]
    blk = tok[:, idx] + pos[:, None, :]
    blk = blk.transpose(0, 1, 3, 2, 4).reshape(bsz, n_cmp, NSA_KV_GROUPS, CMP_BLOCK * NSA_HEAD_DIM)
    return jax.nn.gelu(blk @ w1 + b1) @ w2


def nsa_attention(q, kv, gate_logits, rel_bias, cmp_pos_k, cmp_w1_k, cmp_b1_k, cmp_w2_k,
                  cmp_pos_v, cmp_w1_v, cmp_b1_v, cmp_w2_v):
    bsz, seq = q.shape[0], q.shape[1]
    G, HG, dh = NSA_KV_GROUPS, NSA_HEADS // NSA_KV_GROUPS, NSA_HEAD_DIM
    n_cmp = (seq - CMP_BLOCK) // CMP_STRIDE + 1
    n_sb = seq // SLC_BLOCK
    n_sel = min(N_SELECT, n_sb)
    n_keys = n_sel * SLC_BLOCK
    kw_len = NSA_QBLOCK + WINDOW
    scale = dh ** -0.5
    q = q.reshape(bsz, seq, G, HG, dh)
    kv = kv.reshape(bsz, seq, 6, G, dh)
    k_cmp = compress_blocks(kv[:, :, 0], cmp_pos_k, cmp_w1_k, cmp_b1_k, cmp_w2_k, n_cmp)
    v_cmp = compress_blocks(kv[:, :, 1], cmp_pos_v, cmp_w1_v, cmp_b1_v, cmp_w2_v, n_cmp)
    k_slc = kv[:, :, 2].reshape(bsz, n_sb, SLC_BLOCK, G, dh).transpose(0, 3, 1, 2, 4)
    v_slc = kv[:, :, 3].reshape(bsz, n_sb, SLC_BLOCK, G, dh).transpose(0, 3, 1, 2, 4)
    k_win = jnp.pad(kv[:, :, 4], ((0, 0), (WINDOW, 0), (0, 0), (0, 0)))
    v_win = jnp.pad(kv[:, :, 5], ((0, 0), (WINDOW, 0), (0, 0), (0, 0)))
    gates = jax.nn.sigmoid(gate_logits.astype(jnp.float32)).astype(q.dtype).reshape(bsz, seq, G, HG, 3)
    rb_group = rel_bias.reshape(REL_BUCKETS, G, HG).transpose(1, 0, 2)
    c_start = CMP_STRIDE * jnp.arange(n_cmp)
    c_end = c_start + CMP_BLOCK - 1
    s_start = SLC_BLOCK * jnp.arange(n_sb)
    cover = ((c_start[:, None] < s_start[None, :] + SLC_BLOCK) & (c_start[:, None] + CMP_BLOCK > s_start[None, :])).astype(jnp.float32)
    b_idx = jnp.arange(bsz)[:, None, None, None]
    g_idx = jnp.arange(G)[None, :, None, None]
    j_idx = jnp.arange(n_sb)

    def block(i):
        q0 = i * NSA_QBLOCK
        tq = q0 + jnp.arange(NSA_QBLOCK)
        qb = lax.dynamic_slice_in_dim(q, q0, NSA_QBLOCK, axis=1)
        gb = lax.dynamic_slice_in_dim(gates, q0, NSA_QBLOCK, axis=1)
        dist_c = tq[:, None] - c_end[None, :]
        valid_c = dist_c >= 0
        bias_c = rel_bias[t5_bucket(dist_c)].reshape(NSA_QBLOCK, n_cmp, G, HG).transpose(2, 3, 0, 1)
        lc = jnp.einsum('bqghd,bcgd->bghqc', qb, k_cmp, preferred_element_type=jnp.float32) * scale + bias_c
        pc = jax.nn.softmax(jnp.where(valid_c, lc, NEG_INF), axis=-1) * jnp.any(valid_c, axis=-1)[:, None]
        o_cmp = jnp.einsum('bghqc,bcgd->bqghd', pc.astype(q.dtype), v_cmp)
        imp = jnp.einsum('bghqc,cn->bgqn', pc, cover)
        cur = tq // SLC_BLOCK
        forced = (j_idx[None] == 0) | (j_idx[None] == cur[:, None]) | (j_idx[None] == cur[:, None] - 1)
        causal_s = s_start[None, :] <= tq[:, None]
        score = jnp.where(forced, FORCE_SCORE, jnp.where(causal_s, imp, -1.0))
        top_val, top_idx = lax.top_k(score, n_sel)
        ks = k_slc[b_idx, g_idx, top_idx]
        vs = v_slc[b_idx, g_idx, top_idx]
        k_pos = top_idx[..., None] * SLC_BLOCK + jnp.arange(SLC_BLOCK)
        dist_s = tq[None, None, :, None, None] - k_pos
        valid_s = ((top_val >= 0)[..., None] & (dist_s >= 0)).reshape(bsz, G, 1, NSA_QBLOCK, n_keys)
        bias_s = rb_group[g_idx[..., None], t5_bucket(dist_s)]
        bias_s = bias_s.transpose(0, 1, 5, 2, 3, 4).reshape(bsz, G, HG, NSA_QBLOCK, n_keys)
        ls = jnp.einsum('bqghd,bgqnkd->bghqnk', qb, ks, preferred_element_type=jnp.float32)
        ls = ls.reshape(bsz, G, HG, NSA_QBLOCK, n_keys) * scale + bias_s
        ps = jax.nn.softmax(jnp.where(valid_s, ls, NEG_INF), axis=-1)
        o_slc = jnp.einsum('bghqk,bgqkd->bqghd', ps.astype(q.dtype), vs.reshape(bsz, G, NSA_QBLOCK, n_keys, dh))
        kw = lax.dynamic_slice_in_dim(k_win, q0, kw_len, axis=1)
        vw = lax.dynamic_slice_in_dim(v_win, q0, kw_len, axis=1)
        k_pos_w = q0 - WINDOW + jnp.arange(kw_len)
        dist_w = tq[:, None] - k_pos_w[None, :]
        valid_w = (dist_w >= 0) & (dist_w < WINDOW) & (k_pos_w[None, :] >= 0)
        bias_w = rel_bias[t5_bucket(dist_w)].reshape(NSA_QBLOCK, kw_len, G, HG).transpose(2, 3, 0, 1)
        lw = jnp.einsum('bqghd,bkgd->bghqk', qb, kw, preferred_element_type=jnp.float32) * scale + bias_w
        pw = jax.nn.softmax(jnp.where(valid_w, lw, NEG_INF), axis=-1)
        o_win = jnp.einsum('bghqk,bkgd->bqghd', pw.astype(q.dtype), vw)
        out = gb[..., 0:1] * o_cmp + gb[..., 1:2] * o_slc + gb[..., 2:3] * o_win
        return out.reshape(bsz, NSA_QBLOCK, NSA_HEADS * dh)

    out = lax.map(block, jnp.arange(seq // NSA_QBLOCK))
    return out.transpose(1, 0, 2, 3).reshape(bsz, seq, NSA_HEADS * dh)


def mla_attention(c_q, c_kv, k_rope, positions, norm_q, norm_kv, w_uq, w_ukv):
    bsz, seq = c_q.shape[0], c_q.shape[1]
    q = (rmsnorm(c_q, norm_q) @ w_uq).reshape(bsz, seq, MLA_HEADS, QK_NOPE + QK_ROPE)
    kv = (rmsnorm(c_kv, norm_kv) @ w_ukv).reshape(bsz, seq, MLA_HEADS, QK_NOPE + V_DIM)
    q_nope, q_pe = q[..., :QK_NOPE], q[..., QK_NOPE:]
    k_nope, v = kv[..., :QK_NOPE], kv[..., QK_NOPE:]
    half = QK_ROPE // 2
    inv_freq = ROPE_THETA ** (-jnp.arange(half, dtype=jnp.float32) / half)
    ang = positions.astype(jnp.float32)[..., None] * inv_freq
    cos, sin = jnp.cos(ang), jnp.sin(ang)
    q_pe = apply_rope(q_pe, cos[:, :, None, :], sin[:, :, None, :])
    k_pe = apply_rope(k_rope, cos, sin)
    scale = (QK_NOPE + QK_ROPE) ** -0.5
    k_idx = jnp.arange(seq)

    def block(i):
        q0 = i * ATTN_QBLOCK
        qn = lax.dynamic_slice_in_dim(q_nope, q0, ATTN_QBLOCK, axis=1)
        qp = lax.dynamic_slice_in_dim(q_pe, q0, ATTN_QBLOCK, axis=1)
        logits = (jnp.einsum('bqhd,bkhd->bhqk', qn, k_nope, preferred_element_type=jnp.float32)
                  + jnp.einsum('bqhr,bkr->bhqk', qp, k_pe, preferred_element_type=jnp.float32)) * scale
        causal = k_idx[None, :] <= (q0 + jnp.arange(ATTN_QBLOCK))[:, None]
        p = jax.nn.softmax(jnp.where(causal, logits, NEG_INF), axis=-1)
        return jnp.einsum('bhqk,bkhd->bqhd', p.astype(v.dtype), v)

    out = lax.map(block, jnp.arange(seq // ATTN_QBLOCK))
    return out.transpose(1, 0, 2, 3, 4).reshape(bsz, seq, MLA_HEADS * V_DIM)


def memory_cross_attention(h, mem_n, w_q, w_kv, w_o):
    bsz, seq = h.shape[0], h.shape[1]
    m_len = mem_n.shape[1]
    q = (h @ w_q).reshape(bsz, seq, XATTN_HEADS, XATTN_HEAD_DIM)
    kv = (mem_n @ w_kv).reshape(bsz, m_len, 2, XATTN_HEADS, XATTN_HEAD_DIM)
    logits = jnp.einsum('bqhd,bmhd->bhqm', q, kv[:, :, 0], preferred_element_type=jnp.float32) * XATTN_HEAD_DIM ** -0.5
    p = jax.nn.softmax(logits, axis=-1)
    o = jnp.einsum('bhqm,bmhd->bqhd', p.astype(h.dtype), kv[:, :, 1]).reshape(bsz, seq, XATTN_HEADS * XATTN_HEAD_DIM)
    return o @ w_o


def setup_inputs(seed: int = 0) -> dict:
    key = jax.random.key(seed)
    keys = jax.random.split(key, 48)
    counter = [0]
    L = DEPTH

    def nxt():
        k = keys[counter[0]]
        counter[0] += 1
        return k

    def dense(shape, fan_in):
        return jax.random.normal(nxt(), shape, jnp.float32) * fan_in ** -0.5

    def gain(shape):
        return 1.0 + 0.02 * jax.random.normal(nxt(), shape, jnp.float32)

    def small(shape):
        return 0.02 * jax.random.normal(nxt(), shape, jnp.float32)

    dh = NSA_HEAD_DIM
    x = jax.random.normal(nxt(), (BATCH, SEQ, D_MODEL), jnp.float32)
    mem = jax.random.normal(nxt(), (BATCH, MEM_LEN, D_MODEL), jnp.float32)
    offsets = jax.random.randint(nxt(), (BATCH, 1), 0, 2048, dtype=jnp.int32)
    positions = offsets + jnp.arange(SEQ, dtype=jnp.int32)[None, :]
    return {
        'x': x,
        'mem': mem,
        'positions': positions,
        'rel_bias': 0.1 * jax.random.normal(nxt(), (REL_BUCKETS, NSA_HEADS), jnp.float32),
        'norm_mix': gain((L, D_MODEL)),
        'norm_xattn': gain((L, D_MODEL)),
        'norm_mem': gain((L, D_MODEL)),
        'norm_ffn': gain((L, D_MODEL)),
        'norm_final': gain((D_MODEL,)),
        'w_in': dense((L, D_MODEL, IN_COLS), D_MODEL),
        'conv_w': dense((L, CONV_WIDTH, CONV_CH), CONV_WIDTH),
        'conv_b': small((L, CONV_CH)),
        'conv_ln_g': gain((L, CONV_CH)),
        'conv_ln_b': small((L, CONV_CH)),
        'w_branch_conv': dense((L, CONV_CH, D_MODEL), CONV_CH),
        'cmp_pos_k': dense((L, CMP_BLOCK, dh), 4),
        'cmp_w1_k': dense((L, CMP_BLOCK * dh, CMP_HIDDEN), CMP_BLOCK * dh),
        'cmp_b1_k': small((L, CMP_HIDDEN)),
        'cmp_w2_k': dense((L, CMP_HIDDEN, dh), CMP_HIDDEN),
        'cmp_pos_v': dense((L, CMP_BLOCK, dh), 4),
        'cmp_w1_v': dense((L, CMP_BLOCK * dh, CMP_HIDDEN), CMP_BLOCK * dh),
        'cmp_b1_v': small((L, CMP_HIDDEN)),
        'cmp_w2_v': dense((L, CMP_HIDDEN, dh), CMP_HIDDEN),
        'w_branch_nsa': dense((L, NSA_HEADS * dh, D_MODEL), NSA_HEADS * dh),
        'mla_norm_q': gain((L, Q_RANK)),
        'mla_norm_kv': gain((L, KV_RANK)),
        'w_uq': dense((L, Q_RANK, MLA_HEADS * (QK_NOPE + QK_ROPE)), Q_RANK),
        'w_ukv': dense((L, KV_RANK, MLA_HEADS * (QK_NOPE + V_DIM)), KV_RANK),
        'w_branch_mla': dense((L, MLA_HEADS * V_DIM, D_MODEL), MLA_HEADS * V_DIM),
        'w_out': dense((L, D_MODEL, D_MODEL), D_MODEL),
        'w_xq': dense((L, D_MODEL, XATTN_HEADS * XATTN_HEAD_DIM), D_MODEL),
        'w_xkv': dense((L, D_MODEL, 2 * XATTN_HEADS * XATTN_HEAD_DIM), D_MODEL),
        'w_xo': dense((L, XATTN_HEADS * XATTN_HEAD_DIM, D_MODEL), XATTN_HEADS * XATTN_HEAD_DIM),
        'w_gate_up': dense((L, D_MODEL, 2 * FFN_HIDDEN), D_MODEL),
        'w_down': dense((L, FFN_HIDDEN, D_MODEL), FFN_HIDDEN),
    }


def reference(x, mem, positions, rel_bias, norm_mix, norm_xattn, norm_mem, norm_ffn, norm_final,
              w_in, conv_w, conv_b, conv_ln_g, conv_ln_b, w_branch_conv,
              cmp_pos_k, cmp_w1_k, cmp_b1_k, cmp_w2_k, cmp_pos_v, cmp_w1_v, cmp_b1_v, cmp_w2_v, w_branch_nsa,
              mla_norm_q, mla_norm_kv, w_uq, w_ukv, w_branch_mla, w_out,
              w_xq, w_xkv, w_xo, w_gate_up, w_down):
    split_at = [int(v) for v in np.cumsum(IN_SIZES)[:-1]]
    for layer in range(DEPTH):
        h = rmsnorm(x, norm_mix[layer])
        z = h @ w_in[layer]
        u_glu, q_nsa, kv_nsa, g_nsa, c_q, c_kv, k_rope, g_merge = jnp.split(z, split_at, axis=-1)
        y_conv = conformer_conv(u_glu, conv_w[layer], conv_b[layer], conv_ln_g[layer], conv_ln_b[layer], w_branch_conv[layer])
        y_nsa = nsa_attention(q_nsa, kv_nsa, g_nsa, rel_bias,
                              cmp_pos_k[layer], cmp_w1_k[layer], cmp_b1_k[layer], cmp_w2_k[layer],
                              cmp_pos_v[layer], cmp_w1_v[layer], cmp_b1_v[layer], cmp_w2_v[layer]) @ w_branch_nsa[layer]
        y_mla = mla_attention(c_q, c_kv, k_rope, positions, mla_norm_q[layer], mla_norm_kv[layer],
                              w_uq[layer], w_ukv[layer]) @ w_branch_mla[layer]
        g = jax.nn.sigmoid(g_merge.astype(jnp.float32)).astype(x.dtype)
        g_conv, g_attn, g_mla = jnp.split(g, 3, axis=-1)
        x = x + (g_conv * y_conv + g_attn * y_nsa + g_mla * y_mla) @ w_out[layer]
        x = x + memory_cross_attention(rmsnorm(x, norm_xattn[layer]), rmsnorm(mem, norm_mem[layer]),
                                       w_xq[layer], w_xkv[layer], w_xo[layer])
        gate, up = jnp.split(rmsnorm(x, norm_ffn[layer]) @ w_gate_up[layer], 2, axis=-1)
        x = x + (jax.nn.silu(gate) * up) @ w_down[layer]
    return rmsnorm(x, norm_final)
```

```python
import functools
import math

import numpy as np
import jax
import jax.numpy as jnp
from jax import lax
from jax.experimental import pallas as pl
from jax.experimental.pallas import tpu as pltpu

F32 = jnp.float32
BF16 = jnp.bfloat16

EPS = 1e-6
NEG_INF = -1e30
FORCE_SCORE = 1e4

D_MODEL = 1024
CONV_CH = 512
CONV_WIDTH = 31
NSA_HEADS = 8
NSA_G = 2
NSA_HG = NSA_HEADS // NSA_G
NSA_DH = 64
CMP_BLOCK = 32
CMP_STRIDE = 16
CMP_HIDDEN = 256
SLC_BLOCK = 64
N_SELECT = 16
WINDOW = 512
NSA_QB = 64
MLA_HEADS = 4
Q_RANK = 384
KV_RANK = 256
QK_NOPE = 128
QK_ROPE = 64
V_DIM = 128
ROPE_THETA = 10000.0
REL_BUCKETS = 32
REL_MAX_DIST = 128
XATTN_HEADS = 4
XATTN_DH = 128
FFN_HIDDEN = 2816

LANES = 128
SUBLANES = 8

Z_GM = 0
Z_UA = 3072
Z_UB = 3584
Z_Q = 4096
Z_CKV = 4608
Z_KR = 4864
Z_CQ = 4992
Z_KV = 5376
Z_GN = 6144
Z_COLS = 6272

VMEM_LIMIT = 56 * 1024 * 1024


def _cparams(sem):
    return pltpu.CompilerParams(dimension_semantics=sem, vmem_limit_bytes=VMEM_LIMIT)


def _rms(x, g):
    return x * lax.rsqrt(jnp.mean(x * x, axis=-1, keepdims=True) + EPS) * g


def _dot(a, b):
    return jnp.dot(a, b, preferred_element_type=F32)


def _dot_nt(a, b):
    return lax.dot_general(a, b, (((1,), (1,)), ((), ())), preferred_element_type=F32)


def _norm_matmul_kernel(x_ref, g_ref, w_ref, o_ref, h_ref):
    @pl.when(pl.program_id(1) == 0)
    def _():
        h_ref[...] = _rms(x_ref[...], g_ref[...]).astype(BF16)

    o_ref[...] = _dot(h_ref[...], w_ref[...])


def _norm_matmul(x, g, w, tm, tn):
    m, k = x.shape
    n = w.shape[1]
    return pl.pallas_call(
        _norm_matmul_kernel,
        out_shape=jax.ShapeDtypeStruct((m, n), F32),
        grid=(m // tm, n // tn),
        in_specs=[pl.BlockSpec((tm, k), lambda i, j: (i, 0)),
                  pl.BlockSpec((1, k), lambda i, j: (0, 0)),
                  pl.BlockSpec((k, tn), lambda i, j: (0, j))],
        out_specs=pl.BlockSpec((tm, tn), lambda i, j: (i, j)),
        scratch_shapes=[pltpu.VMEM((tm, k), BF16)],
        compiler_params=_cparams(("parallel", "arbitrary")),
        name="norm_matmul",
    )(x, g, w)


CONV_HALO = 32


def _conv_kernel(a_ref, b_ref, w_ref, cb_ref, lg_ref, lb_ref, o_ref, buf_ref, *, ts):
    @pl.when(pl.program_id(1) == 0)
    def _():
        buf_ref[0:CONV_HALO, :] = jnp.zeros((CONV_HALO, CONV_CH), F32)

    u = a_ref[...] * jax.nn.sigmoid(b_ref[...])
    buf_ref[CONV_HALO:CONV_HALO + ts, :] = u
    off = CONV_HALO - (CONV_WIDTH - 1)
    acc = jnp.zeros((ts, CONV_CH), F32) + cb_ref[...]
    for k in range(CONV_WIDTH):
        acc = acc + buf_ref[off + k:off + k + ts, :] * w_ref[k:k + 1, :]
    buf_ref[0:CONV_HALO, :] = buf_ref[ts:ts + CONV_HALO, :]
    mu = jnp.mean(acc, axis=-1, keepdims=True)
    xc = acc - mu
    var = jnp.mean(xc * xc, axis=-1, keepdims=True)
    y = xc * lax.rsqrt(var + EPS) * lg_ref[...] + lb_ref[...]
    o_ref[...] = (y * jax.nn.sigmoid(y)).astype(BF16)


def _conv_module(z, conv_w, conv_b, ln_g, ln_b, bsz, seq, ts):
    nst = seq // ts
    wpad = jnp.zeros((32, CONV_CH), F32).at[:CONV_WIDTH].set(conv_w)
    return pl.pallas_call(
        functools.partial(_conv_kernel, ts=ts),
        out_shape=jax.ShapeDtypeStruct((bsz * seq, CONV_CH), BF16),
        grid=(bsz, nst),
        in_specs=[pl.BlockSpec((ts, CONV_CH), lambda b, s: (b * nst + s, Z_UA // CONV_CH)),
                  pl.BlockSpec((ts, CONV_CH), lambda b, s: (b * nst + s, Z_UB // CONV_CH)),
                  pl.BlockSpec((32, CONV_CH), lambda b, s: (0, 0)),
                  pl.BlockSpec((1, CONV_CH), lambda b, s: (0, 0)),
                  pl.BlockSpec((1, CONV_CH), lambda b, s: (0, 0)),
                  pl.BlockSpec((1, CONV_CH), lambda b, s: (0, 0))],
        out_specs=pl.BlockSpec((ts, CONV_CH), lambda b, s: (b * nst + s, 0)),
        scratch_shapes=[pltpu.VMEM((ts + CONV_HALO, CONV_CH), F32)],
        compiler_params=_cparams(("arbitrary", "arbitrary")),
        name="conv_module",
    )(z, z, wpad, conv_b[None], ln_g[None], ln_b[None])


def _compress_kernel(x_ref, pa_ref, pb_ref, w1a_ref, w1b_ref, b1_ref, w2_ref, o_ref):
    x = x_ref[0, 0, 0]
    nch = x.shape[0]
    a = _dot((x + pa_ref[0]).astype(BF16), w1a_ref[0])
    b = _dot((x + pb_ref[0]).astype(BF16), w1b_ref[0])
    pre = a + pltpu.roll(b, nch - 1, 0) + b1_ref[0]
    o_ref[0, 0, 0] = _dot(jax.nn.gelu(pre).astype(BF16), w2_ref[0])


def _compress(chunks, pos, w1, b1, w2):
    _, bsz, g, nch, cw = chunks.shape
    half = CMP_STRIDE * NSA_DH
    pflat = pos.reshape(2, 1, CMP_BLOCK * NSA_DH)
    w1 = w1.astype(BF16)
    wspec = lambda shape: pl.BlockSpec(shape, lambda t, b, gg: (t, 0, 0))
    return pl.pallas_call(
        _compress_kernel,
        out_shape=jax.ShapeDtypeStruct((2, bsz, g, nch, NSA_DH), F32),
        grid=(2, bsz, g),
        in_specs=[pl.BlockSpec((1, 1, 1, nch, cw), lambda t, b, gg: (t, b, gg, 0, 0)),
                  wspec((1, 1, half)), wspec((1, 1, half)),
                  wspec((1, half, CMP_HIDDEN)), wspec((1, half, CMP_HIDDEN)),
                  wspec((1, 1, CMP_HIDDEN)), wspec((1, CMP_HIDDEN, NSA_DH))],
        out_specs=pl.BlockSpec((1, 1, 1, nch, NSA_DH), lambda t, b, gg: (t, b, gg, 0, 0)),
        compiler_params=_cparams(("parallel", "parallel", "parallel")),
        name="nsa_compress",
    )(chunks, pflat[:, :, :half], pflat[:, :, half:], w1[:, :half], w1[:, half:],
      b1[:, None], w2.astype(BF16))


SEL_PAD = 8
NEAR_KEYS = 256
FAR_KEYS = 256
WIN_KEYS = 640
SLC_FRONT = 128
CMP_TAB_ROWS = 520


def _softmax_cols(s):
    m = jnp.max(s, axis=0, keepdims=True)
    p = jnp.exp(s - m)
    return m, p, jnp.sum(p, axis=0, keepdims=True)


def _mask_blocks(s, mask_ref, row0, nblk):
    parts = []
    for jj in range(nblk):
        row = mask_ref[pl.ds(row0 + jj, 1), :]
        parts.append(jnp.where(row > 0.0, s[SLC_BLOCK * jj:SLC_BLOCK * (jj + 1)], NEG_INF))
    return jnp.concatenate(parts, axis=0)


def _nsa_kernel(q_ref, kc_ref, vct_ref, ks_ref, vst_ref, kw_ref, vwt_ref, gate_ref,
                tc_ref, tn_ref, tw_ref, cov_ref, rep_ref, o_ref,
                sel_ref, selfar_ref, score_ref, *, n_sb):
    g = pl.program_id(1)
    i = pl.program_id(2)
    p2 = lax.shift_right_logical(i, 1)
    e = lax.bitwise_and(i, 1)
    hq = NSA_HG * NSA_QB
    q = q_ref[0, 0, 0]

    start_c = pl.multiple_of(256 + 4 * e - 4 * i, SUBLANES)
    sc = _dot_nt(kc_ref[0, 0], q) + tc_ref[e, g, pl.ds(start_c, kc_ref.shape[2]), :]
    _, pc, lc = _softmax_cols(sc)
    lane = lax.broadcasted_iota(jnp.int32, (1, hq), 1)
    tq = NSA_QB * i + lax.bitwise_and(lane, NSA_QB - 1)
    anyv = jnp.where(tq >= CMP_BLOCK - 1, 1.0, 0.0)
    pc = pc * (anyv / lc)
    o_cmp = _dot(vct_ref[0, 0], pc.astype(BF16))

    psum = pc[:, 0:NSA_QB]
    for h in range(1, NSA_HG):
        psum = psum + pc[:, NSA_QB * h:NSA_QB * (h + 1)]
    p_hi = psum.astype(BF16)
    p_lo = (psum - p_hi.astype(F32)).astype(BF16)
    imp = _dot(cov_ref[...], p_hi) + _dot(cov_ref[...], p_lo)
    jrow = lax.broadcasted_iota(jnp.int32, (n_sb, NSA_QB), 0)
    forced = (jrow == 0) | (jrow == i) | (jrow == i - 1)
    score = jnp.where(forced, FORCE_SCORE, jnp.where(jrow <= i, imp, -1.0))
    score_ref[...] = score
    cnt = jnp.zeros((n_sb, NSA_QB), F32)
    for jp in range(n_sb):
        row = score_ref[jp:jp + 1, :]
        beats = (row > score) | ((row == score) & (jrow > jp))
        cnt = cnt + jnp.where(beats, 1.0, 0.0)
    sel = jnp.where(cnt < float(min(N_SELECT, n_sb)), 1.0, 0.0).astype(BF16)
    sel4 = _dot(sel, rep_ref[...])
    zeros8 = jnp.zeros((SEL_PAD, hq), F32)
    sel_ref[0:SEL_PAD, :] = zeros8
    sel_ref[SEL_PAD + n_sb:2 * SEL_PAD + n_sb, :] = zeros8
    sel_ref[SEL_PAD:SEL_PAD + n_sb, :] = sel4
    selfar_ref[0:SEL_PAD, :] = zeros8
    selfar_ref[SEL_PAD + n_sb:2 * SEL_PAD + n_sb, :] = zeros8
    jrow4 = lax.broadcasted_iota(jnp.int32, (n_sb, hq), 0)
    selfar_ref[SEL_PAD:SEL_PAD + n_sb, :] = jnp.where(jrow4 < 2 * p2 - 2, sel4, 0.0)

    near0 = pl.multiple_of(NEAR_KEYS // 2 * p2, LANES)
    s = _dot_nt(ks_ref[0, 0, pl.ds(near0, NEAR_KEYS), :], q) + tn_ref[e, g]
    s = _mask_blocks(s, sel_ref, 2 * p2 - 2 + SEL_PAD, NEAR_KEYS // SLC_BLOCK)
    m_s, p_s, l_s = _softmax_cols(s)
    acc_s = _dot(vst_ref[0, 0, :, pl.ds(near0, NEAR_KEYS)], p_s.astype(BF16))

    def far_body(c, carry):
        m_old, l_old, acc_old = carry
        k0 = pl.multiple_of(FAR_KEYS * c + SLC_FRONT, LANES)
        sf = _dot_nt(ks_ref[0, 0, pl.ds(k0, FAR_KEYS), :], q)
        sf = _mask_blocks(sf, selfar_ref, (FAR_KEYS // SLC_BLOCK) * c + SEL_PAD, FAR_KEYS // SLC_BLOCK)
        m_new = jnp.maximum(m_old, jnp.max(sf, axis=0, keepdims=True))
        alpha = jnp.exp(m_old - m_new)
        pf = jnp.exp(sf - m_new)
        l_new = alpha * l_old + jnp.sum(pf, axis=0, keepdims=True)
        acc_new = alpha * acc_old + _dot(vst_ref[0, 0, :, pl.ds(k0, FAR_KEYS)], pf.astype(BF16))
        return m_new, l_new, acc_new

    _, l_s, acc_s = lax.fori_loop(0, lax.shift_right_logical(p2, 1), far_body, (m_s, l_s, acc_s))

    sw = _dot_nt(kw_ref[0, 0, pl.ds(near0, WIN_KEYS), :], q) + tw_ref[e, g]
    krow = lax.broadcasted_iota(jnp.int32, (WIN_KEYS, hq), 0)
    sw = jnp.where(krow + near0 >= WINDOW, sw, NEG_INF)
    _, p_w, l_w = _softmax_cols(sw)
    acc_w = _dot(vwt_ref[0, 0, :, pl.ds(near0, WIN_KEYS)], p_w.astype(BF16))

    gs = jax.nn.sigmoid(gate_ref[0, 0, 0])
    o_ref[0, 0, 0] = (gs[0:1] * o_cmp + (gs[1:2] / l_s) * acc_s + (gs[2:3] / l_w) * acc_w)


def _t5_bucket_np(d):
    exact = REL_BUCKETS // 2
    d = np.maximum(d, 0)
    ratio = np.log(np.maximum(d, 1).astype(np.float32) / np.float32(exact)) / np.float32(math.log(REL_MAX_DIST / exact))
    large = np.minimum(exact + (ratio * (REL_BUCKETS - exact)).astype(np.int32), REL_BUCKETS - 1)
    return np.where(d < exact, d, large).astype(np.int32)


def _bias_table(rel_bias, dist, valid, shift):
    tab = rel_bias[_t5_bucket_np(dist)]
    if shift:
        tab = tab - rel_bias[REL_BUCKETS - 1]
    tab = jnp.where(valid[..., None], tab, NEG_INF)
    e, r, qn = dist.shape
    tab = tab.reshape(e, r, qn, NSA_G, NSA_HG).transpose(0, 3, 1, 4, 2)
    return tab.reshape(e, NSA_G, r, NSA_HG * qn).astype(F32)


def _nsa_tables(rel_bias):
    qv = np.arange(NSA_QB)[None, None, :]
    ev = np.arange(2)[:, None, None]
    r = np.arange(CMP_TAB_ROWS)[None, :, None]
    d_c = qv - CMP_STRIDE * (r - 256 - 4 * ev) - (CMP_BLOCK - 1)
    tc = _bias_table(rel_bias, d_c, d_c >= 0, False)
    r = np.arange(NEAR_KEYS)[None, :, None]
    d_n = NEAR_KEYS // 2 + NSA_QB * ev + qv - r
    tn = _bias_table(rel_bias, d_n, d_n >= 0, True)
    r = np.arange(WIN_KEYS)[None, :, None]
    d_w = WINDOW + NSA_QB * ev + qv - r
    tw = _bias_table(rel_bias, d_w, (d_w >= 0) & (d_w < WINDOW), False)
    return tc, tn, tw


def _nsa_attention(zq, zkv, zg, k_cmp, v_cmp, tables, bsz, seq):
    g, hg, dh, qb = NSA_G, NSA_HG, NSA_DH, NSA_QB
    nqb = seq // qb
    n_sb = seq // SLC_BLOCK
    nch = k_cmp.shape[2]
    hq = hg * qb
    tc, tn, tw = tables
    q = (zq * dh ** -0.5).astype(BF16).reshape(bsz, nqb, qb, g, hg, dh)
    q = q.transpose(0, 3, 1, 4, 2, 5).reshape(bsz, g, nqb, hq, dh)
    kv = zkv.astype(BF16).reshape(bsz, seq, 6, g, dh).transpose(2, 0, 3, 1, 4)
    k_slc = jnp.pad(kv[2], ((0, 0), (0, 0), (SLC_FRONT, 0), (0, 0)))
    v_slc_t = jnp.pad(kv[3], ((0, 0), (0, 0), (SLC_FRONT, 0), (0, 0))).transpose(0, 1, 3, 2)
    k_win = jnp.pad(kv[4], ((0, 0), (0, 0), (WINDOW, 0), (0, 0)))
    v_win_t = jnp.pad(kv[5], ((0, 0), (0, 0), (WINDOW, 0), (0, 0))).transpose(0, 1, 3, 2)
    kc = k_cmp.astype(BF16)
    vct = v_cmp.astype(BF16).transpose(0, 1, 3, 2)
    gates = zg.reshape(bsz, nqb, qb, g, hg, 3).transpose(0, 3, 1, 5, 4, 2).reshape(bsz, g, nqb, 3, hq)
    gates = jnp.pad(gates, ((0, 0), (0, 0), (0, 0), (0, SUBLANES - 3), (0, 0)))
    c_start = CMP_STRIDE * np.arange(nch)
    s_start = SLC_BLOCK * np.arange(n_sb)
    cover_t = ((c_start[None, :] < s_start[:, None] + SLC_BLOCK)
               & (c_start[None, :] + CMP_BLOCK > s_start[:, None])
               & (np.arange(nch)[None, :] < (seq - CMP_BLOCK) // CMP_STRIDE + 1))
    cover_t = jnp.asarray(cover_t.astype(np.float32), BF16)
    rep = jnp.asarray(np.tile(np.eye(qb, dtype=np.float32), (1, hg)), BF16)
    sp = k_slc.shape[2]
    wp = k_win.shape[2]
    full = lambda shape: pl.BlockSpec(shape, lambda b, gg, i: (0,) * len(shape))
    per_bg = lambda shape: pl.BlockSpec(shape, lambda b, gg, i: (b, gg, 0, 0))
    out = pl.pallas_call(
        functools.partial(_nsa_kernel, n_sb=n_sb),
        out_shape=jax.ShapeDtypeStruct((bsz, g, nqb, dh, hq), F32),
        grid=(bsz, g, nqb),
        in_specs=[pl.BlockSpec((1, 1, 1, hq, dh), lambda b, gg, i: (b, gg, i, 0, 0)),
                  per_bg((1, 1, nch, dh)), per_bg((1, 1, dh, nch)),
                  per_bg((1, 1, sp, dh)), per_bg((1, 1, dh, sp)),
                  per_bg((1, 1, wp, dh)), per_bg((1, 1, dh, wp)),
                  pl.BlockSpec((1, 1, 1, SUBLANES, hq), lambda b, gg, i: (b, gg, i, 0, 0)),
                  full(tc.shape), full(tn.shape), full(tw.shape),
                  full(cover_t.shape), full(rep.shape)],
        out_specs=pl.BlockSpec((1, 1, 1, dh, hq), lambda b, gg, i: (b, gg, i, 0, 0)),
        scratch_shapes=[pltpu.VMEM((n_sb + 2 * SEL_PAD, hq), F32),
                        pltpu.VMEM((n_sb + 2 * SEL_PAD, hq), F32),
                        pltpu.VMEM((n_sb, qb), F32)],
        compiler_params=_cparams(("parallel", "parallel", "arbitrary")),
        name="nsa_attention",
    )(q, kc, vct, k_slc, v_slc_t, k_win, v_win_t, gates, tc, tn, tw, cover_t, rep)
    out = out.reshape(bsz, g, nqb, dh, hg, qb).transpose(0, 2, 5, 1, 4, 3)
    return out.reshape(bsz * seq, g * hg * dh).astype(BF16)


def _rope_table_kernel(pos_ref, inv_ref, sign_ref, o_ref):
    ang = pos_ref[...].astype(F32) * inv_ref[...]
    o_ref[...] = jnp.concatenate([jnp.cos(ang), jnp.sin(ang) * sign_ref[...]], axis=-1)


def _rope_table(positions, tm):
    t = positions.size
    half = QK_ROPE // 2
    inv = ROPE_THETA ** (-jnp.arange(half, dtype=F32) / half)
    inv2 = jnp.concatenate([inv, inv])[None]
    sign = jnp.asarray(np.concatenate([-np.ones(half), np.ones(half)]).astype(np.float32))[None]
    return pl.pallas_call(
        _rope_table_kernel,
        out_shape=jax.ShapeDtypeStruct((t, 2 * QK_ROPE), F32),
        grid=(t // tm,),
        in_specs=[pl.BlockSpec((tm, 1), lambda i: (i, 0)),
                  pl.BlockSpec((1, QK_ROPE), lambda i: (0, 0)),
                  pl.BlockSpec((1, QK_ROPE), lambda i: (0, 0))],
        out_specs=pl.BlockSpec((tm, 2 * QK_ROPE), lambda i: (i, 0)),
        compiler_params=_cparams(("parallel",)),
        name="rope_table",
    )(positions.reshape(t, 1), inv2, sign)


MLA_HW = 256


def _mla_proj_kernel(cq_ref, ckv_ref, kr_ref, rope_ref, nq_ref, nkv_ref, wq_ref, wkv_ref,
                     q_ref, k_ref, v_ref):
    scale = (QK_NOPE + QK_ROPE) ** -0.5
    rope = rope_ref[...]
    yq = _dot(_rms(cq_ref[...], nq_ref[...]).astype(BF16), wq_ref[...])
    ykv = _dot(_rms(ckv_ref[...], nkv_ref[...]).astype(BF16), wkv_ref[...])
    kp = kr_ref[...] * rope
    kp = kp + pltpu.roll(kp, QK_ROPE, 1)
    lane = lax.broadcasted_iota(jnp.int32, kp.shape, 1)
    kp = jnp.where(lane < QK_ROPE, kp, 0.0).astype(BF16)
    for h in range(MLA_HEADS):
        base = MLA_HW * h
        q_ref[:, base:base + QK_NOPE] = (yq[:, base:base + QK_NOPE] * scale).astype(BF16)
        qp = yq[:, base + QK_NOPE:base + MLA_HW] * rope
        qp = qp + pltpu.roll(qp, QK_ROPE, 1)
        q_ref[:, base + QK_NOPE:base + MLA_HW] = (qp * scale).astype(BF16)
        k_ref[:, base:base + QK_NOPE] = ykv[:, base:base + QK_NOPE].astype(BF16)
        k_ref[:, base + QK_NOPE:base + MLA_HW] = kp
        v_ref[:, V_DIM * h:V_DIM * (h + 1)] = ykv[:, base + QK_NOPE:base + MLA_HW].astype(BF16)


def _swap_halves(w):
    half = QK_ROPE // 2
    return jnp.concatenate([w[..., half:], w[..., :half]], axis=-1)


def _mla_proj(z, rope_tab, norm_q, norm_kv, w_uq, w_ukv, tm):
    t = z.shape[0]
    wq = w_uq.reshape(Q_RANK, MLA_HEADS, QK_NOPE + QK_ROPE)
    wq = jnp.concatenate([wq, _swap_halves(wq[..., QK_NOPE:])], axis=-1)
    wq = wq.reshape(Q_RANK, MLA_HEADS * MLA_HW).astype(BF16)
    wkv = w_ukv.astype(BF16)
    hw = MLA_HEADS * MLA_HW
    return pl.pallas_call(
        _mla_proj_kernel,
        out_shape=(jax.ShapeDtypeStruct((t, hw), BF16),
                   jax.ShapeDtypeStruct((t, hw), BF16),
                   jax.ShapeDtypeStruct((t, MLA_HEADS * V_DIM), BF16)),
        grid=(t // tm,),
        in_specs=[pl.BlockSpec((tm, Q_RANK), lambda i: (i, Z_CQ // Q_RANK)),
                  pl.BlockSpec((tm, KV_RANK), lambda i: (i, Z_CKV // KV_RANK)),
                  pl.BlockSpec((tm, 2 * QK_ROPE), lambda i: (i, Z_KR // (2 * QK_ROPE))),
                  pl.BlockSpec((tm, 2 * QK_ROPE), lambda i: (i, 0)),
                  pl.BlockSpec((1, Q_RANK), lambda i: (0, 0)),
                  pl.BlockSpec((1, KV_RANK), lambda i: (0, 0)),
                  pl.BlockSpec((Q_RANK, hw), lambda i: (0, 0)),
                  pl.BlockSpec((KV_RANK, hw), lambda i: (0, 0))],
        out_specs=(pl.BlockSpec((tm, hw), lambda i: (i, 0)),
                   pl.BlockSpec((tm, hw), lambda i: (i, 0)),
                   pl.BlockSpec((tm, MLA_HEADS * V_DIM), lambda i: (i, 0))),
        compiler_params=_cparams(("parallel",)),
        name="mla_proj",
    )(z, z, z, rope_tab, norm_q[None], norm_kv[None], wq, wkv)


def _mla_attn_kernel(q_ref, k_ref, vt_ref, o_ref, *, tq, tk):
    iq = pl.program_id(2)
    q = q_ref[0]
    cd = lax.div(iq * tq, tk)

    def scores(c):
        k0 = pl.multiple_of(c * tk, tk)
        return k0, _dot_nt(k_ref[0, pl.ds(k0, tk), :], q)

    k0, s = scores(cd)
    kpos = k0 + lax.broadcasted_iota(jnp.int32, (tk, tq), 0)
    qpos = iq * tq + lax.broadcasted_iota(jnp.int32, (tk, tq), 1)
    s = jnp.where(kpos <= qpos, s, NEG_INF)
    m0, p0, l0 = _softmax_cols(s)
    acc0 = _dot(vt_ref[0, 0, :, pl.ds(k0, tk)], p0.astype(BF16))

    def body(c, carry):
        m_old, l_old, acc_old = carry
        k0, s = scores(c)
        m_new = jnp.maximum(m_old, jnp.max(s, axis=0, keepdims=True))
        alpha = jnp.exp(m_old - m_new)
        p = jnp.exp(s - m_new)
        l_new = alpha * l_old + jnp.sum(p, axis=0, keepdims=True)
        acc_new = alpha * acc_old + _dot(vt_ref[0, 0, :, pl.ds(k0, tk)], p.astype(BF16))
        return m_new, l_new, acc_new

    _, l, acc = lax.fori_loop(0, cd, body, (m0, l0, acc0))
    o_ref[0, 0] = acc / l


def _mla_attention(qf, kf, v, bsz, seq, tq, tk):
    h = MLA_HEADS
    q3 = qf.reshape(bsz, seq, h * MLA_HW)
    k3 = kf.reshape(bsz, seq, h * MLA_HW)
    vt = v.reshape(bsz, seq, h, V_DIM).transpose(0, 2, 3, 1)
    out = pl.pallas_call(
        functools.partial(_mla_attn_kernel, tq=tq, tk=tk),
        out_shape=jax.ShapeDtypeStruct((bsz, h, V_DIM, seq), F32),
        grid=(bsz, h, seq // tq),
        in_specs=[pl.BlockSpec((1, tq, MLA_HW), lambda b, hh, i: (b, i, hh)),
                  pl.BlockSpec((1, seq, MLA_HW), lambda b, hh, i: (b, 0, hh)),
                  pl.BlockSpec((1, 1, V_DIM, seq), lambda b, hh, i: (b, hh, 0, 0))],
        out_specs=pl.BlockSpec((1, 1, V_DIM, tq), lambda b, hh, i: (b, hh, 0, i)),
        compiler_params=_cparams(("parallel", "parallel", "arbitrary")),
        name="mla_attention",
    )(q3, k3, vt)
    return out.transpose(0, 3, 1, 2).reshape(bsz * seq, h * V_DIM).astype(BF16)


def _merge_kernel(ya_ref, yb_ref, yc_ref, ga_ref, gb_ref, gc_ref, x_ref,
                  wa_ref, wb_ref, wc_ref, wo_ref, o_ref):
    y = (jax.nn.sigmoid(ga_ref[...]) * _dot(ya_ref[...], wa_ref[...])
         + jax.nn.sigmoid(gb_ref[...]) * _dot(yb_ref[...], wb_ref[...])
         + jax.nn.sigmoid(gc_ref[...]) * _dot(yc_ref[...], wc_ref[...]))
    o_ref[...] = x_ref[...] + _dot(y.astype(BF16), wo_ref[...])


def _merge(ya, yb, yc, z, x, wa, wb, wc, wo, tm):
    t, d = x.shape
    act = pl.BlockSpec((tm, ya.shape[1]), lambda i: (i, 0))
    gate = lambda k: pl.BlockSpec((tm, d), lambda i: (i, Z_GM // d + k))
    wbr = pl.BlockSpec((ya.shape[1], d), lambda i: (0, 0))
    return pl.pallas_call(
        _merge_kernel,
        out_shape=jax.ShapeDtypeStruct((t, d), F32),
        grid=(t // tm,),
        in_specs=[act, act, act, gate(0), gate(1), gate(2),
                  pl.BlockSpec((tm, d), lambda i: (i, 0)),
                  wbr, wbr, wbr, pl.BlockSpec((d, d), lambda i: (0, 0))],
        out_specs=pl.BlockSpec((tm, d), lambda i: (i, 0)),
        compiler_params=_cparams(("parallel",)),
        name="merge",
    )(ya, yb, yc, z, z, z, x, wa.astype(BF16), wb.astype(BF16), wc.astype(BF16), wo.astype(BF16))


def _xattn_kernel(x_ref, g_ref, wq_ref, kv_ref, wo_ref, o_ref):
    x = x_ref[...]
    h = _rms(x, g_ref[...]).astype(BF16)
    q = _dot(h, wq_ref[...]) * XATTN_DH ** -0.5
    hd = XATTN_HEADS * XATTN_DH
    outs = []
    for hh in range(XATTN_HEADS):
        qh = q[:, XATTN_DH * hh:XATTN_DH * (hh + 1)].astype(BF16)
        kh = kv_ref[0, :, XATTN_DH * hh:XATTN_DH * (hh + 1)]
        vh = kv_ref[0, :, hd + XATTN_DH * hh:hd + XATTN_DH * (hh + 1)]
        s = _dot_nt(qh, kh)
        m = jnp.max(s, axis=-1, keepdims=True)
        p = jnp.exp(s - m)
        p = p / jnp.sum(p, axis=-1, keepdims=True)
        outs.append(_dot(p.astype(BF16), vh))
    o = jnp.concatenate(outs, axis=-1).astype(BF16)
    o_ref[...] = x + _dot(o, wo_ref[...])


def _xattn(x, g, wq, kv, wo, bsz, seq, tm):
    t, d = x.shape
    nst = seq // tm
    m_len = kv.shape[1]
    hd = XATTN_HEADS * XATTN_DH
    return pl.pallas_call(
        _xattn_kernel,
        out_shape=jax.ShapeDtypeStruct((t, d), F32),
        grid=(bsz, nst),
        in_specs=[pl.BlockSpec((tm, d), lambda b, s: (b * nst + s, 0)),
                  pl.BlockSpec((1, d), lambda b, s: (0, 0)),
                  pl.BlockSpec((d, hd), lambda b, s: (0, 0)),
                  pl.BlockSpec((1, m_len, 2 * hd), lambda b, s: (b, 0, 0)),
                  pl.BlockSpec((hd, d), lambda b, s: (0, 0))],
        out_specs=pl.BlockSpec((tm, d), lambda b, s: (b * nst + s, 0)),
        compiler_params=_cparams(("parallel", "parallel")),
        name="xattn",
    )(x, g[None], wq.astype(BF16), kv, wo.astype(BF16))


def _ffn_kernel(x_ref, g_ref, wg_ref, wu_ref, wd_ref, gf_ref, o_ref, h_ref, acc_ref, *, final):
    c = pl.program_id(1)

    @pl.when(c == 0)
    def _():
        h_ref[...] = _rms(x_ref[...], g_ref[...]).astype(BF16)
        acc_ref[...] = x_ref[...]

    h = h_ref[...]
    gate = _dot(h, wg_ref[...])
    up = _dot(h, wu_ref[...])
    act = (gate * jax.nn.sigmoid(gate) * up).astype(BF16)
    acc_ref[...] += _dot(act, wd_ref[...])

    @pl.when(c == pl.num_programs(1) - 1)
    def _():
        y = acc_ref[...]
        o_ref[...] = _rms(y, gf_ref[...]) if final else y


def _ffn(x, g, w_gate_up, w_down, g_final, final, tm, tc):
    t, d = x.shape
    nc = FFN_HIDDEN // tc
    wgu = w_gate_up.astype(BF16)
    return pl.pallas_call(
        functools.partial(_ffn_kernel, final=final),
        out_shape=jax.ShapeDtypeStruct((t, d), F32),
        grid=(t // tm, nc),
        in_specs=[pl.BlockSpec((tm, d), lambda i, c: (i, 0)),
                  pl.BlockSpec((1, d), lambda i, c: (0, 0)),
                  pl.BlockSpec((d, tc), lambda i, c: (0, c)),
                  pl.BlockSpec((d, tc), lambda i, c: (0, nc + c)),
                  pl.BlockSpec((tc, d), lambda i, c: (c, 0)),
                  pl.BlockSpec((1, d), lambda i, c: (0, 0))],
        out_specs=pl.BlockSpec((tm, d), lambda i, c: (i, 0)),
        scratch_shapes=[pltpu.VMEM((tm, d), BF16), pltpu.VMEM((tm, d), F32)],
        compiler_params=_cparams(("parallel", "arbitrary")),
        name="ffn",
    )(x, g[None], wgu, wgu, w_down.astype(BF16), g_final[None])


def _permute_w_in(w):
    o_glu, o_q, o_kv, o_gn, o_cq, o_ckv, o_kr, o_gm = 0, 1024, 1536, 2304, 2328, 2712, 2968, 3032
    k_rope = w[:, o_kr:o_kr + QK_ROPE]
    pad = jnp.zeros((w.shape[0], Z_COLS - Z_GN - 3 * NSA_HEADS), w.dtype)
    return jnp.concatenate([
        w[:, o_gm:o_gm + 3 * D_MODEL],
        w[:, o_glu:o_glu + 2 * CONV_CH],
        w[:, o_q:o_q + 512],
        w[:, o_ckv:o_ckv + KV_RANK],
        k_rope, _swap_halves(k_rope),
        w[:, o_cq:o_cq + Q_RANK],
        w[:, o_kv:o_kv + 768],
        w[:, o_gn:o_gn + 3 * NSA_HEADS], pad], axis=1).astype(BF16)


def kernel(x, mem, positions, rel_bias, norm_mix, norm_xattn, norm_mem, norm_ffn, norm_final, w_in, conv_w, conv_b, conv_ln_g, conv_ln_b, w_branch_conv, cmp_pos_k, cmp_w1_k, cmp_b1_k, cmp_w2_k, cmp_pos_v, cmp_w1_v, cmp_b1_v, cmp_w2_v, w_branch_nsa, mla_norm_q, mla_norm_kv, w_uq, w_ukv, w_branch_mla, w_out, w_xq, w_xkv, w_xo, w_gate_up, w_down):
    bsz, seq, d = x.shape
    depth = w_in.shape[0]
    t = bsz * seq
    m_len = mem.shape[1]
    tm = 512
    xt = x.reshape(t, d)
    memt = mem.reshape(bsz * m_len, d)
    rope_tab = _rope_table(positions, tm)
    tables = _nsa_tables(rel_bias)
    nch = seq // CMP_STRIDE
    for l in range(depth):
        z = _norm_matmul(xt, norm_mix[l][None], _permute_w_in(w_in[l]), tm, 896)
        ya = _conv_module(z, conv_w[l], conv_b[l], conv_ln_g[l], conv_ln_b[l], bsz, seq, tm)
        zkv = z[:, Z_KV:Z_KV + 768]
        chunks = zkv.reshape(bsz, nch, CMP_STRIDE, 6, NSA_G, NSA_DH)[:, :, :, 0:2]
        chunks = chunks.transpose(3, 0, 4, 1, 2, 5).reshape(2, bsz, NSA_G, nch, CMP_STRIDE * NSA_DH)
        cmp_kv = _compress(chunks,
                           jnp.stack([cmp_pos_k[l], cmp_pos_v[l]]), jnp.stack([cmp_w1_k[l], cmp_w1_v[l]]),
                           jnp.stack([cmp_b1_k[l], cmp_b1_v[l]]), jnp.stack([cmp_w2_k[l], cmp_w2_v[l]]))
        yb = _nsa_attention(z[:, Z_Q:Z_Q + 512], zkv, z[:, Z_GN:Z_GN + 3 * NSA_HEADS],
                            cmp_kv[0], cmp_kv[1], tables, bsz, seq)
        qf, kf, v = _mla_proj(z, rope_tab, mla_norm_q[l], mla_norm_kv[l], w_uq[l], w_ukv[l], tm)
        yc = _mla_attention(qf, kf, v, bsz, seq, 256, 512)
        xt = _merge(ya, yb, yc, z, xt, w_branch_conv[l], w_branch_nsa[l], w_branch_mla[l], w_out[l], tm)
        mem_kv = _norm_matmul(memt, norm_mem[l][None], w_xkv[l].astype(BF16), 256, 1024)
        mem_kv = mem_kv.astype(BF16).reshape(bsz, m_len, 2 * XATTN_HEADS * XATTN_DH)
        xt = _xattn(xt, norm_xattn[l], w_xq[l], mem_kv, w_xo[l], bsz, seq, tm)
        xt = _ffn(xt, norm_ffn[l], w_gate_up[l], w_down[l], norm_final, l == depth - 1, 1024, 256)
    return xt.reshape(bsz, seq, d)
```

```python
import functools
import math

import numpy as np
import jax
import jax.numpy as jnp
from jax import lax
from jax.experimental import pallas as pl
from jax.experimental.pallas import tpu as pltpu

F32 = jnp.float32
BF16 = jnp.bfloat16

EPS = 1e-6
NEG_INF = -1e30
FORCE_SCORE = 1e4

D_MODEL = 1024
CONV_CH = 512
CONV_WIDTH = 31
NSA_HEADS = 8
NSA_G = 2
NSA_HG = NSA_HEADS // NSA_G
NSA_DH = 64
CMP_BLOCK = 32
CMP_STRIDE = 16
CMP_HIDDEN = 256
SLC_BLOCK = 64
N_SELECT = 16
WINDOW = 512
NSA_QB = 64
MLA_HEADS = 4
Q_RANK = 384
KV_RANK = 256
QK_NOPE = 128
QK_ROPE = 64
V_DIM = 128
ROPE_THETA = 10000.0
REL_BUCKETS = 32
REL_MAX_DIST = 128
XATTN_HEADS = 4
XATTN_DH = 128
FFN_HIDDEN = 2816

LANES = 128
SUBLANES = 8

Z_GM = 0
Z_UA = 3072
Z_UB = 3584
Z_Q = 4096
Z_CKV = 4608
Z_KR = 4864
Z_CQ = 4992
Z_KV = 5376
Z_GN = 6144
Z_COLS = 6272

VMEM_LIMIT = 56 * 1024 * 1024


def _cparams(sem):
    return pltpu.CompilerParams(dimension_semantics=sem, vmem_limit_bytes=VMEM_LIMIT)


def _rms(x, g):
    return x * lax.rsqrt(jnp.mean(x * x, axis=-1, keepdims=True) + EPS) * g


def _dot(a, b):
    return jnp.dot(a, b, preferred_element_type=F32)


def _dot_nt(a, b):
    return lax.dot_general(a, b, (((1,), (1,)), ((), ())), preferred_element_type=F32)


def _norm_matmul_kernel(x_ref, g_ref, w_ref, o_ref, h_ref):
    @pl.when(pl.program_id(1) == 0)
    def _():
        h_ref[...] = _rms(x_ref[...], g_ref[...]).astype(BF16)

    o_ref[...] = _dot(h_ref[...], w_ref[...])


def _norm_matmul(x, g, w, tm, tn):
    m, k = x.shape
    n = w.shape[1]
    return pl.pallas_call(
        _norm_matmul_kernel,
        out_shape=jax.ShapeDtypeStruct((m, n), F32),
        grid=(m // tm, n // tn),
        in_specs=[pl.BlockSpec((tm, k), lambda i, j: (i, 0)),
                  pl.BlockSpec((1, k), lambda i, j: (0, 0)),
                  pl.BlockSpec((k, tn), lambda i, j: (0, j))],
        out_specs=pl.BlockSpec((tm, tn), lambda i, j: (i, j)),
        scratch_shapes=[pltpu.VMEM((tm, k), BF16)],
        compiler_params=_cparams(("parallel", "arbitrary")),
        name="norm_matmul",
    )(x, g, w)


CONV_HALO = 32


def _conv_kernel(a_ref, b_ref, w_ref, cb_ref, lg_ref, lb_ref, o_ref, buf_ref, *, ts):
    @pl.when(pl.program_id(1) == 0)
    def _():
        buf_ref[0:CONV_HALO, :] = jnp.zeros((CONV_HALO, CONV_CH), F32)

    u = a_ref[...] * jax.nn.sigmoid(b_ref[...])
    buf_ref[CONV_HALO:CONV_HALO + ts, :] = u
    off = CONV_HALO - (CONV_WIDTH - 1)
    acc = jnp.zeros((ts, CONV_CH), F32) + cb_ref[...]
    for k in range(CONV_WIDTH):
        acc = acc + buf_ref[off + k:off + k + ts, :] * w_ref[k:k + 1, :]
    buf_ref[0:CONV_HALO, :] = buf_ref[ts:ts + CONV_HALO, :]
    mu = jnp.mean(acc, axis=-1, keepdims=True)
    xc = acc - mu
    var = jnp.mean(xc * xc, axis=-1, keepdims=True)
    y = xc * lax.rsqrt(var + EPS) * lg_ref[...] + lb_ref[...]
    o_ref[...] = (y * jax.nn.sigmoid(y)).astype(BF16)


def _conv_module(z, conv_w, conv_b, ln_g, ln_b, bsz, seq, ts):
    nst = seq // ts
    wpad = jnp.zeros((32, CONV_CH), F32).at[:CONV_WIDTH].set(conv_w)
    return pl.pallas_call(
        functools.partial(_conv_kernel, ts=ts),
        out_shape=jax.ShapeDtypeStruct((bsz * seq, CONV_CH), BF16),
        grid=(bsz, nst),
        in_specs=[pl.BlockSpec((ts, CONV_CH), lambda b, s: (b * nst + s, Z_UA // CONV_CH)),
                  pl.BlockSpec((ts, CONV_CH), lambda b, s: (b * nst + s, Z_UB // CONV_CH)),
                  pl.BlockSpec((32, CONV_CH), lambda b, s: (0, 0)),
                  pl.BlockSpec((1, CONV_CH), lambda b, s: (0, 0)),
                  pl.BlockSpec((1, CONV_CH), lambda b, s: (0, 0)),
                  pl.BlockSpec((1, CONV_CH), lambda b, s: (0, 0))],
        out_specs=pl.BlockSpec((ts, CONV_CH), lambda b, s: (b * nst + s, 0)),
        scratch_shapes=[pltpu.VMEM((ts + CONV_HALO, CONV_CH), F32)],
        compiler_params=_cparams(("arbitrary", "arbitrary")),
        name="conv_module",
    )(z, z, wpad, conv_b[None], ln_g[None], ln_b[None])


def _compress_kernel(x_ref, pa_ref, pb_ref, w1a_ref, w1b_ref, b1_ref, w2_ref, o_ref):
    x = x_ref[0, 0, 0]
    nch = x.shape[0]
    a = _dot((x + pa_ref[0]).astype(BF16), w1a_ref[0])
    b = _dot((x + pb_ref[0]).astype(BF16), w1b_ref[0])
    pre = a + pltpu.roll(b, nch - 1, 0) + b1_ref[0]
    o_ref[0, 0, 0] = _dot(jax.nn.gelu(pre).astype(BF16), w2_ref[0])


def _compress(chunks, pos, w1, b1, w2):
    _, bsz, g, nch, cw = chunks.shape
    half = CMP_STRIDE * NSA_DH
    pflat = pos.reshape(2, 1, CMP_BLOCK * NSA_DH)
    w1 = w1.astype(BF16)
    wspec = lambda shape: pl.BlockSpec(shape, lambda t, b, gg: (t, 0, 0))
    return pl.pallas_call(
        _compress_kernel,
        out_shape=jax.ShapeDtypeStruct((2, bsz, g, nch, NSA_DH), F32),
        grid=(2, bsz, g),
        in_specs=[pl.BlockSpec((1, 1, 1, nch, cw), lambda t, b, gg: (t, b, gg, 0, 0)),
                  wspec((1, 1, half)), wspec((1, 1, half)),
                  wspec((1, half, CMP_HIDDEN)), wspec((1, half, CMP_HIDDEN)),
                  wspec((1, 1, CMP_HIDDEN)), wspec((1, CMP_HIDDEN, NSA_DH))],
        out_specs=pl.BlockSpec((1, 1, 1, nch, NSA_DH), lambda t, b, gg: (t, b, gg, 0, 0)),
        compiler_params=_cparams(("parallel", "parallel", "parallel")),
        name="nsa_compress",
    )(chunks, pflat[:, :, :half], pflat[:, :, half:], w1[:, :half], w1[:, half:],
      b1[:, None], w2.astype(BF16))


NSA_QP = 2 * NSA_QB
SEL_PAD = 8
NEAR_KEYS = 256
FAR_KEYS = 256
WIN_KEYS = 640
SLC_FRONT = 128
CMP_TAB_ROWS = 512


def _softmax_cols(s):
    m = jnp.max(s, axis=0, keepdims=True)
    p = jnp.exp(s - m)
    return m, p, jnp.sum(p, axis=0, keepdims=True)


def _mask_blocks(s, mask_ref, row0, nblk):
    parts = []
    for jj in range(nblk):
        row = mask_ref[pl.ds(row0 + jj, 1), :]
        parts.append(jnp.where(row > 0.0, s[SLC_BLOCK * jj:SLC_BLOCK * (jj + 1)], NEG_INF))
    return jnp.concatenate(parts, axis=0)


def _rank_select(score_ref, n_sb, n_sel):
    groups = n_sb // SUBLANES
    sub = lax.broadcasted_iota(jnp.int32, (SUBLANES, NSA_QP), 0)
    tiles = [score_ref[SUBLANES * v:SUBLANES * (v + 1), :] for v in range(groups)]
    cnts = [jnp.zeros((SUBLANES, NSA_QP), F32) for _ in range(groups)]
    for jp in range(n_sb):
        row = score_ref[jp:jp + 1, :]
        for v in range(groups):
            lo = SUBLANES * v
            if jp < lo:
                beats = row >= tiles[v]
            elif jp >= lo + SUBLANES - 1:
                beats = row > tiles[v]
            else:
                beats = (row > tiles[v]) | ((row == tiles[v]) & (sub > jp - lo))
            cnts[v] = cnts[v] + jnp.where(beats, 1.0, 0.0)
    cnt = jnp.concatenate(cnts, axis=0)
    return jnp.where(cnt < float(n_sel), 1.0, 0.0)


def _nsa_kernel(q_ref, kc_ref, vct_ref, ks_ref, vst_ref, kw_ref, vwt_ref, gate_ref,
                tc_ref, tn_ref, tw_ref, cov_ref, rep_ref, o_ref,
                sel_ref, selfar_ref, score_ref, *, n_sb):
    g = pl.program_id(1)
    p2 = pl.program_id(2)
    hq = NSA_HG * NSA_QP
    nch = kc_ref.shape[2]
    q = q_ref[0, 0, 0]

    start_c = pl.multiple_of(256 - (NSA_QP // CMP_STRIDE) * p2, SUBLANES)
    sc = _dot_nt(kc_ref[0, 0], q) + tc_ref[g, pl.ds(start_c, nch), :]
    _, pc, lc = _softmax_cols(sc)
    lane = lax.broadcasted_iota(jnp.int32, (1, hq), 1)
    tq = NSA_QP * p2 + lax.bitwise_and(lane, NSA_QP - 1)
    anyv = jnp.where(tq >= CMP_BLOCK - 1, 1.0, 0.0)
    pc = pc * (anyv / lc)
    o_cmp = _dot(vct_ref[0, 0], pc.astype(BF16))

    psum = pc[:, 0:NSA_QP]
    for h in range(1, NSA_HG):
        psum = psum + pc[:, NSA_QP * h:NSA_QP * (h + 1)]
    p_hi = psum.astype(BF16)
    p_lo = (psum - p_hi.astype(F32)).astype(BF16)
    imp = _dot(cov_ref[...], p_hi) + _dot(cov_ref[...], p_lo)
    jrow = lax.broadcasted_iota(jnp.int32, (n_sb, NSA_QP), 0)
    tok = lax.broadcasted_iota(jnp.int32, (n_sb, NSA_QP), 1)
    cur = 2 * p2 + lax.shift_right_logical(tok, 6)
    forced = (jrow == 0) | (jrow == cur) | (jrow == cur - 1)
    score_ref[...] = jnp.where(forced, FORCE_SCORE, jnp.where(jrow <= cur, imp, -1.0))
    sel = _rank_select(score_ref, n_sb, min(N_SELECT, n_sb)).astype(BF16)
    sel4 = _dot(sel, rep_ref[...])
    zeros8 = jnp.zeros((SEL_PAD, hq), F32)
    sel_ref[0:SEL_PAD, :] = zeros8
    sel_ref[SEL_PAD + n_sb:2 * SEL_PAD + n_sb, :] = zeros8
    sel_ref[SEL_PAD:SEL_PAD + n_sb, :] = sel4
    selfar_ref[0:SEL_PAD, :] = zeros8
    selfar_ref[SEL_PAD + n_sb:2 * SEL_PAD + n_sb, :] = zeros8
    jrow4 = lax.broadcasted_iota(jnp.int32, (n_sb, hq), 0)
    selfar_ref[SEL_PAD:SEL_PAD + n_sb, :] = jnp.where(jrow4 < 2 * p2 - 2, sel4, 0.0)

    near0 = pl.multiple_of(NSA_QP * p2, LANES)
    s = _dot_nt(ks_ref[0, 0, pl.ds(near0, NEAR_KEYS), :], q) + tn_ref[g]
    s = _mask_blocks(s, sel_ref, 2 * p2 - 2 + SEL_PAD, NEAR_KEYS // SLC_BLOCK)
    m_s, p_s, l_s = _softmax_cols(s)
    acc_s = _dot(vst_ref[0, 0, :, pl.ds(near0, NEAR_KEYS)], p_s.astype(BF16))

    def far_body(c, carry):
        m_old, l_old, acc_old = carry
        k0 = pl.multiple_of(FAR_KEYS * c + SLC_FRONT, LANES)
        sf = _dot_nt(ks_ref[0, 0, pl.ds(k0, FAR_KEYS), :], q)
        sf = _mask_blocks(sf, selfar_ref, (FAR_KEYS // SLC_BLOCK) * c + SEL_PAD, FAR_KEYS // SLC_BLOCK)
        m_new = jnp.maximum(m_old, jnp.max(sf, axis=0, keepdims=True))
        alpha = jnp.exp(m_old - m_new)
        pf = jnp.exp(sf - m_new)
        l_new = alpha * l_old + jnp.sum(pf, axis=0, keepdims=True)
        acc_new = alpha * acc_old + _dot(vst_ref[0, 0, :, pl.ds(k0, FAR_KEYS)], pf.astype(BF16))
        return m_new, l_new, acc_new

    _, l_s, acc_s = lax.fori_loop(0, lax.shift_right_logical(p2, 1), far_body, (m_s, l_s, acc_s))

    sw = _dot_nt(kw_ref[0, 0, pl.ds(near0, WIN_KEYS), :], q) + tw_ref[g]
    krow = lax.broadcasted_iota(jnp.int32, (WIN_KEYS, hq), 0)
    sw = jnp.where(krow + near0 >= WINDOW, sw, NEG_INF)
    _, p_w, l_w = _softmax_cols(sw)
    acc_w = _dot(vwt_ref[0, 0, :, pl.ds(near0, WIN_KEYS)], p_w.astype(BF16))

    gs = jax.nn.sigmoid(gate_ref[0, 0, 0])
    o_ref[0, 0, 0] = (gs[0:1] * o_cmp + (gs[1:2] / l_s) * acc_s + (gs[2:3] / l_w) * acc_w)


def _t5_bucket_np(d):
    exact = REL_BUCKETS // 2
    d = np.maximum(d, 0)
    ratio = np.log(np.maximum(d, 1).astype(np.float32) / np.float32(exact)) / np.float32(math.log(REL_MAX_DIST / exact))
    large = np.minimum(exact + (ratio * (REL_BUCKETS - exact)).astype(np.int32), REL_BUCKETS - 1)
    return np.where(d < exact, d, large).astype(np.int32)


def _bias_table(rel_bias, dist, valid, shift):
    tab = rel_bias[_t5_bucket_np(dist)]
    if shift:
        tab = tab - rel_bias[REL_BUCKETS - 1]
    tab = jnp.where(valid[..., None], tab, NEG_INF)
    r, qn = dist.shape
    tab = tab.reshape(r, qn, NSA_G, NSA_HG).transpose(2, 0, 3, 1)
    return tab.reshape(NSA_G, r, NSA_HG * qn).astype(F32)


def _nsa_tables(rel_bias):
    tv = np.arange(NSA_QP)[None, :]
    r = np.arange(CMP_TAB_ROWS)[:, None]
    d_c = tv - CMP_STRIDE * (r - 256) - (CMP_BLOCK - 1)
    tc = _bias_table(rel_bias, d_c, d_c >= 0, False)
    r = np.arange(NEAR_KEYS)[:, None]
    d_n = NEAR_KEYS // 2 + tv - r
    tn = _bias_table(rel_bias, d_n, d_n >= 0, True)
    r = np.arange(WIN_KEYS)[:, None]
    d_w = WINDOW + tv - r
    tw = _bias_table(rel_bias, d_w, (d_w >= 0) & (d_w < WINDOW), False)
    return tc, tn, tw


def _nsa_attention(zq, zkv, zg, k_cmp, v_cmp, tables, bsz, seq):
    g, hg, dh, qb = NSA_G, NSA_HG, NSA_DH, NSA_QP
    nqb = seq // qb
    n_sb = seq // SLC_BLOCK
    nch = k_cmp.shape[2]
    hq = hg * qb
    tc, tn, tw = tables
    q = (zq * dh ** -0.5).astype(BF16).reshape(bsz, nqb, qb, g, hg, dh)
    q = q.transpose(0, 3, 1, 4, 2, 5).reshape(bsz, g, nqb, hq, dh)
    kv = zkv.astype(BF16).reshape(bsz, seq, 6, g, dh).transpose(2, 0, 3, 1, 4)
    k_slc = jnp.pad(kv[2], ((0, 0), (0, 0), (SLC_FRONT, 0), (0, 0)))
    v_slc_t = jnp.pad(kv[3], ((0, 0), (0, 0), (SLC_FRONT, 0), (0, 0))).transpose(0, 1, 3, 2)
    k_win = jnp.pad(kv[4], ((0, 0), (0, 0), (WINDOW, 0), (0, 0)))
    v_win_t = jnp.pad(kv[5], ((0, 0), (0, 0), (WINDOW, 0), (0, 0))).transpose(0, 1, 3, 2)
    kc = k_cmp.astype(BF16)
    vct = v_cmp.astype(BF16).transpose(0, 1, 3, 2)
    gates = zg.reshape(bsz, nqb, qb, g, hg, 3).transpose(0, 3, 1, 5, 4, 2).reshape(bsz, g, nqb, 3, hq)
    gates = jnp.pad(gates, ((0, 0), (0, 0), (0, 0), (0, SUBLANES - 3), (0, 0)))
    c_start = CMP_STRIDE * np.arange(nch)
    s_start = SLC_BLOCK * np.arange(n_sb)
    cover_t = ((c_start[None, :] < s_start[:, None] + SLC_BLOCK)
               & (c_start[None, :] + CMP_BLOCK > s_start[:, None])
               & (np.arange(nch)[None, :] < (seq - CMP_BLOCK) // CMP_STRIDE + 1))
    cover_t = jnp.asarray(cover_t.astype(np.float32), BF16)
    rep = jnp.asarray(np.tile(np.eye(qb, dtype=np.float32), (1, hg)), BF16)
    sp = k_slc.shape[2]
    wp = k_win.shape[2]
    full = lambda shape: pl.BlockSpec(shape, lambda b, gg, i: (0,) * len(shape))
    per_bg = lambda shape: pl.BlockSpec(shape, lambda b, gg, i: (b, gg, 0, 0))
    out = pl.pallas_call(
        functools.partial(_nsa_kernel, n_sb=n_sb),
        out_shape=jax.ShapeDtypeStruct((bsz, g, nqb, dh, hq), F32),
        grid=(bsz, g, nqb),
        in_specs=[pl.BlockSpec((1, 1, 1, hq, dh), lambda b, gg, i: (b, gg, i, 0, 0)),
                  per_bg((1, 1, nch, dh)), per_bg((1, 1, dh, nch)),
                  per_bg((1, 1, sp, dh)), per_bg((1, 1, dh, sp)),
                  per_bg((1, 1, wp, dh)), per_bg((1, 1, dh, wp)),
                  pl.BlockSpec((1, 1, 1, SUBLANES, hq), lambda b, gg, i: (b, gg, i, 0, 0)),
                  full(tc.shape), full(tn.shape), full(tw.shape),
                  full(cover_t.shape), full(rep.shape)],
        out_specs=pl.BlockSpec((1, 1, 1, dh, hq), lambda b, gg, i: (b, gg, i, 0, 0)),
        scratch_shapes=[pltpu.VMEM((n_sb + 2 * SEL_PAD, hq), F32),
                        pltpu.VMEM((n_sb + 2 * SEL_PAD, hq), F32),
                        pltpu.VMEM((n_sb, qb), F32)],
        compiler_params=_cparams(("parallel", "parallel", "arbitrary")),
        name="nsa_attention",
    )(q, kc, vct, k_slc, v_slc_t, k_win, v_win_t, gates, tc, tn, tw, cover_t, rep)
    out = out.reshape(bsz, g, nqb, dh, hg, qb).transpose(0, 2, 5, 1, 4, 3)
    return out.reshape(bsz * seq, g * hg * dh).astype(BF16)


def _rope_table_kernel(pos_ref, inv_ref, sign_ref, o_ref):
    ang = pos_ref[...].astype(F32) * inv_ref[...]
    o_ref[...] = jnp.concatenate([jnp.cos(ang), jnp.sin(ang) * sign_ref[...]], axis=-1)


def _rope_table(positions, tm):
    t = positions.size
    half = QK_ROPE // 2
    inv = ROPE_THETA ** (-jnp.arange(half, dtype=F32) / half)
    inv2 = jnp.concatenate([inv, inv])[None]
    sign = jnp.asarray(np.concatenate([-np.ones(half), np.ones(half)]).astype(np.float32))[None]
    return pl.pallas_call(
        _rope_table_kernel,
        out_shape=jax.ShapeDtypeStruct((t, 2 * QK_ROPE), F32),
        grid=(t // tm,),
        in_specs=[pl.BlockSpec((tm, 1), lambda i: (i, 0)),
                  pl.BlockSpec((1, QK_ROPE), lambda i: (0, 0)),
                  pl.BlockSpec((1, QK_ROPE), lambda i: (0, 0))],
        out_specs=pl.BlockSpec((tm, 2 * QK_ROPE), lambda i: (i, 0)),
        compiler_params=_cparams(("parallel",)),
        name="rope_table",
    )(positions.reshape(t, 1), inv2, sign)


MLA_HW = 256


def _mla_proj_kernel(cq_ref, ckv_ref, kr_ref, rope_ref, nq_ref, nkv_ref, wq_ref, wkv_ref,
                     q_ref, k_ref, v_ref):
    scale = (QK_NOPE + QK_ROPE) ** -0.5
    rope = rope_ref[...]
    yq = _dot(_rms(cq_ref[...], nq_ref[...]).astype(BF16), wq_ref[...])
    ykv = _dot(_rms(ckv_ref[...], nkv_ref[...]).astype(BF16), wkv_ref[...])
    kp = kr_ref[...] * rope
    kp = kp + pltpu.roll(kp, QK_ROPE, 1)
    lane = lax.broadcasted_iota(jnp.int32, kp.shape, 1)
    kp = jnp.where(lane < QK_ROPE, kp, 0.0).astype(BF16)
    for h in range(MLA_HEADS):
        base = MLA_HW * h
        q_ref[:, base:base + QK_NOPE] = (yq[:, base:base + QK_NOPE] * scale).astype(BF16)
        qp = yq[:, base + QK_NOPE:base + MLA_HW] * rope
        qp = qp + pltpu.roll(qp, QK_ROPE, 1)
        q_ref[:, base + QK_NOPE:base + MLA_HW] = (qp * scale).astype(BF16)
        k_ref[:, base:base + QK_NOPE] = ykv[:, base:base + QK_NOPE].astype(BF16)
        k_ref[:, base + QK_NOPE:base + MLA_HW] = kp
        v_ref[:, V_DIM * h:V_DIM * (h + 1)] = ykv[:, base + QK_NOPE:base + MLA_HW].astype(BF16)


def _swap_halves(w):
    half = QK_ROPE // 2
    return jnp.concatenate([w[..., half:], w[..., :half]], axis=-1)


def _mla_proj(z, rope_tab, norm_q, norm_kv, w_uq, w_ukv, tm):
    t = z.shape[0]
    wq = w_uq.reshape(Q_RANK, MLA_HEADS, QK_NOPE + QK_ROPE)
    wq = jnp.concatenate([wq, _swap_halves(wq[..., QK_NOPE:])], axis=-1)
    wq = wq.reshape(Q_RANK, MLA_HEADS * MLA_HW).astype(BF16)
    wkv = w_ukv.astype(BF16)
    hw = MLA_HEADS * MLA_HW
    return pl.pallas_call(
        _mla_proj_kernel,
        out_shape=(jax.ShapeDtypeStruct((t, hw), BF16),
                   jax.ShapeDtypeStruct((t, hw), BF16),
                   jax.ShapeDtypeStruct((t, MLA_HEADS * V_DIM), BF16)),
        grid=(t // tm,),
        in_specs=[pl.BlockSpec((tm, Q_RANK), lambda i: (i, Z_CQ // Q_RANK)),
                  pl.BlockSpec((tm, KV_RANK), lambda i: (i, Z_CKV // KV_RANK)),
                  pl.BlockSpec((tm, 2 * QK_ROPE), lambda i: (i, Z_KR // (2 * QK_ROPE))),
                  pl.BlockSpec((tm, 2 * QK_ROPE), lambda i: (i, 0)),
                  pl.BlockSpec((1, Q_RANK), lambda i: (0, 0)),
                  pl.BlockSpec((1, KV_RANK), lambda i: (0, 0)),
                  pl.BlockSpec((Q_RANK, hw), lambda i: (0, 0)),
                  pl.BlockSpec((KV_RANK, hw), lambda i: (0, 0))],
        out_specs=(pl.BlockSpec((tm, hw), lambda i: (i, 0)),
                   pl.BlockSpec((tm, hw), lambda i: (i, 0)),
                   pl.BlockSpec((tm, MLA_HEADS * V_DIM), lambda i: (i, 0))),
        compiler_params=_cparams(("parallel",)),
        name="mla_proj",
    )(z, z, z, rope_tab, norm_q[None], norm_kv[None], wq, wkv)


def _mla_attn_kernel(q_ref, k_ref, vt_ref, o_ref, *, tq, tk):
    iq = pl.program_id(2)
    q = q_ref[0]
    cd = lax.div(iq * tq, tk)

    def scores(c):
        k0 = pl.multiple_of(c * tk, tk)
        return k0, _dot_nt(k_ref[0, pl.ds(k0, tk), :], q)

    k0, s = scores(cd)
    kpos = k0 + lax.broadcasted_iota(jnp.int32, (tk, tq), 0)
    qpos = iq * tq + lax.broadcasted_iota(jnp.int32, (tk, tq), 1)
    s = jnp.where(kpos <= qpos, s, NEG_INF)
    m0, p0, l0 = _softmax_cols(s)
    acc0 = _dot(vt_ref[0, 0, :, pl.ds(k0, tk)], p0.astype(BF16))

    def body(c, carry):
        m_old, l_old, acc_old = carry
        k0, s = scores(c)
        m_new = jnp.maximum(m_old, jnp.max(s, axis=0, keepdims=True))
        alpha = jnp.exp(m_old - m_new)
        p = jnp.exp(s - m_new)
        l_new = alpha * l_old + jnp.sum(p, axis=0, keepdims=True)
        acc_new = alpha * acc_old + _dot(vt_ref[0, 0, :, pl.ds(k0, tk)], p.astype(BF16))
        return m_new, l_new, acc_new

    _, l, acc = lax.fori_loop(0, cd, body, (m0, l0, acc0))
    o_ref[0, 0] = acc / l


def _mla_attention(qf, kf, v, bsz, seq, tq, tk):
    h = MLA_HEADS
    q3 = qf.reshape(bsz, seq, h * MLA_HW)
    k3 = kf.reshape(bsz, seq, h * MLA_HW)
    vt = v.reshape(bsz, seq, h, V_DIM).transpose(0, 2, 3, 1)
    out = pl.pallas_call(
        functools.partial(_mla_attn_kernel, tq=tq, tk=tk),
        out_shape=jax.ShapeDtypeStruct((bsz, h, V_DIM, seq), F32),
        grid=(bsz, h, seq // tq),
        in_specs=[pl.BlockSpec((1, tq, MLA_HW), lambda b, hh, i: (b, i, hh)),
                  pl.BlockSpec((1, seq, MLA_HW), lambda b, hh, i: (b, 0, hh)),
                  pl.BlockSpec((1, 1, V_DIM, seq), lambda b, hh, i: (b, hh, 0, 0))],
        out_specs=pl.BlockSpec((1, 1, V_DIM, tq), lambda b, hh, i: (b, hh, 0, i)),
        compiler_params=_cparams(("parallel", "parallel", "arbitrary")),
        name="mla_attention",
    )(q3, k3, vt)
    return out.transpose(0, 3, 1, 2).reshape(bsz * seq, h * V_DIM).astype(BF16)


def _merge_kernel(ya_ref, yb_ref, yc_ref, ga_ref, gb_ref, gc_ref, x_ref,
                  wa_ref, wb_ref, wc_ref, wo_ref, o_ref):
    y = (jax.nn.sigmoid(ga_ref[...]) * _dot(ya_ref[...], wa_ref[...])
         + jax.nn.sigmoid(gb_ref[...]) * _dot(yb_ref[...], wb_ref[...])
         + jax.nn.sigmoid(gc_ref[...]) * _dot(yc_ref[...], wc_ref[...]))
    o_ref[...] = x_ref[...] + _dot(y.astype(BF16), wo_ref[...])


def _merge(ya, yb, yc, z, x, wa, wb, wc, wo, tm):
    t, d = x.shape
    act = pl.BlockSpec((tm, ya.shape[1]), lambda i: (i, 0))
    gate = lambda k: pl.BlockSpec((tm, d), lambda i: (i, Z_GM // d + k))
    wbr = pl.BlockSpec((ya.shape[1], d), lambda i: (0, 0))
    return pl.pallas_call(
        _merge_kernel,
        out_shape=jax.ShapeDtypeStruct((t, d), F32),
        grid=(t // tm,),
        in_specs=[act, act, act, gate(0), gate(1), gate(2),
                  pl.BlockSpec((tm, d), lambda i: (i, 0)),
                  wbr, wbr, wbr, pl.BlockSpec((d, d), lambda i: (0, 0))],
        out_specs=pl.BlockSpec((tm, d), lambda i: (i, 0)),
        compiler_params=_cparams(("parallel",)),
        name="merge",
    )(ya, yb, yc, z, z, z, x, wa.astype(BF16), wb.astype(BF16), wc.astype(BF16), wo.astype(BF16))


def _xattn_kernel(x_ref, g_ref, wq_ref, kv_ref, wo_ref, o_ref):
    x = x_ref[...]
    h = _rms(x, g_ref[...]).astype(BF16)
    q = _dot(h, wq_ref[...]) * XATTN_DH ** -0.5
    hd = XATTN_HEADS * XATTN_DH
    outs = []
    for hh in range(XATTN_HEADS):
        qh = q[:, XATTN_DH * hh:XATTN_DH * (hh + 1)].astype(BF16)
        kh = kv_ref[0, :, XATTN_DH * hh:XATTN_DH * (hh + 1)]
        vh = kv_ref[0, :, hd + XATTN_DH * hh:hd + XATTN_DH * (hh + 1)]
        s = _dot_nt(qh, kh)
        m = jnp.max(s, axis=-1, keepdims=True)
        p = jnp.exp(s - m)
        p = p / jnp.sum(p, axis=-1, keepdims=True)
        outs.append(_dot(p.astype(BF16), vh))
    o = jnp.concatenate(outs, axis=-1).astype(BF16)
    o_ref[...] = x + _dot(o, wo_ref[...])


def _xattn(x, g, wq, kv, wo, bsz, seq, tm):
    t, d = x.shape
    nst = seq // tm
    m_len = kv.shape[1]
    hd = XATTN_HEADS * XATTN_DH
    return pl.pallas_call(
        _xattn_kernel,
        out_shape=jax.ShapeDtypeStruct((t, d), F32),
        grid=(bsz, nst),
        in_specs=[pl.BlockSpec((tm, d), lambda b, s: (b * nst + s, 0)),
                  pl.BlockSpec((1, d), lambda b, s: (0, 0)),
                  pl.BlockSpec((d, hd), lambda b, s: (0, 0)),
                  pl.BlockSpec((1, m_len, 2 * hd), lambda b, s: (b, 0, 0)),
                  pl.BlockSpec((hd, d), lambda b, s: (0, 0))],
        out_specs=pl.BlockSpec((tm, d), lambda b, s: (b * nst + s, 0)),
        compiler_params=_cparams(("parallel", "parallel")),
        name="xattn",
    )(x, g[None], wq.astype(BF16), kv, wo.astype(BF16))


def _ffn_kernel(x_ref, g_ref, wg_ref, wu_ref, wd_ref, gf_ref, o_ref, h_ref, acc_ref, *, final):
    c = pl.program_id(1)

    @pl.when(c == 0)
    def _():
        h_ref[...] = _rms(x_ref[...], g_ref[...]).astype(BF16)
        acc_ref[...] = x_ref[...]

    h = h_ref[...]
    gate = _dot(h, wg_ref[...])
    up = _dot(h, wu_ref[...])
    act = (gate * jax.nn.sigmoid(gate) * up).astype(BF16)
    acc_ref[...] += _dot(act, wd_ref[...])

    @pl.when(c == pl.num_programs(1) - 1)
    def _():
        y = acc_ref[...]
        o_ref[...] = _rms(y, gf_ref[...]) if final else y


def _ffn(x, g, w_gate_up, w_down, g_final, final, tm, tc):
    t, d = x.shape
    nc = FFN_HIDDEN // tc
    wgu = w_gate_up.astype(BF16)
    return pl.pallas_call(
        functools.partial(_ffn_kernel, final=final),
        out_shape=jax.ShapeDtypeStruct((t, d), F32),
        grid=(t // tm, nc),
        in_specs=[pl.BlockSpec((tm, d), lambda i, c: (i, 0)),
                  pl.BlockSpec((1, d), lambda i, c: (0, 0)),
                  pl.BlockSpec((d, tc), lambda i, c: (0, c)),
                  pl.BlockSpec((d, tc), lambda i, c: (0, nc + c)),
                  pl.BlockSpec((tc, d), lambda i, c: (c, 0)),
                  pl.BlockSpec((1, d), lambda i, c: (0, 0))],
        out_specs=pl.BlockSpec((tm, d), lambda i, c: (i, 0)),
        scratch_shapes=[pltpu.VMEM((tm, d), BF16), pltpu.VMEM((tm, d), F32)],
        compiler_params=_cparams(("parallel", "arbitrary")),
        name="ffn",
    )(x, g[None], wgu, wgu, w_down.astype(BF16), g_final[None])


def _permute_w_in(w):
    o_glu, o_q, o_kv, o_gn, o_cq, o_ckv, o_kr, o_gm = 0, 1024, 1536, 2304, 2328, 2712, 2968, 3032
    k_rope = w[:, o_kr:o_kr + QK_ROPE]
    pad = jnp.zeros((w.shape[0], Z_COLS - Z_GN - 3 * NSA_HEADS), w.dtype)
    return jnp.concatenate([
        w[:, o_gm:o_gm + 3 * D_MODEL],
        w[:, o_glu:o_glu + 2 * CONV_CH],
        w[:, o_q:o_q + 512],
        w[:, o_ckv:o_ckv + KV_RANK],
        k_rope, _swap_halves(k_rope),
        w[:, o_cq:o_cq + Q_RANK],
        w[:, o_kv:o_kv + 768],
        w[:, o_gn:o_gn + 3 * NSA_HEADS], pad], axis=1).astype(BF16)


def kernel(x, mem, positions, rel_bias, norm_mix, norm_xattn, norm_mem, norm_ffn, norm_final, w_in, conv_w, conv_b, conv_ln_g, conv_ln_b, w_branch_conv, cmp_pos_k, cmp_w1_k, cmp_b1_k, cmp_w2_k, cmp_pos_v, cmp_w1_v, cmp_b1_v, cmp_w2_v, w_branch_nsa, mla_norm_q, mla_norm_kv, w_uq, w_ukv, w_branch_mla, w_out, w_xq, w_xkv, w_xo, w_gate_up, w_down):
    bsz, seq, d = x.shape
    depth = w_in.shape[0]
    t = bsz * seq
    m_len = mem.shape[1]
    tm = 512
    xt = x.reshape(t, d)
    memt = mem.reshape(bsz * m_len, d)
    rope_tab = _rope_table(positions, tm)
    tables = _nsa_tables(rel_bias)
    nch = seq // CMP_STRIDE
    for l in range(depth):
        z = _norm_matmul(xt, norm_mix[l][None], _permute_w_in(w_in[l]), tm, 896)
        ya = _conv_module(z, conv_w[l], conv_b[l], conv_ln_g[l], conv_ln_b[l], bsz, seq, tm)
        zkv = z[:, Z_KV:Z_KV + 768]
        chunks = zkv.reshape(bsz, nch, CMP_STRIDE, 6, NSA_G, NSA_DH)[:, :, :, 0:2]
        chunks = chunks.transpose(3, 0, 4, 1, 2, 5).reshape(2, bsz, NSA_G, nch, CMP_STRIDE * NSA_DH)
        cmp_kv = _compress(chunks,
                           jnp.stack([cmp_pos_k[l], cmp_pos_v[l]]), jnp.stack([cmp_w1_k[l], cmp_w1_v[l]]),
                           jnp.stack([cmp_b1_k[l], cmp_b1_v[l]]), jnp.stack([cmp_w2_k[l], cmp_w2_v[l]]))
        yb = _nsa_attention(z[:, Z_Q:Z_Q + 512], zkv, z[:, Z_GN:Z_GN + 3 * NSA_HEADS],
                            cmp_kv[0], cmp_kv[1], tables, bsz, seq)
        qf, kf, v = _mla_proj(z, rope_tab, mla_norm_q[l], mla_norm_kv[l], w_uq[l], w_ukv[l], tm)
        yc = _mla_attention(qf, kf, v, bsz, seq, 256, 512)
        xt = _merge(ya, yb, yc, z, xt, w_branch_conv[l], w_branch_nsa[l], w_branch_mla[l], w_out[l], tm)
        mem_kv = _norm_matmul(memt, norm_mem[l][None], w_xkv[l].astype(BF16), 256, 1024)
        mem_kv = mem_kv.astype(BF16).reshape(bsz, m_len, 2 * XATTN_HEADS * XATTN_DH)
        xt = _xattn(xt, norm_xattn[l], w_xq[l], mem_kv, w_xo[l], bsz, seq, tm)
        xt = _ffn(xt, norm_ffn[l], w_gate_up[l], w_down[l], norm_final, l == depth - 1, 1024, 256)
    return xt.reshape(bsz, seq, d)
```

```python
import functools
import math

import numpy as np
import jax
import jax.numpy as jnp
from jax import lax
from jax.experimental import pallas as pl
from jax.experimental.pallas import tpu as pltpu

F32 = jnp.float32
BF16 = jnp.bfloat16

EPS = 1e-6
NEG_INF = -1e30
FORCE_SCORE = 1e4

D_MODEL = 1024
CONV_CH = 512
CONV_WIDTH = 31
NSA_HEADS = 8
NSA_G = 2
NSA_HG = NSA_HEADS // NSA_G
NSA_DH = 64
CMP_BLOCK = 32
CMP_STRIDE = 16
CMP_HIDDEN = 256
SLC_BLOCK = 64
N_SELECT = 16
WINDOW = 512
NSA_QB = 64
MLA_HEADS = 4
Q_RANK = 384
KV_RANK = 256
QK_NOPE = 128
QK_ROPE = 64
V_DIM = 128
ROPE_THETA = 10000.0
REL_BUCKETS = 32
REL_MAX_DIST = 128
XATTN_HEADS = 4
XATTN_DH = 128
FFN_HIDDEN = 2816

LANES = 128
SUBLANES = 8

O_GLU, O_Q, O_KV, O_GN, O_CQ, O_CKV, O_KR, O_GM = 0, 1024, 1536, 2304, 2328, 2712, 2968, 3032
GD = NSA_G * NSA_DH

Z_GM = 0
Z_UA = 3072
Z_UB = 3584
Z_Q = 4096
Z_CKV = 4608
Z_KR = 4864
Z_CQ = 4992
Z_CMP = 5376
Z_GN = 5632
Z_COLS = 5760

VMEM_LIMIT = 56 * 1024 * 1024

TOK_TILE = 512
IN_PROJ_TN = 1152
FFN_TM = 1024
FFN_TC = 256
MLA_TQ = 256
MLA_TK = 512


def _cparams(sem):
    return pltpu.CompilerParams(dimension_semantics=sem, vmem_limit_bytes=VMEM_LIMIT)


def _rms(x, g):
    return x * lax.rsqrt(jnp.mean(x * x, axis=-1, keepdims=True) + EPS) * g


def _dot(a, b):
    return jnp.dot(a, b, preferred_element_type=F32)


def _dot_nt(a, b):
    return lax.dot_general(a, b, (((1,), (1,)), ((), ())), preferred_element_type=F32)


def _norm_matmul_kernel(x_ref, g_ref, w_ref, o_ref, h_ref):
    @pl.when(pl.program_id(1) == 0)
    def _():
        h_ref[...] = _rms(x_ref[...], g_ref[...]).astype(BF16)

    o_ref[...] = _dot(h_ref[...], w_ref[...]).astype(o_ref.dtype)


def _norm_matmul(x, g, w, tm, tn, out_dtype):
    m, k = x.shape
    n = w.shape[1]
    return pl.pallas_call(
        _norm_matmul_kernel,
        out_shape=jax.ShapeDtypeStruct((m, n), out_dtype),
        grid=(m // tm, n // tn),
        in_specs=[pl.BlockSpec((tm, k), lambda i, j: (i, 0)),
                  pl.BlockSpec((1, k), lambda i, j: (0, 0)),
                  pl.BlockSpec((k, tn), lambda i, j: (0, j))],
        out_specs=pl.BlockSpec((tm, tn), lambda i, j: (i, j)),
        scratch_shapes=[pltpu.VMEM((tm, k), BF16)],
        compiler_params=_cparams(("parallel", "arbitrary")),
        name="norm_matmul",
    )(x, g, w)


def _kv_proj_kernel(x_ref, g_ref, wk_ref, wvt_ref, k_ref, vt_ref):
    @pl.when(pl.program_id(1) == 0)
    def _():
        k_ref[...] = jnp.zeros(k_ref.shape, BF16)
        vt_ref[...] = jnp.zeros(vt_ref.shape, BF16)

    @pl.when(pl.program_id(1) > 0)
    def _():
        h = _rms(x_ref[...], g_ref[...]).astype(BF16)
        k_ref[0] = _dot(h, wk_ref[...]).astype(BF16)
        vt_ref[0] = _dot_nt(wvt_ref[...], h).astype(BF16)


def _kv_proj(x, g, wk, wvt, bsz, seq):
    tm = WINDOW
    nst = seq // tm
    d = x.shape[1]
    nk = wk.shape[1]
    return pl.pallas_call(
        _kv_proj_kernel,
        out_shape=(jax.ShapeDtypeStruct((bsz, seq + tm, nk), BF16),
                   jax.ShapeDtypeStruct((bsz, nk, seq + tm), BF16)),
        grid=(bsz, nst + 1),
        in_specs=[pl.BlockSpec((tm, d), lambda b, s: (b * nst + jnp.maximum(s - 1, 0), 0)),
                  pl.BlockSpec((1, d), lambda b, s: (0, 0)),
                  pl.BlockSpec((d, nk), lambda b, s: (0, 0)),
                  pl.BlockSpec((nk, d), lambda b, s: (0, 0))],
        out_specs=(pl.BlockSpec((1, tm, nk), lambda b, s: (b, s, 0)),
                   pl.BlockSpec((1, nk, tm), lambda b, s: (b, 0, s))),
        compiler_params=_cparams(("parallel", "arbitrary")),
        name="nsa_kv_proj",
    )(x, g, wk, wvt)


CONV_HALO = 32


def _conv_kernel(a_ref, b_ref, w_ref, cb_ref, lg_ref, lb_ref, o_ref, buf_ref, *, ts):
    @pl.when(pl.program_id(1) == 0)
    def _():
        buf_ref[0:CONV_HALO, :] = jnp.zeros((CONV_HALO, CONV_CH), F32)

    u = a_ref[...] * jax.nn.sigmoid(b_ref[...])
    buf_ref[CONV_HALO:CONV_HALO + ts, :] = u
    off = CONV_HALO - (CONV_WIDTH - 1)
    acc = jnp.zeros((ts, CONV_CH), F32) + cb_ref[...]
    for k in range(CONV_WIDTH):
        acc = acc + buf_ref[off + k:off + k + ts, :] * w_ref[k:k + 1, :]
    buf_ref[0:CONV_HALO, :] = buf_ref[ts:ts + CONV_HALO, :]
    mu = jnp.mean(acc, axis=-1, keepdims=True)
    xc = acc - mu
    var = jnp.mean(xc * xc, axis=-1, keepdims=True)
    y = xc * lax.rsqrt(var + EPS) * lg_ref[...] + lb_ref[...]
    o_ref[...] = (y * jax.nn.sigmoid(y)).astype(BF16)


def _conv_module(z, conv_w, conv_b, ln_g, ln_b, bsz, seq):
    ts = TOK_TILE
    nst = seq // ts
    wpad = jnp.zeros((32, CONV_CH), F32).at[:CONV_WIDTH].set(conv_w)
    return pl.pallas_call(
        functools.partial(_conv_kernel, ts=ts),
        out_shape=jax.ShapeDtypeStruct((bsz * seq, CONV_CH), BF16),
        grid=(bsz, nst),
        in_specs=[pl.BlockSpec((ts, CONV_CH), lambda b, s: (b * nst + s, Z_UA // CONV_CH)),
                  pl.BlockSpec((ts, CONV_CH), lambda b, s: (b * nst + s, Z_UB // CONV_CH)),
                  pl.BlockSpec((32, CONV_CH), lambda b, s: (0, 0)),
                  pl.BlockSpec((1, CONV_CH), lambda b, s: (0, 0)),
                  pl.BlockSpec((1, CONV_CH), lambda b, s: (0, 0)),
                  pl.BlockSpec((1, CONV_CH), lambda b, s: (0, 0))],
        out_specs=pl.BlockSpec((ts, CONV_CH), lambda b, s: (b * nst + s, 0)),
        scratch_shapes=[pltpu.VMEM((ts + CONV_HALO, CONV_CH), F32)],
        compiler_params=_cparams(("arbitrary", "arbitrary")),
        name="conv_module",
    )(z, z, wpad, conv_b[None], ln_g[None], ln_b[None])


def _compress_kernel(xk_ref, xv_ref, pos_ref, w1_ref, b1_ref, w2_ref, kc_ref, vct_ref, *, nch):
    for kind, x_ref in enumerate((xk_ref, xv_ref)):
        a = jnp.zeros((nch, NSA_G * CMP_HIDDEN), F32)
        b = jnp.zeros((nch, NSA_G * CMP_HIDDEN), F32)
        for l in range(CMP_STRIDE):
            xs = x_ref[pl.ds(l, nch, stride=CMP_STRIDE), :]
            a = a + _dot((xs + pos_ref[kind, l:l + 1, :]).astype(BF16), w1_ref[kind, l])
            b = b + _dot((xs + pos_ref[kind, CMP_STRIDE + l:CMP_STRIDE + l + 1, :]).astype(BF16),
                         w1_ref[kind, CMP_STRIDE + l])
        pre = a + pltpu.roll(b, nch - 1, 0) + b1_ref[kind]
        out = _dot(jax.nn.gelu(pre).astype(BF16), w2_ref[kind])
        if kind == 0:
            kc_ref[0] = out.astype(BF16)
        else:
            vct_ref[0] = out.T.astype(BF16)


def _blockdiag2(w):
    z = jnp.zeros_like(w)
    return jnp.concatenate([jnp.concatenate([w, z], axis=-1), jnp.concatenate([z, w], axis=-1)], axis=-2)


def _compress(z, pos, w1, b1, w2, bsz, seq):
    nch = seq // CMP_STRIDE
    pos2 = jnp.concatenate([pos, pos], axis=-1)
    w1e = _blockdiag2(w1.reshape(2, CMP_BLOCK, NSA_DH, CMP_HIDDEN)).astype(BF16)
    b1e = jnp.concatenate([b1, b1], axis=-1)[:, None]
    w2e = _blockdiag2(w2).astype(BF16)
    full = lambda a: pl.BlockSpec(a.shape, lambda b: (0,) * a.ndim)
    return pl.pallas_call(
        functools.partial(_compress_kernel, nch=nch),
        out_shape=(jax.ShapeDtypeStruct((bsz, nch, GD), BF16),
                   jax.ShapeDtypeStruct((bsz, GD, nch), BF16)),
        grid=(bsz,),
        in_specs=[pl.BlockSpec((seq, GD), lambda b: (b, Z_CMP // GD)),
                  pl.BlockSpec((seq, GD), lambda b: (b, Z_CMP // GD + 1)),
                  full(pos2), full(w1e), full(b1e), full(w2e)],
        out_specs=(pl.BlockSpec((1, nch, GD), lambda b: (b, 0, 0)),
                   pl.BlockSpec((1, GD, nch), lambda b: (b, 0, 0))),
        compiler_params=_cparams(("parallel",)),
        name="nsa_compress",
    )(z, z, pos2, w1e, b1e, w2e)


NSA_QP = 2 * NSA_QB
NEAR_KEYS = 256
WIN_KEYS = 640
CMP_TAB_ROWS = 512
CMP_TAB_ZERO = 256


def _t5_bucket_np(d):
    exact = REL_BUCKETS // 2
    d = np.maximum(d, 0)
    ratio = np.log(np.maximum(d, 1).astype(np.float32) / np.float32(exact)) / np.float32(math.log(REL_MAX_DIST / exact))
    large = np.minimum(exact + (ratio * (REL_BUCKETS - exact)).astype(np.int32), REL_BUCKETS - 1)
    return np.where(d < exact, d, large).astype(np.int32)


def _bucket_thresholds():
    exact = REL_BUCKETS // 2
    bk = _t5_bucket_np(np.arange(4 * REL_MAX_DIST))
    assert np.all(np.diff(bk) >= 0) and bk[-1] == REL_BUCKETS - 1
    return [int(np.argmax(bk >= k)) for k in range(exact + 1, REL_BUCKETS)]


def _bias_rows(rel_ref, dist, valid, shift):
    exact = REL_BUCKETS // 2
    bucket = jnp.full(dist.shape, exact, jnp.int32)
    for thr in _bucket_thresholds():
        bucket = bucket + jnp.where(dist >= thr, 1, 0)
    bucket = jnp.where(dist < exact, dist, bucket)
    val = jnp.zeros(dist.shape, F32)
    for bkt in range(REL_BUCKETS):
        val = jnp.where(bucket == bkt, rel_ref[0, bkt:bkt + 1, :], val)
    if shift:
        val = val - rel_ref[0, REL_BUCKETS - 1:REL_BUCKETS, :]
    return jnp.where(valid, val, NEG_INF)


def _nsa_bias_kernel(rel_ref, tc_ref, tn_ref, tw_ref):
    hq = NSA_HG * NSA_QP
    rows = 128

    def dist_of(nrows, r0, fn):
        r = r0 + lax.broadcasted_iota(jnp.int32, (nrows, hq), 0)
        t = lax.bitwise_and(lax.broadcasted_iota(jnp.int32, (nrows, hq), 1), NSA_QP - 1)
        return fn(r, t)

    for r0 in range(0, CMP_TAB_ROWS, rows):
        d = dist_of(rows, r0, lambda r, t: t - CMP_STRIDE * (r - CMP_TAB_ZERO) - (CMP_BLOCK - 1))
        tc_ref[0, r0:r0 + rows, :] = _bias_rows(rel_ref, d, d >= 0, False)
    for r0 in range(0, NEAR_KEYS, rows):
        d = dist_of(rows, r0, lambda r, t: NEAR_KEYS // 2 + t - r)
        tn_ref[0, r0:r0 + rows, :] = _bias_rows(rel_ref, d, d >= 0, True)
    for r0 in range(0, WIN_KEYS, rows):
        d = dist_of(rows, r0, lambda r, t: WINDOW + t - r)
        tw_ref[0, r0:r0 + rows, :] = _bias_rows(rel_ref, d, (d >= 0) & (d < WINDOW), False)


def _nsa_tables(rel_bias):
    hq = NSA_HG * NSA_QP
    rel4 = jnp.repeat(rel_bias.reshape(REL_BUCKETS, NSA_G, NSA_HG).transpose(1, 0, 2), NSA_QP, axis=-1)
    spec = lambda r: pl.BlockSpec((1, r, hq), lambda g: (g, 0, 0))
    return pl.pallas_call(
        _nsa_bias_kernel,
        out_shape=(jax.ShapeDtypeStruct((NSA_G, CMP_TAB_ROWS, hq), F32),
                   jax.ShapeDtypeStruct((NSA_G, NEAR_KEYS, hq), F32),
                   jax.ShapeDtypeStruct((NSA_G, WIN_KEYS, hq), F32)),
        grid=(NSA_G,),
        in_specs=[spec(REL_BUCKETS)],
        out_specs=(spec(CMP_TAB_ROWS), spec(NEAR_KEYS), spec(WIN_KEYS)),
        compiler_params=_cparams(("parallel",)),
        name="nsa_bias_tables",
    )(rel4)


SEL_PAD = 8
FAR_KEYS = 256
KV_FRONT = WINDOW


def _softmax_cols(s):
    m = jnp.max(s, axis=0, keepdims=True)
    p = jnp.exp(s - m)
    return m, p, jnp.sum(p, axis=0, keepdims=True)


def _mask_blocks(s, mask_ref, row0, nblk):
    parts = []
    for jj in range(nblk):
        row = mask_ref[pl.ds(row0 + jj, 1), :]
        parts.append(jnp.where(row > 0.0, s[SLC_BLOCK * jj:SLC_BLOCK * (jj + 1)], NEG_INF))
    return jnp.concatenate(parts, axis=0)


def _rank_select(score_ref, n_sb, n_sel):
    groups = n_sb // SUBLANES
    sub = lax.broadcasted_iota(jnp.int32, (SUBLANES, NSA_QP), 0)
    tiles = [score_ref[SUBLANES * v:SUBLANES * (v + 1), :] for v in range(groups)]
    cnts = [jnp.zeros((SUBLANES, NSA_QP), F32) for _ in range(groups)]
    for jp in range(n_sb):
        row = score_ref[jp:jp + 1, :]
        for v in range(groups):
            lo = SUBLANES * v
            if jp < lo:
                beats = row >= tiles[v]
            elif jp >= lo + SUBLANES - 1:
                beats = row > tiles[v]
            else:
                beats = (row > tiles[v]) | ((row == tiles[v]) & (sub > jp - lo))
            cnts[v] = cnts[v] + jnp.where(beats, 1.0, 0.0)
    cnt = jnp.concatenate(cnts, axis=0)
    return jnp.where(cnt < float(n_sel), 1.0, 0.0)


def _nsa_kernel(q_ref, gate_ref, kc_ref, vct_ref, ks_ref, kw_ref, vst_ref, vwt_ref,
                tc_ref, tn_ref, tw_ref, cov_ref, rep_ref, o_ref,
                sel_ref, selfar_ref, score_ref, *, n_sb):
    g = pl.program_id(1)
    p2 = pl.program_id(2)
    hq = NSA_HG * NSA_QP
    nch = kc_ref.shape[1]

    def group_rows(x):
        return jnp.where(g == 0, x[0:NSA_DH], x[NSA_DH:GD])

    qf = q_ref[...] * NSA_DH ** -0.5
    q64 = jnp.concatenate([qf[:, NSA_DH * h:NSA_DH * (h + 1)] for h in range(NSA_HG)], axis=0).astype(BF16)
    z64 = jnp.zeros_like(q64)
    q = jnp.where(g == 0, jnp.concatenate([q64, z64], axis=1), jnp.concatenate([z64, q64], axis=1))

    start_c = pl.multiple_of(CMP_TAB_ZERO - (NSA_QP // CMP_STRIDE) * p2, SUBLANES)
    sc = _dot_nt(kc_ref[0], q) + tc_ref[0, pl.ds(start_c, nch), :]
    _, pc, lc = _softmax_cols(sc)
    lane = lax.broadcasted_iota(jnp.int32, (1, hq), 1)
    tq = NSA_QP * p2 + lax.bitwise_and(lane, NSA_QP - 1)
    anyv = jnp.where(tq >= CMP_BLOCK - 1, 1.0, 0.0)
    pc = pc * (anyv / lc)
    o_cmp = group_rows(_dot(vct_ref[0], pc.astype(BF16)))

    psum = pc[:, 0:NSA_QP]
    for h in range(1, NSA_HG):
        psum = psum + pc[:, NSA_QP * h:NSA_QP * (h + 1)]
    p_hi = psum.astype(BF16)
    p_lo = (psum - p_hi.astype(F32)).astype(BF16)
    imp = _dot(cov_ref[...], p_hi) + _dot(cov_ref[...], p_lo)
    jrow = lax.broadcasted_iota(jnp.int32, (n_sb, NSA_QP), 0)
    tok = lax.broadcasted_iota(jnp.int32, (n_sb, NSA_QP), 1)
    cur = 2 * p2 + lax.shift_right_logical(tok, 6)
    forced = (jrow == 0) | (jrow == cur) | (jrow == cur - 1)
    score_ref[...] = jnp.where(forced, FORCE_SCORE, jnp.where(jrow <= cur, imp, -1.0))
    sel = _rank_select(score_ref, n_sb, min(N_SELECT, n_sb)).astype(BF16)
    sel4 = _dot(sel, rep_ref[...])
    zeros8 = jnp.zeros((SEL_PAD, hq), F32)
    sel_ref[0:SEL_PAD, :] = zeros8
    sel_ref[SEL_PAD + n_sb:2 * SEL_PAD + n_sb, :] = zeros8
    sel_ref[SEL_PAD:SEL_PAD + n_sb, :] = sel4
    selfar_ref[0:SEL_PAD, :] = zeros8
    selfar_ref[SEL_PAD + n_sb:2 * SEL_PAD + n_sb, :] = zeros8
    jrow4 = lax.broadcasted_iota(jnp.int32, (n_sb, hq), 0)
    selfar_ref[SEL_PAD:SEL_PAD + n_sb, :] = jnp.where(jrow4 < 2 * p2 - 2, sel4, 0.0)

    win0 = pl.multiple_of(NSA_QP * p2, LANES)
    near0 = pl.multiple_of(win0 + KV_FRONT - NEAR_KEYS // 2, LANES)
    s = _dot_nt(ks_ref[0, pl.ds(near0, NEAR_KEYS), :], q) + tn_ref[0]
    s = _mask_blocks(s, sel_ref, 2 * p2 - 2 + SEL_PAD, NEAR_KEYS // SLC_BLOCK)
    m_s, p_s, l_s = _softmax_cols(s)
    acc_s = _dot(vst_ref[0, :, pl.ds(near0, NEAR_KEYS)], p_s.astype(BF16))

    def far_body(c, carry):
        m_old, l_old, acc_old = carry
        k0 = pl.multiple_of(FAR_KEYS * c + KV_FRONT, LANES)
        sf = _dot_nt(ks_ref[0, pl.ds(k0, FAR_KEYS), :], q)
        sf = _mask_blocks(sf, selfar_ref, (FAR_KEYS // SLC_BLOCK) * c + SEL_PAD, FAR_KEYS // SLC_BLOCK)
        m_new = jnp.maximum(m_old, jnp.max(sf, axis=0, keepdims=True))
        alpha = jnp.exp(m_old - m_new)
        pf = jnp.exp(sf - m_new)
        l_new = alpha * l_old + jnp.sum(pf, axis=0, keepdims=True)
        acc_new = alpha * acc_old + _dot(vst_ref[0, :, pl.ds(k0, FAR_KEYS)], pf.astype(BF16))
        return m_new, l_new, acc_new

    _, l_s, acc_s = lax.fori_loop(0, lax.shift_right_logical(p2, 1), far_body, (m_s, l_s, acc_s))

    sw = _dot_nt(kw_ref[0, pl.ds(win0, WIN_KEYS), :], q) + tw_ref[0]
    krow = lax.broadcasted_iota(jnp.int32, (WIN_KEYS, hq), 0)
    sw = jnp.where(krow + win0 >= KV_FRONT, sw, NEG_INF)
    _, p_w, l_w = _softmax_cols(sw)
    acc_w = _dot(vwt_ref[0, :, pl.ds(win0, WIN_KEYS)], p_w.astype(BF16))

    gt = gate_ref[...].T
    gsel = jax.nn.sigmoid(jnp.where(g == 0, gt[0:3 * NSA_HG], gt[3 * NSA_HG:6 * NSA_HG]))
    gate = lambda r: jnp.concatenate([gsel[3 * h + r:3 * h + r + 1] for h in range(NSA_HG)], axis=1)
    out_t = (gate(0) * o_cmp + (gate(1) / l_s) * group_rows(acc_s) + (gate(2) / l_w) * group_rows(acc_w))
    out = out_t.T
    o_ref[...] = jnp.concatenate([out[NSA_QP * h:NSA_QP * (h + 1)] for h in range(NSA_HG)], axis=1).astype(BF16)


def _nsa_attention(z, kc, vct, kk, vvt, tables, bsz, seq):
    g, hg, dh, qp = NSA_G, NSA_HG, NSA_DH, NSA_QP
    nstep = seq // qp
    n_sb = seq // SLC_BLOCK
    nch = kc.shape[1]
    hq = hg * qp
    sp = kk.shape[1]
    tc, tn, tw = tables
    c_start = CMP_STRIDE * np.arange(nch)
    s_start = SLC_BLOCK * np.arange(n_sb)
    cover_t = ((c_start[None, :] < s_start[:, None] + SLC_BLOCK)
               & (c_start[None, :] + CMP_BLOCK > s_start[:, None])
               & (np.arange(nch)[None, :] < (seq - CMP_BLOCK) // CMP_STRIDE + 1))
    cover_t = jnp.asarray(cover_t.astype(np.float32), BF16)
    rep = jnp.asarray(np.tile(np.eye(qp, dtype=np.float32), (1, hg)), BF16)
    full = lambda a: pl.BlockSpec(a.shape, lambda b, gg, i: (0,) * a.ndim)
    tab = lambda a: pl.BlockSpec((1,) + a.shape[1:], lambda b, gg, i: (gg, 0, 0))
    return pl.pallas_call(
        functools.partial(_nsa_kernel, n_sb=n_sb),
        out_shape=jax.ShapeDtypeStruct((bsz * seq, g * hg * dh), BF16),
        grid=(bsz, g, nstep),
        in_specs=[pl.BlockSpec((qp, hg * dh), lambda b, gg, i: (b * nstep + i, Z_Q // (hg * dh) + gg)),
                  pl.BlockSpec((qp, LANES), lambda b, gg, i: (b * nstep + i, Z_GN // LANES)),
                  pl.BlockSpec((1, nch, GD), lambda b, gg, i: (b, 0, 0)),
                  pl.BlockSpec((1, GD, nch), lambda b, gg, i: (b, 0, 0)),
                  pl.BlockSpec((1, sp, GD), lambda b, gg, i: (b, 0, 0)),
                  pl.BlockSpec((1, sp, GD), lambda b, gg, i: (b, 0, 1)),
                  pl.BlockSpec((1, GD, sp), lambda b, gg, i: (b, 0, 0)),
                  pl.BlockSpec((1, GD, sp), lambda b, gg, i: (b, 1, 0)),
                  tab(tc), tab(tn), tab(tw), full(cover_t), full(rep)],
        out_specs=pl.BlockSpec((qp, hg * dh), lambda b, gg, i: (b * nstep + i, gg)),
        scratch_shapes=[pltpu.VMEM((n_sb + 2 * SEL_PAD, hq), F32),
                        pltpu.VMEM((n_sb + 2 * SEL_PAD, hq), F32),
                        pltpu.VMEM((n_sb, qp), F32)],
        compiler_params=_cparams(("parallel", "arbitrary", "arbitrary")),
        name="nsa_attention",
    )(z, z, kc, vct, kk, kk, vvt, vvt, tc, tn, tw, cover_t, rep)


def _rope_table_kernel(pos_ref, inv_ref, sign_ref, o_ref):
    ang = pos_ref[...].astype(F32) * inv_ref[...]
    o_ref[...] = jnp.concatenate([jnp.cos(ang), jnp.sin(ang) * sign_ref[...]], axis=-1)


def _rope_table(positions):
    tm = TOK_TILE
    t = positions.size
    half = QK_ROPE // 2
    inv = ROPE_THETA ** (-jnp.arange(half, dtype=F32) / half)
    inv2 = jnp.concatenate([inv, inv])[None]
    sign = jnp.asarray(np.concatenate([-np.ones(half), np.ones(half)]).astype(np.float32))[None]
    return pl.pallas_call(
        _rope_table_kernel,
        out_shape=jax.ShapeDtypeStruct((t, 2 * QK_ROPE), F32),
        grid=(t // tm,),
        in_specs=[pl.BlockSpec((tm, 1), lambda i: (i, 0)),
                  pl.BlockSpec((1, QK_ROPE), lambda i: (0, 0)),
                  pl.BlockSpec((1, QK_ROPE), lambda i: (0, 0))],
        out_specs=pl.BlockSpec((tm, 2 * QK_ROPE), lambda i: (i, 0)),
        compiler_params=_cparams(("parallel",)),
        name="rope_table",
    )(positions.reshape(t, 1), inv2, sign)


MLA_HW = 256


def _mla_proj_kernel(cq_ref, ckv_ref, kr_ref, rope_ref, nq_ref, nkv_ref, wq_ref, wkn_ref, wvt_ref,
                     q_ref, k_ref, vt_ref):
    scale = (QK_NOPE + QK_ROPE) ** -0.5
    rope = rope_ref[...]
    yq = _dot(_rms(cq_ref[...], nq_ref[...]).astype(BF16), wq_ref[...])
    ckv = _rms(ckv_ref[...], nkv_ref[...]).astype(BF16)
    ykn = _dot(ckv, wkn_ref[...])
    vt_ref[0] = _dot_nt(wvt_ref[...], ckv).astype(BF16)
    kp = kr_ref[...] * rope
    kp = kp + pltpu.roll(kp, QK_ROPE, 1)
    lane = lax.broadcasted_iota(jnp.int32, kp.shape, 1)
    kp = jnp.where(lane < QK_ROPE, kp, 0.0).astype(BF16)
    for h in range(MLA_HEADS):
        base = MLA_HW * h
        q_ref[:, base:base + QK_NOPE] = (yq[:, base:base + QK_NOPE] * scale).astype(BF16)
        qp = yq[:, base + QK_NOPE:base + MLA_HW] * rope
        qp = qp + pltpu.roll(qp, QK_ROPE, 1)
        q_ref[:, base + QK_NOPE:base + MLA_HW] = (qp * scale).astype(BF16)
        k_ref[:, base:base + QK_NOPE] = ykn[:, QK_NOPE * h:QK_NOPE * (h + 1)].astype(BF16)
        k_ref[:, base + QK_NOPE:base + MLA_HW] = kp


def _swap_halves(w):
    half = QK_ROPE // 2
    return jnp.concatenate([w[..., half:], w[..., :half]], axis=-1)


def _mla_proj(z, rope_tab, norm_q, norm_kv, w_uq, w_ukv, bsz, seq):
    tm = TOK_TILE
    t = z.shape[0]
    nst = seq // tm
    wq = w_uq.reshape(Q_RANK, MLA_HEADS, QK_NOPE + QK_ROPE)
    wq = jnp.concatenate([wq, _swap_halves(wq[..., QK_NOPE:])], axis=-1)
    wq = wq.reshape(Q_RANK, MLA_HEADS * MLA_HW).astype(BF16)
    wkv = w_ukv.reshape(KV_RANK, MLA_HEADS, QK_NOPE + V_DIM)
    wkn = wkv[..., :QK_NOPE].reshape(KV_RANK, MLA_HEADS * QK_NOPE).astype(BF16)
    wvt = wkv[..., QK_NOPE:].reshape(KV_RANK, MLA_HEADS * V_DIM).T.astype(BF16)
    hw = MLA_HEADS * MLA_HW
    hv = MLA_HEADS * V_DIM
    row = lambda b, s: b * nst + s
    return pl.pallas_call(
        _mla_proj_kernel,
        out_shape=(jax.ShapeDtypeStruct((t, hw), BF16),
                   jax.ShapeDtypeStruct((t, hw), BF16),
                   jax.ShapeDtypeStruct((bsz, hv, seq), BF16)),
        grid=(bsz, nst),
        in_specs=[pl.BlockSpec((tm, Q_RANK), lambda b, s: (row(b, s), Z_CQ // Q_RANK)),
                  pl.BlockSpec((tm, KV_RANK), lambda b, s: (row(b, s), Z_CKV // KV_RANK)),
                  pl.BlockSpec((tm, 2 * QK_ROPE), lambda b, s: (row(b, s), Z_KR // (2 * QK_ROPE))),
                  pl.BlockSpec((tm, 2 * QK_ROPE), lambda b, s: (row(b, s), 0)),
                  pl.BlockSpec((1, Q_RANK), lambda b, s: (0, 0)),
                  pl.BlockSpec((1, KV_RANK), lambda b, s: (0, 0)),
                  pl.BlockSpec((Q_RANK, hw), lambda b, s: (0, 0)),
                  pl.BlockSpec((KV_RANK, hv), lambda b, s: (0, 0)),
                  pl.BlockSpec((hv, KV_RANK), lambda b, s: (0, 0))],
        out_specs=(pl.BlockSpec((tm, hw), lambda b, s: (row(b, s), 0)),
                   pl.BlockSpec((tm, hw), lambda b, s: (row(b, s), 0)),
                   pl.BlockSpec((1, hv, tm), lambda b, s: (b, 0, s))),
        compiler_params=_cparams(("parallel", "parallel")),
        name="mla_proj",
    )(z, z, z, rope_tab, norm_q[None], norm_kv[None], wq, wkn, wvt)


def _mla_attn_kernel(q_ref, k_ref, vt_ref, o_ref, *, tq, tk):
    iq = pl.program_id(2)
    q = q_ref[...]
    cd = lax.div(iq * tq, tk)

    def scores(c):
        k0 = pl.multiple_of(c * tk, tk)
        return k0, _dot_nt(k_ref[0, pl.ds(k0, tk), :], q)

    k0, s = scores(cd)
    kpos = k0 + lax.broadcasted_iota(jnp.int32, (tk, tq), 0)
    qpos = iq * tq + lax.broadcasted_iota(jnp.int32, (tk, tq), 1)
    s = jnp.where(kpos <= qpos, s, NEG_INF)
    m0, p0, l0 = _softmax_cols(s)
    acc0 = _dot(vt_ref[0, :, pl.ds(k0, tk)], p0.astype(BF16))

    def body(c, carry):
        m_old, l_old, acc_old = carry
        k0, s = scores(c)
        m_new = jnp.maximum(m_old, jnp.max(s, axis=0, keepdims=True))
        alpha = jnp.exp(m_old - m_new)
        p = jnp.exp(s - m_new)
        l_new = alpha * l_old + jnp.sum(p, axis=0, keepdims=True)
        acc_new = alpha * acc_old + _dot(vt_ref[0, :, pl.ds(k0, tk)], p.astype(BF16))
        return m_new, l_new, acc_new

    _, l, acc = lax.fori_loop(0, cd, body, (m0, l0, acc0))
    o_ref[...] = (acc / l).T.astype(BF16)


def _mla_attention(qf, kf, vt, bsz, seq):
    tq, tk = MLA_TQ, MLA_TK
    h = MLA_HEADS
    nq = seq // tq
    k3 = kf.reshape(bsz, seq, h * MLA_HW)
    return pl.pallas_call(
        functools.partial(_mla_attn_kernel, tq=tq, tk=tk),
        out_shape=jax.ShapeDtypeStruct((bsz * seq, h * V_DIM), BF16),
        grid=(bsz, h, nq),
        in_specs=[pl.BlockSpec((tq, MLA_HW), lambda b, hh, i: (b * nq + i, hh)),
                  pl.BlockSpec((1, seq, MLA_HW), lambda b, hh, i: (b, 0, hh)),
                  pl.BlockSpec((1, V_DIM, seq), lambda b, hh, i: (b, hh, 0))],
        out_specs=pl.BlockSpec((tq, V_DIM), lambda b, hh, i: (b * nq + i, hh)),
        compiler_params=_cparams(("parallel", "parallel", "arbitrary")),
        name="mla_attention",
    )(qf, k3, vt)


def _merge_kernel(ya_ref, yb_ref, yc_ref, ga_ref, gb_ref, gc_ref, x_ref,
                  wa_ref, wb_ref, wc_ref, wo_ref, o_ref):
    y = (jax.nn.sigmoid(ga_ref[...]) * _dot(ya_ref[...], wa_ref[...])
         + jax.nn.sigmoid(gb_ref[...]) * _dot(yb_ref[...], wb_ref[...])
         + jax.nn.sigmoid(gc_ref[...]) * _dot(yc_ref[...], wc_ref[...]))
    o_ref[...] = x_ref[...] + _dot(y.astype(BF16), wo_ref[...])


def _merge(ya, yb, yc, z, x, wa, wb, wc, wo):
    tm = TOK_TILE
    t, d = x.shape
    act = pl.BlockSpec((tm, ya.shape[1]), lambda i: (i, 0))
    gate = lambda k: pl.BlockSpec((tm, d), lambda i: (i, Z_GM // d + k))
    wbr = pl.BlockSpec((ya.shape[1], d), lambda i: (0, 0))
    return pl.pallas_call(
        _merge_kernel,
        out_shape=jax.ShapeDtypeStruct((t, d), F32),
        grid=(t // tm,),
        in_specs=[act, act, act, gate(0), gate(1), gate(2),
                  pl.BlockSpec((tm, d), lambda i: (i, 0)),
                  wbr, wbr, wbr, pl.BlockSpec((d, d), lambda i: (0, 0))],
        out_specs=pl.BlockSpec((tm, d), lambda i: (i, 0)),
        compiler_params=_cparams(("parallel",)),
        name="merge",
    )(ya, yb, yc, z, z, z, x, wa.astype(BF16), wb.astype(BF16), wc.astype(BF16), wo.astype(BF16))


def _xattn_kernel(x_ref, g_ref, wq_ref, kv_ref, wo_ref, o_ref):
    x = x_ref[...]
    h = _rms(x, g_ref[...]).astype(BF16)
    q = _dot(h, wq_ref[...]) * XATTN_DH ** -0.5
    hd = XATTN_HEADS * XATTN_DH
    outs = []
    for hh in range(XATTN_HEADS):
        qh = q[:, XATTN_DH * hh:XATTN_DH * (hh + 1)].astype(BF16)
        kh = kv_ref[0, :, XATTN_DH * hh:XATTN_DH * (hh + 1)]
        vh = kv_ref[0, :, hd + XATTN_DH * hh:hd + XATTN_DH * (hh + 1)]
        s = _dot_nt(qh, kh)
        m = jnp.max(s, axis=-1, keepdims=True)
        p = jnp.exp(s - m)
        p = p / jnp.sum(p, axis=-1, keepdims=True)
        outs.append(_dot(p.astype(BF16), vh))
    o = jnp.concatenate(outs, axis=-1).astype(BF16)
    o_ref[...] = x + _dot(o, wo_ref[...])


def _xattn(x, g, wq, kv, wo, bsz, seq):
    tm = TOK_TILE
    t, d = x.shape
    nst = seq // tm
    m_len = kv.shape[1]
    hd = XATTN_HEADS * XATTN_DH
    return pl.pallas_call(
        _xattn_kernel,
        out_shape=jax.ShapeDtypeStruct((t, d), F32),
        grid=(bsz, nst),
        in_specs=[pl.BlockSpec((tm, d), lambda b, s: (b * nst + s, 0)),
                  pl.BlockSpec((1, d), lambda b, s: (0, 0)),
                  pl.BlockSpec((d, hd), lambda b, s: (0, 0)),
                  pl.BlockSpec((1, m_len, 2 * hd), lambda b, s: (b, 0, 0)),
                  pl.BlockSpec((hd, d), lambda b, s: (0, 0))],
        out_specs=pl.BlockSpec((tm, d), lambda b, s: (b * nst + s, 0)),
        compiler_params=_cparams(("parallel", "parallel")),
        name="xattn",
    )(x, g[None], wq.astype(BF16), kv, wo.astype(BF16))


def _ffn_kernel(x_ref, g_ref, wg_ref, wu_ref, wd_ref, gf_ref, o_ref, h_ref, acc_ref, *, final):
    c = pl.program_id(1)

    @pl.when(c == 0)
    def _():
        h_ref[...] = _rms(x_ref[...], g_ref[...]).astype(BF16)
        acc_ref[...] = x_ref[...]

    h = h_ref[...]
    gate = _dot(h, wg_ref[...])
    up = _dot(h, wu_ref[...])
    act = (gate * jax.nn.sigmoid(gate) * up).astype(BF16)
    acc_ref[...] += _dot(act, wd_ref[...])

    @pl.when(c == pl.num_programs(1) - 1)
    def _():
        y = acc_ref[...]
        o_ref[...] = _rms(y, gf_ref[...]) if final else y


def _ffn(x, g, w_gate_up, w_down, g_final, final):
    tm, tc = FFN_TM, FFN_TC
    t, d = x.shape
    nc = FFN_HIDDEN // tc
    wgu = w_gate_up.astype(BF16)
    return pl.pallas_call(
        functools.partial(_ffn_kernel, final=final),
        out_shape=jax.ShapeDtypeStruct((t, d), F32),
        grid=(t // tm, nc),
        in_specs=[pl.BlockSpec((tm, d), lambda i, c: (i, 0)),
                  pl.BlockSpec((1, d), lambda i, c: (0, 0)),
                  pl.BlockSpec((d, tc), lambda i, c: (0, c)),
                  pl.BlockSpec((d, tc), lambda i, c: (0, nc + c)),
                  pl.BlockSpec((tc, d), lambda i, c: (c, 0)),
                  pl.BlockSpec((1, d), lambda i, c: (0, 0))],
        out_specs=pl.BlockSpec((tm, d), lambda i, c: (i, 0)),
        scratch_shapes=[pltpu.VMEM((tm, d), BF16), pltpu.VMEM((tm, d), F32)],
        compiler_params=_cparams(("parallel", "arbitrary")),
        name="ffn",
    )(x, g[None], wgu, wgu, w_down.astype(BF16), g_final[None])


def _split_w_in(w):
    k_rope = w[:, O_KR:O_KR + QK_ROPE]
    kv = lambda kind: w[:, O_KV + GD * kind:O_KV + GD * (kind + 1)]
    pad = jnp.zeros((w.shape[0], Z_COLS - Z_GN - 3 * NSA_HEADS), w.dtype)
    wz = jnp.concatenate([
        w[:, O_GM:O_GM + 3 * D_MODEL],
        w[:, O_GLU:O_GLU + 2 * CONV_CH],
        w[:, O_Q:O_Q + NSA_HEADS * NSA_DH],
        w[:, O_CKV:O_CKV + KV_RANK],
        k_rope, _swap_halves(k_rope),
        w[:, O_CQ:O_CQ + Q_RANK],
        kv(0), kv(1),
        w[:, O_GN:O_GN + 3 * NSA_HEADS], pad], axis=1).astype(BF16)
    wk = jnp.concatenate([kv(2), kv(4)], axis=1).astype(BF16)
    wvt = jnp.concatenate([kv(3), kv(5)], axis=1).T.astype(BF16)
    return wz, wk, wvt


def kernel(x, mem, positions, rel_bias, norm_mix, norm_xattn, norm_mem, norm_ffn, norm_final, w_in, conv_w, conv_b, conv_ln_g, conv_ln_b, w_branch_conv, cmp_pos_k, cmp_w1_k, cmp_b1_k, cmp_w2_k, cmp_pos_v, cmp_w1_v, cmp_b1_v, cmp_w2_v, w_branch_nsa, mla_norm_q, mla_norm_kv, w_uq, w_ukv, w_branch_mla, w_out, w_xq, w_xkv, w_xo, w_gate_up, w_down):
    bsz, seq, d = x.shape
    depth = w_in.shape[0]
    t = bsz * seq
    m_len = mem.shape[1]
    xt = x.reshape(t, d)
    memt = mem.reshape(bsz * m_len, d)
    rope_tab = _rope_table(positions)
    tables = _nsa_tables(rel_bias)
    for l in range(depth):
        wz, wk, wvt = _split_w_in(w_in[l])
        z = _norm_matmul(xt, norm_mix[l][None], wz, TOK_TILE, IN_PROJ_TN, F32)
        kk, vvt = _kv_proj(xt, norm_mix[l][None], wk, wvt, bsz, seq)
        ya = _conv_module(z, conv_w[l], conv_b[l], conv_ln_g[l], conv_ln_b[l], bsz, seq)
        kc, vct = _compress(z, jnp.stack([cmp_pos_k[l], cmp_pos_v[l]]), jnp.stack([cmp_w1_k[l], cmp_w1_v[l]]),
                            jnp.stack([cmp_b1_k[l], cmp_b1_v[l]]), jnp.stack([cmp_w2_k[l], cmp_w2_v[l]]), bsz, seq)
        yb = _nsa_attention(z, kc, vct, kk, vvt, tables, bsz, seq)
        qf, kf, vt = _mla_proj(z, rope_tab, mla_norm_q[l], mla_norm_kv[l], w_uq[l], w_ukv[l], bsz, seq)
        yc = _mla_attention(qf, kf, vt, bsz, seq)
        xt = _merge(ya, yb, yc, z, xt, w_branch_conv[l], w_branch_nsa[l], w_branch_mla[l], w_out[l])
        mem_kv = _norm_matmul(memt, norm_mem[l][None], w_xkv[l].astype(BF16), 256, 1024, BF16)
        mem_kv = mem_kv.reshape(bsz, m_len, 2 * XATTN_HEADS * XATTN_DH)
        xt = _xattn(xt, norm_xattn[l], w_xq[l], mem_kv, w_xo[l], bsz, seq)
        xt = _ffn(xt, norm_ffn[l], w_gate_up[l], w_down[l], norm_final, l == depth - 1)
    return xt.reshape(bsz, seq, d)
```

```python
import functools
import math

import numpy as np
import jax
import jax.numpy as jnp
from jax import lax
from jax.experimental import pallas as pl
from jax.experimental.pallas import tpu as pltpu

F32 = jnp.float32
BF16 = jnp.bfloat16

EPS = 1e-6
NEG_INF = -1e30
FORCE_SCORE = 1e4

D_MODEL = 1024
CONV_CH = 512
CONV_WIDTH = 31
NSA_HEADS = 8
NSA_G = 2
NSA_HG = NSA_HEADS // NSA_G
NSA_DH = 64
CMP_BLOCK = 32
CMP_STRIDE = 16
CMP_HIDDEN = 256
SLC_BLOCK = 64
N_SELECT = 16
WINDOW = 512
NSA_QB = 64
MLA_HEADS = 4
Q_RANK = 384
KV_RANK = 256
QK_NOPE = 128
QK_ROPE = 64
V_DIM = 128
ROPE_THETA = 10000.0
REL_BUCKETS = 32
REL_MAX_DIST = 128
XATTN_HEADS = 4
XATTN_DH = 128
FFN_HIDDEN = 2816

LANES = 128
SUBLANES = 8

O_GLU, O_Q, O_KV, O_GN, O_CQ, O_CKV, O_KR, O_GM = 0, 1024, 1536, 2304, 2328, 2712, 2968, 3032
GD = NSA_G * NSA_DH

Z_GM = 0
Z_UA = 3072
Z_UB = 3584
Z_Q = 4096
Z_CKV = 4608
Z_KR = 4864
Z_CQ = 4992
Z_CMP = 5376
Z_GN = 5632
Z_COLS = 5760

VMEM_LIMIT = 56 * 1024 * 1024

TOK_TILE = 512
IN_PROJ_TN = 1152
FFN_TM = 1024
FFN_TC = 256
MLA_TQ = 512
MLA_TK = 1024

LOG2E = math.log2(math.e)


def _cparams(sem):
    return pltpu.CompilerParams(dimension_semantics=sem, vmem_limit_bytes=VMEM_LIMIT)


def _rms(x, g):
    return x * lax.rsqrt(jnp.mean(x * x, axis=-1, keepdims=True) + EPS) * g


def _dot(a, b):
    return jnp.dot(a, b, preferred_element_type=F32)


def _dot_nt(a, b):
    return lax.dot_general(a, b, (((1,), (1,)), ((), ())), preferred_element_type=F32)


def _norm_matmul_kernel(x_ref, g_ref, w_ref, o_ref, h_ref):
    @pl.when(pl.program_id(1) == 0)
    def _():
        h_ref[...] = _rms(x_ref[...], g_ref[...]).astype(BF16)

    o_ref[...] = _dot(h_ref[...], w_ref[...]).astype(o_ref.dtype)


def _norm_matmul(x, g, w, tm, tn, out_dtype):
    m, k = x.shape
    n = w.shape[1]
    return pl.pallas_call(
        _norm_matmul_kernel,
        out_shape=jax.ShapeDtypeStruct((m, n), out_dtype),
        grid=(m // tm, n // tn),
        in_specs=[pl.BlockSpec((tm, k), lambda i, j: (i, 0)),
                  pl.BlockSpec((1, k), lambda i, j: (0, 0)),
                  pl.BlockSpec((k, tn), lambda i, j: (0, j))],
        out_specs=pl.BlockSpec((tm, tn), lambda i, j: (i, j)),
        scratch_shapes=[pltpu.VMEM((tm, k), BF16)],
        compiler_params=_cparams(("parallel", "arbitrary")),
        name="norm_matmul",
    )(x, g, w)


def _kv_proj_kernel(x_ref, g_ref, wk_ref, wvt_ref, k_ref, vt_ref):
    @pl.when(pl.program_id(1) == 0)
    def _():
        k_ref[...] = jnp.zeros(k_ref.shape, BF16)
        vt_ref[...] = jnp.zeros(vt_ref.shape, BF16)

    @pl.when(pl.program_id(1) > 0)
    def _():
        h = _rms(x_ref[...], g_ref[...]).astype(BF16)
        k_ref[0] = _dot(h, wk_ref[...]).astype(BF16)
        vt_ref[0] = _dot_nt(wvt_ref[...], h).astype(BF16)


def _kv_proj(x, g, wk, wvt, bsz, seq):
    tm = WINDOW
    nst = seq // tm
    d = x.shape[1]
    nk = wk.shape[1]
    nv = wvt.shape[0]
    return pl.pallas_call(
        _kv_proj_kernel,
        out_shape=(jax.ShapeDtypeStruct((bsz, seq + tm, nk), BF16),
                   jax.ShapeDtypeStruct((bsz, nv, seq + tm), BF16)),
        grid=(bsz, nst + 1),
        in_specs=[pl.BlockSpec((tm, d), lambda b, s: (b * nst + jnp.maximum(s - 1, 0), 0)),
                  pl.BlockSpec((1, d), lambda b, s: (0, 0)),
                  pl.BlockSpec((d, nk), lambda b, s: (0, 0)),
                  pl.BlockSpec((nv, d), lambda b, s: (0, 0))],
        out_specs=(pl.BlockSpec((1, tm, nk), lambda b, s: (b, s, 0)),
                   pl.BlockSpec((1, nv, tm), lambda b, s: (b, 0, s))),
        compiler_params=_cparams(("parallel", "arbitrary")),
        name="nsa_kv_proj",
    )(x, g, wk, wvt)


CONV_HALO = 32


def _conv_kernel(a_ref, b_ref, w_ref, cb_ref, lg_ref, lb_ref, o_ref, buf_ref, *, ts):
    @pl.when(pl.program_id(1) == 0)
    def _():
        buf_ref[0:CONV_HALO, :] = jnp.zeros((CONV_HALO, CONV_CH), F32)

    u = a_ref[...] * jax.nn.sigmoid(b_ref[...])
    buf_ref[CONV_HALO:CONV_HALO + ts, :] = u
    off = CONV_HALO - (CONV_WIDTH - 1)
    acc = jnp.zeros((ts, CONV_CH), F32) + cb_ref[...]
    for k in range(CONV_WIDTH):
        acc = acc + buf_ref[off + k:off + k + ts, :] * w_ref[k:k + 1, :]
    buf_ref[0:CONV_HALO, :] = buf_ref[ts:ts + CONV_HALO, :]
    mu = jnp.mean(acc, axis=-1, keepdims=True)
    xc = acc - mu
    var = jnp.mean(xc * xc, axis=-1, keepdims=True)
    y = xc * lax.rsqrt(var + EPS) * lg_ref[...] + lb_ref[...]
    o_ref[...] = (y * jax.nn.sigmoid(y)).astype(BF16)


def _conv_module(z, conv_w, conv_b, ln_g, ln_b, bsz, seq):
    ts = TOK_TILE
    nst = seq // ts
    wpad = jnp.zeros((32, CONV_CH), F32).at[:CONV_WIDTH].set(conv_w)
    return pl.pallas_call(
        functools.partial(_conv_kernel, ts=ts),
        out_shape=jax.ShapeDtypeStruct((bsz * seq, CONV_CH), BF16),
        grid=(bsz, nst),
        in_specs=[pl.BlockSpec((ts, CONV_CH), lambda b, s: (b * nst + s, Z_UA // CONV_CH)),
                  pl.BlockSpec((ts, CONV_CH), lambda b, s: (b * nst + s, Z_UB // CONV_CH)),
                  pl.BlockSpec((32, CONV_CH), lambda b, s: (0, 0)),
                  pl.BlockSpec((1, CONV_CH), lambda b, s: (0, 0)),
                  pl.BlockSpec((1, CONV_CH), lambda b, s: (0, 0)),
                  pl.BlockSpec((1, CONV_CH), lambda b, s: (0, 0))],
        out_specs=pl.BlockSpec((ts, CONV_CH), lambda b, s: (b * nst + s, 0)),
        scratch_shapes=[pltpu.VMEM((ts + CONV_HALO, CONV_CH), F32)],
        compiler_params=_cparams(("arbitrary", "arbitrary")),
        name="conv_module",
    )(z, z, wpad, conv_b[None], ln_g[None], ln_b[None])


def _compress_kernel(xk_ref, xv_ref, pos_ref, w1_ref, b1_ref, w2k_ref, w2v_ref, kc_ref, vct_ref, *, nch):
    for kind, (x_ref, w2_ref) in enumerate(((xk_ref, w2k_ref), (xv_ref, w2v_ref))):
        a = jnp.zeros((nch, NSA_G * CMP_HIDDEN), F32)
        b = jnp.zeros((nch, NSA_G * CMP_HIDDEN), F32)
        for l in range(CMP_STRIDE):
            xs = x_ref[pl.ds(l, nch, stride=CMP_STRIDE), :]
            a = a + _dot((xs + pos_ref[kind, l:l + 1, :]).astype(BF16), w1_ref[kind, l])
            b = b + _dot((xs + pos_ref[kind, CMP_STRIDE + l:CMP_STRIDE + l + 1, :]).astype(BF16),
                         w1_ref[kind, CMP_STRIDE + l])
        pre = a + pltpu.roll(b, nch - 1, 0) + b1_ref[kind]
        out = _dot(jax.nn.gelu(pre).astype(BF16), w2_ref[...])
        if kind == 0:
            kc_ref[0] = out.astype(BF16)
        else:
            vct_ref[0] = out.T.astype(BF16)


def _blockdiag2(w):
    z = jnp.zeros_like(w)
    return jnp.concatenate([jnp.concatenate([w, z], axis=-1), jnp.concatenate([z, w], axis=-1)], axis=-2)


def _compress(z, pos, w1, b1, w2, bsz, seq):
    nch = seq // CMP_STRIDE
    pos2 = jnp.concatenate([pos, pos], axis=-1)
    w1e = _blockdiag2(w1.reshape(2, CMP_BLOCK, NSA_DH, CMP_HIDDEN)).astype(BF16)
    b1e = jnp.concatenate([b1, b1], axis=-1)[:, None]
    w2k = _blockdiag2(jnp.tile(w2[0], (1, NSA_HG))).astype(BF16)
    w2v = _blockdiag2(w2[1]).astype(BF16)
    full = lambda a: pl.BlockSpec(a.shape, lambda b: (0,) * a.ndim)
    return pl.pallas_call(
        functools.partial(_compress_kernel, nch=nch),
        out_shape=(jax.ShapeDtypeStruct((bsz, nch, NSA_G * NSA_HG * NSA_DH), BF16),
                   jax.ShapeDtypeStruct((bsz, GD, nch), BF16)),
        grid=(bsz,),
        in_specs=[pl.BlockSpec((seq, GD), lambda b: (b, Z_CMP // GD)),
                  pl.BlockSpec((seq, GD), lambda b: (b, Z_CMP // GD + 1)),
                  full(pos2), full(w1e), full(b1e), full(w2k), full(w2v)],
        out_specs=(pl.BlockSpec((1, nch, NSA_G * NSA_HG * NSA_DH), lambda b: (b, 0, 0)),
                   pl.BlockSpec((1, GD, nch), lambda b: (b, 0, 0))),
        compiler_params=_cparams(("parallel",)),
        name="nsa_compress",
    )(z, z, pos2, w1e, b1e, w2k, w2v)


NSA_QP = 2 * NSA_QB
NEAR_KEYS = 256
WIN_KEYS = 640
CMP_TAB_ROWS = 512
CMP_TAB_ZERO = 256


def _t5_bucket_np(d):
    exact = REL_BUCKETS // 2
    d = np.maximum(d, 0)
    ratio = np.log(np.maximum(d, 1).astype(np.float32) / np.float32(exact)) / np.float32(math.log(REL_MAX_DIST / exact))
    large = np.minimum(exact + (ratio * (REL_BUCKETS - exact)).astype(np.int32), REL_BUCKETS - 1)
    return np.where(d < exact, d, large).astype(np.int32)


def _bucket_thresholds():
    exact = REL_BUCKETS // 2
    bk = _t5_bucket_np(np.arange(4 * REL_MAX_DIST))
    assert np.all(np.diff(bk) >= 0) and bk[-1] == REL_BUCKETS - 1
    return [int(np.argmax(bk >= k)) for k in range(exact + 1, REL_BUCKETS)]


def _bias_rows(rel_ref, dist, valid, shift):
    exact = REL_BUCKETS // 2
    bucket = jnp.full(dist.shape, exact, jnp.int32)
    for thr in _bucket_thresholds():
        bucket = bucket + jnp.where(dist >= thr, 1, 0)
    bucket = jnp.where(dist < exact, dist, bucket)
    val = jnp.zeros(dist.shape, F32)
    for bkt in range(REL_BUCKETS):
        val = jnp.where(bucket == bkt, rel_ref[0, bkt:bkt + 1, :], val)
    if shift:
        val = val - rel_ref[0, REL_BUCKETS - 1:REL_BUCKETS, :]
    return jnp.where(valid, val * LOG2E, NEG_INF)


def _nsa_bias_kernel(rel_ref, tc_ref, tn_ref, tw_ref):
    hq = NSA_HG * NSA_QP
    rows = 128

    def dist_of(nrows, r0, fn):
        r = r0 + lax.broadcasted_iota(jnp.int32, (nrows, hq), 0)
        t = lax.bitwise_and(lax.broadcasted_iota(jnp.int32, (nrows, hq), 1), NSA_QP - 1)
        return fn(r, t)

    for r0 in range(0, CMP_TAB_ROWS, rows):
        d = dist_of(rows, r0, lambda r, t: t - CMP_STRIDE * (r - CMP_TAB_ZERO) - (CMP_BLOCK - 1))
        tc_ref[0, r0:r0 + rows, :] = _bias_rows(rel_ref, d, d >= 0, False)
    for r0 in range(0, NEAR_KEYS, rows):
        d = dist_of(rows, r0, lambda r, t: NEAR_KEYS // 2 + t - r)
        tn_ref[0, r0:r0 + rows, :] = _bias_rows(rel_ref, d, d >= 0, True)
    for r0 in range(0, WIN_KEYS, rows):
        d = dist_of(rows, r0, lambda r, t: WINDOW + t - r)
        tw_ref[0, r0:r0 + rows, :] = _bias_rows(rel_ref, d, (d >= 0) & (d < WINDOW), False)


def _nsa_tables(rel_bias):
    hq = NSA_HG * NSA_QP
    rel4 = jnp.repeat(rel_bias.reshape(REL_BUCKETS, NSA_G, NSA_HG).transpose(1, 0, 2), NSA_QP, axis=-1)
    spec = lambda r: pl.BlockSpec((1, r, hq), lambda g: (g, 0, 0))
    return pl.pallas_call(
        _nsa_bias_kernel,
        out_shape=(jax.ShapeDtypeStruct((NSA_G, CMP_TAB_ROWS, hq), F32),
                   jax.ShapeDtypeStruct((NSA_G, NEAR_KEYS, hq), F32),
                   jax.ShapeDtypeStruct((NSA_G, WIN_KEYS, hq), F32)),
        grid=(NSA_G,),
        in_specs=[spec(REL_BUCKETS)],
        out_specs=(spec(CMP_TAB_ROWS), spec(NEAR_KEYS), spec(WIN_KEYS)),
        compiler_params=_cparams(("parallel",)),
        name="nsa_bias_tables",
    )(rel4)


SEL_PAD = 8
FAR_KEYS = 1024
KV_FRONT = WINDOW
KREP = NSA_HG * NSA_DH


def _softmax_cols(s):
    m = jnp.max(s, axis=0, keepdims=True)
    p = jnp.exp2(s - m)
    return m, p, jnp.sum(p, axis=0, keepdims=True)


def _mask_blocks(s, mask_ref, row0, nblk):
    parts = []
    for jj in range(nblk):
        row = mask_ref[pl.ds(row0 + jj, 1), :]
        parts.append(jnp.where(row > 0.0, s[SLC_BLOCK * jj:SLC_BLOCK * (jj + 1)], NEG_INF))
    return jnp.concatenate(parts, axis=0)


def _rank_select(score_ref, n_sb, n_sel):
    groups = n_sb // SUBLANES
    sub = lax.broadcasted_iota(jnp.int32, (SUBLANES, NSA_QP), 0)
    tiles = [score_ref[SUBLANES * v:SUBLANES * (v + 1), :] for v in range(groups)]
    cnts = [jnp.zeros((SUBLANES, NSA_QP), F32) for _ in range(groups)]
    for jp in range(n_sb):
        row = score_ref[jp:jp + 1, :]
        for v in range(groups):
            lo = SUBLANES * v
            if jp < lo:
                beats = row >= tiles[v]
            elif jp >= lo + SUBLANES - 1:
                beats = row > tiles[v]
            else:
                beats = (row > tiles[v]) | ((row == tiles[v]) & (sub > jp - lo))
            cnts[v] = cnts[v] + jnp.where(beats, 1.0, 0.0)
    cnt = jnp.concatenate(cnts, axis=0)
    return jnp.where(cnt < float(n_sel), 1.0, 0.0)


def _nsa_kernel(q_ref, gate_ref, kc_ref, vct_ref, ks_ref, kw_ref, vst_ref, vwt_ref,
                tc_ref, tn_ref, tw_ref, cov_ref, rep_ref, o_ref,
                sel_ref, selfar_ref, score_ref, *, n_sb):
    g = pl.program_id(1)
    p2 = pl.program_id(2)
    hq = NSA_HG * NSA_QP
    nch = kc_ref.shape[1]

    qb = (q_ref[...] * (NSA_DH ** -0.5 * LOG2E)).astype(BF16)
    lane_head = lax.shift_right_logical(lax.broadcasted_iota(jnp.int32, qb.shape, 1), 6)
    q = jnp.concatenate([jnp.where(lane_head == h, qb, jnp.zeros_like(qb)) for h in range(NSA_HG)], axis=0)

    start_c = pl.multiple_of(CMP_TAB_ZERO - (NSA_QP // CMP_STRIDE) * p2, SUBLANES)
    sc = _dot_nt(kc_ref[0], q) + tc_ref[0, pl.ds(start_c, nch), :]
    _, pc, lc = _softmax_cols(sc)
    lane = lax.broadcasted_iota(jnp.int32, (1, hq), 1)
    tq = NSA_QP * p2 + lax.bitwise_and(lane, NSA_QP - 1)
    anyv = jnp.where(tq >= CMP_BLOCK - 1, 1.0, 0.0)
    pc = pc * (anyv / lc)
    o_cmp = _dot(vct_ref[0], pc.astype(BF16))

    psum = pc[:, 0:NSA_QP]
    for h in range(1, NSA_HG):
        psum = psum + pc[:, NSA_QP * h:NSA_QP * (h + 1)]
    p_hi = psum.astype(BF16)
    p_lo = (psum - p_hi.astype(F32)).astype(BF16)
    imp = _dot(cov_ref[...], p_hi) + _dot(cov_ref[...], p_lo)
    jrow = lax.broadcasted_iota(jnp.int32, (n_sb, NSA_QP), 0)
    tok = lax.broadcasted_iota(jnp.int32, (n_sb, NSA_QP), 1)
    cur = 2 * p2 + lax.shift_right_logical(tok, 6)
    forced = (jrow == 0) | (jrow == cur) | (jrow == cur - 1)
    score_ref[...] = jnp.where(forced, FORCE_SCORE, jnp.where(jrow <= cur, imp, -1.0))
    sel = _rank_select(score_ref, n_sb, min(N_SELECT, n_sb)).astype(BF16)
    sel4 = _dot(sel, rep_ref[...])
    zeros8 = jnp.zeros((SEL_PAD, hq), F32)
    sel_ref[0:SEL_PAD, :] = zeros8
    sel_ref[SEL_PAD + n_sb:2 * SEL_PAD + n_sb, :] = zeros8
    sel_ref[SEL_PAD:SEL_PAD + n_sb, :] = sel4
    selfar_ref[0:SEL_PAD, :] = zeros8
    selfar_ref[SEL_PAD + n_sb:2 * SEL_PAD + n_sb, :] = zeros8
    jrow4 = lax.broadcasted_iota(jnp.int32, (n_sb, hq), 0)
    selfar_ref[SEL_PAD:SEL_PAD + n_sb, :] = jnp.where(jrow4 < 2 * p2 - 2, sel4, 0.0)

    win0 = pl.multiple_of(NSA_QP * p2, LANES)
    near0 = pl.multiple_of(win0 + KV_FRONT - NEAR_KEYS // 2, LANES)
    s = _dot_nt(ks_ref[0, pl.ds(near0, NEAR_KEYS), :], q) + tn_ref[0]
    s = _mask_blocks(s, sel_ref, 2 * p2 - 2 + SEL_PAD, NEAR_KEYS // SLC_BLOCK)
    m_s, p_s, l_s = _softmax_cols(s)
    acc_s = _dot(vst_ref[0, :, pl.ds(near0, NEAR_KEYS)], p_s.astype(BF16))

    def far_body(c, carry):
        m_old, l_old, acc_old = carry
        k0 = pl.multiple_of(FAR_KEYS * c + KV_FRONT, LANES)
        sf = _dot_nt(ks_ref[0, pl.ds(k0, FAR_KEYS), :], q)
        sf = _mask_blocks(sf, selfar_ref, (FAR_KEYS // SLC_BLOCK) * c + SEL_PAD, FAR_KEYS // SLC_BLOCK)
        m_new = jnp.maximum(m_old, jnp.max(sf, axis=0, keepdims=True))
        alpha = jnp.exp2(m_old - m_new)
        pf = jnp.exp2(sf - m_new)
        l_new = alpha * l_old + jnp.sum(pf, axis=0, keepdims=True)
        acc_new = alpha * acc_old + _dot(vst_ref[0, :, pl.ds(k0, FAR_KEYS)], pf.astype(BF16))
        return m_new, l_new, acc_new

    n_far = lax.div(jnp.maximum(p2 - 1, 0) * NSA_QP + FAR_KEYS - 1, FAR_KEYS)
    _, l_s, acc_s = lax.fori_loop(0, n_far, far_body, (m_s, l_s, acc_s))

    sw = _dot_nt(kw_ref[0, pl.ds(win0, WIN_KEYS), :], q) + tw_ref[0]
    slabs = [sw[NSA_QP * j:NSA_QP * (j + 1)] for j in range(WIN_KEYS // NSA_QP)]
    for j in range(KV_FRONT // NSA_QP):
        slabs[j] = jnp.where(NSA_QP * j + win0 >= KV_FRONT, slabs[j], NEG_INF)
    _, p_w, l_w = _softmax_cols(jnp.concatenate(slabs, axis=0))
    acc_w = _dot(vwt_ref[0, :, pl.ds(win0, WIN_KEYS)], p_w.astype(BF16))

    gt = gate_ref[...].T
    gsel = jax.nn.sigmoid(jnp.where(g == 0, gt[0:3 * NSA_HG], gt[3 * NSA_HG:6 * NSA_HG]))
    gate = lambda r: jnp.concatenate([gsel[3 * h + r:3 * h + r + 1] for h in range(NSA_HG)], axis=1)
    out_t = (gate(0) * o_cmp + (gate(1) / l_s) * acc_s + (gate(2) / l_w) * acc_w).astype(BF16)
    stacked = jnp.concatenate([out_t[:, NSA_QP * h:NSA_QP * (h + 1)] for h in range(NSA_HG)], axis=0)
    r = lax.broadcasted_iota(jnp.int32, (NSA_QP, NSA_QP), 0)
    c = lax.broadcasted_iota(jnp.int32, (NSA_QP, NSA_QP), 1)
    eye = jnp.where(r == c, 1.0, 0.0).astype(BF16)
    o_ref[...] = _dot_nt(eye, stacked).astype(BF16)


def _nsa_attention(z, kc, vct, kk, vvt, tables, bsz, seq):
    g, hg, dh, qp = NSA_G, NSA_HG, NSA_DH, NSA_QP
    nstep = seq // qp
    n_sb = seq // SLC_BLOCK
    nch = kc.shape[1]
    hq = hg * qp
    sp = kk.shape[1]
    tc, tn, tw = tables
    c_start = CMP_STRIDE * np.arange(nch)
    s_start = SLC_BLOCK * np.arange(n_sb)
    cover_t = ((c_start[None, :] < s_start[:, None] + SLC_BLOCK)
               & (c_start[None, :] + CMP_BLOCK > s_start[:, None])
               & (np.arange(nch)[None, :] < (seq - CMP_BLOCK) // CMP_STRIDE + 1))
    cover_t = jnp.asarray(cover_t.astype(np.float32), BF16)
    rep = jnp.asarray(np.tile(np.eye(qp, dtype=np.float32), (1, hg)), BF16)
    full = lambda a: pl.BlockSpec(a.shape, lambda b, gg, i: (0,) * a.ndim)
    tab = lambda a: pl.BlockSpec((1,) + a.shape[1:], lambda b, gg, i: (gg, 0, 0))
    return pl.pallas_call(
        functools.partial(_nsa_kernel, n_sb=n_sb),
        out_shape=jax.ShapeDtypeStruct((bsz * seq, g * hg * dh), BF16),
        grid=(bsz, g, nstep),
        in_specs=[pl.BlockSpec((qp, hg * dh), lambda b, gg, i: (b * nstep + i, Z_Q // (hg * dh) + gg)),
                  pl.BlockSpec((qp, LANES), lambda b, gg, i: (b * nstep + i, Z_GN // LANES)),
                  pl.BlockSpec((1, nch, KREP), lambda b, gg, i: (b, 0, gg)),
                  pl.BlockSpec((1, dh, nch), lambda b, gg, i: (b, gg, 0)),
                  pl.BlockSpec((1, sp, KREP), lambda b, gg, i: (b, 0, gg)),
                  pl.BlockSpec((1, sp, KREP), lambda b, gg, i: (b, 0, g + gg)),
                  pl.BlockSpec((1, dh, sp), lambda b, gg, i: (b, gg, 0)),
                  pl.BlockSpec((1, dh, sp), lambda b, gg, i: (b, g + gg, 0)),
                  tab(tc), tab(tn), tab(tw), full(cover_t), full(rep)],
        out_specs=pl.BlockSpec((qp, hg * dh), lambda b, gg, i: (b * nstep + i, gg)),
        scratch_shapes=[pltpu.VMEM((n_sb + 2 * SEL_PAD, hq), F32),
                        pltpu.VMEM((n_sb + 2 * SEL_PAD, hq), F32),
                        pltpu.VMEM((n_sb, qp), F32)],
        compiler_params=_cparams(("parallel", "arbitrary", "arbitrary")),
        name="nsa_attention",
    )(z, z, kc, vct, kk, kk, vvt, vvt, tc, tn, tw, cover_t, rep)


def _rope_table_kernel(pos_ref, inv_ref, sign_ref, o_ref):
    ang = pos_ref[...].astype(F32) * inv_ref[...]
    o_ref[...] = jnp.concatenate([jnp.cos(ang), jnp.sin(ang) * sign_ref[...]], axis=-1)


def _rope_table(positions):
    tm = TOK_TILE
    t = positions.size
    half = QK_ROPE // 2
    inv = ROPE_THETA ** (-jnp.arange(half, dtype=F32) / half)
    inv2 = jnp.concatenate([inv, inv])[None]
    sign = jnp.asarray(np.concatenate([-np.ones(half), np.ones(half)]).astype(np.float32))[None]
    return pl.pallas_call(
        _rope_table_kernel,
        out_shape=jax.ShapeDtypeStruct((t, 2 * QK_ROPE), F32),
        grid=(t // tm,),
        in_specs=[pl.BlockSpec((tm, 1), lambda i: (i, 0)),
                  pl.BlockSpec((1, QK_ROPE), lambda i: (0, 0)),
                  pl.BlockSpec((1, QK_ROPE), lambda i: (0, 0))],
        out_specs=pl.BlockSpec((tm, 2 * QK_ROPE), lambda i: (i, 0)),
        compiler_params=_cparams(("parallel",)),
        name="rope_table",
    )(positions.reshape(t, 1), inv2, sign)


MLA_HW = 256


def _mla_proj_kernel(cq_ref, ckv_ref, kr_ref, rope_ref, nq_ref, nkv_ref, wq_ref, wkn_ref, wvt_ref,
                     q_ref, k_ref, vt_ref):
    scale = (QK_NOPE + QK_ROPE) ** -0.5 * LOG2E
    rope = rope_ref[...]
    yq = _dot(_rms(cq_ref[...], nq_ref[...]).astype(BF16), wq_ref[...])
    ckv = _rms(ckv_ref[...], nkv_ref[...]).astype(BF16)
    ykn = _dot(ckv, wkn_ref[...])
    vt_ref[0] = _dot_nt(wvt_ref[...], ckv).astype(BF16)
    kp = kr_ref[...] * rope
    kp = kp + pltpu.roll(kp, QK_ROPE, 1)
    lane = lax.broadcasted_iota(jnp.int32, kp.shape, 1)
    kp = jnp.where(lane < QK_ROPE, kp, 0.0).astype(BF16)
    for h in range(MLA_HEADS):
        base = MLA_HW * h
        q_ref[:, base:base + QK_NOPE] = (yq[:, base:base + QK_NOPE] * scale).astype(BF16)
        qp = yq[:, base + QK_NOPE:base + MLA_HW] * rope
        qp = qp + pltpu.roll(qp, QK_ROPE, 1)
        q_ref[:, base + QK_NOPE:base + MLA_HW] = (qp * scale).astype(BF16)
        k_ref[:, base:base + QK_NOPE] = ykn[:, QK_NOPE * h:QK_NOPE * (h + 1)].astype(BF16)
        k_ref[:, base + QK_NOPE:base + MLA_HW] = kp


def _swap_halves(w):
    half = QK_ROPE // 2
    return jnp.concatenate([w[..., half:], w[..., :half]], axis=-1)


def _mla_proj(z, rope_tab, norm_q, norm_kv, w_uq, w_ukv, bsz, seq):
    tm = TOK_TILE
    t = z.shape[0]
    nst = seq // tm
    wq = w_uq.reshape(Q_RANK, MLA_HEADS, QK_NOPE + QK_ROPE)
    wq = jnp.concatenate([wq, _swap_halves(wq[..., QK_NOPE:])], axis=-1)
    wq = wq.reshape(Q_RANK, MLA_HEADS * MLA_HW).astype(BF16)
    wkv = w_ukv.reshape(KV_RANK, MLA_HEADS, QK_NOPE + V_DIM)
    wkn = wkv[..., :QK_NOPE].reshape(KV_RANK, MLA_HEADS * QK_NOPE).astype(BF16)
    wvt = wkv[..., QK_NOPE:].reshape(KV_RANK, MLA_HEADS * V_DIM).T.astype(BF16)
    hw = MLA_HEADS * MLA_HW
    hv = MLA_HEADS * V_DIM
    row = lambda b, s: b * nst + s
    return pl.pallas_call(
        _mla_proj_kernel,
        out_shape=(jax.ShapeDtypeStruct((t, hw), BF16),
                   jax.ShapeDtypeStruct((t, hw), BF16),
                   jax.ShapeDtypeStruct((bsz, hv, seq), BF16)),
        grid=(bsz, nst),
        in_specs=[pl.BlockSpec((tm, Q_RANK), lambda b, s: (row(b, s), Z_CQ // Q_RANK)),
                  pl.BlockSpec((tm, KV_RANK), lambda b, s: (row(b, s), Z_CKV // KV_RANK)),
                  pl.BlockSpec((tm, 2 * QK_ROPE), lambda b, s: (row(b, s), Z_KR // (2 * QK_ROPE))),
                  pl.BlockSpec((tm, 2 * QK_ROPE), lambda b, s: (row(b, s), 0)),
                  pl.BlockSpec((1, Q_RANK), lambda b, s: (0, 0)),
                  pl.BlockSpec((1, KV_RANK), lambda b, s: (0, 0)),
                  pl.BlockSpec((Q_RANK, hw), lambda b, s: (0, 0)),
                  pl.BlockSpec((KV_RANK, hv), lambda b, s: (0, 0)),
                  pl.BlockSpec((hv, KV_RANK), lambda b, s: (0, 0))],
        out_specs=(pl.BlockSpec((tm, hw), lambda b, s: (row(b, s), 0)),
                   pl.BlockSpec((tm, hw), lambda b, s: (row(b, s), 0)),
                   pl.BlockSpec((1, hv, tm), lambda b, s: (b, 0, s))),
        compiler_params=_cparams(("parallel", "parallel")),
        name="mla_proj",
    )(z, z, z, rope_tab, norm_q[None], norm_kv[None], wq, wkn, wvt)


def _mla_attn_kernel(q_ref, k_ref, vt_ref, o_ref, *, tq, tk):
    iq = pl.program_id(2)
    q = q_ref[...]
    cd = lax.div(iq * tq, tk)

    def scores(c):
        k0 = pl.multiple_of(c * tk, tk)
        return k0, _dot_nt(k_ref[0, pl.ds(k0, tk), :], q)

    k0, s = scores(cd)
    kpos = k0 + lax.broadcasted_iota(jnp.int32, (tk, tq), 0)
    qpos = iq * tq + lax.broadcasted_iota(jnp.int32, (tk, tq), 1)
    s = jnp.where(kpos <= qpos, s, NEG_INF)
    m0, p0, l0 = _softmax_cols(s)
    acc0 = _dot(vt_ref[0, :, pl.ds(k0, tk)], p0.astype(BF16))

    def body(c, carry):
        m_old, l_old, acc_old = carry
        k0, s = scores(c)
        m_new = jnp.maximum(m_old, jnp.max(s, axis=0, keepdims=True))
        alpha = jnp.exp2(m_old - m_new)
        p = jnp.exp2(s - m_new)
        l_new = alpha * l_old + jnp.sum(p, axis=0, keepdims=True)
        acc_new = alpha * acc_old + _dot(vt_ref[0, :, pl.ds(k0, tk)], p.astype(BF16))
        return m_new, l_new, acc_new

    _, l, acc = lax.fori_loop(0, cd, body, (m0, l0, acc0))
    o_ref[...] = (acc / l).T.astype(BF16)


def _mla_attention(qf, kf, vt, bsz, seq):
    tq, tk = MLA_TQ, MLA_TK
    h = MLA_HEADS
    nq = seq // tq
    k3 = kf.reshape(bsz, seq, h * MLA_HW)
    return pl.pallas_call(
        functools.partial(_mla_attn_kernel, tq=tq, tk=tk),
        out_shape=jax.ShapeDtypeStruct((bsz * seq, h * V_DIM), BF16),
        grid=(bsz, h, nq),
        in_specs=[pl.BlockSpec((tq, MLA_HW), lambda b, hh, i: (b * nq + i, hh)),
                  pl.BlockSpec((1, seq, MLA_HW), lambda b, hh, i: (b, 0, hh)),
                  pl.BlockSpec((1, V_DIM, seq), lambda b, hh, i: (b, hh, 0))],
        out_specs=pl.BlockSpec((tq, V_DIM), lambda b, hh, i: (b * nq + i, hh)),
        compiler_params=_cparams(("parallel", "parallel", "arbitrary")),
        name="mla_attention",
    )(qf, k3, vt)


def _merge_kernel(ya_ref, yb_ref, yc_ref, ga_ref, gb_ref, gc_ref, x_ref,
                  wa_ref, wb_ref, wc_ref, wo_ref, o_ref):
    y = (jax.nn.sigmoid(ga_ref[...]) * _dot(ya_ref[...], wa_ref[...])
         + jax.nn.sigmoid(gb_ref[...]) * _dot(yb_ref[...], wb_ref[...])
         + jax.nn.sigmoid(gc_ref[...]) * _dot(yc_ref[...], wc_ref[...]))
    o_ref[...] = x_ref[...] + _dot(y.astype(BF16), wo_ref[...])


def _merge(ya, yb, yc, z, x, wa, wb, wc, wo):
    tm = TOK_TILE
    t, d = x.shape
    act = pl.BlockSpec((tm, ya.shape[1]), lambda i: (i, 0))
    gate = lambda k: pl.BlockSpec((tm, d), lambda i: (i, Z_GM // d + k))
    wbr = pl.BlockSpec((ya.shape[1], d), lambda i: (0, 0))
    return pl.pallas_call(
        _merge_kernel,
        out_shape=jax.ShapeDtypeStruct((t, d), F32),
        grid=(t // tm,),
        in_specs=[act, act, act, gate(0), gate(1), gate(2),
                  pl.BlockSpec((tm, d), lambda i: (i, 0)),
                  wbr, wbr, wbr, pl.BlockSpec((d, d), lambda i: (0, 0))],
        out_specs=pl.BlockSpec((tm, d), lambda i: (i, 0)),
        compiler_params=_cparams(("parallel",)),
        name="merge",
    )(ya, yb, yc, z, z, z, x, wa.astype(BF16), wb.astype(BF16), wc.astype(BF16), wo.astype(BF16))


def _xattn_kernel(x_ref, g_ref, wq_ref, kv_ref, wo_ref, o_ref):
    x = x_ref[...]
    h = _rms(x, g_ref[...]).astype(BF16)
    q = _dot(h, wq_ref[...]) * XATTN_DH ** -0.5
    hd = XATTN_HEADS * XATTN_DH
    outs = []
    for hh in range(XATTN_HEADS):
        qh = q[:, XATTN_DH * hh:XATTN_DH * (hh + 1)].astype(BF16)
        kh = kv_ref[0, :, XATTN_DH * hh:XATTN_DH * (hh + 1)]
        vh = kv_ref[0, :, hd + XATTN_DH * hh:hd + XATTN_DH * (hh + 1)]
        s = _dot_nt(qh, kh)
        m = jnp.max(s, axis=-1, keepdims=True)
        p = jnp.exp(s - m)
        p = p / jnp.sum(p, axis=-1, keepdims=True)
        outs.append(_dot(p.astype(BF16), vh))
    o = jnp.concatenate(outs, axis=-1).astype(BF16)
    o_ref[...] = x + _dot(o, wo_ref[...])


def _xattn(x, g, wq, kv, wo, bsz, seq):
    tm = TOK_TILE
    t, d = x.shape
    nst = seq // tm
    m_len = kv.shape[1]
    hd = XATTN_HEADS * XATTN_DH
    return pl.pallas_call(
        _xattn_kernel,
        out_shape=jax.ShapeDtypeStruct((t, d), F32),
        grid=(bsz, nst),
        in_specs=[pl.BlockSpec((tm, d), lambda b, s: (b * nst + s, 0)),
                  pl.BlockSpec((1, d), lambda b, s: (0, 0)),
                  pl.BlockSpec((d, hd), lambda b, s: (0, 0)),
                  pl.BlockSpec((1, m_len, 2 * hd), lambda b, s: (b, 0, 0)),
                  pl.BlockSpec((hd, d), lambda b, s: (0, 0))],
        out_specs=pl.BlockSpec((tm, d), lambda b, s: (b * nst + s, 0)),
        compiler_params=_cparams(("parallel", "parallel")),
        name="xattn",
    )(x, g[None], wq.astype(BF16), kv, wo.astype(BF16))


def _ffn_kernel(x_ref, g_ref, wg_ref, wu_ref, wd_ref, gf_ref, o_ref, h_ref, acc_ref, *, final):
    c = pl.program_id(1)

    @pl.when(c == 0)
    def _():
        h_ref[...] = _rms(x_ref[...], g_ref[...]).astype(BF16)
        acc_ref[...] = x_ref[...]

    h = h_ref[...]
    gate = _dot(h, wg_ref[...])
    up = _dot(h, wu_ref[...])
    act = (gate * jax.nn.sigmoid(gate) * up).astype(BF16)
    acc_ref[...] += _dot(act, wd_ref[...])

    @pl.when(c == pl.num_programs(1) - 1)
    def _():
        y = acc_ref[...]
        o_ref[...] = _rms(y, gf_ref[...]) if final else y


def _ffn(x, g, w_gate_up, w_down, g_final, final):
    tm, tc = FFN_TM, FFN_TC
    t, d = x.shape
    nc = FFN_HIDDEN // tc
    wgu = w_gate_up.astype(BF16)
    return pl.pallas_call(
        functools.partial(_ffn_kernel, final=final),
        out_shape=jax.ShapeDtypeStruct((t, d), F32),
        grid=(t // tm, nc),
        in_specs=[pl.BlockSpec((tm, d), lambda i, c: (i, 0)),
                  pl.BlockSpec((1, d), lambda i, c: (0, 0)),
                  pl.BlockSpec((d, tc), lambda i, c: (0, c)),
                  pl.BlockSpec((d, tc), lambda i, c: (0, nc + c)),
                  pl.BlockSpec((tc, d), lambda i, c: (c, 0)),
                  pl.BlockSpec((1, d), lambda i, c: (0, 0))],
        out_specs=pl.BlockSpec((tm, d), lambda i, c: (i, 0)),
        scratch_shapes=[pltpu.VMEM((tm, d), BF16), pltpu.VMEM((tm, d), F32)],
        compiler_params=_cparams(("parallel", "arbitrary")),
        name="ffn",
    )(x, g[None], wgu, wgu, w_down.astype(BF16), g_final[None])


def _split_w_in(w):
    k_rope = w[:, O_KR:O_KR + QK_ROPE]
    kv = lambda kind: w[:, O_KV + GD * kind:O_KV + GD * (kind + 1)]
    pad = jnp.zeros((w.shape[0], Z_COLS - Z_GN - 3 * NSA_HEADS), w.dtype)
    wz = jnp.concatenate([
        w[:, O_GM:O_GM + 3 * D_MODEL],
        w[:, O_GLU:O_GLU + 2 * CONV_CH],
        w[:, O_Q:O_Q + NSA_HEADS * NSA_DH],
        w[:, O_CKV:O_CKV + KV_RANK],
        k_rope, _swap_halves(k_rope),
        w[:, O_CQ:O_CQ + Q_RANK],
        kv(0), kv(1),
        w[:, O_GN:O_GN + 3 * NSA_HEADS], pad], axis=1).astype(BF16)
    per_head = lambda wkind: jnp.tile(wkind.reshape(-1, NSA_G, 1, NSA_DH), (1, 1, NSA_HG, 1)).reshape(-1, NSA_G * KREP)
    wk = jnp.concatenate([per_head(kv(2)), per_head(kv(4))], axis=1).astype(BF16)
    wvt = jnp.concatenate([kv(3), kv(5)], axis=1).T.astype(BF16)
    return wz, wk, wvt


def kernel(x, mem, positions, rel_bias, norm_mix, norm_xattn, norm_mem, norm_ffn, norm_final, w_in, conv_w, conv_b, conv_ln_g, conv_ln_b, w_branch_conv, cmp_pos_k, cmp_w1_k, cmp_b1_k, cmp_w2_k, cmp_pos_v, cmp_w1_v, cmp_b1_v, cmp_w2_v, w_branch_nsa, mla_norm_q, mla_norm_kv, w_uq, w_ukv, w_branch_mla, w_out, w_xq, w_xkv, w_xo, w_gate_up, w_down):
    bsz, seq, d = x.shape
    depth = w_in.shape[0]
    t = bsz * seq
    m_len = mem.shape[1]
    xt = x.reshape(t, d)
    memt = mem.reshape(bsz * m_len, d)
    rope_tab = _rope_table(positions)
    tables = _nsa_tables(rel_bias)
    for l in range(depth):
        wz, wk, wvt = _split_w_in(w_in[l])
        z = _norm_matmul(xt, norm_mix[l][None], wz, TOK_TILE, IN_PROJ_TN, F32)
        kk, vvt = _kv_proj(xt, norm_mix[l][None], wk, wvt, bsz, seq)
        ya = _conv_module(z, conv_w[l], conv_b[l], conv_ln_g[l], conv_ln_b[l], bsz, seq)
        kc, vct = _compress(z, jnp.stack([cmp_pos_k[l], cmp_pos_v[l]]), jnp.stack([cmp_w1_k[l], cmp_w1_v[l]]),
                            jnp.stack([cmp_b1_k[l], cmp_b1_v[l]]), jnp.stack([cmp_w2_k[l], cmp_w2_v[l]]), bsz, seq)
        yb = _nsa_attention(z, kc, vct, kk, vvt, tables, bsz, seq)
        qf, kf, vt = _mla_proj(z, rope_tab, mla_norm_q[l], mla_norm_kv[l], w_uq[l], w_ukv[l], bsz, seq)
        yc = _mla_attention(qf, kf, vt, bsz, seq)
        xt = _merge(ya, yb, yc, z, xt, w_branch_conv[l], w_branch_nsa[l], w_branch_mla[l], w_out[l])
        mem_kv = _norm_matmul(memt, norm_mem[l][None], w_xkv[l].astype(BF16), 256, 1024, BF16)
        mem_kv = mem_kv.reshape(bsz, m_len, 2 * XATTN_HEADS * XATTN_DH)
        xt = _xattn(xt, norm_xattn[l], w_xq[l], mem_kv, w_xo[l], bsz, seq)
        xt = _ffn(xt, norm_ffn[l], w_gate_up[l], w_down[l], norm_final, l == depth - 1)
    return xt.reshape(bsz, seq, d)
```

```python
import functools
import math

import numpy as np
import jax
import jax.numpy as jnp
from jax import lax
from jax.experimental import pallas as pl
from jax.experimental.pallas import tpu as pltpu

F32 = jnp.float32
BF16 = jnp.bfloat16

EPS = 1e-6
NEG_INF = -1e30
FORCE_SCORE = 1e4

D_MODEL = 1024
CONV_CH = 512
CONV_WIDTH = 31
NSA_HEADS = 8
NSA_G = 2
NSA_HG = NSA_HEADS // NSA_G
NSA_DH = 64
CMP_BLOCK = 32
CMP_STRIDE = 16
CMP_HIDDEN = 256
SLC_BLOCK = 64
N_SELECT = 16
WINDOW = 512
NSA_QB = 64
MLA_HEADS = 4
Q_RANK = 384
KV_RANK = 256
QK_NOPE = 128
QK_ROPE = 64
V_DIM = 128
ROPE_THETA = 10000.0
REL_BUCKETS = 32
REL_MAX_DIST = 128
XATTN_HEADS = 4
XATTN_DH = 128
FFN_HIDDEN = 2816

LANES = 128
SUBLANES = 8

O_GLU, O_Q, O_KV, O_GN, O_CQ, O_CKV, O_KR, O_GM = 0, 1024, 1536, 2304, 2328, 2712, 2968, 3032
GD = NSA_G * NSA_DH

Z_GM = 0
Z_UA = 3072
Z_UB = 3584
Z_Q = 4096
Z_CKV = 4608
Z_KR = 4864
Z_CQ = 4992
Z_CMP = 5376
Z_GN = 5632
Z_COLS = 5760

VMEM_LIMIT = 56 * 1024 * 1024

TOK_TILE = 512
IN_PROJ_TM = 1024
IN_PROJ_TN = 1152
FFN_TM = 1024
FFN_TC = 256
MLA_TQ = 512
MLA_TK = 1024

LOG2E = math.log2(math.e)


def _cparams(sem):
    return pltpu.CompilerParams(dimension_semantics=sem, vmem_limit_bytes=VMEM_LIMIT)


def _rms(x, g):
    return x * lax.rsqrt(jnp.mean(x * x, axis=-1, keepdims=True) + EPS) * g


def _dot(a, b):
    return jnp.dot(a, b, preferred_element_type=F32)


def _dot_nt(a, b):
    return lax.dot_general(a, b, (((1,), (1,)), ((), ())), preferred_element_type=F32)


def _norm_matmul_kernel(x_ref, g_ref, w_ref, o_ref, h_ref):
    @pl.when(pl.program_id(1) == 0)
    def _():
        h_ref[...] = _rms(x_ref[...], g_ref[...]).astype(BF16)

    o_ref[...] = _dot(h_ref[...], w_ref[...]).astype(o_ref.dtype)


def _norm_matmul(x, g, w, tm, tn, out_dtype):
    m, k = x.shape
    n = w.shape[1]
    return pl.pallas_call(
        _norm_matmul_kernel,
        out_shape=jax.ShapeDtypeStruct((m, n), out_dtype),
        grid=(m // tm, n // tn),
        in_specs=[pl.BlockSpec((tm, k), lambda i, j: (i, 0)),
                  pl.BlockSpec((1, k), lambda i, j: (0, 0)),
                  pl.BlockSpec((k, tn), lambda i, j: (0, j))],
        out_specs=pl.BlockSpec((tm, tn), lambda i, j: (i, j)),
        scratch_shapes=[pltpu.VMEM((tm, k), BF16)],
        compiler_params=_cparams(("parallel", "arbitrary")),
        name="norm_matmul",
    )(x, g, w)


def _in_proj_kernel(x_ref, g_ref, w_ref, z_ref, zc_ref, h_ref, *, cmp_tile, cmp_off):
    @pl.when(pl.program_id(1) == 0)
    def _():
        h_ref[...] = _rms(x_ref[...], g_ref[...]).astype(BF16)

    acc = _dot(h_ref[...], w_ref[...])
    z_ref[...] = acc.astype(BF16)

    @pl.when(pl.program_id(1) == cmp_tile)
    def _():
        zc_ref[...] = acc[:, cmp_off:cmp_off + 2 * GD]


def _in_proj(x, g, w):
    tm, tn = IN_PROJ_TM, IN_PROJ_TN
    m, k = x.shape
    n = w.shape[1]
    return pl.pallas_call(
        functools.partial(_in_proj_kernel, cmp_tile=Z_CMP // tn, cmp_off=Z_CMP % tn),
        out_shape=(jax.ShapeDtypeStruct((m, n), BF16), jax.ShapeDtypeStruct((m, 2 * GD), F32)),
        grid=(m // tm, n // tn),
        in_specs=[pl.BlockSpec((tm, k), lambda i, j: (i, 0)),
                  pl.BlockSpec((1, k), lambda i, j: (0, 0)),
                  pl.BlockSpec((k, tn), lambda i, j: (0, j))],
        out_specs=(pl.BlockSpec((tm, tn), lambda i, j: (i, j)),
                   pl.BlockSpec((tm, 2 * GD), lambda i, j: (i, 0))),
        scratch_shapes=[pltpu.VMEM((tm, k), BF16)],
        compiler_params=_cparams(("parallel", "arbitrary")),
        name="in_proj",
    )(x, g, w)


def _kv_proj_kernel(x_ref, g_ref, wk_ref, wvt_ref, k_ref, vt_ref):
    @pl.when(pl.program_id(1) == 0)
    def _():
        k_ref[...] = jnp.zeros(k_ref.shape, BF16)
        vt_ref[...] = jnp.zeros(vt_ref.shape, BF16)

    @pl.when(pl.program_id(1) > 0)
    def _():
        h = _rms(x_ref[...], g_ref[...]).astype(BF16)
        k_ref[0] = _dot(h, wk_ref[...]).astype(BF16)
        vt_ref[0] = _dot_nt(wvt_ref[...], h).astype(BF16)


def _kv_proj(x, g, wk, wvt, bsz, seq):
    tm = WINDOW
    nst = seq // tm
    d = x.shape[1]
    nk = wk.shape[1]
    nv = wvt.shape[0]
    return pl.pallas_call(
        _kv_proj_kernel,
        out_shape=(jax.ShapeDtypeStruct((bsz, seq + tm, nk), BF16),
                   jax.ShapeDtypeStruct((bsz, nv, seq + tm), BF16)),
        grid=(bsz, nst + 1),
        in_specs=[pl.BlockSpec((tm, d), lambda b, s: (b * nst + jnp.maximum(s - 1, 0), 0)),
                  pl.BlockSpec((1, d), lambda b, s: (0, 0)),
                  pl.BlockSpec((d, nk), lambda b, s: (0, 0)),
                  pl.BlockSpec((nv, d), lambda b, s: (0, 0))],
        out_specs=(pl.BlockSpec((1, tm, nk), lambda b, s: (b, s, 0)),
                   pl.BlockSpec((1, nv, tm), lambda b, s: (b, 0, s))),
        compiler_params=_cparams(("parallel", "arbitrary")),
        name="nsa_kv_proj",
    )(x, g, wk, wvt)


CONV_HALO = 32


CONV_ROWS = 64


def _conv_kernel(a_ref, b_ref, w_ref, cb_ref, lg_ref, lb_ref, o_ref, buf_ref, sh_ref, *, ts):
    @pl.when(pl.program_id(1) == 0)
    def _():
        buf_ref[0:CONV_HALO, :] = jnp.zeros((CONV_HALO, CONV_CH), F32)

    buf_ref[CONV_HALO:CONV_HALO + ts, :] = a_ref[...].astype(F32) * jax.nn.sigmoid(b_ref[...].astype(F32))
    span = ts + CONV_HALO - SUBLANES
    for r in range(1, SUBLANES):
        sh_ref[r - 1, 0:span, :] = buf_ref[r:r + span, :]
    off = CONV_HALO - (CONV_WIDTH - 1)

    def rows(i, carry):
        r0 = pl.multiple_of(i * CONV_ROWS, CONV_ROWS)
        acc = jnp.zeros((CONV_ROWS, CONV_CH), F32) + cb_ref[...]
        for k in range(CONV_WIDTH):
            res, base = (off + k) % SUBLANES, (off + k) // SUBLANES * SUBLANES
            if res == 0:
                tap = buf_ref[pl.ds(r0 + base, CONV_ROWS), :]
            else:
                tap = sh_ref[res - 1, pl.ds(r0 + base, CONV_ROWS), :]
            acc = acc + tap * w_ref[k:k + 1, :]
        mu = jnp.mean(acc, axis=-1, keepdims=True)
        xc = acc - mu
        var = jnp.mean(xc * xc, axis=-1, keepdims=True)
        y = xc * lax.rsqrt(var + EPS) * lg_ref[...] + lb_ref[...]
        o_ref[pl.ds(r0, CONV_ROWS), :] = (y * jax.nn.sigmoid(y)).astype(BF16)
        return carry

    lax.fori_loop(0, ts // CONV_ROWS, rows, 0)
    buf_ref[0:CONV_HALO, :] = buf_ref[ts:ts + CONV_HALO, :]


def _conv_module(z, conv_w, conv_b, ln_g, ln_b, bsz, seq):
    ts = TOK_TILE
    nst = seq // ts
    wpad = jnp.zeros((32, CONV_CH), F32).at[:CONV_WIDTH].set(conv_w)
    return pl.pallas_call(
        functools.partial(_conv_kernel, ts=ts),
        out_shape=jax.ShapeDtypeStruct((bsz * seq, CONV_CH), BF16),
        grid=(bsz, nst),
        in_specs=[pl.BlockSpec((ts, CONV_CH), lambda b, s: (b * nst + s, Z_UA // CONV_CH)),
                  pl.BlockSpec((ts, CONV_CH), lambda b, s: (b * nst + s, Z_UB // CONV_CH)),
                  pl.BlockSpec((32, CONV_CH), lambda b, s: (0, 0)),
                  pl.BlockSpec((1, CONV_CH), lambda b, s: (0, 0)),
                  pl.BlockSpec((1, CONV_CH), lambda b, s: (0, 0)),
                  pl.BlockSpec((1, CONV_CH), lambda b, s: (0, 0))],
        out_specs=pl.BlockSpec((ts, CONV_CH), lambda b, s: (b * nst + s, 0)),
        scratch_shapes=[pltpu.VMEM((ts + CONV_HALO, CONV_CH), F32),
                        pltpu.VMEM((SUBLANES - 1, ts + CONV_HALO - SUBLANES, CONV_CH), F32)],
        compiler_params=_cparams(("arbitrary", "arbitrary")),
        name="conv_module",
    )(z, z, wpad, conv_b[None], ln_g[None], ln_b[None])


def _compress_kernel(xk_ref, xv_ref, pos_ref, w1_ref, b1_ref, w2k_ref, w2v_ref, kc_ref, vct_ref, *, nch):
    for kind, (x_ref, w2_ref) in enumerate(((xk_ref, w2k_ref), (xv_ref, w2v_ref))):
        a = jnp.zeros((nch, NSA_G * CMP_HIDDEN), F32)
        b = jnp.zeros((nch, NSA_G * CMP_HIDDEN), F32)
        for l in range(CMP_STRIDE):
            xs = x_ref[pl.ds(l, nch, stride=CMP_STRIDE), :]
            a = a + _dot((xs + pos_ref[kind, l:l + 1, :]).astype(BF16), w1_ref[kind, l])
            b = b + _dot((xs + pos_ref[kind, CMP_STRIDE + l:CMP_STRIDE + l + 1, :]).astype(BF16),
                         w1_ref[kind, CMP_STRIDE + l])
        pre = a + pltpu.roll(b, nch - 1, 0) + b1_ref[kind]
        out = _dot(jax.nn.gelu(pre).astype(BF16), w2_ref[...])
        if kind == 0:
            kc_ref[0] = out.astype(BF16)
        else:
            vct_ref[0] = out.T.astype(BF16)


def _blockdiag2(w):
    z = jnp.zeros_like(w)
    return jnp.concatenate([jnp.concatenate([w, z], axis=-1), jnp.concatenate([z, w], axis=-1)], axis=-2)


def _compress(z, pos, w1, b1, w2, bsz, seq):
    nch = seq // CMP_STRIDE
    pos2 = jnp.concatenate([pos, pos], axis=-1)
    w1e = _blockdiag2(w1.reshape(2, CMP_BLOCK, NSA_DH, CMP_HIDDEN)).astype(BF16)
    b1e = jnp.concatenate([b1, b1], axis=-1)[:, None]
    w2k = _blockdiag2(jnp.tile(w2[0], (1, NSA_HG))).astype(BF16)
    w2v = _blockdiag2(w2[1]).astype(BF16)
    full = lambda a: pl.BlockSpec(a.shape, lambda b: (0,) * a.ndim)
    return pl.pallas_call(
        functools.partial(_compress_kernel, nch=nch),
        out_shape=(jax.ShapeDtypeStruct((bsz, nch, NSA_G * NSA_HG * NSA_DH), BF16),
                   jax.ShapeDtypeStruct((bsz, GD, nch), BF16)),
        grid=(bsz,),
        in_specs=[pl.BlockSpec((seq, GD), lambda b: (b, 0)),
                  pl.BlockSpec((seq, GD), lambda b: (b, 1)),
                  full(pos2), full(w1e), full(b1e), full(w2k), full(w2v)],
        out_specs=(pl.BlockSpec((1, nch, NSA_G * NSA_HG * NSA_DH), lambda b: (b, 0, 0)),
                   pl.BlockSpec((1, GD, nch), lambda b: (b, 0, 0))),
        compiler_params=_cparams(("parallel",)),
        name="nsa_compress",
    )(z, z, pos2, w1e, b1e, w2k, w2v)


NSA_QP = 2 * NSA_QB
NEAR_KEYS = 256
WIN_KEYS = 640
CMP_TAB_ROWS = 512
CMP_TAB_ZERO = 256


def _t5_bucket_np(d):
    exact = REL_BUCKETS // 2
    d = np.maximum(d, 0)
    ratio = np.log(np.maximum(d, 1).astype(np.float32) / np.float32(exact)) / np.float32(math.log(REL_MAX_DIST / exact))
    large = np.minimum(exact + (ratio * (REL_BUCKETS - exact)).astype(np.int32), REL_BUCKETS - 1)
    return np.where(d < exact, d, large).astype(np.int32)


def _bucket_thresholds():
    exact = REL_BUCKETS // 2
    bk = _t5_bucket_np(np.arange(4 * REL_MAX_DIST))
    assert np.all(np.diff(bk) >= 0) and bk[-1] == REL_BUCKETS - 1
    return [int(np.argmax(bk >= k)) for k in range(exact + 1, REL_BUCKETS)]


def _bias_rows(rel_ref, dist, valid, shift):
    exact = REL_BUCKETS // 2
    bucket = jnp.full(dist.shape, exact, jnp.int32)
    for thr in _bucket_thresholds():
        bucket = bucket + jnp.where(dist >= thr, 1, 0)
    bucket = jnp.where(dist < exact, dist, bucket)
    val = jnp.zeros(dist.shape, F32)
    for bkt in range(REL_BUCKETS):
        val = jnp.where(bucket == bkt, rel_ref[0, bkt:bkt + 1, :], val)
    if shift:
        val = val - rel_ref[0, REL_BUCKETS - 1:REL_BUCKETS, :]
    return jnp.where(valid, val * LOG2E, NEG_INF)


def _nsa_bias_kernel(rel_ref, tc_ref, tn_ref, tw_ref):
    hq = NSA_HG * NSA_QP
    rows = 128

    def dist_of(nrows, r0, fn):
        r = r0 + lax.broadcasted_iota(jnp.int32, (nrows, hq), 0)
        t = lax.bitwise_and(lax.broadcasted_iota(jnp.int32, (nrows, hq), 1), NSA_QP - 1)
        return fn(r, t)

    for r0 in range(0, CMP_TAB_ROWS, rows):
        d = dist_of(rows, r0, lambda r, t: t - CMP_STRIDE * (r - CMP_TAB_ZERO) - (CMP_BLOCK - 1))
        tc_ref[0, r0:r0 + rows, :] = _bias_rows(rel_ref, d, d >= 0, False)
    for r0 in range(0, NEAR_KEYS, rows):
        d = dist_of(rows, r0, lambda r, t: NEAR_KEYS // 2 + t - r)
        tn_ref[0, r0:r0 + rows, :] = _bias_rows(rel_ref, d, d >= 0, True)
    for r0 in range(0, WIN_KEYS, rows):
        d = dist_of(rows, r0, lambda r, t: WINDOW + t - r)
        tw_ref[0, r0:r0 + rows, :] = _bias_rows(rel_ref, d, (d >= 0) & (d < WINDOW), False)


def _nsa_tables(rel_bias):
    hq = NSA_HG * NSA_QP
    rel4 = jnp.repeat(rel_bias.reshape(REL_BUCKETS, NSA_G, NSA_HG).transpose(1, 0, 2), NSA_QP, axis=-1)
    spec = lambda r: pl.BlockSpec((1, r, hq), lambda g: (g, 0, 0))
    return pl.pallas_call(
        _nsa_bias_kernel,
        out_shape=(jax.ShapeDtypeStruct((NSA_G, CMP_TAB_ROWS, hq), F32),
                   jax.ShapeDtypeStruct((NSA_G, NEAR_KEYS, hq), F32),
                   jax.ShapeDtypeStruct((NSA_G, WIN_KEYS, hq), F32)),
        grid=(NSA_G,),
        in_specs=[spec(REL_BUCKETS)],
        out_specs=(spec(CMP_TAB_ROWS), spec(NEAR_KEYS), spec(WIN_KEYS)),
        compiler_params=_cparams(("parallel",)),
        name="nsa_bias_tables",
    )(rel4)


SEL_PAD = 8
FAR_KEYS = 1024
KV_FRONT = WINDOW
KREP = NSA_HG * NSA_DH


def _softmax_cols(s):
    m = jnp.max(s, axis=0, keepdims=True)
    p = jnp.exp2(s - m)
    return m, p, jnp.sum(p, axis=0, keepdims=True)


def _mask_blocks(s, mask_ref, row0, nblk):
    parts = []
    for jj in range(nblk):
        row = mask_ref[pl.ds(row0 + jj, 1), :]
        parts.append(jnp.where(row > 0.0, s[SLC_BLOCK * jj:SLC_BLOCK * (jj + 1)], NEG_INF))
    return jnp.concatenate(parts, axis=0)


def _rank_select(score_ref, n_sb, n_sel):
    groups = n_sb // SUBLANES
    sub = lax.broadcasted_iota(jnp.int32, (SUBLANES, NSA_QP), 0)
    tiles = [score_ref[SUBLANES * v:SUBLANES * (v + 1), :] for v in range(groups)]
    cnts = [jnp.zeros((SUBLANES, NSA_QP), F32) for _ in range(groups)]
    for jp in range(n_sb):
        row = score_ref[jp:jp + 1, :]
        for v in range(groups):
            lo = SUBLANES * v
            if jp < lo:
                beats = row >= tiles[v]
            elif jp >= lo + SUBLANES - 1:
                beats = row > tiles[v]
            else:
                beats = (row > tiles[v]) | ((row == tiles[v]) & (sub > jp - lo))
            cnts[v] = cnts[v] + jnp.where(beats, 1.0, 0.0)
    cnt = jnp.concatenate(cnts, axis=0)
    return jnp.where(cnt < float(n_sel), 1.0, 0.0)


def _nsa_kernel(q_ref, gate_ref, kc_ref, vct_ref, ks_ref, kw_ref, vst_ref, vwt_ref,
                tc_ref, tn_ref, tw_ref, cov_ref, rep_ref, o_ref,
                sel_ref, selfar_ref, score_ref, *, n_sb):
    g = pl.program_id(1)
    p2 = pl.program_id(2)
    hq = NSA_HG * NSA_QP
    nch = kc_ref.shape[1]

    qb = (q_ref[...].astype(F32) * (NSA_DH ** -0.5 * LOG2E)).astype(BF16)
    lane_head = lax.shift_right_logical(lax.broadcasted_iota(jnp.int32, qb.shape, 1), 6)
    q = jnp.concatenate([jnp.where(lane_head == h, qb, jnp.zeros_like(qb)) for h in range(NSA_HG)], axis=0)

    start_c = pl.multiple_of(CMP_TAB_ZERO - (NSA_QP // CMP_STRIDE) * p2, SUBLANES)
    sc = _dot_nt(kc_ref[0], q) + tc_ref[0, pl.ds(start_c, nch), :]
    _, pc, lc = _softmax_cols(sc)
    lane = lax.broadcasted_iota(jnp.int32, (1, hq), 1)
    tq = NSA_QP * p2 + lax.bitwise_and(lane, NSA_QP - 1)
    anyv = jnp.where(tq >= CMP_BLOCK - 1, 1.0, 0.0)
    pc = pc * (anyv / lc)
    o_cmp = _dot(vct_ref[0], pc.astype(BF16))

    psum = pc[:, 0:NSA_QP]
    for h in range(1, NSA_HG):
        psum = psum + pc[:, NSA_QP * h:NSA_QP * (h + 1)]
    p_hi = psum.astype(BF16)
    p_lo = (psum - p_hi.astype(F32)).astype(BF16)
    imp = _dot(cov_ref[...], p_hi) + _dot(cov_ref[...], p_lo)
    jrow = lax.broadcasted_iota(jnp.int32, (n_sb, NSA_QP), 0)
    tok = lax.broadcasted_iota(jnp.int32, (n_sb, NSA_QP), 1)
    cur = 2 * p2 + lax.shift_right_logical(tok, 6)
    forced = (jrow == 0) | (jrow == cur) | (jrow == cur - 1)
    score_ref[...] = jnp.where(forced, FORCE_SCORE, jnp.where(jrow <= cur, imp, -1.0))
    sel = _rank_select(score_ref, n_sb, min(N_SELECT, n_sb)).astype(BF16)
    sel4 = _dot(sel, rep_ref[...])
    zeros8 = jnp.zeros((SEL_PAD, hq), F32)
    sel_ref[0:SEL_PAD, :] = zeros8
    sel_ref[SEL_PAD + n_sb:2 * SEL_PAD + n_sb, :] = zeros8
    sel_ref[SEL_PAD:SEL_PAD + n_sb, :] = sel4
    selfar_ref[0:SEL_PAD, :] = zeros8
    selfar_ref[SEL_PAD + n_sb:2 * SEL_PAD + n_sb, :] = zeros8
    jrow4 = lax.broadcasted_iota(jnp.int32, (n_sb, hq), 0)
    selfar_ref[SEL_PAD:SEL_PAD + n_sb, :] = jnp.where(jrow4 < 2 * p2 - 2, sel4, 0.0)

    win0 = pl.multiple_of(NSA_QP * p2, LANES)
    near0 = pl.multiple_of(win0 + KV_FRONT - NEAR_KEYS // 2, LANES)
    s = _dot_nt(ks_ref[0, pl.ds(near0, NEAR_KEYS), :], q) + tn_ref[0]
    s = _mask_blocks(s, sel_ref, 2 * p2 - 2 + SEL_PAD, NEAR_KEYS // SLC_BLOCK)
    m_s, p_s, l_s = _softmax_cols(s)
    acc_s = _dot(vst_ref[0, :, pl.ds(near0, NEAR_KEYS)], p_s.astype(BF16))

    def far_body(c, carry):
        m_old, l_old, acc_old = carry
        k0 = pl.multiple_of(FAR_KEYS * c + KV_FRONT, LANES)
        sf = _dot_nt(ks_ref[0, pl.ds(k0, FAR_KEYS), :], q)
        sf = _mask_blocks(sf, selfar_ref, (FAR_KEYS // SLC_BLOCK) * c + SEL_PAD, FAR_KEYS // SLC_BLOCK)
        m_new = jnp.maximum(m_old, jnp.max(sf, axis=0, keepdims=True))
        alpha = jnp.exp2(m_old - m_new)
        pf = jnp.exp2(sf - m_new)
        l_new = alpha * l_old + jnp.sum(pf, axis=0, keepdims=True)
        acc_new = alpha * acc_old + _dot(vst_ref[0, :, pl.ds(k0, FAR_KEYS)], pf.astype(BF16))
        return m_new, l_new, acc_new

    n_far = lax.div(jnp.maximum(p2 - 1, 0) * NSA_QP + FAR_KEYS - 1, FAR_KEYS)
    _, l_s, acc_s = lax.fori_loop(0, n_far, far_body, (m_s, l_s, acc_s))

    sw = _dot_nt(kw_ref[0, pl.ds(win0, WIN_KEYS), :], q) + tw_ref[0]
    slabs = [sw[NSA_QP * j:NSA_QP * (j + 1)] for j in range(WIN_KEYS // NSA_QP)]
    for j in range(KV_FRONT // NSA_QP):
        slabs[j] = jnp.where(NSA_QP * j + win0 >= KV_FRONT, slabs[j], NEG_INF)
    _, p_w, l_w = _softmax_cols(jnp.concatenate(slabs, axis=0))
    acc_w = _dot(vwt_ref[0, :, pl.ds(win0, WIN_KEYS)], p_w.astype(BF16))

    gt = gate_ref[...].astype(F32).T
    gsel = jax.nn.sigmoid(jnp.where(g == 0, gt[0:3 * NSA_HG], gt[3 * NSA_HG:6 * NSA_HG]))
    gate = lambda r: jnp.concatenate([gsel[3 * h + r:3 * h + r + 1] for h in range(NSA_HG)], axis=1)
    out_t = (gate(0) * o_cmp + (gate(1) / l_s) * acc_s + (gate(2) / l_w) * acc_w).astype(BF16)
    stacked = jnp.concatenate([out_t[:, NSA_QP * h:NSA_QP * (h + 1)] for h in range(NSA_HG)], axis=0)
    r = lax.broadcasted_iota(jnp.int32, (NSA_QP, NSA_QP), 0)
    c = lax.broadcasted_iota(jnp.int32, (NSA_QP, NSA_QP), 1)
    eye = jnp.where(r == c, 1.0, 0.0).astype(BF16)
    o_ref[...] = _dot_nt(eye, stacked).astype(BF16)


def _nsa_attention(z, kc, vct, kk, vvt, tables, bsz, seq):
    g, hg, dh, qp = NSA_G, NSA_HG, NSA_DH, NSA_QP
    nstep = seq // qp
    n_sb = seq // SLC_BLOCK
    nch = kc.shape[1]
    hq = hg * qp
    sp = kk.shape[1]
    tc, tn, tw = tables
    c_start = CMP_STRIDE * np.arange(nch)
    s_start = SLC_BLOCK * np.arange(n_sb)
    cover_t = ((c_start[None, :] < s_start[:, None] + SLC_BLOCK)
               & (c_start[None, :] + CMP_BLOCK > s_start[:, None])
               & (np.arange(nch)[None, :] < (seq - CMP_BLOCK) // CMP_STRIDE + 1))
    cover_t = jnp.asarray(cover_t.astype(np.float32), BF16)
    rep = jnp.asarray(np.tile(np.eye(qp, dtype=np.float32), (1, hg)), BF16)
    full = lambda a: pl.BlockSpec(a.shape, lambda b, gg, i: (0,) * a.ndim)
    tab = lambda a: pl.BlockSpec((1,) + a.shape[1:], lambda b, gg, i: (gg, 0, 0))
    return pl.pallas_call(
        functools.partial(_nsa_kernel, n_sb=n_sb),
        out_shape=jax.ShapeDtypeStruct((bsz * seq, g * hg * dh), BF16),
        grid=(bsz, g, nstep),
        in_specs=[pl.BlockSpec((qp, hg * dh), lambda b, gg, i: (b * nstep + i, Z_Q // (hg * dh) + gg)),
                  pl.BlockSpec((qp, LANES), lambda b, gg, i: (b * nstep + i, Z_GN // LANES)),
                  pl.BlockSpec((1, nch, KREP), lambda b, gg, i: (b, 0, gg)),
                  pl.BlockSpec((1, dh, nch), lambda b, gg, i: (b, gg, 0)),
                  pl.BlockSpec((1, sp, KREP), lambda b, gg, i: (b, 0, gg)),
                  pl.BlockSpec((1, sp, KREP), lambda b, gg, i: (b, 0, g + gg)),
                  pl.BlockSpec((1, dh, sp), lambda b, gg, i: (b, gg, 0)),
                  pl.BlockSpec((1, dh, sp), lambda b, gg, i: (b, g + gg, 0)),
                  tab(tc), tab(tn), tab(tw), full(cover_t), full(rep)],
        out_specs=pl.BlockSpec((qp, hg * dh), lambda b, gg, i: (b * nstep + i, gg)),
        scratch_shapes=[pltpu.VMEM((n_sb + 2 * SEL_PAD, hq), F32),
                        pltpu.VMEM((n_sb + 2 * SEL_PAD, hq), F32),
                        pltpu.VMEM((n_sb, qp), F32)],
        compiler_params=_cparams(("parallel", "arbitrary", "arbitrary")),
        name="nsa_attention",
    )(z, z, kc, vct, kk, kk, vvt, vvt, tc, tn, tw, cover_t, rep)


def _rope_table_kernel(pos_ref, inv_ref, sign_ref, o_ref):
    ang = pos_ref[...].astype(F32) * inv_ref[...]
    o_ref[...] = jnp.concatenate([jnp.cos(ang), jnp.sin(ang) * sign_ref[...]], axis=-1)


def _rope_table(positions):
    tm = TOK_TILE
    t = positions.size
    half = QK_ROPE // 2
    inv = ROPE_THETA ** (-jnp.arange(half, dtype=F32) / half)
    inv2 = jnp.concatenate([inv, inv])[None]
    sign = jnp.asarray(np.concatenate([-np.ones(half), np.ones(half)]).astype(np.float32))[None]
    return pl.pallas_call(
        _rope_table_kernel,
        out_shape=jax.ShapeDtypeStruct((t, 2 * QK_ROPE), F32),
        grid=(t // tm,),
        in_specs=[pl.BlockSpec((tm, 1), lambda i: (i, 0)),
                  pl.BlockSpec((1, QK_ROPE), lambda i: (0, 0)),
                  pl.BlockSpec((1, QK_ROPE), lambda i: (0, 0))],
        out_specs=pl.BlockSpec((tm, 2 * QK_ROPE), lambda i: (i, 0)),
        compiler_params=_cparams(("parallel",)),
        name="rope_table",
    )(positions.reshape(t, 1), inv2, sign)


MLA_HW = 256


def _mla_proj_kernel(cq_ref, ckv_ref, kr_ref, rope_ref, nq_ref, nkv_ref, wq_ref, wkn_ref, wvt_ref,
                     q_ref, k_ref, vt_ref):
    scale = (QK_NOPE + QK_ROPE) ** -0.5 * LOG2E
    rope = rope_ref[...]
    yq = _dot(_rms(cq_ref[...].astype(F32), nq_ref[...]).astype(BF16), wq_ref[...])
    ckv = _rms(ckv_ref[...].astype(F32), nkv_ref[...]).astype(BF16)
    ykn = _dot(ckv, wkn_ref[...])
    vt_ref[0] = _dot_nt(wvt_ref[...], ckv).astype(BF16)
    kp = kr_ref[...].astype(F32) * rope
    kp = kp + pltpu.roll(kp, QK_ROPE, 1)
    lane = lax.broadcasted_iota(jnp.int32, kp.shape, 1)
    kp = jnp.where(lane < QK_ROPE, kp, 0.0).astype(BF16)
    for h in range(MLA_HEADS):
        base = MLA_HW * h
        q_ref[:, base:base + QK_NOPE] = (yq[:, base:base + QK_NOPE] * scale).astype(BF16)
        qp = yq[:, base + QK_NOPE:base + MLA_HW] * rope
        qp = qp + pltpu.roll(qp, QK_ROPE, 1)
        q_ref[:, base + QK_NOPE:base + MLA_HW] = (qp * scale).astype(BF16)
        k_ref[:, base:base + QK_NOPE] = ykn[:, QK_NOPE * h:QK_NOPE * (h + 1)].astype(BF16)
        k_ref[:, base + QK_NOPE:base + MLA_HW] = kp


def _swap_halves(w):
    half = QK_ROPE // 2
    return jnp.concatenate([w[..., half:], w[..., :half]], axis=-1)


def _mla_proj(z, rope_tab, norm_q, norm_kv, w_uq, w_ukv, bsz, seq):
    tm = TOK_TILE
    t = z.shape[0]
    nst = seq // tm
    wq = w_uq.reshape(Q_RANK, MLA_HEADS, QK_NOPE + QK_ROPE)
    wq = jnp.concatenate([wq, _swap_halves(wq[..., QK_NOPE:])], axis=-1)
    wq = wq.reshape(Q_RANK, MLA_HEADS * MLA_HW).astype(BF16)
    wkv = w_ukv.reshape(KV_RANK, MLA_HEADS, QK_NOPE + V_DIM)
    wkn = wkv[..., :QK_NOPE].reshape(KV_RANK, MLA_HEADS * QK_NOPE).astype(BF16)
    wvt = wkv[..., QK_NOPE:].reshape(KV_RANK, MLA_HEADS * V_DIM).T.astype(BF16)
    hw = MLA_HEADS * MLA_HW
    hv = MLA_HEADS * V_DIM
    row = lambda b, s: b * nst + s
    return pl.pallas_call(
        _mla_proj_kernel,
        out_shape=(jax.ShapeDtypeStruct((t, hw), BF16),
                   jax.ShapeDtypeStruct((t, hw), BF16),
                   jax.ShapeDtypeStruct((bsz, hv, seq), BF16)),
        grid=(bsz, nst),
        in_specs=[pl.BlockSpec((tm, Q_RANK), lambda b, s: (row(b, s), Z_CQ // Q_RANK)),
                  pl.BlockSpec((tm, KV_RANK), lambda b, s: (row(b, s), Z_CKV // KV_RANK)),
                  pl.BlockSpec((tm, 2 * QK_ROPE), lambda b, s: (row(b, s), Z_KR // (2 * QK_ROPE))),
                  pl.BlockSpec((tm, 2 * QK_ROPE), lambda b, s: (row(b, s), 0)),
                  pl.BlockSpec((1, Q_RANK), lambda b, s: (0, 0)),
                  pl.BlockSpec((1, KV_RANK), lambda b, s: (0, 0)),
                  pl.BlockSpec((Q_RANK, hw), lambda b, s: (0, 0)),
                  pl.BlockSpec((KV_RANK, hv), lambda b, s: (0, 0)),
                  pl.BlockSpec((hv, KV_RANK), lambda b, s: (0, 0))],
        out_specs=(pl.BlockSpec((tm, hw), lambda b, s: (row(b, s), 0)),
                   pl.BlockSpec((tm, hw), lambda b, s: (row(b, s), 0)),
                   pl.BlockSpec((1, hv, tm), lambda b, s: (b, 0, s))),
        compiler_params=_cparams(("parallel", "parallel")),
        name="mla_proj",
    )(z, z, z, rope_tab, norm_q[None], norm_kv[None], wq, wkn, wvt)


def _mla_attn_kernel(q_ref, k_ref, vt_ref, o_ref, *, tq, tk):
    iq = pl.program_id(2)
    q = q_ref[...]
    cd = lax.div(iq * tq, tk)

    def scores(c):
        k0 = pl.multiple_of(c * tk, tk)
        return k0, _dot_nt(k_ref[0, pl.ds(k0, tk), :], q)

    k0, s = scores(cd)
    kpos = k0 + lax.broadcasted_iota(jnp.int32, (tk, tq), 0)
    qpos = iq * tq + lax.broadcasted_iota(jnp.int32, (tk, tq), 1)
    s = jnp.where(kpos <= qpos, s, NEG_INF)
    m0, p0, l0 = _softmax_cols(s)
    acc0 = _dot(vt_ref[0, :, pl.ds(k0, tk)], p0.astype(BF16))

    def body(c, carry):
        m_old, l_old, acc_old = carry
        k0, s = scores(c)
        m_new = jnp.maximum(m_old, jnp.max(s, axis=0, keepdims=True))
        alpha = jnp.exp2(m_old - m_new)
        p = jnp.exp2(s - m_new)
        l_new = alpha * l_old + jnp.sum(p, axis=0, keepdims=True)
        acc_new = alpha * acc_old + _dot(vt_ref[0, :, pl.ds(k0, tk)], p.astype(BF16))
        return m_new, l_new, acc_new

    _, l, acc = lax.fori_loop(0, cd, body, (m0, l0, acc0))
    o_ref[...] = (acc / l).T.astype(BF16)


def _mla_attention(qf, kf, vt, bsz, seq):
    tq, tk = MLA_TQ, MLA_TK
    h = MLA_HEADS
    nq = seq // tq
    k3 = kf.reshape(bsz, seq, h * MLA_HW)
    return pl.pallas_call(
        functools.partial(_mla_attn_kernel, tq=tq, tk=tk),
        out_shape=jax.ShapeDtypeStruct((bsz * seq, h * V_DIM), BF16),
        grid=(bsz, h, nq),
        in_specs=[pl.BlockSpec((tq, MLA_HW), lambda b, hh, i: (b * nq + i, hh)),
                  pl.BlockSpec((1, seq, MLA_HW), lambda b, hh, i: (b, 0, hh)),
                  pl.BlockSpec((1, V_DIM, seq), lambda b, hh, i: (b, hh, 0))],
        out_specs=pl.BlockSpec((tq, V_DIM), lambda b, hh, i: (b * nq + i, hh)),
        compiler_params=_cparams(("parallel", "parallel", "arbitrary")),
        name="mla_attention",
    )(qf, k3, vt)


def _merge_kernel(ya_ref, yb_ref, yc_ref, ga_ref, gb_ref, gc_ref, x_ref,
                  wa_ref, wb_ref, wc_ref, wo_ref, o_ref):
    sig = lambda ref: jax.nn.sigmoid(ref[...].astype(F32))
    y = (sig(ga_ref) * _dot(ya_ref[...], wa_ref[...])
         + sig(gb_ref) * _dot(yb_ref[...], wb_ref[...])
         + sig(gc_ref) * _dot(yc_ref[...], wc_ref[...]))
    o_ref[...] = x_ref[...] + _dot(y.astype(BF16), wo_ref[...])


def _merge(ya, yb, yc, z, x, wa, wb, wc, wo):
    tm = TOK_TILE
    t, d = x.shape
    act = pl.BlockSpec((tm, ya.shape[1]), lambda i: (i, 0))
    gate = lambda k: pl.BlockSpec((tm, d), lambda i: (i, Z_GM // d + k))
    wbr = pl.BlockSpec((ya.shape[1], d), lambda i: (0, 0))
    return pl.pallas_call(
        _merge_kernel,
        out_shape=jax.ShapeDtypeStruct((t, d), F32),
        grid=(t // tm,),
        in_specs=[act, act, act, gate(0), gate(1), gate(2),
                  pl.BlockSpec((tm, d), lambda i: (i, 0)),
                  wbr, wbr, wbr, pl.BlockSpec((d, d), lambda i: (0, 0))],
        out_specs=pl.BlockSpec((tm, d), lambda i: (i, 0)),
        compiler_params=_cparams(("parallel",)),
        name="merge",
    )(ya, yb, yc, z, z, z, x, wa.astype(BF16), wb.astype(BF16), wc.astype(BF16), wo.astype(BF16))


def _xattn_kernel(x_ref, g_ref, wq_ref, kv_ref, wo_ref, o_ref):
    x = x_ref[...]
    h = _rms(x, g_ref[...]).astype(BF16)
    q = _dot(h, wq_ref[...]) * XATTN_DH ** -0.5
    hd = XATTN_HEADS * XATTN_DH
    outs = []
    for hh in range(XATTN_HEADS):
        qh = q[:, XATTN_DH * hh:XATTN_DH * (hh + 1)].astype(BF16)
        kh = kv_ref[0, :, XATTN_DH * hh:XATTN_DH * (hh + 1)]
        vh = kv_ref[0, :, hd + XATTN_DH * hh:hd + XATTN_DH * (hh + 1)]
        s = _dot_nt(qh, kh)
        m = jnp.max(s, axis=-1, keepdims=True)
        p = jnp.exp(s - m)
        p = p / jnp.sum(p, axis=-1, keepdims=True)
        outs.append(_dot(p.astype(BF16), vh))
    o = jnp.concatenate(outs, axis=-1).astype(BF16)
    o_ref[...] = x + _dot(o, wo_ref[...])


def _xattn(x, g, wq, kv, wo, bsz, seq):
    tm = TOK_TILE
    t, d = x.shape
    nst = seq // tm
    m_len = kv.shape[1]
    hd = XATTN_HEADS * XATTN_DH
    return pl.pallas_call(
        _xattn_kernel,
        out_shape=jax.ShapeDtypeStruct((t, d), F32),
        grid=(bsz, nst),
        in_specs=[pl.BlockSpec((tm, d), lambda b, s: (b * nst + s, 0)),
                  pl.BlockSpec((1, d), lambda b, s: (0, 0)),
                  pl.BlockSpec((d, hd), lambda b, s: (0, 0)),
                  pl.BlockSpec((1, m_len, 2 * hd), lambda b, s: (b, 0, 0)),
                  pl.BlockSpec((hd, d), lambda b, s: (0, 0))],
        out_specs=pl.BlockSpec((tm, d), lambda b, s: (b * nst + s, 0)),
        compiler_params=_cparams(("parallel", "parallel")),
        name="xattn",
    )(x, g[None], wq.astype(BF16), kv, wo.astype(BF16))


def _ffn_kernel(x_ref, g_ref, wg_ref, wu_ref, wd_ref, gf_ref, o_ref, h_ref, acc_ref, *, final):
    c = pl.program_id(1)

    @pl.when(c == 0)
    def _():
        h_ref[...] = _rms(x_ref[...], g_ref[...]).astype(BF16)
        acc_ref[...] = x_ref[...]

    h = h_ref[...]
    gate = _dot(h, wg_ref[...])
    up = _dot(h, wu_ref[...])
    act = (gate * jax.nn.sigmoid(gate) * up).astype(BF16)
    acc_ref[...] += _dot(act, wd_ref[...])

    @pl.when(c == pl.num_programs(1) - 1)
    def _():
        y = acc_ref[...]
        o_ref[...] = _rms(y, gf_ref[...]) if final else y


def _ffn(x, g, w_gate_up, w_down, g_final, final):
    tm, tc = FFN_TM, FFN_TC
    t, d = x.shape
    nc = FFN_HIDDEN // tc
    wgu = w_gate_up.astype(BF16)
    return pl.pallas_call(
        functools.partial(_ffn_kernel, final=final),
        out_shape=jax.ShapeDtypeStruct((t, d), F32),
        grid=(t // tm, nc),
        in_specs=[pl.BlockSpec((tm, d), lambda i, c: (i, 0)),
                  pl.BlockSpec((1, d), lambda i, c: (0, 0)),
                  pl.BlockSpec((d, tc), lambda i, c: (0, c)),
                  pl.BlockSpec((d, tc), lambda i, c: (0, nc + c)),
                  pl.BlockSpec((tc, d), lambda i, c: (c, 0)),
                  pl.BlockSpec((1, d), lambda i, c: (0, 0))],
        out_specs=pl.BlockSpec((tm, d), lambda i, c: (i, 0)),
        scratch_shapes=[pltpu.VMEM((tm, d), BF16), pltpu.VMEM((tm, d), F32)],
        compiler_params=_cparams(("parallel", "arbitrary")),
        name="ffn",
    )(x, g[None], wgu, wgu, w_down.astype(BF16), g_final[None])


def _split_w_in(w):
    k_rope = w[:, O_KR:O_KR + QK_ROPE]
    kv = lambda kind: w[:, O_KV + GD * kind:O_KV + GD * (kind + 1)]
    pad = jnp.zeros((w.shape[0], Z_COLS - Z_GN - 3 * NSA_HEADS), w.dtype)
    wz = jnp.concatenate([
        w[:, O_GM:O_GM + 3 * D_MODEL],
        w[:, O_GLU:O_GLU + 2 * CONV_CH],
        w[:, O_Q:O_Q + NSA_HEADS * NSA_DH],
        w[:, O_CKV:O_CKV + KV_RANK],
        k_rope, _swap_halves(k_rope),
        w[:, O_CQ:O_CQ + Q_RANK],
        kv(0), kv(1),
        w[:, O_GN:O_GN + 3 * NSA_HEADS], pad], axis=1).astype(BF16)
    per_head = lambda wkind: jnp.tile(wkind.reshape(-1, NSA_G, 1, NSA_DH), (1, 1, NSA_HG, 1)).reshape(-1, NSA_G * KREP)
    wk = jnp.concatenate([per_head(kv(2)), per_head(kv(4))], axis=1).astype(BF16)
    wvt = jnp.concatenate([kv(3), kv(5)], axis=1).T.astype(BF16)
    return wz, wk, wvt


def kernel(x, mem, positions, rel_bias, norm_mix, norm_xattn, norm_mem, norm_ffn, norm_final, w_in, conv_w, conv_b, conv_ln_g, conv_ln_b, w_branch_conv, cmp_pos_k, cmp_w1_k, cmp_b1_k, cmp_w2_k, cmp_pos_v, cmp_w1_v, cmp_b1_v, cmp_w2_v, w_branch_nsa, mla_norm_q, mla_norm_kv, w_uq, w_ukv, w_branch_mla, w_out, w_xq, w_xkv, w_xo, w_gate_up, w_down):
    bsz, seq, d = x.shape
    depth = w_in.shape[0]
    t = bsz * seq
    m_len = mem.shape[1]
    xt = x.reshape(t, d)
    memt = mem.reshape(bsz * m_len, d)
    rope_tab = _rope_table(positions)
    tables = _nsa_tables(rel_bias)
    for l in range(depth):
        wz, wk, wvt = _split_w_in(w_in[l])
        z, zc = _in_proj(xt, norm_mix[l][None], wz)
        kk, vvt = _kv_proj(xt, norm_mix[l][None], wk, wvt, bsz, seq)
        ya = _conv_module(z, conv_w[l], conv_b[l], conv_ln_g[l], conv_ln_b[l], bsz, seq)
        kc, vct = _compress(zc, jnp.stack([cmp_pos_k[l], cmp_pos_v[l]]), jnp.stack([cmp_w1_k[l], cmp_w1_v[l]]),
                            jnp.stack([cmp_b1_k[l], cmp_b1_v[l]]), jnp.stack([cmp_w2_k[l], cmp_w2_v[l]]), bsz, seq)
        yb = _nsa_attention(z, kc, vct, kk, vvt, tables, bsz, seq)
        qf, kf, vt = _mla_proj(z, rope_tab, mla_norm_q[l], mla_norm_kv[l], w_uq[l], w_ukv[l], bsz, seq)
        yc = _mla_attention(qf, kf, vt, bsz, seq)
        xt = _merge(ya, yb, yc, z, xt, w_branch_conv[l], w_branch_nsa[l], w_branch_mla[l], w_out[l])
        mem_kv = _norm_matmul(memt, norm_mem[l][None], w_xkv[l].astype(BF16), 256, 1024, BF16)
        mem_kv = mem_kv.reshape(bsz, m_len, 2 * XATTN_HEADS * XATTN_DH)
        xt = _xattn(xt, norm_xattn[l], w_xq[l], mem_kv, w_xo[l], bsz, seq)
        xt = _ffn(xt, norm_ffn[l], w_gate_up[l], w_down[l], norm_final, l == depth - 1)
    return xt.reshape(bsz, seq, d)
```

```python
import functools
import math

import numpy as np
import jax
import jax.numpy as jnp
from jax import lax
from jax.experimental import pallas as pl
from jax.experimental.pallas import tpu as pltpu

F32 = jnp.float32
BF16 = jnp.bfloat16

EPS = 1e-6
NEG_INF = -1e30
FORCE_SCORE = 1e4

D_MODEL = 1024
CONV_CH = 512
CONV_WIDTH = 31
NSA_HEADS = 8
NSA_G = 2
NSA_HG = NSA_HEADS // NSA_G
NSA_DH = 64
CMP_BLOCK = 32
CMP_STRIDE = 16
CMP_HIDDEN = 256
SLC_BLOCK = 64
N_SELECT = 16
WINDOW = 512
NSA_QB = 64
MLA_HEADS = 4
Q_RANK = 384
KV_RANK = 256
QK_NOPE = 128
QK_ROPE = 64
V_DIM = 128
ROPE_THETA = 10000.0
REL_BUCKETS = 32
REL_MAX_DIST = 128
XATTN_HEADS = 4
XATTN_DH = 128
FFN_HIDDEN = 2816

LANES = 128
SUBLANES = 8

O_GLU, O_Q, O_KV, O_GN, O_CQ, O_CKV, O_KR, O_GM = 0, 1024, 1536, 2304, 2328, 2712, 2968, 3032
GD = NSA_G * NSA_DH

Z_GM = 0
Z_UA = 3072
Z_UB = 3584
Z_Q = 4096
Z_CKV = 4608
Z_KR = 4864
Z_CQ = 4992
Z_CMP = 5376
Z_GN = 5632
Z_COLS = 5760

VMEM_LIMIT = 56 * 1024 * 1024

TOK_TILE = 512
IN_PROJ_TM = 1024
IN_PROJ_TN = 1152
FFN_TM = 1024
FFN_TC = 256
MLA_TQ = 512
MLA_TK = 1024
MLA_HEADS_PER_STEP = 2

LOG2E = math.log2(math.e)


def _cparams(sem):
    return pltpu.CompilerParams(dimension_semantics=sem, vmem_limit_bytes=VMEM_LIMIT)


def _rms(x, g):
    return x * lax.rsqrt(jnp.mean(x * x, axis=-1, keepdims=True) + EPS) * g


def _dot(a, b):
    return jnp.dot(a, b, preferred_element_type=F32)


def _dot_nt(a, b):
    return lax.dot_general(a, b, (((1,), (1,)), ((), ())), preferred_element_type=F32)


def _norm_matmul_kernel(x_ref, g_ref, w_ref, o_ref, h_ref):
    @pl.when(pl.program_id(1) == 0)
    def _():
        h_ref[...] = _rms(x_ref[...], g_ref[...]).astype(BF16)

    o_ref[...] = _dot(h_ref[...], w_ref[...]).astype(o_ref.dtype)


def _norm_matmul(x, g, w, tm, tn, out_dtype):
    m, k = x.shape
    n = w.shape[1]
    return pl.pallas_call(
        _norm_matmul_kernel,
        out_shape=jax.ShapeDtypeStruct((m, n), out_dtype),
        grid=(m // tm, n // tn),
        in_specs=[pl.BlockSpec((tm, k), lambda i, j: (i, 0)),
                  pl.BlockSpec((1, k), lambda i, j: (0, 0)),
                  pl.BlockSpec((k, tn), lambda i, j: (0, j))],
        out_specs=pl.BlockSpec((tm, tn), lambda i, j: (i, j)),
        scratch_shapes=[pltpu.VMEM((tm, k), BF16)],
        compiler_params=_cparams(("parallel", "arbitrary")),
        name="norm_matmul",
    )(x, g, w)


def _in_proj_kernel(x_ref, g_ref, w_ref, z_ref, zc_ref, h_ref, *, cmp_tile, cmp_off):
    @pl.when(pl.program_id(1) == 0)
    def _():
        h_ref[...] = _rms(x_ref[...], g_ref[...]).astype(BF16)

    acc = _dot(h_ref[...], w_ref[...])
    z_ref[...] = acc.astype(BF16)

    @pl.when(pl.program_id(1) == cmp_tile)
    def _():
        zc_ref[...] = acc[:, cmp_off:cmp_off + 2 * GD]


def _in_proj(x, g, w):
    tm, tn = IN_PROJ_TM, IN_PROJ_TN
    m, k = x.shape
    n = w.shape[1]
    return pl.pallas_call(
        functools.partial(_in_proj_kernel, cmp_tile=Z_CMP // tn, cmp_off=Z_CMP % tn),
        out_shape=(jax.ShapeDtypeStruct((m, n), BF16), jax.ShapeDtypeStruct((m, 2 * GD), F32)),
        grid=(m // tm, n // tn),
        in_specs=[pl.BlockSpec((tm, k), lambda i, j: (i, 0)),
                  pl.BlockSpec((1, k), lambda i, j: (0, 0)),
                  pl.BlockSpec((k, tn), lambda i, j: (0, j))],
        out_specs=(pl.BlockSpec((tm, tn), lambda i, j: (i, j)),
                   pl.BlockSpec((tm, 2 * GD), lambda i, j: (i, 0))),
        scratch_shapes=[pltpu.VMEM((tm, k), BF16)],
        compiler_params=_cparams(("parallel", "arbitrary")),
        name="in_proj",
    )(x, g, w)


def _kv_proj_kernel(x_ref, g_ref, wk_ref, wvt_ref, k_ref, vt_ref):
    @pl.when(pl.program_id(1) == 0)
    def _():
        k_ref[...] = jnp.zeros(k_ref.shape, BF16)
        vt_ref[...] = jnp.zeros(vt_ref.shape, BF16)

    @pl.when(pl.program_id(1) > 0)
    def _():
        h = _rms(x_ref[...], g_ref[...]).astype(BF16)
        k_ref[0] = _dot(h, wk_ref[...]).astype(BF16)
        vt_ref[0] = _dot_nt(wvt_ref[...], h).astype(BF16)


def _kv_proj(x, g, wk, wvt, bsz, seq):
    tm = WINDOW
    nst = seq // tm
    d = x.shape[1]
    nk = wk.shape[1]
    nv = wvt.shape[0]
    return pl.pallas_call(
        _kv_proj_kernel,
        out_shape=(jax.ShapeDtypeStruct((bsz, seq + tm, nk), BF16),
                   jax.ShapeDtypeStruct((bsz, nv, seq + tm), BF16)),
        grid=(bsz, nst + 1),
        in_specs=[pl.BlockSpec((tm, d), lambda b, s: (b * nst + jnp.maximum(s - 1, 0), 0)),
                  pl.BlockSpec((1, d), lambda b, s: (0, 0)),
                  pl.BlockSpec((d, nk), lambda b, s: (0, 0)),
                  pl.BlockSpec((nv, d), lambda b, s: (0, 0))],
        out_specs=(pl.BlockSpec((1, tm, nk), lambda b, s: (b, s, 0)),
                   pl.BlockSpec((1, nv, tm), lambda b, s: (b, 0, s))),
        compiler_params=_cparams(("parallel", "arbitrary")),
        name="nsa_kv_proj",
    )(x, g, wk, wvt)


CONV_HALO = 32


CONV_ROWS = 64


def _conv_kernel(a_ref, b_ref, w_ref, cb_ref, lg_ref, lb_ref, o_ref, buf_ref, sh_ref, *, ts):
    @pl.when(pl.program_id(1) == 0)
    def _():
        buf_ref[0:CONV_HALO, :] = jnp.zeros((CONV_HALO, CONV_CH), F32)

    buf_ref[CONV_HALO:CONV_HALO + ts, :] = a_ref[...].astype(F32) * jax.nn.sigmoid(b_ref[...].astype(F32))
    span = ts + CONV_HALO - SUBLANES
    for r in range(1, SUBLANES):
        sh_ref[r - 1, 0:span, :] = buf_ref[r:r + span, :]
    off = CONV_HALO - (CONV_WIDTH - 1)

    def rows(i, carry):
        r0 = pl.multiple_of(i * CONV_ROWS, CONV_ROWS)
        acc = jnp.zeros((CONV_ROWS, CONV_CH), F32) + cb_ref[...]
        for k in range(CONV_WIDTH):
            res, base = (off + k) % SUBLANES, (off + k) // SUBLANES * SUBLANES
            if res == 0:
                tap = buf_ref[pl.ds(r0 + base, CONV_ROWS), :]
            else:
                tap = sh_ref[res - 1, pl.ds(r0 + base, CONV_ROWS), :]
            acc = acc + tap * w_ref[k:k + 1, :]
        mu = jnp.mean(acc, axis=-1, keepdims=True)
        xc = acc - mu
        var = jnp.mean(xc * xc, axis=-1, keepdims=True)
        y = xc * lax.rsqrt(var + EPS) * lg_ref[...] + lb_ref[...]
        o_ref[pl.ds(r0, CONV_ROWS), :] = (y * jax.nn.sigmoid(y)).astype(BF16)
        return carry

    lax.fori_loop(0, ts // CONV_ROWS, rows, 0)
    buf_ref[0:CONV_HALO, :] = buf_ref[ts:ts + CONV_HALO, :]


def _conv_module(z, conv_w, conv_b, ln_g, ln_b, bsz, seq):
    ts = TOK_TILE
    nst = seq // ts
    wpad = jnp.zeros((32, CONV_CH), F32).at[:CONV_WIDTH].set(conv_w)
    return pl.pallas_call(
        functools.partial(_conv_kernel, ts=ts),
        out_shape=jax.ShapeDtypeStruct((bsz * seq, CONV_CH), BF16),
        grid=(bsz, nst),
        in_specs=[pl.BlockSpec((ts, CONV_CH), lambda b, s: (b * nst + s, Z_UA // CONV_CH)),
                  pl.BlockSpec((ts, CONV_CH), lambda b, s: (b * nst + s, Z_UB // CONV_CH)),
                  pl.BlockSpec((32, CONV_CH), lambda b, s: (0, 0)),
                  pl.BlockSpec((1, CONV_CH), lambda b, s: (0, 0)),
                  pl.BlockSpec((1, CONV_CH), lambda b, s: (0, 0)),
                  pl.BlockSpec((1, CONV_CH), lambda b, s: (0, 0))],
        out_specs=pl.BlockSpec((ts, CONV_CH), lambda b, s: (b * nst + s, 0)),
        scratch_shapes=[pltpu.VMEM((ts + CONV_HALO, CONV_CH), F32),
                        pltpu.VMEM((SUBLANES - 1, ts + CONV_HALO - SUBLANES, CONV_CH), F32)],
        compiler_params=_cparams(("arbitrary", "arbitrary")),
        name="conv_module",
    )(z, z, wpad, conv_b[None], ln_g[None], ln_b[None])


def _compress_kernel(xk_ref, xv_ref, pos_ref, w1_ref, b1_ref, w2k_ref, w2v_ref, kc_ref, vct_ref, *, nch):
    for kind, (x_ref, w2_ref) in enumerate(((xk_ref, w2k_ref), (xv_ref, w2v_ref))):
        a = jnp.zeros((nch, NSA_G * CMP_HIDDEN), F32)
        b = jnp.zeros((nch, NSA_G * CMP_HIDDEN), F32)
        for l in range(CMP_STRIDE):
            xs = x_ref[pl.ds(l, nch, stride=CMP_STRIDE), :]
            a = a + _dot((xs + pos_ref[kind, l:l + 1, :]).astype(BF16), w1_ref[kind, l])
            b = b + _dot((xs + pos_ref[kind, CMP_STRIDE + l:CMP_STRIDE + l + 1, :]).astype(BF16),
                         w1_ref[kind, CMP_STRIDE + l])
        pre = a + pltpu.roll(b, nch - 1, 0) + b1_ref[kind]
        out = _dot(jax.nn.gelu(pre).astype(BF16), w2_ref[...])
        if kind == 0:
            kc_ref[0] = out.astype(BF16)
        else:
            vct_ref[0] = out.T.astype(BF16)


def _blockdiag2(w):
    z = jnp.zeros_like(w)
    return jnp.concatenate([jnp.concatenate([w, z], axis=-1), jnp.concatenate([z, w], axis=-1)], axis=-2)


def _compress(z, pos, w1, b1, w2, bsz, seq):
    nch = seq // CMP_STRIDE
    pos2 = jnp.concatenate([pos, pos], axis=-1)
    w1e = _blockdiag2(w1.reshape(2, CMP_BLOCK, NSA_DH, CMP_HIDDEN)).astype(BF16)
    b1e = jnp.concatenate([b1, b1], axis=-1)[:, None]
    w2k = _blockdiag2(jnp.tile(w2[0], (1, NSA_HG))).astype(BF16)
    w2v = _blockdiag2(w2[1]).astype(BF16)
    full = lambda a: pl.BlockSpec(a.shape, lambda b: (0,) * a.ndim)
    return pl.pallas_call(
        functools.partial(_compress_kernel, nch=nch),
        out_shape=(jax.ShapeDtypeStruct((bsz, nch, NSA_G * NSA_HG * NSA_DH), BF16),
                   jax.ShapeDtypeStruct((bsz, GD, nch), BF16)),
        grid=(bsz,),
        in_specs=[pl.BlockSpec((seq, GD), lambda b: (b, 0)),
                  pl.BlockSpec((seq, GD), lambda b: (b, 1)),
                  full(pos2), full(w1e), full(b1e), full(w2k), full(w2v)],
        out_specs=(pl.BlockSpec((1, nch, NSA_G * NSA_HG * NSA_DH), lambda b: (b, 0, 0)),
                   pl.BlockSpec((1, GD, nch), lambda b: (b, 0, 0))),
        compiler_params=_cparams(("parallel",)),
        name="nsa_compress",
    )(z, z, pos2, w1e, b1e, w2k, w2v)


NSA_QP = 2 * NSA_QB
NEAR_KEYS = 256
WIN_KEYS = 640
CMP_TAB_ROWS = 512
CMP_TAB_ZERO = 256


def _t5_bucket_np(d):
    exact = REL_BUCKETS // 2
    d = np.maximum(d, 0)
    ratio = np.log(np.maximum(d, 1).astype(np.float32) / np.float32(exact)) / np.float32(math.log(REL_MAX_DIST / exact))
    large = np.minimum(exact + (ratio * (REL_BUCKETS - exact)).astype(np.int32), REL_BUCKETS - 1)
    return np.where(d < exact, d, large).astype(np.int32)


def _bucket_thresholds():
    exact = REL_BUCKETS // 2
    bk = _t5_bucket_np(np.arange(4 * REL_MAX_DIST))
    assert np.all(np.diff(bk) >= 0) and bk[-1] == REL_BUCKETS - 1
    return [int(np.argmax(bk >= k)) for k in range(exact + 1, REL_BUCKETS)]


def _bias_rows(rel_ref, dist, valid, shift):
    exact = REL_BUCKETS // 2
    bucket = jnp.full(dist.shape, exact, jnp.int32)
    for thr in _bucket_thresholds():
        bucket = bucket + jnp.where(dist >= thr, 1, 0)
    bucket = jnp.where(dist < exact, dist, bucket)
    val = jnp.zeros(dist.shape, F32)
    for bkt in range(REL_BUCKETS):
        val = jnp.where(bucket == bkt, rel_ref[0, bkt:bkt + 1, :], val)
    if shift:
        val = val - rel_ref[0, REL_BUCKETS - 1:REL_BUCKETS, :]
    return jnp.where(valid, val * LOG2E, NEG_INF)


def _nsa_bias_kernel(rel_ref, tc_ref, tn_ref, tw_ref):
    hq = NSA_HG * NSA_QP
    rows = 128

    def dist_of(nrows, r0, fn):
        r = r0 + lax.broadcasted_iota(jnp.int32, (nrows, hq), 0)
        t = lax.bitwise_and(lax.broadcasted_iota(jnp.int32, (nrows, hq), 1), NSA_QP - 1)
        return fn(r, t)

    for r0 in range(0, CMP_TAB_ROWS, rows):
        d = dist_of(rows, r0, lambda r, t: t - CMP_STRIDE * (r - CMP_TAB_ZERO) - (CMP_BLOCK - 1))
        tc_ref[0, r0:r0 + rows, :] = _bias_rows(rel_ref, d, d >= 0, False)
    for r0 in range(0, NEAR_KEYS, rows):
        d = dist_of(rows, r0, lambda r, t: NEAR_KEYS // 2 + t - r)
        tn_ref[0, r0:r0 + rows, :] = _bias_rows(rel_ref, d, d >= 0, True)
    for r0 in range(0, WIN_KEYS, rows):
        d = dist_of(rows, r0, lambda r, t: WINDOW + t - r)
        tw_ref[0, r0:r0 + rows, :] = _bias_rows(rel_ref, d, (d >= 0) & (d < WINDOW), False)


def _nsa_tables(rel_bias):
    hq = NSA_HG * NSA_QP
    rel4 = jnp.repeat(rel_bias.reshape(REL_BUCKETS, NSA_G, NSA_HG).transpose(1, 0, 2), NSA_QP, axis=-1)
    spec = lambda r: pl.BlockSpec((1, r, hq), lambda g: (g, 0, 0))
    return pl.pallas_call(
        _nsa_bias_kernel,
        out_shape=(jax.ShapeDtypeStruct((NSA_G, CMP_TAB_ROWS, hq), F32),
                   jax.ShapeDtypeStruct((NSA_G, NEAR_KEYS, hq), F32),
                   jax.ShapeDtypeStruct((NSA_G, WIN_KEYS, hq), F32)),
        grid=(NSA_G,),
        in_specs=[spec(REL_BUCKETS)],
        out_specs=(spec(CMP_TAB_ROWS), spec(NEAR_KEYS), spec(WIN_KEYS)),
        compiler_params=_cparams(("parallel",)),
        name="nsa_bias_tables",
    )(rel4)


SEL_PAD = 8
FAR_KEYS = 1024
KV_FRONT = WINDOW
KREP = NSA_HG * NSA_DH


def _softmax_cols(s):
    m = jnp.max(s, axis=0, keepdims=True)
    p = jnp.exp2(s - m)
    return m, p, jnp.sum(p, axis=0, keepdims=True)


def _mask_blocks(s, mask_ref, row0, nblk):
    parts = []
    for jj in range(nblk):
        row = mask_ref[pl.ds(row0 + jj, 1), :]
        parts.append(jnp.where(row > 0.0, s[SLC_BLOCK * jj:SLC_BLOCK * (jj + 1)], NEG_INF))
    return jnp.concatenate(parts, axis=0)


def _rank_select(score_ref, n_sb, n_sel):
    groups = n_sb // SUBLANES
    sub = lax.broadcasted_iota(jnp.int32, (SUBLANES, NSA_QP), 0)
    tiles = [score_ref[SUBLANES * v:SUBLANES * (v + 1), :] for v in range(groups)]
    cnts = [jnp.zeros((SUBLANES, NSA_QP), F32) for _ in range(groups)]
    for jp in range(n_sb):
        row = score_ref[jp:jp + 1, :]
        for v in range(groups):
            lo = SUBLANES * v
            if jp < lo:
                beats = row >= tiles[v]
            elif jp >= lo + SUBLANES - 1:
                beats = row > tiles[v]
            else:
                beats = (row > tiles[v]) | ((row == tiles[v]) & (sub > jp - lo))
            cnts[v] = cnts[v] + jnp.where(beats, 1.0, 0.0)
    cnt = jnp.concatenate(cnts, axis=0)
    return jnp.where(cnt < float(n_sel), 1.0, 0.0)


def _nsa_kernel(q_ref, gate_ref, kc_ref, vct_ref, ks_ref, kw_ref, vst_ref, vwt_ref,
                tc_ref, tn_ref, tw_ref, cov_ref, rep_ref, o_ref,
                sel_ref, selfar_ref, score_ref, *, n_sb):
    p2 = pl.program_id(1)
    hq = NSA_HG * NSA_QP
    nch = kc_ref.shape[1]
    groups = range(NSA_G)
    kcol = lambda g: slice(KREP * g, KREP * (g + 1))
    vrow = lambda g: slice(NSA_DH * g, NSA_DH * (g + 1))

    qs = []
    for g in groups:
        qb = (q_ref[:, kcol(g)].astype(F32) * (NSA_DH ** -0.5 * LOG2E)).astype(BF16)
        lane_head = lax.shift_right_logical(lax.broadcasted_iota(jnp.int32, qb.shape, 1), 6)
        qs.append(jnp.concatenate([jnp.where(lane_head == h, qb, jnp.zeros_like(qb)) for h in range(NSA_HG)], axis=0))

    start_c = pl.multiple_of(CMP_TAB_ZERO - (NSA_QP // CMP_STRIDE) * p2, SUBLANES)
    lane = lax.broadcasted_iota(jnp.int32, (1, hq), 1)
    tq = NSA_QP * p2 + lax.bitwise_and(lane, NSA_QP - 1)
    anyv = jnp.where(tq >= CMP_BLOCK - 1, 1.0, 0.0)
    jrow = lax.broadcasted_iota(jnp.int32, (n_sb, NSA_QP), 0)
    tok = lax.broadcasted_iota(jnp.int32, (n_sb, NSA_QP), 1)
    cur = 2 * p2 + lax.shift_right_logical(tok, 6)
    forced = (jrow == 0) | (jrow == cur) | (jrow == cur - 1)
    o_cmp = []
    for g in groups:
        sc = _dot_nt(kc_ref[0, :, kcol(g)], qs[g]) + tc_ref[g, pl.ds(start_c, nch), :]
        _, pc, lc = _softmax_cols(sc)
        pc = pc * (anyv / lc)
        o_cmp.append(_dot(vct_ref[0, vrow(g), :], pc.astype(BF16)))
        psum = pc[:, 0:NSA_QP]
        for h in range(1, NSA_HG):
            psum = psum + pc[:, NSA_QP * h:NSA_QP * (h + 1)]
        p_hi = psum.astype(BF16)
        p_lo = (psum - p_hi.astype(F32)).astype(BF16)
        imp = _dot(cov_ref[...], p_hi) + _dot(cov_ref[...], p_lo)
        score_ref[g] = jnp.where(forced, FORCE_SCORE, jnp.where(jrow <= cur, imp, -1.0))

    zeros8 = jnp.zeros((SEL_PAD, hq), F32)
    jrow4 = lax.broadcasted_iota(jnp.int32, (n_sb, hq), 0)
    for g in groups:
        sel = _rank_select(score_ref.at[g], n_sb, min(N_SELECT, n_sb)).astype(BF16)
        sel4 = _dot(sel, rep_ref[...])
        sel_ref[g, 0:SEL_PAD, :] = zeros8
        sel_ref[g, SEL_PAD + n_sb:2 * SEL_PAD + n_sb, :] = zeros8
        sel_ref[g, SEL_PAD:SEL_PAD + n_sb, :] = sel4
        selfar_ref[g, 0:SEL_PAD, :] = zeros8
        selfar_ref[g, SEL_PAD + n_sb:2 * SEL_PAD + n_sb, :] = zeros8
        selfar_ref[g, SEL_PAD:SEL_PAD + n_sb, :] = jnp.where(jrow4 < 2 * p2 - 2, sel4, 0.0)

    win0 = pl.multiple_of(NSA_QP * p2, LANES)
    near0 = pl.multiple_of(win0 + KV_FRONT - NEAR_KEYS // 2, LANES)
    state = []
    for g in groups:
        s = _dot_nt(ks_ref[0, pl.ds(near0, NEAR_KEYS), kcol(g)], qs[g]) + tn_ref[g]
        s = _mask_blocks(s, sel_ref.at[g], 2 * p2 - 2 + SEL_PAD, NEAR_KEYS // SLC_BLOCK)
        m_s, p_s, l_s = _softmax_cols(s)
        state += [m_s, l_s, _dot(vst_ref[0, vrow(g), pl.ds(near0, NEAR_KEYS)], p_s.astype(BF16))]

    def far_body(c, carry):
        k0 = pl.multiple_of(FAR_KEYS * c + KV_FRONT, LANES)
        sfs = [_dot_nt(ks_ref[0, pl.ds(k0, FAR_KEYS), kcol(g)], qs[g]) for g in groups]
        out = []
        for g in groups:
            m_old, l_old, acc_old = carry[3 * g:3 * g + 3]
            sf = _mask_blocks(sfs[g], selfar_ref.at[g], (FAR_KEYS // SLC_BLOCK) * c + SEL_PAD, FAR_KEYS // SLC_BLOCK)
            m_new = jnp.maximum(m_old, jnp.max(sf, axis=0, keepdims=True))
            alpha = jnp.exp2(m_old - m_new)
            pf = jnp.exp2(sf - m_new)
            l_new = alpha * l_old + jnp.sum(pf, axis=0, keepdims=True)
            acc_new = alpha * acc_old + _dot(vst_ref[0, vrow(g), pl.ds(k0, FAR_KEYS)], pf.astype(BF16))
            out += [m_new, l_new, acc_new]
        return tuple(out)

    n_far = lax.div(jnp.maximum(p2 - 1, 0) * NSA_QP + FAR_KEYS - 1, FAR_KEYS)
    state = lax.fori_loop(0, n_far, far_body, tuple(state))

    gt = gate_ref[...].astype(F32).T
    r = lax.broadcasted_iota(jnp.int32, (NSA_QP, NSA_QP), 0)
    c = lax.broadcasted_iota(jnp.int32, (NSA_QP, NSA_QP), 1)
    eye = jnp.where(r == c, 1.0, 0.0).astype(BF16)
    for g in groups:
        _, l_s, acc_s = state[3 * g:3 * g + 3]
        sw = _dot_nt(kw_ref[0, pl.ds(win0, WIN_KEYS), kcol(g)], qs[g]) + tw_ref[g]
        slabs = [sw[NSA_QP * j:NSA_QP * (j + 1)] for j in range(WIN_KEYS // NSA_QP)]
        for j in range(KV_FRONT // NSA_QP):
            slabs[j] = jnp.where(NSA_QP * j + win0 >= KV_FRONT, slabs[j], NEG_INF)
        _, p_w, l_w = _softmax_cols(jnp.concatenate(slabs, axis=0))
        acc_w = _dot(vwt_ref[0, vrow(g), pl.ds(win0, WIN_KEYS)], p_w.astype(BF16))
        gsel = jax.nn.sigmoid(gt[3 * NSA_HG * g:3 * NSA_HG * (g + 1)])
        gate = lambda b: jnp.concatenate([gsel[3 * h + b:3 * h + b + 1] for h in range(NSA_HG)], axis=1)
        out_t = (gate(0) * o_cmp[g] + (gate(1) / l_s) * acc_s + (gate(2) / l_w) * acc_w).astype(BF16)
        stacked = jnp.concatenate([out_t[:, NSA_QP * h:NSA_QP * (h + 1)] for h in range(NSA_HG)], axis=0)
        o_ref[:, kcol(g)] = _dot_nt(eye, stacked).astype(BF16)


def _nsa_attention(z, kc, vct, kk, vvt, tables, bsz, seq):
    g, hg, dh, qp = NSA_G, NSA_HG, NSA_DH, NSA_QP
    nstep = seq // qp
    n_sb = seq // SLC_BLOCK
    nch = kc.shape[1]
    hq = hg * qp
    sp = kk.shape[1]
    tc, tn, tw = tables
    c_start = CMP_STRIDE * np.arange(nch)
    s_start = SLC_BLOCK * np.arange(n_sb)
    cover_t = ((c_start[None, :] < s_start[:, None] + SLC_BLOCK)
               & (c_start[None, :] + CMP_BLOCK > s_start[:, None])
               & (np.arange(nch)[None, :] < (seq - CMP_BLOCK) // CMP_STRIDE + 1))
    cover_t = jnp.asarray(cover_t.astype(np.float32), BF16)
    rep = jnp.asarray(np.tile(np.eye(qp, dtype=np.float32), (1, hg)), BF16)
    full = lambda a: pl.BlockSpec(a.shape, lambda b, i: (0,) * a.ndim, pipeline_mode=pl.Buffered(1))
    qd = g * hg * dh
    return pl.pallas_call(
        functools.partial(_nsa_kernel, n_sb=n_sb),
        out_shape=jax.ShapeDtypeStruct((bsz * seq, qd), BF16),
        grid=(bsz, nstep),
        in_specs=[pl.BlockSpec((qp, qd), lambda b, i: (b * nstep + i, Z_Q // qd)),
                  pl.BlockSpec((qp, LANES), lambda b, i: (b * nstep + i, Z_GN // LANES)),
                  pl.BlockSpec((1, nch, g * KREP), lambda b, i: (b, 0, 0)),
                  pl.BlockSpec((1, g * dh, nch), lambda b, i: (b, 0, 0)),
                  pl.BlockSpec((1, sp, g * KREP), lambda b, i: (b, 0, 0)),
                  pl.BlockSpec((1, sp, g * KREP), lambda b, i: (b, 0, 1)),
                  pl.BlockSpec((1, g * dh, sp), lambda b, i: (b, 0, 0)),
                  pl.BlockSpec((1, g * dh, sp), lambda b, i: (b, 1, 0)),
                  full(tc), full(tn), full(tw), full(cover_t), full(rep)],
        out_specs=pl.BlockSpec((qp, qd), lambda b, i: (b * nstep + i, 0)),
        scratch_shapes=[pltpu.VMEM((g, n_sb + 2 * SEL_PAD, hq), F32),
                        pltpu.VMEM((g, n_sb + 2 * SEL_PAD, hq), F32),
                        pltpu.VMEM((g, n_sb, qp), F32)],
        compiler_params=_cparams(("parallel", "arbitrary")),
        name="nsa_attention",
    )(z, z, kc, vct, kk, kk, vvt, vvt, tc, tn, tw, cover_t, rep)


def _rope_table_kernel(pos_ref, inv_ref, sign_ref, o_ref):
    ang = pos_ref[...].astype(F32) * inv_ref[...]
    o_ref[...] = jnp.concatenate([jnp.cos(ang), jnp.sin(ang) * sign_ref[...]], axis=-1)


def _rope_table(positions):
    tm = TOK_TILE
    t = positions.size
    half = QK_ROPE // 2
    inv = ROPE_THETA ** (-jnp.arange(half, dtype=F32) / half)
    inv2 = jnp.concatenate([inv, inv])[None]
    sign = jnp.asarray(np.concatenate([-np.ones(half), np.ones(half)]).astype(np.float32))[None]
    return pl.pallas_call(
        _rope_table_kernel,
        out_shape=jax.ShapeDtypeStruct((t, 2 * QK_ROPE), F32),
        grid=(t // tm,),
        in_specs=[pl.BlockSpec((tm, 1), lambda i: (i, 0)),
                  pl.BlockSpec((1, QK_ROPE), lambda i: (0, 0)),
                  pl.BlockSpec((1, QK_ROPE), lambda i: (0, 0))],
        out_specs=pl.BlockSpec((tm, 2 * QK_ROPE), lambda i: (i, 0)),
        compiler_params=_cparams(("parallel",)),
        name="rope_table",
    )(positions.reshape(t, 1), inv2, sign)


MLA_HW = 256


def _mla_proj_kernel(cq_ref, ckv_ref, kr_ref, rope_ref, nq_ref, nkv_ref, wq_ref, wkn_ref, wvt_ref,
                     q_ref, k_ref, vt_ref):
    scale = (QK_NOPE + QK_ROPE) ** -0.5 * LOG2E
    rope = rope_ref[...]
    yq = _dot(_rms(cq_ref[...].astype(F32), nq_ref[...]).astype(BF16), wq_ref[...])
    ckv = _rms(ckv_ref[...].astype(F32), nkv_ref[...]).astype(BF16)
    ykn = _dot(ckv, wkn_ref[...])
    vt_ref[0] = _dot_nt(wvt_ref[...], ckv).astype(BF16)
    kp = kr_ref[...].astype(F32) * rope
    kp = kp + pltpu.roll(kp, QK_ROPE, 1)
    lane = lax.broadcasted_iota(jnp.int32, kp.shape, 1)
    kp = jnp.where(lane < QK_ROPE, kp, 0.0).astype(BF16)
    for h in range(MLA_HEADS):
        base = MLA_HW * h
        q_ref[:, base:base + QK_NOPE] = (yq[:, base:base + QK_NOPE] * scale).astype(BF16)
        qp = yq[:, base + QK_NOPE:base + MLA_HW] * rope
        qp = qp + pltpu.roll(qp, QK_ROPE, 1)
        q_ref[:, base + QK_NOPE:base + MLA_HW] = (qp * scale).astype(BF16)
        k_ref[:, base:base + QK_NOPE] = ykn[:, QK_NOPE * h:QK_NOPE * (h + 1)].astype(BF16)
        k_ref[:, base + QK_NOPE:base + MLA_HW] = kp


def _swap_halves(w):
    half = QK_ROPE // 2
    return jnp.concatenate([w[..., half:], w[..., :half]], axis=-1)


def _mla_proj(z, rope_tab, norm_q, norm_kv, w_uq, w_ukv, bsz, seq):
    tm = TOK_TILE
    t = z.shape[0]
    nst = seq // tm
    wq = w_uq.reshape(Q_RANK, MLA_HEADS, QK_NOPE + QK_ROPE)
    wq = jnp.concatenate([wq, _swap_halves(wq[..., QK_NOPE:])], axis=-1)
    wq = wq.reshape(Q_RANK, MLA_HEADS * MLA_HW).astype(BF16)
    wkv = w_ukv.reshape(KV_RANK, MLA_HEADS, QK_NOPE + V_DIM)
    wkn = wkv[..., :QK_NOPE].reshape(KV_RANK, MLA_HEADS * QK_NOPE).astype(BF16)
    wvt = wkv[..., QK_NOPE:].reshape(KV_RANK, MLA_HEADS * V_DIM).T.astype(BF16)
    hw = MLA_HEADS * MLA_HW
    hv = MLA_HEADS * V_DIM
    row = lambda b, s: b * nst + s
    return pl.pallas_call(
        _mla_proj_kernel,
        out_shape=(jax.ShapeDtypeStruct((t, hw), BF16),
                   jax.ShapeDtypeStruct((t, hw), BF16),
                   jax.ShapeDtypeStruct((bsz, hv, seq), BF16)),
        grid=(bsz, nst),
        in_specs=[pl.BlockSpec((tm, Q_RANK), lambda b, s: (row(b, s), Z_CQ // Q_RANK)),
                  pl.BlockSpec((tm, KV_RANK), lambda b, s: (row(b, s), Z_CKV // KV_RANK)),
                  pl.BlockSpec((tm, 2 * QK_ROPE), lambda b, s: (row(b, s), Z_KR // (2 * QK_ROPE))),
                  pl.BlockSpec((tm, 2 * QK_ROPE), lambda b, s: (row(b, s), 0)),
                  pl.BlockSpec((1, Q_RANK), lambda b, s: (0, 0)),
                  pl.BlockSpec((1, KV_RANK), lambda b, s: (0, 0)),
                  pl.BlockSpec((Q_RANK, hw), lambda b, s: (0, 0)),
                  pl.BlockSpec((KV_RANK, hv), lambda b, s: (0, 0)),
                  pl.BlockSpec((hv, KV_RANK), lambda b, s: (0, 0))],
        out_specs=(pl.BlockSpec((tm, hw), lambda b, s: (row(b, s), 0)),
                   pl.BlockSpec((tm, hw), lambda b, s: (row(b, s), 0)),
                   pl.BlockSpec((1, hv, tm), lambda b, s: (b, 0, s))),
        compiler_params=_cparams(("parallel", "parallel")),
        name="mla_proj",
    )(z, z, z, rope_tab, norm_q[None], norm_kv[None], wq, wkn, wvt)


def _mla_attn_kernel(q_ref, k_ref, vt_ref, o_ref, *, tq, tk, nh):
    iq = pl.program_id(2)
    cd = lax.div(iq * tq, tk)
    heads = range(nh)
    hcol = lambda h: slice(MLA_HW * h, MLA_HW * (h + 1))
    vrow = lambda h: slice(V_DIM * h, V_DIM * (h + 1))
    qs = [q_ref[:, hcol(h)] for h in heads]

    def scores(c, h):
        k0 = pl.multiple_of(c * tk, tk)
        return _dot_nt(k_ref[0, pl.ds(k0, tk), hcol(h)], qs[h])

    k0 = pl.multiple_of(cd * tk, tk)
    kpos = k0 + lax.broadcasted_iota(jnp.int32, (tk, tq), 0)
    qpos = iq * tq + lax.broadcasted_iota(jnp.int32, (tk, tq), 1)
    state = []
    for h in heads:
        s = jnp.where(kpos <= qpos, scores(cd, h), NEG_INF)
        m0, p0, l0 = _softmax_cols(s)
        state += [m0, l0, _dot(vt_ref[0, vrow(h), pl.ds(k0, tk)], p0.astype(BF16))]

    def body(c, carry):
        k0 = pl.multiple_of(c * tk, tk)
        ss = [scores(c, h) for h in heads]
        out = []
        for h in heads:
            m_old, l_old, acc_old = carry[3 * h:3 * h + 3]
            m_new = jnp.maximum(m_old, jnp.max(ss[h], axis=0, keepdims=True))
            alpha = jnp.exp2(m_old - m_new)
            p = jnp.exp2(ss[h] - m_new)
            l_new = alpha * l_old + jnp.sum(p, axis=0, keepdims=True)
            acc_new = alpha * acc_old + _dot(vt_ref[0, vrow(h), pl.ds(k0, tk)], p.astype(BF16))
            out += [m_new, l_new, acc_new]
        return tuple(out)

    state = lax.fori_loop(0, cd, body, tuple(state))
    for h in heads:
        _, l, acc = state[3 * h:3 * h + 3]
        o_ref[:, vrow(h)] = (acc / l).T.astype(BF16)


def _mla_attention(qf, kf, vt, bsz, seq):
    tq, tk, nh = MLA_TQ, MLA_TK, MLA_HEADS_PER_STEP
    h = MLA_HEADS
    nq = seq // tq
    k3 = kf.reshape(bsz, seq, h * MLA_HW)
    return pl.pallas_call(
        functools.partial(_mla_attn_kernel, tq=tq, tk=tk, nh=nh),
        out_shape=jax.ShapeDtypeStruct((bsz * seq, h * V_DIM), BF16),
        grid=(bsz, h // nh, nq),
        in_specs=[pl.BlockSpec((tq, nh * MLA_HW), lambda b, hh, i: (b * nq + i, hh)),
                  pl.BlockSpec((1, seq, nh * MLA_HW), lambda b, hh, i: (b, 0, hh)),
                  pl.BlockSpec((1, nh * V_DIM, seq), lambda b, hh, i: (b, hh, 0))],
        out_specs=pl.BlockSpec((tq, nh * V_DIM), lambda b, hh, i: (b * nq + i, hh)),
        compiler_params=_cparams(("parallel", "parallel", "arbitrary")),
        name="mla_attention",
    )(qf, k3, vt)


def _merge_kernel(ya_ref, yb_ref, yc_ref, ga_ref, gb_ref, gc_ref, x_ref,
                  wa_ref, wb_ref, wc_ref, wo_ref, o_ref):
    sig = lambda ref: jax.nn.sigmoid(ref[...].astype(F32))
    y = (sig(ga_ref) * _dot(ya_ref[...], wa_ref[...])
         + sig(gb_ref) * _dot(yb_ref[...], wb_ref[...])
         + sig(gc_ref) * _dot(yc_ref[...], wc_ref[...]))
    o_ref[...] = x_ref[...] + _dot(y.astype(BF16), wo_ref[...])


def _merge(ya, yb, yc, z, x, wa, wb, wc, wo):
    tm = TOK_TILE
    t, d = x.shape
    act = pl.BlockSpec((tm, ya.shape[1]), lambda i: (i, 0))
    gate = lambda k: pl.BlockSpec((tm, d), lambda i: (i, Z_GM // d + k))
    wbr = pl.BlockSpec((ya.shape[1], d), lambda i: (0, 0))
    return pl.pallas_call(
        _merge_kernel,
        out_shape=jax.ShapeDtypeStruct((t, d), F32),
        grid=(t // tm,),
        in_specs=[act, act, act, gate(0), gate(1), gate(2),
                  pl.BlockSpec((tm, d), lambda i: (i, 0)),
                  wbr, wbr, wbr, pl.BlockSpec((d, d), lambda i: (0, 0))],
        out_specs=pl.BlockSpec((tm, d), lambda i: (i, 0)),
        compiler_params=_cparams(("parallel",)),
        name="merge",
    )(ya, yb, yc, z, z, z, x, wa.astype(BF16), wb.astype(BF16), wc.astype(BF16), wo.astype(BF16))


def _xattn_kernel(x_ref, g_ref, wq_ref, kv_ref, wo_ref, o_ref):
    x = x_ref[...]
    h = _rms(x, g_ref[...]).astype(BF16)
    q = _dot(h, wq_ref[...]) * XATTN_DH ** -0.5
    hd = XATTN_HEADS * XATTN_DH
    outs = []
    for hh in range(XATTN_HEADS):
        qh = q[:, XATTN_DH * hh:XATTN_DH * (hh + 1)].astype(BF16)
        kh = kv_ref[0, :, XATTN_DH * hh:XATTN_DH * (hh + 1)]
        vh = kv_ref[0, :, hd + XATTN_DH * hh:hd + XATTN_DH * (hh + 1)]
        s = _dot_nt(qh, kh)
        m = jnp.max(s, axis=-1, keepdims=True)
        p = jnp.exp(s - m)
        p = p / jnp.sum(p, axis=-1, keepdims=True)
        outs.append(_dot(p.astype(BF16), vh))
    o = jnp.concatenate(outs, axis=-1).astype(BF16)
    o_ref[...] = x + _dot(o, wo_ref[...])


def _xattn(x, g, wq, kv, wo, bsz, seq):
    tm = TOK_TILE
    t, d = x.shape
    nst = seq // tm
    m_len = kv.shape[1]
    hd = XATTN_HEADS * XATTN_DH
    return pl.pallas_call(
        _xattn_kernel,
        out_shape=jax.ShapeDtypeStruct((t, d), F32),
        grid=(bsz, nst),
        in_specs=[pl.BlockSpec((tm, d), lambda b, s: (b * nst + s, 0)),
                  pl.BlockSpec((1, d), lambda b, s: (0, 0)),
                  pl.BlockSpec((d, hd), lambda b, s: (0, 0)),
                  pl.BlockSpec((1, m_len, 2 * hd), lambda b, s: (b, 0, 0)),
                  pl.BlockSpec((hd, d), lambda b, s: (0, 0))],
        out_specs=pl.BlockSpec((tm, d), lambda b, s: (b * nst + s, 0)),
        compiler_params=_cparams(("parallel", "parallel")),
        name="xattn",
    )(x, g[None], wq.astype(BF16), kv, wo.astype(BF16))


def _ffn_kernel(x_ref, g_ref, wg_ref, wu_ref, wd_ref, gf_ref, o_ref, h_ref, acc_ref, *, final):
    c = pl.program_id(1)

    @pl.when(c == 0)
    def _():
        h_ref[...] = _rms(x_ref[...], g_ref[...]).astype(BF16)
        acc_ref[...] = x_ref[...]

    h = h_ref[...]
    gate = _dot(h, wg_ref[...])
    up = _dot(h, wu_ref[...])
    act = (gate * jax.nn.sigmoid(gate) * up).astype(BF16)
    acc_ref[...] += _dot(act, wd_ref[...])

    @pl.when(c == pl.num_programs(1) - 1)
    def _():
        y = acc_ref[...]
        o_ref[...] = _rms(y, gf_ref[...]) if final else y


def _ffn(x, g, w_gate_up, w_down, g_final, final):
    tm, tc = FFN_TM, FFN_TC
    t, d = x.shape
    nc = FFN_HIDDEN // tc
    wgu = w_gate_up.astype(BF16)
    return pl.pallas_call(
        functools.partial(_ffn_kernel, final=final),
        out_shape=jax.ShapeDtypeStruct((t, d), F32),
        grid=(t // tm, nc),
        in_specs=[pl.BlockSpec((tm, d), lambda i, c: (i, 0)),
                  pl.BlockSpec((1, d), lambda i, c: (0, 0)),
                  pl.BlockSpec((d, tc), lambda i, c: (0, c)),
                  pl.BlockSpec((d, tc), lambda i, c: (0, nc + c)),
                  pl.BlockSpec((tc, d), lambda i, c: (c, 0)),
                  pl.BlockSpec((1, d), lambda i, c: (0, 0))],
        out_specs=pl.BlockSpec((tm, d), lambda i, c: (i, 0)),
        scratch_shapes=[pltpu.VMEM((tm, d), BF16), pltpu.VMEM((tm, d), F32)],
        compiler_params=_cparams(("parallel", "arbitrary")),
        name="ffn",
    )(x, g[None], wgu, wgu, w_down.astype(BF16), g_final[None])


def _split_w_in(w):
    k_rope = w[:, O_KR:O_KR + QK_ROPE]
    kv = lambda kind: w[:, O_KV + GD * kind:O_KV + GD * (kind + 1)]
    pad = jnp.zeros((w.shape[0], Z_COLS - Z_GN - 3 * NSA_HEADS), w.dtype)
    wz = jnp.concatenate([
        w[:, O_GM:O_GM + 3 * D_MODEL],
        w[:, O_GLU:O_GLU + 2 * CONV_CH],
        w[:, O_Q:O_Q + NSA_HEADS * NSA_DH],
        w[:, O_CKV:O_CKV + KV_RANK],
        k_rope, _swap_halves(k_rope),
        w[:, O_CQ:O_CQ + Q_RANK],
        kv(0), kv(1),
        w[:, O_GN:O_GN + 3 * NSA_HEADS], pad], axis=1).astype(BF16)
    per_head = lambda wkind: jnp.tile(wkind.reshape(-1, NSA_G, 1, NSA_DH), (1, 1, NSA_HG, 1)).reshape(-1, NSA_G * KREP)
    wk = jnp.concatenate([per_head(kv(2)), per_head(kv(4))], axis=1).astype(BF16)
    wvt = jnp.concatenate([kv(3), kv(5)], axis=1).T.astype(BF16)
    return wz, wk, wvt


def kernel(x, mem, positions, rel_bias, norm_mix, norm_xattn, norm_mem, norm_ffn, norm_final, w_in, conv_w, conv_b, conv_ln_g, conv_ln_b, w_branch_conv, cmp_pos_k, cmp_w1_k, cmp_b1_k, cmp_w2_k, cmp_pos_v, cmp_w1_v, cmp_b1_v, cmp_w2_v, w_branch_nsa, mla_norm_q, mla_norm_kv, w_uq, w_ukv, w_branch_mla, w_out, w_xq, w_xkv, w_xo, w_gate_up, w_down):
    bsz, seq, d = x.shape
    depth = w_in.shape[0]
    t = bsz * seq
    m_len = mem.shape[1]
    xt = x.reshape(t, d)
    memt = mem.reshape(bsz * m_len, d)
    rope_tab = _rope_table(positions)
    tables = _nsa_tables(rel_bias)
    for l in range(depth):
        wz, wk, wvt = _split_w_in(w_in[l])
        z, zc = _in_proj(xt, norm_mix[l][None], wz)
        kk, vvt = _kv_proj(xt, norm_mix[l][None], wk, wvt, bsz, seq)
        ya = _conv_module(z, conv_w[l], conv_b[l], conv_ln_g[l], conv_ln_b[l], bsz, seq)
        kc, vct = _compress(zc, jnp.stack([cmp_pos_k[l], cmp_pos_v[l]]), jnp.stack([cmp_w1_k[l], cmp_w1_v[l]]),
                            jnp.stack([cmp_b1_k[l], cmp_b1_v[l]]), jnp.stack([cmp_w2_k[l], cmp_w2_v[l]]), bsz, seq)
        yb = _nsa_attention(z, kc, vct, kk, vvt, tables, bsz, seq)
        qf, kf, vt = _mla_proj(z, rope_tab, mla_norm_q[l], mla_norm_kv[l], w_uq[l], w_ukv[l], bsz, seq)
        yc = _mla_attention(qf, kf, vt, bsz, seq)
        xt = _merge(ya, yb, yc, z, xt, w_branch_conv[l], w_branch_nsa[l], w_branch_mla[l], w_out[l])
        mem_kv = _norm_matmul(memt, norm_mem[l][None], w_xkv[l].astype(BF16), 256, 1024, BF16)
        mem_kv = mem_kv.reshape(bsz, m_len, 2 * XATTN_HEADS * XATTN_DH)
        xt = _xattn(xt, norm_xattn[l], w_xq[l], mem_kv, w_xo[l], bsz, seq)
        xt = _ffn(xt, norm_ffn[l], w_gate_up[l], w_down[l], norm_final, l == depth - 1)
    return xt.reshape(bsz, seq, d)
```

```python
import functools
import math

import numpy as np
import jax
import jax.numpy as jnp
from jax import lax
from jax.experimental import pallas as pl
from jax.experimental.pallas import tpu as pltpu

F32 = jnp.float32
BF16 = jnp.bfloat16

EPS = 1e-6
NEG_INF = -1e30
FORCE_SCORE = 1e4

D_MODEL = 1024
CONV_CH = 512
CONV_WIDTH = 31
NSA_HEADS = 8
NSA_G = 2
NSA_HG = NSA_HEADS // NSA_G
NSA_DH = 64
CMP_BLOCK = 32
CMP_STRIDE = 16
CMP_HIDDEN = 256
SLC_BLOCK = 64
N_SELECT = 16
WINDOW = 512
NSA_QB = 64
MLA_HEADS = 4
Q_RANK = 384
KV_RANK = 256
QK_NOPE = 128
QK_ROPE = 64
V_DIM = 128
ROPE_THETA = 10000.0
REL_BUCKETS = 32
REL_MAX_DIST = 128
XATTN_HEADS = 4
XATTN_DH = 128
FFN_HIDDEN = 2816

LANES = 128
SUBLANES = 8

O_GLU, O_Q, O_KV, O_GN, O_CQ, O_CKV, O_KR, O_GM = 0, 1024, 1536, 2304, 2328, 2712, 2968, 3032
GD = NSA_G * NSA_DH

Z_GM = 0
Z_UA = 3072
Z_UB = 3584
Z_Q = 4096
Z_CKV = 4608
Z_KR = 4864
Z_CQ = 4992
Z_CMP = 5376
Z_GN = 5632
Z_COLS = 5760

VMEM_LIMIT = 56 * 1024 * 1024

TOK_TILE = 512
IN_PROJ_TM = 1024
IN_PROJ_TN = 1152
FFN_TM = 1024
FFN_TC = 256
MLA_TQ = 512
MLA_TK = 1024
MLA_HEADS_PER_STEP = 2

LOG2E = math.log2(math.e)


def _cparams(sem):
    return pltpu.CompilerParams(dimension_semantics=sem, vmem_limit_bytes=VMEM_LIMIT)


def _rms(x, g):
    return x * lax.rsqrt(jnp.mean(x * x, axis=-1, keepdims=True) + EPS) * g


def _dot(a, b):
    return jnp.dot(a, b, preferred_element_type=F32)


def _dot_nt(a, b):
    return lax.dot_general(a, b, (((1,), (1,)), ((), ())), preferred_element_type=F32)


def _norm_matmul_kernel(x_ref, g_ref, w_ref, o_ref, h_ref):
    @pl.when(pl.program_id(1) == 0)
    def _():
        h_ref[...] = _rms(x_ref[...], g_ref[...]).astype(BF16)

    o_ref[...] = _dot(h_ref[...], w_ref[...]).astype(o_ref.dtype)


def _norm_matmul(x, g, w, tm, tn, out_dtype):
    m, k = x.shape
    n = w.shape[1]
    return pl.pallas_call(
        _norm_matmul_kernel,
        out_shape=jax.ShapeDtypeStruct((m, n), out_dtype),
        grid=(m // tm, n // tn),
        in_specs=[pl.BlockSpec((tm, k), lambda i, j: (i, 0)),
                  pl.BlockSpec((1, k), lambda i, j: (0, 0)),
                  pl.BlockSpec((k, tn), lambda i, j: (0, j))],
        out_specs=pl.BlockSpec((tm, tn), lambda i, j: (i, j)),
        scratch_shapes=[pltpu.VMEM((tm, k), BF16)],
        compiler_params=_cparams(("parallel", "arbitrary")),
        name="norm_matmul",
    )(x, g, w)


def _in_proj_kernel(x_ref, g_ref, w_ref, z_ref, zc_ref, h_ref, *, cmp_tile, cmp_off):
    @pl.when(pl.program_id(1) == 0)
    def _():
        h_ref[...] = _rms(x_ref[...], g_ref[...]).astype(BF16)

    acc = _dot(h_ref[...], w_ref[...])
    z_ref[...] = acc.astype(BF16)

    @pl.when(pl.program_id(1) == cmp_tile)
    def _():
        zc_ref[...] = acc[:, cmp_off:cmp_off + 2 * GD]


def _in_proj(x, g, w):
    tm, tn = IN_PROJ_TM, IN_PROJ_TN
    m, k = x.shape
    n = w.shape[1]
    return pl.pallas_call(
        functools.partial(_in_proj_kernel, cmp_tile=Z_CMP // tn, cmp_off=Z_CMP % tn),
        out_shape=(jax.ShapeDtypeStruct((m, n), BF16), jax.ShapeDtypeStruct((m, 2 * GD), F32)),
        grid=(m // tm, n // tn),
        in_specs=[pl.BlockSpec((tm, k), lambda i, j: (i, 0)),
                  pl.BlockSpec((1, k), lambda i, j: (0, 0)),
                  pl.BlockSpec((k, tn), lambda i, j: (0, j))],
        out_specs=(pl.BlockSpec((tm, tn), lambda i, j: (i, j)),
                   pl.BlockSpec((tm, 2 * GD), lambda i, j: (i, 0))),
        scratch_shapes=[pltpu.VMEM((tm, k), BF16)],
        compiler_params=_cparams(("parallel", "arbitrary")),
        name="in_proj",
    )(x, g, w)


def _kv_proj_kernel(x_ref, g_ref, wk_ref, wvt_ref, k_ref, vt_ref):
    @pl.when(pl.program_id(1) == 0)
    def _():
        k_ref[...] = jnp.zeros(k_ref.shape, BF16)
        vt_ref[...] = jnp.zeros(vt_ref.shape, BF16)

    @pl.when(pl.program_id(1) > 0)
    def _():
        h = _rms(x_ref[...], g_ref[...]).astype(BF16)
        k_ref[0] = _dot(h, wk_ref[...]).astype(BF16)
        vt_ref[0] = _dot_nt(wvt_ref[...], h).astype(BF16)


def _kv_proj(x, g, wk, wvt, bsz, seq):
    tm = WINDOW
    nst = seq // tm
    d = x.shape[1]
    nk = wk.shape[1]
    nv = wvt.shape[0]
    return pl.pallas_call(
        _kv_proj_kernel,
        out_shape=(jax.ShapeDtypeStruct((bsz, seq + tm, nk), BF16),
                   jax.ShapeDtypeStruct((bsz, nv, seq + tm), BF16)),
        grid=(bsz, nst + 1),
        in_specs=[pl.BlockSpec((tm, d), lambda b, s: (b * nst + jnp.maximum(s - 1, 0), 0)),
                  pl.BlockSpec((1, d), lambda b, s: (0, 0)),
                  pl.BlockSpec((d, nk), lambda b, s: (0, 0)),
                  pl.BlockSpec((nv, d), lambda b, s: (0, 0))],
        out_specs=(pl.BlockSpec((1, tm, nk), lambda b, s: (b, s, 0)),
                   pl.BlockSpec((1, nv, tm), lambda b, s: (b, 0, s))),
        compiler_params=_cparams(("parallel", "arbitrary")),
        name="nsa_kv_proj",
    )(x, g, wk, wvt)


CONV_HALO = 32


CONV_ROWS = 64


def _conv_kernel(a_ref, b_ref, w_ref, cb_ref, lg_ref, lb_ref, o_ref, buf_ref, sh_ref, *, ts):
    @pl.when(pl.program_id(1) == 0)
    def _():
        buf_ref[0:CONV_HALO, :] = jnp.zeros((CONV_HALO, CONV_CH), F32)

    buf_ref[CONV_HALO:CONV_HALO + ts, :] = a_ref[...].astype(F32) * jax.nn.sigmoid(b_ref[...].astype(F32))
    span = ts + CONV_HALO - SUBLANES
    for r in range(1, SUBLANES):
        sh_ref[r - 1, 0:span, :] = buf_ref[r:r + span, :]
    off = CONV_HALO - (CONV_WIDTH - 1)

    def rows(i, carry):
        r0 = pl.multiple_of(i * CONV_ROWS, CONV_ROWS)
        acc = jnp.zeros((CONV_ROWS, CONV_CH), F32) + cb_ref[...]
        for k in range(CONV_WIDTH):
            res, base = (off + k) % SUBLANES, (off + k) // SUBLANES * SUBLANES
            if res == 0:
                tap = buf_ref[pl.ds(r0 + base, CONV_ROWS), :]
            else:
                tap = sh_ref[res - 1, pl.ds(r0 + base, CONV_ROWS), :]
            acc = acc + tap * w_ref[k:k + 1, :]
        mu = jnp.mean(acc, axis=-1, keepdims=True)
        xc = acc - mu
        var = jnp.mean(xc * xc, axis=-1, keepdims=True)
        y = xc * lax.rsqrt(var + EPS) * lg_ref[...] + lb_ref[...]
        o_ref[pl.ds(r0, CONV_ROWS), :] = (y * jax.nn.sigmoid(y)).astype(BF16)
        return carry

    lax.fori_loop(0, ts // CONV_ROWS, rows, 0)
    buf_ref[0:CONV_HALO, :] = buf_ref[ts:ts + CONV_HALO, :]


def _conv_module(z, conv_w, conv_b, ln_g, ln_b, bsz, seq):
    ts = TOK_TILE
    nst = seq // ts
    wpad = jnp.zeros((32, CONV_CH), F32).at[:CONV_WIDTH].set(conv_w)
    return pl.pallas_call(
        functools.partial(_conv_kernel, ts=ts),
        out_shape=jax.ShapeDtypeStruct((bsz * seq, CONV_CH), BF16),
        grid=(bsz, nst),
        in_specs=[pl.BlockSpec((ts, CONV_CH), lambda b, s: (b * nst + s, Z_UA // CONV_CH)),
                  pl.BlockSpec((ts, CONV_CH), lambda b, s: (b * nst + s, Z_UB // CONV_CH)),
                  pl.BlockSpec((32, CONV_CH), lambda b, s: (0, 0)),
                  pl.BlockSpec((1, CONV_CH), lambda b, s: (0, 0)),
                  pl.BlockSpec((1, CONV_CH), lambda b, s: (0, 0)),
                  pl.BlockSpec((1, CONV_CH), lambda b, s: (0, 0))],
        out_specs=pl.BlockSpec((ts, CONV_CH), lambda b, s: (b * nst + s, 0)),
        scratch_shapes=[pltpu.VMEM((ts + CONV_HALO, CONV_CH), F32),
                        pltpu.VMEM((SUBLANES - 1, ts + CONV_HALO - SUBLANES, CONV_CH), F32)],
        compiler_params=_cparams(("arbitrary", "arbitrary")),
        name="conv_module",
    )(z, z, wpad, conv_b[None], ln_g[None], ln_b[None])


def _compress_kernel(xk_ref, xv_ref, pos_ref, w1_ref, b1_ref, w2k_ref, w2v_ref, kc_ref, vct_ref, *, nch):
    for kind, (x_ref, w2_ref) in enumerate(((xk_ref, w2k_ref), (xv_ref, w2v_ref))):
        a = jnp.zeros((nch, NSA_G * CMP_HIDDEN), F32)
        b = jnp.zeros((nch, NSA_G * CMP_HIDDEN), F32)
        for l in range(CMP_STRIDE):
            xs = x_ref[pl.ds(l, nch, stride=CMP_STRIDE), :]
            a = a + _dot((xs + pos_ref[kind, l:l + 1, :]).astype(BF16), w1_ref[kind, l])
            b = b + _dot((xs + pos_ref[kind, CMP_STRIDE + l:CMP_STRIDE + l + 1, :]).astype(BF16),
                         w1_ref[kind, CMP_STRIDE + l])
        pre = a + pltpu.roll(b, nch - 1, 0) + b1_ref[kind]
        out = _dot(jax.nn.gelu(pre).astype(BF16), w2_ref[...])
        if kind == 0:
            kc_ref[0] = out.astype(BF16)
        else:
            vct_ref[0] = out.T.astype(BF16)


def _blockdiag2(w):
    z = jnp.zeros_like(w)
    return jnp.concatenate([jnp.concatenate([w, z], axis=-1), jnp.concatenate([z, w], axis=-1)], axis=-2)


def _compress(z, pos, w1, b1, w2, bsz, seq):
    nch = seq // CMP_STRIDE
    pos2 = jnp.concatenate([pos, pos], axis=-1)
    w1e = _blockdiag2(w1.reshape(2, CMP_BLOCK, NSA_DH, CMP_HIDDEN)).astype(BF16)
    b1e = jnp.concatenate([b1, b1], axis=-1)[:, None]
    w2k = _blockdiag2(jnp.tile(w2[0], (1, NSA_HG))).astype(BF16)
    w2v = _blockdiag2(w2[1]).astype(BF16)
    full = lambda a: pl.BlockSpec(a.shape, lambda b: (0,) * a.ndim)
    return pl.pallas_call(
        functools.partial(_compress_kernel, nch=nch),
        out_shape=(jax.ShapeDtypeStruct((bsz, nch, NSA_G * NSA_HG * NSA_DH), BF16),
                   jax.ShapeDtypeStruct((bsz, GD, nch), BF16)),
        grid=(bsz,),
        in_specs=[pl.BlockSpec((seq, GD), lambda b: (b, 0)),
                  pl.BlockSpec((seq, GD), lambda b: (b, 1)),
                  full(pos2), full(w1e), full(b1e), full(w2k), full(w2v)],
        out_specs=(pl.BlockSpec((1, nch, NSA_G * NSA_HG * NSA_DH), lambda b: (b, 0, 0)),
                   pl.BlockSpec((1, GD, nch), lambda b: (b, 0, 0))),
        compiler_params=_cparams(("parallel",)),
        name="nsa_compress",
    )(z, z, pos2, w1e, b1e, w2k, w2v)


NSA_QP = 2 * NSA_QB
NEAR_KEYS = 256
WIN_KEYS = 640
CMP_TAB_ROWS = 512
CMP_TAB_ZERO = 256


def _t5_bucket_np(d):
    exact = REL_BUCKETS // 2
    d = np.maximum(d, 0)
    ratio = np.log(np.maximum(d, 1).astype(np.float32) / np.float32(exact)) / np.float32(math.log(REL_MAX_DIST / exact))
    large = np.minimum(exact + (ratio * (REL_BUCKETS - exact)).astype(np.int32), REL_BUCKETS - 1)
    return np.where(d < exact, d, large).astype(np.int32)


def _bucket_thresholds():
    exact = REL_BUCKETS // 2
    bk = _t5_bucket_np(np.arange(4 * REL_MAX_DIST))
    assert np.all(np.diff(bk) >= 0) and bk[-1] == REL_BUCKETS - 1
    return [int(np.argmax(bk >= k)) for k in range(exact + 1, REL_BUCKETS)]


def _bias_rows(rel_ref, dist, valid, shift):
    exact = REL_BUCKETS // 2
    bucket = jnp.full(dist.shape, exact, jnp.int32)
    for thr in _bucket_thresholds():
        bucket = bucket + jnp.where(dist >= thr, 1, 0)
    bucket = jnp.where(dist < exact, dist, bucket)
    val = jnp.zeros(dist.shape, F32)
    for bkt in range(REL_BUCKETS):
        val = jnp.where(bucket == bkt, rel_ref[0, bkt:bkt + 1, :], val)
    if shift:
        val = val - rel_ref[0, REL_BUCKETS - 1:REL_BUCKETS, :]
    return jnp.where(valid, val * LOG2E, NEG_INF)


def _nsa_bias_kernel(rel_ref, tc_ref, tn_ref, tw_ref):
    hq = NSA_HG * NSA_QP
    rows = 128

    def dist_of(nrows, r0, fn):
        r = r0 + lax.broadcasted_iota(jnp.int32, (nrows, hq), 0)
        t = lax.bitwise_and(lax.broadcasted_iota(jnp.int32, (nrows, hq), 1), NSA_QP - 1)
        return fn(r, t)

    for r0 in range(0, CMP_TAB_ROWS, rows):
        d = dist_of(rows, r0, lambda r, t: t - CMP_STRIDE * (r - CMP_TAB_ZERO) - (CMP_BLOCK - 1))
        tc_ref[0, r0:r0 + rows, :] = _bias_rows(rel_ref, d, d >= 0, False)
    for r0 in range(0, NEAR_KEYS, rows):
        d = dist_of(rows, r0, lambda r, t: NEAR_KEYS // 2 + t - r)
        tn_ref[0, r0:r0 + rows, :] = _bias_rows(rel_ref, d, d >= 0, True)
    for r0 in range(0, WIN_KEYS, rows):
        d = dist_of(rows, r0, lambda r, t: WINDOW + t - r)
        tw_ref[0, r0:r0 + rows, :] = _bias_rows(rel_ref, d, (d >= 0) & (d < WINDOW), False)


def _nsa_tables(rel_bias):
    hq = NSA_HG * NSA_QP
    rel4 = jnp.repeat(rel_bias.reshape(REL_BUCKETS, NSA_G, NSA_HG).transpose(1, 0, 2), NSA_QP, axis=-1)
    spec = lambda r: pl.BlockSpec((1, r, hq), lambda g: (g, 0, 0))
    return pl.pallas_call(
        _nsa_bias_kernel,
        out_shape=(jax.ShapeDtypeStruct((NSA_G, CMP_TAB_ROWS, hq), F32),
                   jax.ShapeDtypeStruct((NSA_G, NEAR_KEYS, hq), F32),
                   jax.ShapeDtypeStruct((NSA_G, WIN_KEYS, hq), F32)),
        grid=(NSA_G,),
        in_specs=[spec(REL_BUCKETS)],
        out_specs=(spec(CMP_TAB_ROWS), spec(NEAR_KEYS), spec(WIN_KEYS)),
        compiler_params=_cparams(("parallel",)),
        name="nsa_bias_tables",
    )(rel4)


SEL_PAD = 8
FAR_KEYS = 1024
KV_FRONT = WINDOW
KREP = NSA_HG * NSA_DH


def _softmax_cols(s):
    m = jnp.max(s, axis=0, keepdims=True)
    p = jnp.exp2(s - m)
    return m, p, jnp.sum(p, axis=0, keepdims=True)


def _mask_blocks(s, mask_ref, row0, nblk):
    parts = []
    for jj in range(nblk):
        row = mask_ref[pl.ds(row0 + jj, 1), :]
        parts.append(jnp.where(row > 0.0, s[SLC_BLOCK * jj:SLC_BLOCK * (jj + 1)], NEG_INF))
    return jnp.concatenate(parts, axis=0)


def _rank_select(score_ref, n_sb, n_sel):
    groups = n_sb // SUBLANES
    sub = lax.broadcasted_iota(jnp.int32, (SUBLANES, NSA_QP), 0)
    tiles = [score_ref[SUBLANES * v:SUBLANES * (v + 1), :] for v in range(groups)]
    cnts = [jnp.zeros((SUBLANES, NSA_QP), F32) for _ in range(groups)]
    for jp in range(n_sb):
        row = score_ref[jp:jp + 1, :]
        for v in range(groups):
            lo = SUBLANES * v
            if jp < lo:
                beats = row >= tiles[v]
            elif jp >= lo + SUBLANES - 1:
                beats = row > tiles[v]
            else:
                beats = (row > tiles[v]) | ((row == tiles[v]) & (sub > jp - lo))
            cnts[v] = cnts[v] + jnp.where(beats, 1.0, 0.0)
    cnt = jnp.concatenate(cnts, axis=0)
    return jnp.where(cnt < float(n_sel), 1.0, 0.0)


def _nsa_kernel(q_ref, gate_ref, kc_ref, vct_ref, ks_ref, kw_ref, vst_ref, vwt_ref,
                tc_ref, tn_ref, tw_ref, cov_ref, rep_ref, o_ref,
                sel_ref, selfar_ref, score_ref, *, n_sb):
    p2 = pl.program_id(1)
    hq = NSA_HG * NSA_QP
    nch = kc_ref.shape[1]
    groups = range(NSA_G)
    kcol = lambda g: slice(KREP * g, KREP * (g + 1))
    vrow = lambda g: slice(NSA_DH * g, NSA_DH * (g + 1))

    qs = []
    for g in groups:
        qb = (q_ref[:, kcol(g)].astype(F32) * (NSA_DH ** -0.5 * LOG2E)).astype(BF16)
        lane_head = lax.shift_right_logical(lax.broadcasted_iota(jnp.int32, qb.shape, 1), 6)
        qs.append(jnp.concatenate([jnp.where(lane_head == h, qb, jnp.zeros_like(qb)) for h in range(NSA_HG)], axis=0))

    start_c = pl.multiple_of(CMP_TAB_ZERO - (NSA_QP // CMP_STRIDE) * p2, SUBLANES)
    lane = lax.broadcasted_iota(jnp.int32, (1, hq), 1)
    tq = NSA_QP * p2 + lax.bitwise_and(lane, NSA_QP - 1)
    anyv = jnp.where(tq >= CMP_BLOCK - 1, 1.0, 0.0)
    jrow = lax.broadcasted_iota(jnp.int32, (n_sb, NSA_QP), 0)
    tok = lax.broadcasted_iota(jnp.int32, (n_sb, NSA_QP), 1)
    cur = 2 * p2 + lax.shift_right_logical(tok, 6)
    forced = (jrow == 0) | (jrow == cur) | (jrow == cur - 1)
    o_cmp = []
    for g in groups:
        sc = _dot_nt(kc_ref[0, :, kcol(g)], qs[g]) + tc_ref[g, pl.ds(start_c, nch), :]
        _, pc, lc = _softmax_cols(sc)
        pc = pc * (anyv / lc)
        o_cmp.append(_dot(vct_ref[0, vrow(g), :], pc.astype(BF16)))
        psum = pc[:, 0:NSA_QP]
        for h in range(1, NSA_HG):
            psum = psum + pc[:, NSA_QP * h:NSA_QP * (h + 1)]
        p_hi = psum.astype(BF16)
        p_lo = (psum - p_hi.astype(F32)).astype(BF16)
        imp = _dot(cov_ref[...], p_hi) + _dot(cov_ref[...], p_lo)
        score_ref[g] = jnp.where(forced, FORCE_SCORE, jnp.where(jrow <= cur, imp, -1.0))

    zeros8 = jnp.zeros((SEL_PAD, hq), F32)
    jrow4 = lax.broadcasted_iota(jnp.int32, (n_sb, hq), 0)
    for g in groups:
        sel = _rank_select(score_ref.at[g], n_sb, min(N_SELECT, n_sb)).astype(BF16)
        sel4 = _dot(sel, rep_ref[...])
        sel_ref[g, 0:SEL_PAD, :] = zeros8
        sel_ref[g, SEL_PAD + n_sb:2 * SEL_PAD + n_sb, :] = zeros8
        sel_ref[g, SEL_PAD:SEL_PAD + n_sb, :] = sel4
        selfar_ref[g, 0:SEL_PAD, :] = zeros8
        selfar_ref[g, SEL_PAD + n_sb:2 * SEL_PAD + n_sb, :] = zeros8
        selfar_ref[g, SEL_PAD:SEL_PAD + n_sb, :] = jnp.where(jrow4 < 2 * p2 - 2, sel4, 0.0)

    win0 = pl.multiple_of(NSA_QP * p2, LANES)
    near0 = pl.multiple_of(win0 + KV_FRONT - NEAR_KEYS // 2, LANES)
    state = []
    for g in groups:
        s = _dot_nt(ks_ref[0, pl.ds(near0, NEAR_KEYS), kcol(g)], qs[g]) + tn_ref[g]
        s = _mask_blocks(s, sel_ref.at[g], 2 * p2 - 2 + SEL_PAD, NEAR_KEYS // SLC_BLOCK)
        m_s, p_s, l_s = _softmax_cols(s)
        state += [m_s, l_s, _dot(vst_ref[0, vrow(g), pl.ds(near0, NEAR_KEYS)], p_s.astype(BF16))]

    def far_body(c, carry):
        k0 = pl.multiple_of(FAR_KEYS * c + KV_FRONT, LANES)
        sfs = [_dot_nt(ks_ref[0, pl.ds(k0, FAR_KEYS), kcol(g)], qs[g]) for g in groups]
        out = []
        for g in groups:
            m_old, l_old, acc_old = carry[3 * g:3 * g + 3]
            sf = _mask_blocks(sfs[g], selfar_ref.at[g], (FAR_KEYS // SLC_BLOCK) * c + SEL_PAD, FAR_KEYS // SLC_BLOCK)
            m_new = jnp.maximum(m_old, jnp.max(sf, axis=0, keepdims=True))
            alpha = jnp.exp2(m_old - m_new)
            pf = jnp.exp2(sf - m_new)
            l_new = alpha * l_old + jnp.sum(pf, axis=0, keepdims=True)
            acc_new = alpha * acc_old + _dot(vst_ref[0, vrow(g), pl.ds(k0, FAR_KEYS)], pf.astype(BF16))
            out += [m_new, l_new, acc_new]
        return tuple(out)

    n_far = lax.div(jnp.maximum(p2 - 1, 0) * NSA_QP + FAR_KEYS - 1, FAR_KEYS)
    state = lax.fori_loop(0, n_far, far_body, tuple(state))

    gt = gate_ref[...].astype(F32).T
    r = lax.broadcasted_iota(jnp.int32, (NSA_QP, NSA_QP), 0)
    c = lax.broadcasted_iota(jnp.int32, (NSA_QP, NSA_QP), 1)
    eye = jnp.where(r == c, 1.0, 0.0).astype(BF16)
    for g in groups:
        _, l_s, acc_s = state[3 * g:3 * g + 3]
        sw = _dot_nt(kw_ref[0, pl.ds(win0, WIN_KEYS), kcol(g)], qs[g]) + tw_ref[g]
        slabs = [sw[NSA_QP * j:NSA_QP * (j + 1)] for j in range(WIN_KEYS // NSA_QP)]
        for j in range(KV_FRONT // NSA_QP):
            slabs[j] = jnp.where(NSA_QP * j + win0 >= KV_FRONT, slabs[j], NEG_INF)
        _, p_w, l_w = _softmax_cols(jnp.concatenate(slabs, axis=0))
        acc_w = _dot(vwt_ref[0, vrow(g), pl.ds(win0, WIN_KEYS)], p_w.astype(BF16))
        gsel = jax.nn.sigmoid(gt[3 * NSA_HG * g:3 * NSA_HG * (g + 1)])
        gate = lambda b: jnp.concatenate([gsel[3 * h + b:3 * h + b + 1] for h in range(NSA_HG)], axis=1)
        out_t = (gate(0) * o_cmp[g] + (gate(1) / l_s) * acc_s + (gate(2) / l_w) * acc_w).astype(BF16)
        stacked = jnp.concatenate([out_t[:, NSA_QP * h:NSA_QP * (h + 1)] for h in range(NSA_HG)], axis=0)
        o_ref[:, kcol(g)] = _dot_nt(eye, stacked).astype(BF16)


def _nsa_attention(z, kc, vct, kk, vvt, tables, bsz, seq):
    g, hg, dh, qp = NSA_G, NSA_HG, NSA_DH, NSA_QP
    nstep = seq // qp
    n_sb = seq // SLC_BLOCK
    nch = kc.shape[1]
    hq = hg * qp
    sp = kk.shape[1]
    tc, tn, tw = tables
    c_start = CMP_STRIDE * np.arange(nch)
    s_start = SLC_BLOCK * np.arange(n_sb)
    cover_t = ((c_start[None, :] < s_start[:, None] + SLC_BLOCK)
               & (c_start[None, :] + CMP_BLOCK > s_start[:, None])
               & (np.arange(nch)[None, :] < (seq - CMP_BLOCK) // CMP_STRIDE + 1))
    cover_t = jnp.asarray(cover_t.astype(np.float32), BF16)
    rep = jnp.asarray(np.tile(np.eye(qp, dtype=np.float32), (1, hg)), BF16)
    full = lambda a: pl.BlockSpec(a.shape, lambda b, i: (0,) * a.ndim, pipeline_mode=pl.Buffered(1))
    qd = g * hg * dh
    return pl.pallas_call(
        functools.partial(_nsa_kernel, n_sb=n_sb),
        out_shape=jax.ShapeDtypeStruct((bsz * seq, qd), BF16),
        grid=(bsz, nstep),
        in_specs=[pl.BlockSpec((qp, qd), lambda b, i: (b * nstep + i, Z_Q // qd)),
                  pl.BlockSpec((qp, LANES), lambda b, i: (b * nstep + i, Z_GN // LANES)),
                  pl.BlockSpec((1, nch, g * KREP), lambda b, i: (b, 0, 0)),
                  pl.BlockSpec((1, g * dh, nch), lambda b, i: (b, 0, 0)),
                  pl.BlockSpec((1, sp, g * KREP), lambda b, i: (b, 0, 0)),
                  pl.BlockSpec((1, sp, g * KREP), lambda b, i: (b, 0, 1)),
                  pl.BlockSpec((1, g * dh, sp), lambda b, i: (b, 0, 0)),
                  pl.BlockSpec((1, g * dh, sp), lambda b, i: (b, 1, 0)),
                  full(tc), full(tn), full(tw), full(cover_t), full(rep)],
        out_specs=pl.BlockSpec((qp, qd), lambda b, i: (b * nstep + i, 0)),
        scratch_shapes=[pltpu.VMEM((g, n_sb + 2 * SEL_PAD, hq), F32),
                        pltpu.VMEM((g, n_sb + 2 * SEL_PAD, hq), F32),
                        pltpu.VMEM((g, n_sb, qp), F32)],
        compiler_params=_cparams(("parallel", "arbitrary")),
        name="nsa_attention",
    )(z, z, kc, vct, kk, kk, vvt, vvt, tc, tn, tw, cover_t, rep)


def _rope_table_kernel(pos_ref, inv_ref, sign_ref, o_ref):
    ang = pos_ref[...].astype(F32) * inv_ref[...]
    o_ref[...] = jnp.concatenate([jnp.cos(ang), jnp.sin(ang) * sign_ref[...]], axis=-1)


def _rope_table(positions):
    tm = TOK_TILE
    t = positions.size
    half = QK_ROPE // 2
    inv = ROPE_THETA ** (-jnp.arange(half, dtype=F32) / half)
    inv2 = jnp.concatenate([inv, inv])[None]
    sign = jnp.asarray(np.concatenate([-np.ones(half), np.ones(half)]).astype(np.float32))[None]
    return pl.pallas_call(
        _rope_table_kernel,
        out_shape=jax.ShapeDtypeStruct((t, 2 * QK_ROPE), F32),
        grid=(t // tm,),
        in_specs=[pl.BlockSpec((tm, 1), lambda i: (i, 0)),
                  pl.BlockSpec((1, QK_ROPE), lambda i: (0, 0)),
                  pl.BlockSpec((1, QK_ROPE), lambda i: (0, 0))],
        out_specs=pl.BlockSpec((tm, 2 * QK_ROPE), lambda i: (i, 0)),
        compiler_params=_cparams(("parallel",)),
        name="rope_table",
    )(positions.reshape(t, 1), inv2, sign)


MLA_HW = 256


def _mla_proj_kernel(cq_ref, ckv_ref, kr_ref, rope_ref, nq_ref, nkv_ref, wq_ref, wkn_ref, wvt_ref,
                     q_ref, k_ref, vt_ref):
    scale = (QK_NOPE + QK_ROPE) ** -0.5 * LOG2E
    rope = rope_ref[...]
    yq = _dot(_rms(cq_ref[...].astype(F32), nq_ref[...]).astype(BF16), wq_ref[...])
    ckv = _rms(ckv_ref[...].astype(F32), nkv_ref[...]).astype(BF16)
    ykn = _dot(ckv, wkn_ref[...])
    vt_ref[0] = _dot_nt(wvt_ref[...], ckv).astype(BF16)
    kp = kr_ref[...].astype(F32) * rope
    kp = kp + pltpu.roll(kp, QK_ROPE, 1)
    lane = lax.broadcasted_iota(jnp.int32, kp.shape, 1)
    kp = jnp.where(lane < QK_ROPE, kp, 0.0).astype(BF16)
    for h in range(MLA_HEADS):
        base = MLA_HW * h
        q_ref[:, base:base + QK_NOPE] = (yq[:, base:base + QK_NOPE] * scale).astype(BF16)
        qp = yq[:, base + QK_NOPE:base + MLA_HW] * rope
        qp = qp + pltpu.roll(qp, QK_ROPE, 1)
        q_ref[:, base + QK_NOPE:base + MLA_HW] = (qp * scale).astype(BF16)
        k_ref[:, base:base + QK_NOPE] = ykn[:, QK_NOPE * h:QK_NOPE * (h + 1)].astype(BF16)
        k_ref[:, base + QK_NOPE:base + MLA_HW] = kp


def _swap_halves(w):
    half = QK_ROPE // 2
    return jnp.concatenate([w[..., half:], w[..., :half]], axis=-1)


def _mla_proj(z, rope_tab, norm_q, norm_kv, w_uq, w_ukv, bsz, seq):
    tm = TOK_TILE
    t = z.shape[0]
    nst = seq // tm
    wq = w_uq.reshape(Q_RANK, MLA_HEADS, QK_NOPE + QK_ROPE)
    wq = jnp.concatenate([wq, _swap_halves(wq[..., QK_NOPE:])], axis=-1)
    wq = wq.reshape(Q_RANK, MLA_HEADS * MLA_HW).astype(BF16)
    wkv = w_ukv.reshape(KV_RANK, MLA_HEADS, QK_NOPE + V_DIM)
    wkn = wkv[..., :QK_NOPE].reshape(KV_RANK, MLA_HEADS * QK_NOPE).astype(BF16)
    wvt = wkv[..., QK_NOPE:].reshape(KV_RANK, MLA_HEADS * V_DIM).T.astype(BF16)
    hw = MLA_HEADS * MLA_HW
    hv = MLA_HEADS * V_DIM
    row = lambda b, s: b * nst + s
    return pl.pallas_call(
        _mla_proj_kernel,
        out_shape=(jax.ShapeDtypeStruct((t, hw), BF16),
                   jax.ShapeDtypeStruct((t, hw), BF16),
                   jax.ShapeDtypeStruct((bsz, hv, seq), BF16)),
        grid=(bsz, nst),
        in_specs=[pl.BlockSpec((tm, Q_RANK), lambda b, s: (row(b, s), Z_CQ // Q_RANK)),
                  pl.BlockSpec((tm, KV_RANK), lambda b, s: (row(b, s), Z_CKV // KV_RANK)),
                  pl.BlockSpec((tm, 2 * QK_ROPE), lambda b, s: (row(b, s), Z_KR // (2 * QK_ROPE))),
                  pl.BlockSpec((tm, 2 * QK_ROPE), lambda b, s: (row(b, s), 0)),
                  pl.BlockSpec((1, Q_RANK), lambda b, s: (0, 0)),
                  pl.BlockSpec((1, KV_RANK), lambda b, s: (0, 0)),
                  pl.BlockSpec((Q_RANK, hw), lambda b, s: (0, 0)),
                  pl.BlockSpec((KV_RANK, hv), lambda b, s: (0, 0)),
                  pl.BlockSpec((hv, KV_RANK), lambda b, s: (0, 0))],
        out_specs=(pl.BlockSpec((tm, hw), lambda b, s: (row(b, s), 0)),
                   pl.BlockSpec((tm, hw), lambda b, s: (row(b, s), 0)),
                   pl.BlockSpec((1, hv, tm), lambda b, s: (b, 0, s))),
        compiler_params=_cparams(("parallel", "parallel")),
        name="mla_proj",
    )(z, z, z, rope_tab, norm_q[None], norm_kv[None], wq, wkn, wvt)


def _mla_attn_kernel(q_ref, k_ref, vt_ref, o_ref, *, tq, tk, nh):
    iq = pl.program_id(2)
    cd = lax.div(iq * tq, tk)
    heads = range(nh)
    hcol = lambda h: slice(MLA_HW * h, MLA_HW * (h + 1))
    vrow = lambda h: slice(V_DIM * h, V_DIM * (h + 1))
    qs = [q_ref[:, hcol(h)] for h in heads]

    def scores(c, h):
        k0 = pl.multiple_of(c * tk, tk)
        return _dot_nt(k_ref[0, pl.ds(k0, tk), hcol(h)], qs[h])

    def diagonal(nk):
        k0 = pl.multiple_of((iq + 1) * tq - nk, tq)
        kpos = k0 + lax.broadcasted_iota(jnp.int32, (nk, tq), 0)
        qpos = iq * tq + lax.broadcasted_iota(jnp.int32, (nk, tq), 1)
        st = []
        for h in heads:
            s = jnp.where(kpos <= qpos, _dot_nt(k_ref[0, pl.ds(k0, nk), hcol(h)], qs[h]), NEG_INF)
            m0, p0, l0 = _softmax_cols(s)
            st += [m0, l0, _dot(vt_ref[0, vrow(h), pl.ds(k0, nk)], p0.astype(BF16))]
        return tuple(st)

    assert tk == 2 * tq
    state = lax.cond(lax.rem(iq, 2) == 0, lambda: diagonal(tq), lambda: diagonal(tk))

    def body(c, carry):
        k0 = pl.multiple_of(c * tk, tk)
        ss = [scores(c, h) for h in heads]
        out = []
        for h in heads:
            m_old, l_old, acc_old = carry[3 * h:3 * h + 3]
            m_new = jnp.maximum(m_old, jnp.max(ss[h], axis=0, keepdims=True))
            alpha = jnp.exp2(m_old - m_new)
            p = jnp.exp2(ss[h] - m_new)
            l_new = alpha * l_old + jnp.sum(p, axis=0, keepdims=True)
            acc_new = alpha * acc_old + _dot(vt_ref[0, vrow(h), pl.ds(k0, tk)], p.astype(BF16))
            out += [m_new, l_new, acc_new]
        return tuple(out)

    state = lax.fori_loop(0, cd, body, tuple(state))
    for h in heads:
        _, l, acc = state[3 * h:3 * h + 3]
        o_ref[:, vrow(h)] = (acc / l).T.astype(BF16)


def _mla_attention(qf, kf, vt, bsz, seq):
    tq, tk, nh = MLA_TQ, MLA_TK, MLA_HEADS_PER_STEP
    h = MLA_HEADS
    nq = seq // tq
    k3 = kf.reshape(bsz, seq, h * MLA_HW)
    return pl.pallas_call(
        functools.partial(_mla_attn_kernel, tq=tq, tk=tk, nh=nh),
        out_shape=jax.ShapeDtypeStruct((bsz * seq, h * V_DIM), BF16),
        grid=(bsz, h // nh, nq),
        in_specs=[pl.BlockSpec((tq, nh * MLA_HW), lambda b, hh, i: (b * nq + i, hh)),
                  pl.BlockSpec((1, seq, nh * MLA_HW), lambda b, hh, i: (b, 0, hh)),
                  pl.BlockSpec((1, nh * V_DIM, seq), lambda b, hh, i: (b, hh, 0))],
        out_specs=pl.BlockSpec((tq, nh * V_DIM), lambda b, hh, i: (b * nq + i, hh)),
        compiler_params=_cparams(("parallel", "parallel", "arbitrary")),
        name="mla_attention",
    )(qf, k3, vt)


def _merge_xattn_kernel(ya_ref, yb_ref, yc_ref, ga_ref, gb_ref, gc_ref, x_ref,
                        wa_ref, wb_ref, wc_ref, wo_ref,
                        gx_ref, wq_ref, kv_ref, wxo_ref, o_ref):
    sig = lambda ref: jax.nn.sigmoid(ref[...].astype(F32))
    y = (sig(ga_ref) * _dot(ya_ref[...], wa_ref[...])
         + sig(gb_ref) * _dot(yb_ref[...], wb_ref[...])
         + sig(gc_ref) * _dot(yc_ref[...], wc_ref[...]))
    x = x_ref[...] + _dot(y.astype(BF16), wo_ref[...])
    h = _rms(x, gx_ref[...]).astype(BF16)
    q = _dot(h, wq_ref[...]) * XATTN_DH ** -0.5
    hd = XATTN_HEADS * XATTN_DH
    outs = []
    for hh in range(XATTN_HEADS):
        qh = q[:, XATTN_DH * hh:XATTN_DH * (hh + 1)].astype(BF16)
        kh = kv_ref[0, :, XATTN_DH * hh:XATTN_DH * (hh + 1)]
        vh = kv_ref[0, :, hd + XATTN_DH * hh:hd + XATTN_DH * (hh + 1)]
        s = _dot_nt(qh, kh)
        m = jnp.max(s, axis=-1, keepdims=True)
        p = jnp.exp(s - m)
        p = p / jnp.sum(p, axis=-1, keepdims=True)
        outs.append(_dot(p.astype(BF16), vh))
    o = jnp.concatenate(outs, axis=-1).astype(BF16)
    o_ref[...] = x + _dot(o, wxo_ref[...])


def _merge_xattn(ya, yb, yc, z, x, wa, wb, wc, wo, gx, wq, kv, wxo, bsz, seq):
    tm = TOK_TILE
    t, d = x.shape
    nst = seq // tm
    m_len = kv.shape[1]
    hd = XATTN_HEADS * XATTN_DH
    row = lambda b, s: b * nst + s
    act = pl.BlockSpec((tm, ya.shape[1]), lambda b, s: (row(b, s), 0))
    gate = lambda k: pl.BlockSpec((tm, d), lambda b, s: (row(b, s), Z_GM // d + k))
    const = lambda shape: pl.BlockSpec(shape, lambda b, s: (0, 0))
    bf = lambda w: w.astype(BF16)
    return pl.pallas_call(
        _merge_xattn_kernel,
        out_shape=jax.ShapeDtypeStruct((t, d), F32),
        grid=(bsz, nst),
        in_specs=[act, act, act, gate(0), gate(1), gate(2),
                  pl.BlockSpec((tm, d), lambda b, s: (row(b, s), 0)),
                  const((ya.shape[1], d)), const((ya.shape[1], d)), const((ya.shape[1], d)), const((d, d)),
                  const((1, d)), const((d, hd)),
                  pl.BlockSpec((1, m_len, 2 * hd), lambda b, s: (b, 0, 0)),
                  const((hd, d))],
        out_specs=pl.BlockSpec((tm, d), lambda b, s: (row(b, s), 0)),
        compiler_params=_cparams(("parallel", "parallel")),
        name="merge_xattn",
    )(ya, yb, yc, z, z, z, x, bf(wa), bf(wb), bf(wc), bf(wo), gx[None], bf(wq), kv, bf(wxo))


def _ffn_kernel(x_ref, g_ref, wg_ref, wu_ref, wd_ref, gf_ref, o_ref, h_ref, acc_ref, *, final):
    c = pl.program_id(1)

    @pl.when(c == 0)
    def _():
        h_ref[...] = _rms(x_ref[...], g_ref[...]).astype(BF16)
        acc_ref[...] = x_ref[...]

    h = h_ref[...]
    gate = _dot(h, wg_ref[...])
    up = _dot(h, wu_ref[...])
    act = (gate * jax.nn.sigmoid(gate) * up).astype(BF16)
    acc_ref[...] += _dot(act, wd_ref[...])

    @pl.when(c == pl.num_programs(1) - 1)
    def _():
        y = acc_ref[...]
        o_ref[...] = _rms(y, gf_ref[...]) if final else y


def _ffn(x, g, w_gate_up, w_down, g_final, final):
    tm, tc = FFN_TM, FFN_TC
    t, d = x.shape
    nc = FFN_HIDDEN // tc
    wgu = w_gate_up.astype(BF16)
    return pl.pallas_call(
        functools.partial(_ffn_kernel, final=final),
        out_shape=jax.ShapeDtypeStruct((t, d), F32),
        grid=(t // tm, nc),
        in_specs=[pl.BlockSpec((tm, d), lambda i, c: (i, 0)),
                  pl.BlockSpec((1, d), lambda i, c: (0, 0)),
                  pl.BlockSpec((d, tc), lambda i, c: (0, c)),
                  pl.BlockSpec((d, tc), lambda i, c: (0, nc + c)),
                  pl.BlockSpec((tc, d), lambda i, c: (c, 0)),
                  pl.BlockSpec((1, d), lambda i, c: (0, 0))],
        out_specs=pl.BlockSpec((tm, d), lambda i, c: (i, 0)),
        scratch_shapes=[pltpu.VMEM((tm, d), BF16), pltpu.VMEM((tm, d), F32)],
        compiler_params=_cparams(("parallel", "arbitrary")),
        name="ffn",
    )(x, g[None], wgu, wgu, w_down.astype(BF16), g_final[None])


def _split_w_in(w):
    k_rope = w[:, O_KR:O_KR + QK_ROPE]
    kv = lambda kind: w[:, O_KV + GD * kind:O_KV + GD * (kind + 1)]
    pad = jnp.zeros((w.shape[0], Z_COLS - Z_GN - 3 * NSA_HEADS), w.dtype)
    wz = jnp.concatenate([
        w[:, O_GM:O_GM + 3 * D_MODEL],
        w[:, O_GLU:O_GLU + 2 * CONV_CH],
        w[:, O_Q:O_Q + NSA_HEADS * NSA_DH],
        w[:, O_CKV:O_CKV + KV_RANK],
        k_rope, _swap_halves(k_rope),
        w[:, O_CQ:O_CQ + Q_RANK],
        kv(0), kv(1),
        w[:, O_GN:O_GN + 3 * NSA_HEADS], pad], axis=1).astype(BF16)
    per_head = lambda wkind: jnp.tile(wkind.reshape(-1, NSA_G, 1, NSA_DH), (1, 1, NSA_HG, 1)).reshape(-1, NSA_G * KREP)
    wk = jnp.concatenate([per_head(kv(2)), per_head(kv(4))], axis=1).astype(BF16)
    wvt = jnp.concatenate([kv(3), kv(5)], axis=1).T.astype(BF16)
    return wz, wk, wvt


def kernel(x, mem, positions, rel_bias, norm_mix, norm_xattn, norm_mem, norm_ffn, norm_final, w_in, conv_w, conv_b, conv_ln_g, conv_ln_b, w_branch_conv, cmp_pos_k, cmp_w1_k, cmp_b1_k, cmp_w2_k, cmp_pos_v, cmp_w1_v, cmp_b1_v, cmp_w2_v, w_branch_nsa, mla_norm_q, mla_norm_kv, w_uq, w_ukv, w_branch_mla, w_out, w_xq, w_xkv, w_xo, w_gate_up, w_down):
    bsz, seq, d = x.shape
    depth = w_in.shape[0]
    t = bsz * seq
    m_len = mem.shape[1]
    xt = x.reshape(t, d)
    memt = mem.reshape(bsz * m_len, d)
    rope_tab = _rope_table(positions)
    tables = _nsa_tables(rel_bias)
    for l in range(depth):
        wz, wk, wvt = _split_w_in(w_in[l])
        z, zc = _in_proj(xt, norm_mix[l][None], wz)
        kk, vvt = _kv_proj(xt, norm_mix[l][None], wk, wvt, bsz, seq)
        ya = _conv_module(z, conv_w[l], conv_b[l], conv_ln_g[l], conv_ln_b[l], bsz, seq)
        kc, vct = _compress(zc, jnp.stack([cmp_pos_k[l], cmp_pos_v[l]]), jnp.stack([cmp_w1_k[l], cmp_w1_v[l]]),
                            jnp.stack([cmp_b1_k[l], cmp_b1_v[l]]), jnp.stack([cmp_w2_k[l], cmp_w2_v[l]]), bsz, seq)
        yb = _nsa_attention(z, kc, vct, kk, vvt, tables, bsz, seq)
        qf, kf, vt = _mla_proj(z, rope_tab, mla_norm_q[l], mla_norm_kv[l], w_uq[l], w_ukv[l], bsz, seq)
        yc = _mla_attention(qf, kf, vt, bsz, seq)
        mem_kv = _norm_matmul(memt, norm_mem[l][None], w_xkv[l].astype(BF16), 256, 1024, BF16)
        mem_kv = mem_kv.reshape(bsz, m_len, 2 * XATTN_HEADS * XATTN_DH)
        xt = _merge_xattn(ya, yb, yc, z, xt, w_branch_conv[l], w_branch_nsa[l], w_branch_mla[l], w_out[l],
                          norm_xattn[l], w_xq[l], mem_kv, w_xo[l], bsz, seq)
        xt = _ffn(xt, norm_ffn[l], w_gate_up[l], w_down[l], norm_final, l == depth - 1)
    return xt.reshape(bsz, seq, d)
```

```python
import functools
import math

import numpy as np
import jax
import jax.numpy as jnp
from jax import lax
from jax.experimental import pallas as pl
from jax.experimental.pallas import tpu as pltpu

F32 = jnp.float32
BF16 = jnp.bfloat16

EPS = 1e-6
NEG_INF = -1e30
FORCE_SCORE = 1e4

D_MODEL = 1024
CONV_CH = 512
CONV_WIDTH = 31
NSA_HEADS = 8
NSA_G = 2
NSA_HG = NSA_HEADS // NSA_G
NSA_DH = 64
CMP_BLOCK = 32
CMP_STRIDE = 16
CMP_HIDDEN = 256
SLC_BLOCK = 64
N_SELECT = 16
WINDOW = 512
NSA_QB = 64
MLA_HEADS = 4
Q_RANK = 384
KV_RANK = 256
QK_NOPE = 128
QK_ROPE = 64
V_DIM = 128
ROPE_THETA = 10000.0
REL_BUCKETS = 32
REL_MAX_DIST = 128
XATTN_HEADS = 4
XATTN_DH = 128
FFN_HIDDEN = 2816

LANES = 128
SUBLANES = 8

O_GLU, O_Q, O_KV, O_GN, O_CQ, O_CKV, O_KR, O_GM = 0, 1024, 1536, 2304, 2328, 2712, 2968, 3032
GD = NSA_G * NSA_DH

Z_GM = 0
Z_UA = 3072
Z_UB = 3584
Z_Q = 4096
Z_CKV = 4608
Z_KR = 4864
Z_CQ = 4992
Z_CMP = 5376
Z_GN = 5632
Z_COLS = 5760

VMEM_LIMIT = 56 * 1024 * 1024

TOK_TILE = 512
IN_PROJ_TM = 1024
IN_PROJ_TN = 1152
FFN_TM = 1024
FFN_TC = 256
MLA_TQ = 512
MLA_TK = 1024
MLA_HEADS_PER_STEP = 2

LOG2E = math.log2(math.e)


def _cparams(sem):
    return pltpu.CompilerParams(dimension_semantics=sem, vmem_limit_bytes=VMEM_LIMIT)


def _rms(x, g):
    return x * lax.rsqrt(jnp.mean(x * x, axis=-1, keepdims=True) + EPS) * g


def _dot(a, b):
    return jnp.dot(a, b, preferred_element_type=F32)


def _dot_nt(a, b):
    return lax.dot_general(a, b, (((1,), (1,)), ((), ())), preferred_element_type=F32)


def _norm_matmul_kernel(x_ref, g_ref, w_ref, o_ref, h_ref):
    @pl.when(pl.program_id(1) == 0)
    def _():
        h_ref[...] = _rms(x_ref[...], g_ref[...]).astype(BF16)

    o_ref[...] = _dot(h_ref[...], w_ref[...]).astype(o_ref.dtype)


def _norm_matmul(x, g, w, tm, tn, out_dtype):
    m, k = x.shape
    n = w.shape[1]
    return pl.pallas_call(
        _norm_matmul_kernel,
        out_shape=jax.ShapeDtypeStruct((m, n), out_dtype),
        grid=(m // tm, n // tn),
        in_specs=[pl.BlockSpec((tm, k), lambda i, j: (i, 0)),
                  pl.BlockSpec((1, k), lambda i, j: (0, 0)),
                  pl.BlockSpec((k, tn), lambda i, j: (0, j))],
        out_specs=pl.BlockSpec((tm, tn), lambda i, j: (i, j)),
        scratch_shapes=[pltpu.VMEM((tm, k), BF16)],
        compiler_params=_cparams(("parallel", "arbitrary")),
        name="norm_matmul",
    )(x, g, w)


def _in_proj_kernel(x_ref, g_ref, w_ref, z_ref, zc_ref, h_ref, *, cmp_tile, cmp_off):
    @pl.when(pl.program_id(1) == 0)
    def _():
        h_ref[...] = _rms(x_ref[...], g_ref[...]).astype(BF16)

    acc = _dot(h_ref[...], w_ref[...])
    z_ref[...] = acc.astype(BF16)

    @pl.when(pl.program_id(1) == cmp_tile)
    def _():
        zc_ref[...] = acc[:, cmp_off:cmp_off + 2 * GD]


def _in_proj(x, g, w):
    tm, tn = IN_PROJ_TM, IN_PROJ_TN
    m, k = x.shape
    n = w.shape[1]
    return pl.pallas_call(
        functools.partial(_in_proj_kernel, cmp_tile=Z_CMP // tn, cmp_off=Z_CMP % tn),
        out_shape=(jax.ShapeDtypeStruct((m, n), BF16), jax.ShapeDtypeStruct((m, 2 * GD), F32)),
        grid=(m // tm, n // tn),
        in_specs=[pl.BlockSpec((tm, k), lambda i, j: (i, 0)),
                  pl.BlockSpec((1, k), lambda i, j: (0, 0)),
                  pl.BlockSpec((k, tn), lambda i, j: (0, j))],
        out_specs=(pl.BlockSpec((tm, tn), lambda i, j: (i, j)),
                   pl.BlockSpec((tm, 2 * GD), lambda i, j: (i, 0))),
        scratch_shapes=[pltpu.VMEM((tm, k), BF16)],
        compiler_params=_cparams(("parallel", "arbitrary")),
        name="in_proj",
    )(x, g, w)


def _kv_proj_kernel(x_ref, g_ref, wk_ref, wvt_ref, k_ref, vt_ref):
    @pl.when(pl.program_id(1) == 0)
    def _():
        k_ref[...] = jnp.zeros(k_ref.shape, BF16)
        vt_ref[...] = jnp.zeros(vt_ref.shape, BF16)

    @pl.when(pl.program_id(1) > 0)
    def _():
        h = _rms(x_ref[...], g_ref[...]).astype(BF16)
        k_ref[0] = _dot(h, wk_ref[...]).astype(BF16)
        vt_ref[0] = _dot_nt(wvt_ref[...], h).astype(BF16)


def _kv_proj(x, g, wk, wvt, bsz, seq):
    tm = WINDOW
    nst = seq // tm
    d = x.shape[1]
    nk = wk.shape[1]
    nv = wvt.shape[0]
    return pl.pallas_call(
        _kv_proj_kernel,
        out_shape=(jax.ShapeDtypeStruct((bsz, seq + tm, nk), BF16),
                   jax.ShapeDtypeStruct((bsz, nv, seq + tm), BF16)),
        grid=(bsz, nst + 1),
        in_specs=[pl.BlockSpec((tm, d), lambda b, s: (b * nst + jnp.maximum(s - 1, 0), 0)),
                  pl.BlockSpec((1, d), lambda b, s: (0, 0)),
                  pl.BlockSpec((d, nk), lambda b, s: (0, 0)),
                  pl.BlockSpec((nv, d), lambda b, s: (0, 0))],
        out_specs=(pl.BlockSpec((1, tm, nk), lambda b, s: (b, s, 0)),
                   pl.BlockSpec((1, nv, tm), lambda b, s: (b, 0, s))),
        compiler_params=_cparams(("parallel", "arbitrary")),
        name="nsa_kv_proj",
    )(x, g, wk, wvt)


CONV_HALO = 32


CONV_ROWS = 64


def _conv_kernel(a_ref, b_ref, w_ref, cb_ref, lg_ref, lb_ref, o_ref, buf_ref, sh_ref, *, ts):
    @pl.when(pl.program_id(1) == 0)
    def _():
        buf_ref[0:CONV_HALO, :] = jnp.zeros((CONV_HALO, CONV_CH), F32)

    buf_ref[CONV_HALO:CONV_HALO + ts, :] = a_ref[...].astype(F32) * jax.nn.sigmoid(b_ref[...].astype(F32))
    span = ts + CONV_HALO - SUBLANES
    for r in range(1, SUBLANES):
        sh_ref[r - 1, 0:span, :] = buf_ref[r:r + span, :]
    off = CONV_HALO - (CONV_WIDTH - 1)

    def rows(i, carry):
        r0 = pl.multiple_of(i * CONV_ROWS, CONV_ROWS)
        acc = jnp.zeros((CONV_ROWS, CONV_CH), F32) + cb_ref[...]
        for k in range(CONV_WIDTH):
            res, base = (off + k) % SUBLANES, (off + k) // SUBLANES * SUBLANES
            if res == 0:
                tap = buf_ref[pl.ds(r0 + base, CONV_ROWS), :]
            else:
                tap = sh_ref[res - 1, pl.ds(r0 + base, CONV_ROWS), :]
            acc = acc + tap * w_ref[k:k + 1, :]
        mu = jnp.mean(acc, axis=-1, keepdims=True)
        xc = acc - mu
        var = jnp.mean(xc * xc, axis=-1, keepdims=True)
        y = xc * lax.rsqrt(var + EPS) * lg_ref[...] + lb_ref[...]
        o_ref[pl.ds(r0, CONV_ROWS), :] = (y * jax.nn.sigmoid(y)).astype(BF16)
        return carry

    lax.fori_loop(0, ts // CONV_ROWS, rows, 0)
    buf_ref[0:CONV_HALO, :] = buf_ref[ts:ts + CONV_HALO, :]


def _conv_module(z, conv_w, conv_b, ln_g, ln_b, bsz, seq):
    ts = TOK_TILE
    nst = seq // ts
    wpad = jnp.zeros((32, CONV_CH), F32).at[:CONV_WIDTH].set(conv_w)
    return pl.pallas_call(
        functools.partial(_conv_kernel, ts=ts),
        out_shape=jax.ShapeDtypeStruct((bsz * seq, CONV_CH), BF16),
        grid=(bsz, nst),
        in_specs=[pl.BlockSpec((ts, CONV_CH), lambda b, s: (b * nst + s, Z_UA // CONV_CH)),
                  pl.BlockSpec((ts, CONV_CH), lambda b, s: (b * nst + s, Z_UB // CONV_CH)),
                  pl.BlockSpec((32, CONV_CH), lambda b, s: (0, 0)),
                  pl.BlockSpec((1, CONV_CH), lambda b, s: (0, 0)),
                  pl.BlockSpec((1, CONV_CH), lambda b, s: (0, 0)),
                  pl.BlockSpec((1, CONV_CH), lambda b, s: (0, 0))],
        out_specs=pl.BlockSpec((ts, CONV_CH), lambda b, s: (b * nst + s, 0)),
        scratch_shapes=[pltpu.VMEM((ts + CONV_HALO, CONV_CH), F32),
                        pltpu.VMEM((SUBLANES - 1, ts + CONV_HALO - SUBLANES, CONV_CH), F32)],
        compiler_params=_cparams(("arbitrary", "arbitrary")),
        name="conv_module",
    )(z, z, wpad, conv_b[None], ln_g[None], ln_b[None])


def _compress_kernel(xk_ref, xv_ref, pos_ref, w1_ref, b1_ref, w2k_ref, w2v_ref, kc_ref, vct_ref, *, nch):
    for kind, (x_ref, w2_ref) in enumerate(((xk_ref, w2k_ref), (xv_ref, w2v_ref))):
        a = jnp.zeros((nch, NSA_G * CMP_HIDDEN), F32)
        b = jnp.zeros((nch, NSA_G * CMP_HIDDEN), F32)
        for l in range(CMP_STRIDE):
            xs = x_ref[pl.ds(l, nch, stride=CMP_STRIDE), :]
            a = a + _dot((xs + pos_ref[kind, l:l + 1, :]).astype(BF16), w1_ref[kind, l])
            b = b + _dot((xs + pos_ref[kind, CMP_STRIDE + l:CMP_STRIDE + l + 1, :]).astype(BF16),
                         w1_ref[kind, CMP_STRIDE + l])
        pre = a + pltpu.roll(b, nch - 1, 0) + b1_ref[kind]
        out = _dot(jax.nn.gelu(pre).astype(BF16), w2_ref[...])
        if kind == 0:
            kc_ref[0] = out.astype(BF16)
        else:
            vct_ref[0] = out.T.astype(BF16)


def _blockdiag2(w):
    z = jnp.zeros_like(w)
    return jnp.concatenate([jnp.concatenate([w, z], axis=-1), jnp.concatenate([z, w], axis=-1)], axis=-2)


def _compress(z, pos, w1, b1, w2, bsz, seq):
    nch = seq // CMP_STRIDE
    pos2 = jnp.concatenate([pos, pos], axis=-1)
    w1e = _blockdiag2(w1.reshape(2, CMP_BLOCK, NSA_DH, CMP_HIDDEN)).astype(BF16)
    b1e = jnp.concatenate([b1, b1], axis=-1)[:, None]
    w2k = _blockdiag2(jnp.tile(w2[0], (1, NSA_HG))).astype(BF16)
    w2v = _blockdiag2(w2[1]).astype(BF16)
    full = lambda a: pl.BlockSpec(a.shape, lambda b: (0,) * a.ndim)
    return pl.pallas_call(
        functools.partial(_compress_kernel, nch=nch),
        out_shape=(jax.ShapeDtypeStruct((bsz, nch, NSA_G * NSA_HG * NSA_DH), BF16),
                   jax.ShapeDtypeStruct((bsz, GD, nch), BF16)),
        grid=(bsz,),
        in_specs=[pl.BlockSpec((seq, GD), lambda b: (b, 0)),
                  pl.BlockSpec((seq, GD), lambda b: (b, 1)),
                  full(pos2), full(w1e), full(b1e), full(w2k), full(w2v)],
        out_specs=(pl.BlockSpec((1, nch, NSA_G * NSA_HG * NSA_DH), lambda b: (b, 0, 0)),
                   pl.BlockSpec((1, GD, nch), lambda b: (b, 0, 0))),
        compiler_params=_cparams(("parallel",)),
        name="nsa_compress",
    )(z, z, pos2, w1e, b1e, w2k, w2v)


NSA_QP = 2 * NSA_QB
NEAR_KEYS = 256
WIN_KEYS = 640
CMP_TAB_ROWS = 512
CMP_TAB_ZERO = 256


def _t5_bucket_np(d):
    exact = REL_BUCKETS // 2
    d = np.maximum(d, 0)
    ratio = np.log(np.maximum(d, 1).astype(np.float32) / np.float32(exact)) / np.float32(math.log(REL_MAX_DIST / exact))
    large = np.minimum(exact + (ratio * (REL_BUCKETS - exact)).astype(np.int32), REL_BUCKETS - 1)
    return np.where(d < exact, d, large).astype(np.int32)


def _bucket_thresholds():
    exact = REL_BUCKETS // 2
    bk = _t5_bucket_np(np.arange(4 * REL_MAX_DIST))
    assert np.all(np.diff(bk) >= 0) and bk[-1] == REL_BUCKETS - 1
    return [int(np.argmax(bk >= k)) for k in range(exact + 1, REL_BUCKETS)]


def _bias_rows(rel_ref, dist, valid, shift):
    exact = REL_BUCKETS // 2
    bucket = jnp.full(dist.shape, exact, jnp.int32)
    for thr in _bucket_thresholds():
        bucket = bucket + jnp.where(dist >= thr, 1, 0)
    bucket = jnp.where(dist < exact, dist, bucket)
    val = jnp.zeros(dist.shape, F32)
    for bkt in range(REL_BUCKETS):
        val = jnp.where(bucket == bkt, rel_ref[0, bkt:bkt + 1, :], val)
    if shift:
        val = val - rel_ref[0, REL_BUCKETS - 1:REL_BUCKETS, :]
    return jnp.where(valid, val * LOG2E, NEG_INF)


def _nsa_bias_kernel(rel_ref, tc_ref, tn_ref, tw_ref):
    hq = NSA_HG * NSA_QP
    rows = 128

    def dist_of(nrows, r0, fn):
        r = r0 + lax.broadcasted_iota(jnp.int32, (nrows, hq), 0)
        t = lax.bitwise_and(lax.broadcasted_iota(jnp.int32, (nrows, hq), 1), NSA_QP - 1)
        return fn(r, t)

    for r0 in range(0, CMP_TAB_ROWS, rows):
        d = dist_of(rows, r0, lambda r, t: t - CMP_STRIDE * (r - CMP_TAB_ZERO) - (CMP_BLOCK - 1))
        tc_ref[0, r0:r0 + rows, :] = _bias_rows(rel_ref, d, d >= 0, False)
    for r0 in range(0, NEAR_KEYS, rows):
        d = dist_of(rows, r0, lambda r, t: NEAR_KEYS // 2 + t - r)
        tn_ref[0, r0:r0 + rows, :] = _bias_rows(rel_ref, d, d >= 0, True)
    for r0 in range(0, WIN_KEYS, rows):
        d = dist_of(rows, r0, lambda r, t: WINDOW + t - r)
        tw_ref[0, r0:r0 + rows, :] = _bias_rows(rel_ref, d, (d >= 0) & (d < WINDOW), False)


def _nsa_tables(rel_bias):
    hq = NSA_HG * NSA_QP
    rel4 = jnp.repeat(rel_bias.reshape(REL_BUCKETS, NSA_G, NSA_HG).transpose(1, 0, 2), NSA_QP, axis=-1)
    spec = lambda r: pl.BlockSpec((1, r, hq), lambda g: (g, 0, 0))
    return pl.pallas_call(
        _nsa_bias_kernel,
        out_shape=(jax.ShapeDtypeStruct((NSA_G, CMP_TAB_ROWS, hq), F32),
                   jax.ShapeDtypeStruct((NSA_G, NEAR_KEYS, hq), F32),
                   jax.ShapeDtypeStruct((NSA_G, WIN_KEYS, hq), F32)),
        grid=(NSA_G,),
        in_specs=[spec(REL_BUCKETS)],
        out_specs=(spec(CMP_TAB_ROWS), spec(NEAR_KEYS), spec(WIN_KEYS)),
        compiler_params=_cparams(("parallel",)),
        name="nsa_bias_tables",
    )(rel4)


SEL_PAD = 8
FAR_KEYS = 1024
KV_FRONT = WINDOW
KREP = NSA_HG * NSA_DH


def _softmax_cols(s):
    m = jnp.max(s, axis=0, keepdims=True)
    p = jnp.exp2(s - m)
    return m, p, jnp.sum(p, axis=0, keepdims=True)


def _mask_blocks(s, mask_ref, row0, nblk):
    parts = []
    for jj in range(nblk):
        row = mask_ref[pl.ds(row0 + jj, 1), :]
        parts.append(jnp.where(row > 0.0, s[SLC_BLOCK * jj:SLC_BLOCK * (jj + 1)], NEG_INF))
    return jnp.concatenate(parts, axis=0)


RANK_STEP = 16


def _switch(index, branches):
    if len(branches) == 1:
        return branches[0]()
    return lax.cond(index == 0, branches[0], lambda: _switch(index - 1, branches[1:]))


def _rank_select(score_ref, n_sb, n_sel, n_live):
    groups = n_live // SUBLANES
    sub = lax.broadcasted_iota(jnp.int32, (SUBLANES, NSA_QP), 0)
    tiles = [score_ref[SUBLANES * v:SUBLANES * (v + 1), :] for v in range(groups)]
    cnts = [jnp.zeros((SUBLANES, NSA_QP), F32) for _ in range(groups)]
    for jp in range(n_live):
        row = score_ref[jp:jp + 1, :]
        for v in range(groups):
            lo = SUBLANES * v
            if jp < lo:
                beats = row >= tiles[v]
            elif jp >= lo + SUBLANES - 1:
                beats = row > tiles[v]
            else:
                beats = (row > tiles[v]) | ((row == tiles[v]) & (sub > jp - lo))
            cnts[v] = cnts[v] + jnp.where(beats, 1.0, 0.0)
    sel = jnp.where(jnp.concatenate(cnts, axis=0) < float(n_sel), 1.0, 0.0)
    if n_live < n_sb:
        sel = jnp.concatenate([sel, jnp.zeros((n_sb - n_live, NSA_QP), F32)], axis=0)
    return sel


def _nsa_kernel(q_ref, gate_ref, kc_ref, vct_ref, ks_ref, kw_ref, vst_ref, vwt_ref,
                tc_ref, tn_ref, tw_ref, cov_ref, rep_ref, o_ref,
                sel_ref, selfar_ref, score_ref, *, n_sb):
    p2 = pl.program_id(1)
    hq = NSA_HG * NSA_QP
    nch = kc_ref.shape[1]
    groups = range(NSA_G)
    kcol = lambda g: slice(KREP * g, KREP * (g + 1))
    vrow = lambda g: slice(NSA_DH * g, NSA_DH * (g + 1))

    qs = []
    for g in groups:
        qb = (q_ref[:, kcol(g)].astype(F32) * (NSA_DH ** -0.5 * LOG2E)).astype(BF16)
        lane_head = lax.shift_right_logical(lax.broadcasted_iota(jnp.int32, qb.shape, 1), 6)
        qs.append(jnp.concatenate([jnp.where(lane_head == h, qb, jnp.zeros_like(qb)) for h in range(NSA_HG)], axis=0))

    start_c = pl.multiple_of(CMP_TAB_ZERO - (NSA_QP // CMP_STRIDE) * p2, SUBLANES)
    lane = lax.broadcasted_iota(jnp.int32, (1, hq), 1)
    tq = NSA_QP * p2 + lax.bitwise_and(lane, NSA_QP - 1)
    anyv = jnp.where(tq >= CMP_BLOCK - 1, 1.0, 0.0)
    jrow = lax.broadcasted_iota(jnp.int32, (n_sb, NSA_QP), 0)
    tok = lax.broadcasted_iota(jnp.int32, (n_sb, NSA_QP), 1)
    cur = 2 * p2 + lax.shift_right_logical(tok, 6)
    forced = (jrow == 0) | (jrow == cur) | (jrow == cur - 1)
    def cmp_branch(g, rows):
        sc = _dot_nt(kc_ref[0, 0:rows, kcol(g)], qs[g]) + tc_ref[g, pl.ds(start_c, rows), :]
        _, pc, lc = _softmax_cols(sc)
        pc = pc * (anyv / lc)
        o = _dot(vct_ref[0, vrow(g), 0:rows], pc.astype(BF16))
        psum = pc[:, 0:NSA_QP]
        for h in range(1, NSA_HG):
            psum = psum + pc[:, NSA_QP * h:NSA_QP * (h + 1)]
        p_hi = psum.astype(BF16)
        p_lo = (psum - p_hi.astype(F32)).astype(BF16)
        return o, _dot(cov_ref[:, 0:rows], p_hi) + _dot(cov_ref[:, 0:rows], p_lo)

    o_cmp = []
    for g in groups:
        o, imp = lax.cond(p2 < nch // (2 * NSA_QP // CMP_STRIDE),
                          functools.partial(cmp_branch, g, nch // 2), functools.partial(cmp_branch, g, nch))
        o_cmp.append(o)
        score_ref[g] = jnp.where(forced, FORCE_SCORE, jnp.where(jrow <= cur, imp, -1.0))

    zeros8 = jnp.zeros((SEL_PAD, hq), F32)
    jrow4 = lax.broadcasted_iota(jnp.int32, (n_sb, hq), 0)
    n_sel = min(N_SELECT, n_sb)
    for g in groups:
        variants = [functools.partial(_rank_select, score_ref.at[g], n_sb, n_sel, live)
                    for live in range(RANK_STEP, n_sb + 1, RANK_STEP)]
        sel = _switch(lax.div(2 * p2 + 1, RANK_STEP), variants).astype(BF16)
        sel4 = _dot(sel, rep_ref[...])
        sel_ref[g, 0:SEL_PAD, :] = zeros8
        sel_ref[g, SEL_PAD + n_sb:2 * SEL_PAD + n_sb, :] = zeros8
        sel_ref[g, SEL_PAD:SEL_PAD + n_sb, :] = sel4
        selfar_ref[g, 0:SEL_PAD, :] = zeros8
        selfar_ref[g, SEL_PAD + n_sb:2 * SEL_PAD + n_sb, :] = zeros8
        selfar_ref[g, SEL_PAD:SEL_PAD + n_sb, :] = jnp.where(jrow4 < 2 * p2 - 2, sel4, 0.0)

    win0 = pl.multiple_of(NSA_QP * p2, LANES)
    near0 = pl.multiple_of(win0 + KV_FRONT - NEAR_KEYS // 2, LANES)
    state = []
    for g in groups:
        s = _dot_nt(ks_ref[0, pl.ds(near0, NEAR_KEYS), kcol(g)], qs[g]) + tn_ref[g]
        s = _mask_blocks(s, sel_ref.at[g], 2 * p2 - 2 + SEL_PAD, NEAR_KEYS // SLC_BLOCK)
        m_s, p_s, l_s = _softmax_cols(s)
        state += [m_s, l_s, _dot(vst_ref[0, vrow(g), pl.ds(near0, NEAR_KEYS)], p_s.astype(BF16))]

    def far_body(c, carry):
        k0 = pl.multiple_of(FAR_KEYS * c + KV_FRONT, LANES)
        sfs = [_dot_nt(ks_ref[0, pl.ds(k0, FAR_KEYS), kcol(g)], qs[g]) for g in groups]
        out = []
        for g in groups:
            m_old, l_old, acc_old = carry[3 * g:3 * g + 3]
            sf = _mask_blocks(sfs[g], selfar_ref.at[g], (FAR_KEYS // SLC_BLOCK) * c + SEL_PAD, FAR_KEYS // SLC_BLOCK)
            m_new = jnp.maximum(m_old, jnp.max(sf, axis=0, keepdims=True))
            alpha = jnp.exp2(m_old - m_new)
            pf = jnp.exp2(sf - m_new)
            l_new = alpha * l_old + jnp.sum(pf, axis=0, keepdims=True)
            acc_new = alpha * acc_old + _dot(vst_ref[0, vrow(g), pl.ds(k0, FAR_KEYS)], pf.astype(BF16))
            out += [m_new, l_new, acc_new]
        return tuple(out)

    n_far = lax.div(jnp.maximum(p2 - 1, 0) * NSA_QP + FAR_KEYS - 1, FAR_KEYS)
    state = lax.fori_loop(0, n_far, far_body, tuple(state))

    gt = gate_ref[...].astype(F32).T
    r = lax.broadcasted_iota(jnp.int32, (NSA_QP, NSA_QP), 0)
    c = lax.broadcasted_iota(jnp.int32, (NSA_QP, NSA_QP), 1)
    eye = jnp.where(r == c, 1.0, 0.0).astype(BF16)
    for g in groups:
        _, l_s, acc_s = state[3 * g:3 * g + 3]
        sw = _dot_nt(kw_ref[0, pl.ds(win0, WIN_KEYS), kcol(g)], qs[g]) + tw_ref[g]
        slabs = [sw[NSA_QP * j:NSA_QP * (j + 1)] for j in range(WIN_KEYS // NSA_QP)]
        for j in range(KV_FRONT // NSA_QP):
            slabs[j] = jnp.where(NSA_QP * j + win0 >= KV_FRONT, slabs[j], NEG_INF)
        _, p_w, l_w = _softmax_cols(jnp.concatenate(slabs, axis=0))
        acc_w = _dot(vwt_ref[0, vrow(g), pl.ds(win0, WIN_KEYS)], p_w.astype(BF16))
        gsel = jax.nn.sigmoid(gt[3 * NSA_HG * g:3 * NSA_HG * (g + 1)])
        gate = lambda b: jnp.concatenate([gsel[3 * h + b:3 * h + b + 1] for h in range(NSA_HG)], axis=1)
        out_t = (gate(0) * o_cmp[g] + (gate(1) / l_s) * acc_s + (gate(2) / l_w) * acc_w).astype(BF16)
        stacked = jnp.concatenate([out_t[:, NSA_QP * h:NSA_QP * (h + 1)] for h in range(NSA_HG)], axis=0)
        o_ref[:, kcol(g)] = _dot_nt(eye, stacked).astype(BF16)


def _nsa_attention(z, kc, vct, kk, vvt, tables, bsz, seq):
    g, hg, dh, qp = NSA_G, NSA_HG, NSA_DH, NSA_QP
    nstep = seq // qp
    n_sb = seq // SLC_BLOCK
    nch = kc.shape[1]
    hq = hg * qp
    sp = kk.shape[1]
    tc, tn, tw = tables
    c_start = CMP_STRIDE * np.arange(nch)
    s_start = SLC_BLOCK * np.arange(n_sb)
    cover_t = ((c_start[None, :] < s_start[:, None] + SLC_BLOCK)
               & (c_start[None, :] + CMP_BLOCK > s_start[:, None])
               & (np.arange(nch)[None, :] < (seq - CMP_BLOCK) // CMP_STRIDE + 1))
    cover_t = jnp.asarray(cover_t.astype(np.float32), BF16)
    rep = jnp.asarray(np.tile(np.eye(qp, dtype=np.float32), (1, hg)), BF16)
    full = lambda a: pl.BlockSpec(a.shape, lambda b, i: (0,) * a.ndim, pipeline_mode=pl.Buffered(1))
    qd = g * hg * dh
    return pl.pallas_call(
        functools.partial(_nsa_kernel, n_sb=n_sb),
        out_shape=jax.ShapeDtypeStruct((bsz * seq, qd), BF16),
        grid=(bsz, nstep),
        in_specs=[pl.BlockSpec((qp, qd), lambda b, i: (b * nstep + i, Z_Q // qd)),
                  pl.BlockSpec((qp, LANES), lambda b, i: (b * nstep + i, Z_GN // LANES)),
                  pl.BlockSpec((1, nch, g * KREP), lambda b, i: (b, 0, 0)),
                  pl.BlockSpec((1, g * dh, nch), lambda b, i: (b, 0, 0)),
                  pl.BlockSpec((1, sp, g * KREP), lambda b, i: (b, 0, 0)),
                  pl.BlockSpec((1, sp, g * KREP), lambda b, i: (b, 0, 1)),
                  pl.BlockSpec((1, g * dh, sp), lambda b, i: (b, 0, 0)),
                  pl.BlockSpec((1, g * dh, sp), lambda b, i: (b, 1, 0)),
                  full(tc), full(tn), full(tw), full(cover_t), full(rep)],
        out_specs=pl.BlockSpec((qp, qd), lambda b, i: (b * nstep + i, 0)),
        scratch_shapes=[pltpu.VMEM((g, n_sb + 2 * SEL_PAD, hq), F32),
                        pltpu.VMEM((g, n_sb + 2 * SEL_PAD, hq), F32),
                        pltpu.VMEM((g, n_sb, qp), F32)],
        compiler_params=_cparams(("parallel", "arbitrary")),
        name="nsa_attention",
    )(z, z, kc, vct, kk, kk, vvt, vvt, tc, tn, tw, cover_t, rep)


def _rope_table_kernel(pos_ref, inv_ref, sign_ref, o_ref):
    ang = pos_ref[...].astype(F32) * inv_ref[...]
    o_ref[...] = jnp.concatenate([jnp.cos(ang), jnp.sin(ang) * sign_ref[...]], axis=-1)


def _rope_table(positions):
    tm = TOK_TILE
    t = positions.size
    half = QK_ROPE // 2
    inv = ROPE_THETA ** (-jnp.arange(half, dtype=F32) / half)
    inv2 = jnp.concatenate([inv, inv])[None]
    sign = jnp.asarray(np.concatenate([-np.ones(half), np.ones(half)]).astype(np.float32))[None]
    return pl.pallas_call(
        _rope_table_kernel,
        out_shape=jax.ShapeDtypeStruct((t, 2 * QK_ROPE), F32),
        grid=(t // tm,),
        in_specs=[pl.BlockSpec((tm, 1), lambda i: (i, 0)),
                  pl.BlockSpec((1, QK_ROPE), lambda i: (0, 0)),
                  pl.BlockSpec((1, QK_ROPE), lambda i: (0, 0))],
        out_specs=pl.BlockSpec((tm, 2 * QK_ROPE), lambda i: (i, 0)),
        compiler_params=_cparams(("parallel",)),
        name="rope_table",
    )(positions.reshape(t, 1), inv2, sign)


MLA_HW = 256


def _mla_proj_kernel(cq_ref, ckv_ref, kr_ref, rope_ref, nq_ref, nkv_ref, wq_ref, wkn_ref, wvt_ref,
                     q_ref, k_ref, vt_ref):
    scale = (QK_NOPE + QK_ROPE) ** -0.5 * LOG2E
    rope = rope_ref[...]
    yq = _dot(_rms(cq_ref[...].astype(F32), nq_ref[...]).astype(BF16), wq_ref[...])
    ckv = _rms(ckv_ref[...].astype(F32), nkv_ref[...]).astype(BF16)
    ykn = _dot(ckv, wkn_ref[...])
    vt_ref[0] = _dot_nt(wvt_ref[...], ckv).astype(BF16)
    kp = kr_ref[...].astype(F32) * rope
    kp = kp + pltpu.roll(kp, QK_ROPE, 1)
    lane = lax.broadcasted_iota(jnp.int32, kp.shape, 1)
    kp = jnp.where(lane < QK_ROPE, kp, 0.0).astype(BF16)
    for h in range(MLA_HEADS):
        base = MLA_HW * h
        q_ref[:, base:base + QK_NOPE] = (yq[:, base:base + QK_NOPE] * scale).astype(BF16)
        qp = yq[:, base + QK_NOPE:base + MLA_HW] * rope
        qp = qp + pltpu.roll(qp, QK_ROPE, 1)
        q_ref[:, base + QK_NOPE:base + MLA_HW] = (qp * scale).astype(BF16)
        k_ref[:, base:base + QK_NOPE] = ykn[:, QK_NOPE * h:QK_NOPE * (h + 1)].astype(BF16)
        k_ref[:, base + QK_NOPE:base + MLA_HW] = kp


def _swap_halves(w):
    half = QK_ROPE // 2
    return jnp.concatenate([w[..., half:], w[..., :half]], axis=-1)


def _mla_proj(z, rope_tab, norm_q, norm_kv, w_uq, w_ukv, bsz, seq):
    tm = TOK_TILE
    t = z.shape[0]
    nst = seq // tm
    wq = w_uq.reshape(Q_RANK, MLA_HEADS, QK_NOPE + QK_ROPE)
    wq = jnp.concatenate([wq, _swap_halves(wq[..., QK_NOPE:])], axis=-1)
    wq = wq.reshape(Q_RANK, MLA_HEADS * MLA_HW).astype(BF16)
    wkv = w_ukv.reshape(KV_RANK, MLA_HEADS, QK_NOPE + V_DIM)
    wkn = wkv[..., :QK_NOPE].reshape(KV_RANK, MLA_HEADS * QK_NOPE).astype(BF16)
    wvt = wkv[..., QK_NOPE:].reshape(KV_RANK, MLA_HEADS * V_DIM).T.astype(BF16)
    hw = MLA_HEADS * MLA_HW
    hv = MLA_HEADS * V_DIM
    row = lambda b, s: b * nst + s
    return pl.pallas_call(
        _mla_proj_kernel,
        out_shape=(jax.ShapeDtypeStruct((t, hw), BF16),
                   jax.ShapeDtypeStruct((t, hw), BF16),
                   jax.ShapeDtypeStruct((bsz, hv, seq), BF16)),
        grid=(bsz, nst),
        in_specs=[pl.BlockSpec((tm, Q_RANK), lambda b, s: (row(b, s), Z_CQ // Q_RANK)),
                  pl.BlockSpec((tm, KV_RANK), lambda b, s: (row(b, s), Z_CKV // KV_RANK)),
                  pl.BlockSpec((tm, 2 * QK_ROPE), lambda b, s: (row(b, s), Z_KR // (2 * QK_ROPE))),
                  pl.BlockSpec((tm, 2 * QK_ROPE), lambda b, s: (row(b, s), 0)),
                  pl.BlockSpec((1, Q_RANK), lambda b, s: (0, 0)),
                  pl.BlockSpec((1, KV_RANK), lambda b, s: (0, 0)),
                  pl.BlockSpec((Q_RANK, hw), lambda b, s: (0, 0)),
                  pl.BlockSpec((KV_RANK, hv), lambda b, s: (0, 0)),
                  pl.BlockSpec((hv, KV_RANK), lambda b, s: (0, 0))],
        out_specs=(pl.BlockSpec((tm, hw), lambda b, s: (row(b, s), 0)),
                   pl.BlockSpec((tm, hw), lambda b, s: (row(b, s), 0)),
                   pl.BlockSpec((1, hv, tm), lambda b, s: (b, 0, s))),
        compiler_params=_cparams(("parallel", "parallel")),
        name="mla_proj",
    )(z, z, z, rope_tab, norm_q[None], norm_kv[None], wq, wkn, wvt)


def _mla_attn_kernel(q_ref, k_ref, vt_ref, o_ref, *, tq, tk, nh):
    iq = pl.program_id(2)
    cd = lax.div(iq * tq, tk)
    heads = range(nh)
    hcol = lambda h: slice(MLA_HW * h, MLA_HW * (h + 1))
    vrow = lambda h: slice(V_DIM * h, V_DIM * (h + 1))
    qs = [q_ref[:, hcol(h)] for h in heads]

    def scores(c, h):
        k0 = pl.multiple_of(c * tk, tk)
        return _dot_nt(k_ref[0, pl.ds(k0, tk), hcol(h)], qs[h])

    def diagonal(nk):
        k0 = pl.multiple_of((iq + 1) * tq - nk, tq)
        kpos = k0 + lax.broadcasted_iota(jnp.int32, (nk, tq), 0)
        qpos = iq * tq + lax.broadcasted_iota(jnp.int32, (nk, tq), 1)
        st = []
        for h in heads:
            s = jnp.where(kpos <= qpos, _dot_nt(k_ref[0, pl.ds(k0, nk), hcol(h)], qs[h]), NEG_INF)
            m0, p0, l0 = _softmax_cols(s)
            st += [m0, l0, _dot(vt_ref[0, vrow(h), pl.ds(k0, nk)], p0.astype(BF16))]
        return tuple(st)

    assert tk == 2 * tq
    state = lax.cond(lax.rem(iq, 2) == 0, lambda: diagonal(tq), lambda: diagonal(tk))

    def body(c, carry):
        k0 = pl.multiple_of(c * tk, tk)
        ss = [scores(c, h) for h in heads]
        out = []
        for h in heads:
            m_old, l_old, acc_old = carry[3 * h:3 * h + 3]
            m_new = jnp.maximum(m_old, jnp.max(ss[h], axis=0, keepdims=True))
            alpha = jnp.exp2(m_old - m_new)
            p = jnp.exp2(ss[h] - m_new)
            l_new = alpha * l_old + jnp.sum(p, axis=0, keepdims=True)
            acc_new = alpha * acc_old + _dot(vt_ref[0, vrow(h), pl.ds(k0, tk)], p.astype(BF16))
            out += [m_new, l_new, acc_new]
        return tuple(out)

    state = lax.fori_loop(0, cd, body, tuple(state))
    for h in heads:
        _, l, acc = state[3 * h:3 * h + 3]
        o_ref[:, vrow(h)] = (acc / l).T.astype(BF16)


def _mla_attention(qf, kf, vt, bsz, seq):
    tq, tk, nh = MLA_TQ, MLA_TK, MLA_HEADS_PER_STEP
    h = MLA_HEADS
    nq = seq // tq
    k3 = kf.reshape(bsz, seq, h * MLA_HW)
    return pl.pallas_call(
        functools.partial(_mla_attn_kernel, tq=tq, tk=tk, nh=nh),
        out_shape=jax.ShapeDtypeStruct((bsz * seq, h * V_DIM), BF16),
        grid=(bsz, h // nh, nq),
        in_specs=[pl.BlockSpec((tq, nh * MLA_HW), lambda b, hh, i: (b * nq + i, hh)),
                  pl.BlockSpec((1, seq, nh * MLA_HW), lambda b, hh, i: (b, 0, hh)),
                  pl.BlockSpec((1, nh * V_DIM, seq), lambda b, hh, i: (b, hh, 0))],
        out_specs=pl.BlockSpec((tq, nh * V_DIM), lambda b, hh, i: (b * nq + i, hh)),
        compiler_params=_cparams(("parallel", "parallel", "arbitrary")),
        name="mla_attention",
    )(qf, k3, vt)


def _merge_xattn_kernel(ya_ref, yb_ref, yc_ref, ga_ref, gb_ref, gc_ref, x_ref,
                        wa_ref, wb_ref, wc_ref, wo_ref,
                        gx_ref, wq_ref, kv_ref, wxo_ref, o_ref):
    sig = lambda ref: jax.nn.sigmoid(ref[...].astype(F32))
    y = (sig(ga_ref) * _dot(ya_ref[...], wa_ref[...])
         + sig(gb_ref) * _dot(yb_ref[...], wb_ref[...])
         + sig(gc_ref) * _dot(yc_ref[...], wc_ref[...]))
    x = x_ref[...] + _dot(y.astype(BF16), wo_ref[...])
    h = _rms(x, gx_ref[...]).astype(BF16)
    q = _dot(h, wq_ref[...]) * XATTN_DH ** -0.5
    hd = XATTN_HEADS * XATTN_DH
    outs = []
    for hh in range(XATTN_HEADS):
        qh = q[:, XATTN_DH * hh:XATTN_DH * (hh + 1)].astype(BF16)
        kh = kv_ref[0, :, XATTN_DH * hh:XATTN_DH * (hh + 1)]
        vh = kv_ref[0, :, hd + XATTN_DH * hh:hd + XATTN_DH * (hh + 1)]
        s = _dot_nt(qh, kh)
        m = jnp.max(s, axis=-1, keepdims=True)
        p = jnp.exp(s - m)
        p = p / jnp.sum(p, axis=-1, keepdims=True)
        outs.append(_dot(p.astype(BF16), vh))
    o = jnp.concatenate(outs, axis=-1).astype(BF16)
    o_ref[...] = x + _dot(o, wxo_ref[...])


def _merge_xattn(ya, yb, yc, z, x, wa, wb, wc, wo, gx, wq, kv, wxo, bsz, seq):
    tm = TOK_TILE
    t, d = x.shape
    nst = seq // tm
    m_len = kv.shape[1]
    hd = XATTN_HEADS * XATTN_DH
    row = lambda b, s: b * nst + s
    act = pl.BlockSpec((tm, ya.shape[1]), lambda b, s: (row(b, s), 0))
    gate = lambda k: pl.BlockSpec((tm, d), lambda b, s: (row(b, s), Z_GM // d + k))
    const = lambda shape: pl.BlockSpec(shape, lambda b, s: (0, 0))
    bf = lambda w: w.astype(BF16)
    return pl.pallas_call(
        _merge_xattn_kernel,
        out_shape=jax.ShapeDtypeStruct((t, d), F32),
        grid=(bsz, nst),
        in_specs=[act, act, act, gate(0), gate(1), gate(2),
                  pl.BlockSpec((tm, d), lambda b, s: (row(b, s), 0)),
                  const((ya.shape[1], d)), const((ya.shape[1], d)), const((ya.shape[1], d)), const((d, d)),
                  const((1, d)), const((d, hd)),
                  pl.BlockSpec((1, m_len, 2 * hd), lambda b, s: (b, 0, 0)),
                  const((hd, d))],
        out_specs=pl.BlockSpec((tm, d), lambda b, s: (row(b, s), 0)),
        compiler_params=_cparams(("parallel", "parallel")),
        name="merge_xattn",
    )(ya, yb, yc, z, z, z, x, bf(wa), bf(wb), bf(wc), bf(wo), gx[None], bf(wq), kv, bf(wxo))


def _ffn_kernel(x_ref, g_ref, wg_ref, wu_ref, wd_ref, gf_ref, o_ref, h_ref, acc_ref, *, final):
    c = pl.program_id(1)

    @pl.when(c == 0)
    def _():
        h_ref[...] = _rms(x_ref[...], g_ref[...]).astype(BF16)
        acc_ref[...] = x_ref[...]

    h = h_ref[...]
    gate = _dot(h, wg_ref[...])
    up = _dot(h, wu_ref[...])
    act = (gate * jax.nn.sigmoid(gate) * up).astype(BF16)
    acc_ref[...] += _dot(act, wd_ref[...])

    @pl.when(c == pl.num_programs(1) - 1)
    def _():
        y = acc_ref[...]
        o_ref[...] = _rms(y, gf_ref[...]) if final else y


def _ffn(x, g, w_gate_up, w_down, g_final, final):
    tm, tc = FFN_TM, FFN_TC
    t, d = x.shape
    nc = FFN_HIDDEN // tc
    wgu = w_gate_up.astype(BF16)
    return pl.pallas_call(
        functools.partial(_ffn_kernel, final=final),
        out_shape=jax.ShapeDtypeStruct((t, d), F32),
        grid=(t // tm, nc),
        in_specs=[pl.BlockSpec((tm, d), lambda i, c: (i, 0)),
                  pl.BlockSpec((1, d), lambda i, c: (0, 0)),
                  pl.BlockSpec((d, tc), lambda i, c: (0, c)),
                  pl.BlockSpec((d, tc), lambda i, c: (0, nc + c)),
                  pl.BlockSpec((tc, d), lambda i, c: (c, 0)),
                  pl.BlockSpec((1, d), lambda i, c: (0, 0))],
        out_specs=pl.BlockSpec((tm, d), lambda i, c: (i, 0)),
        scratch_shapes=[pltpu.VMEM((tm, d), BF16), pltpu.VMEM((tm, d), F32)],
        compiler_params=_cparams(("parallel", "arbitrary")),
        name="ffn",
    )(x, g[None], wgu, wgu, w_down.astype(BF16), g_final[None])


def _split_w_in(w):
    k_rope = w[:, O_KR:O_KR + QK_ROPE]
    kv = lambda kind: w[:, O_KV + GD * kind:O_KV + GD * (kind + 1)]
    pad = jnp.zeros((w.shape[0], Z_COLS - Z_GN - 3 * NSA_HEADS), w.dtype)
    wz = jnp.concatenate([
        w[:, O_GM:O_GM + 3 * D_MODEL],
        w[:, O_GLU:O_GLU + 2 * CONV_CH],
        w[:, O_Q:O_Q + NSA_HEADS * NSA_DH],
        w[:, O_CKV:O_CKV + KV_RANK],
        k_rope, _swap_halves(k_rope),
        w[:, O_CQ:O_CQ + Q_RANK],
        kv(0), kv(1),
        w[:, O_GN:O_GN + 3 * NSA_HEADS], pad], axis=1).astype(BF16)
    per_head = lambda wkind: jnp.tile(wkind.reshape(-1, NSA_G, 1, NSA_DH), (1, 1, NSA_HG, 1)).reshape(-1, NSA_G * KREP)
    wk = jnp.concatenate([per_head(kv(2)), per_head(kv(4))], axis=1).astype(BF16)
    wvt = jnp.concatenate([kv(3), kv(5)], axis=1).T.astype(BF16)
    return wz, wk, wvt


def kernel(x, mem, positions, rel_bias, norm_mix, norm_xattn, norm_mem, norm_ffn, norm_final, w_in, conv_w, conv_b, conv_ln_g, conv_ln_b, w_branch_conv, cmp_pos_k, cmp_w1_k, cmp_b1_k, cmp_w2_k, cmp_pos_v, cmp_w1_v, cmp_b1_v, cmp_w2_v, w_branch_nsa, mla_norm_q, mla_norm_kv, w_uq, w_ukv, w_branch_mla, w_out, w_xq, w_xkv, w_xo, w_gate_up, w_down):
    bsz, seq, d = x.shape
    depth = w_in.shape[0]
    t = bsz * seq
    m_len = mem.shape[1]
    xt = x.reshape(t, d)
    memt = mem.reshape(bsz * m_len, d)
    rope_tab = _rope_table(positions)
    tables = _nsa_tables(rel_bias)
    for l in range(depth):
        wz, wk, wvt = _split_w_in(w_in[l])
        z, zc = _in_proj(xt, norm_mix[l][None], wz)
        kk, vvt = _kv_proj(xt, norm_mix[l][None], wk, wvt, bsz, seq)
        ya = _conv_module(z, conv_w[l], conv_b[l], conv_ln_g[l], conv_ln_b[l], bsz, seq)
        kc, vct = _compress(zc, jnp.stack([cmp_pos_k[l], cmp_pos_v[l]]), jnp.stack([cmp_w1_k[l], cmp_w1_v[l]]),
                            jnp.stack([cmp_b1_k[l], cmp_b1_v[l]]), jnp.stack([cmp_w2_k[l], cmp_w2_v[l]]), bsz, seq)
        yb = _nsa_attention(z, kc, vct, kk, vvt, tables, bsz, seq)
        qf, kf, vt = _mla_proj(z, rope_tab, mla_norm_q[l], mla_norm_kv[l], w_uq[l], w_ukv[l], bsz, seq)
        yc = _mla_attention(qf, kf, vt, bsz, seq)
        mem_kv = _norm_matmul(memt, norm_mem[l][None], w_xkv[l].astype(BF16), 256, 1024, BF16)
        mem_kv = mem_kv.reshape(bsz, m_len, 2 * XATTN_HEADS * XATTN_DH)
        xt = _merge_xattn(ya, yb, yc, z, xt, w_branch_conv[l], w_branch_nsa[l], w_branch_mla[l], w_out[l],
                          norm_xattn[l], w_xq[l], mem_kv, w_xo[l], bsz, seq)
        xt = _ffn(xt, norm_ffn[l], w_gate_up[l], w_down[l], norm_final, l == depth - 1)
    return xt.reshape(bsz, seq, d)
```

```python
import functools
import math

import numpy as np
import jax
import jax.numpy as jnp
from jax import lax
from jax.experimental import pallas as pl
from jax.experimental.pallas import tpu as pltpu

F32 = jnp.float32
BF16 = jnp.bfloat16

EPS = 1e-6
NEG_INF = -1e30
FORCE_SCORE = 1e4

D_MODEL = 1024
CONV_CH = 512
CONV_WIDTH = 31
NSA_HEADS = 8
NSA_G = 2
NSA_HG = NSA_HEADS // NSA_G
NSA_DH = 64
CMP_BLOCK = 32
CMP_STRIDE = 16
CMP_HIDDEN = 256
SLC_BLOCK = 64
N_SELECT = 16
WINDOW = 512
NSA_QB = 64
MLA_HEADS = 4
Q_RANK = 384
KV_RANK = 256
QK_NOPE = 128
QK_ROPE = 64
V_DIM = 128
ROPE_THETA = 10000.0
REL_BUCKETS = 32
REL_MAX_DIST = 128
XATTN_HEADS = 4
XATTN_DH = 128
FFN_HIDDEN = 2816

LANES = 128
SUBLANES = 8

O_GLU, O_Q, O_KV, O_GN, O_CQ, O_CKV, O_KR, O_GM = 0, 1024, 1536, 2304, 2328, 2712, 2968, 3032
GD = NSA_G * NSA_DH

Z_GM = 0
Z_UA = 3072
Z_UB = 3584
Z_Q = 4096
Z_CKV = 4608
Z_KR = 4864
Z_CQ = 4992
Z_CMP = 5376
Z_GN = 5632
Z_COLS = 5760

VMEM_LIMIT = 56 * 1024 * 1024

TOK_TILE = 512
IN_PROJ_TM = 1024
IN_PROJ_TN = 1152
FFN_TM = 1024
FFN_TC = 256
MLA_TQ = 512
MLA_TK = 1024
MLA_HEADS_PER_STEP = 2

LOG2E = math.log2(math.e)


def _cparams(sem):
    return pltpu.CompilerParams(dimension_semantics=sem, vmem_limit_bytes=VMEM_LIMIT)


def _rms(x, g):
    return x * lax.rsqrt(jnp.mean(x * x, axis=-1, keepdims=True) + EPS) * g


def _dot(a, b):
    return jnp.dot(a, b, preferred_element_type=F32)


def _dot_nt(a, b):
    return lax.dot_general(a, b, (((1,), (1,)), ((), ())), preferred_element_type=F32)


def _norm_matmul_kernel(x_ref, g_ref, w_ref, o_ref, h_ref):
    @pl.when(pl.program_id(1) == 0)
    def _():
        h_ref[...] = _rms(x_ref[...], g_ref[...]).astype(BF16)

    o_ref[...] = _dot(h_ref[...], w_ref[...]).astype(o_ref.dtype)


def _norm_matmul(x, g, w, tm, tn, out_dtype):
    m, k = x.shape
    n = w.shape[1]
    return pl.pallas_call(
        _norm_matmul_kernel,
        out_shape=jax.ShapeDtypeStruct((m, n), out_dtype),
        grid=(m // tm, n // tn),
        in_specs=[pl.BlockSpec((tm, k), lambda i, j: (i, 0)),
                  pl.BlockSpec((1, k), lambda i, j: (0, 0)),
                  pl.BlockSpec((k, tn), lambda i, j: (0, j))],
        out_specs=pl.BlockSpec((tm, tn), lambda i, j: (i, j)),
        scratch_shapes=[pltpu.VMEM((tm, k), BF16)],
        compiler_params=_cparams(("parallel", "arbitrary")),
        name="norm_matmul",
    )(x, g, w)


def _in_proj_kernel(x_ref, g_ref, w_ref, z_ref, zc_ref, h_ref, *, cmp_tile, cmp_off):
    @pl.when(pl.program_id(1) == 0)
    def _():
        h_ref[...] = _rms(x_ref[...], g_ref[...]).astype(BF16)

    acc = _dot(h_ref[...], w_ref[...])
    z_ref[...] = acc.astype(BF16)

    @pl.when(pl.program_id(1) == cmp_tile)
    def _():
        zc_ref[...] = acc[:, cmp_off:cmp_off + 2 * GD]


def _in_proj(x, g, w):
    tm, tn = IN_PROJ_TM, IN_PROJ_TN
    m, k = x.shape
    n = w.shape[1]
    return pl.pallas_call(
        functools.partial(_in_proj_kernel, cmp_tile=Z_CMP // tn, cmp_off=Z_CMP % tn),
        out_shape=(jax.ShapeDtypeStruct((m, n), BF16), jax.ShapeDtypeStruct((m, 2 * GD), F32)),
        grid=(m // tm, n // tn),
        in_specs=[pl.BlockSpec((tm, k), lambda i, j: (i, 0)),
                  pl.BlockSpec((1, k), lambda i, j: (0, 0)),
                  pl.BlockSpec((k, tn), lambda i, j: (0, j))],
        out_specs=(pl.BlockSpec((tm, tn), lambda i, j: (i, j)),
                   pl.BlockSpec((tm, 2 * GD), lambda i, j: (i, 0))),
        scratch_shapes=[pltpu.VMEM((tm, k), BF16)],
        compiler_params=_cparams(("parallel", "arbitrary")),
        name="in_proj",
    )(x, g, w)


def _kv_proj_kernel(x_ref, g_ref, wk_ref, wvt_ref, k_ref, vt_ref):
    @pl.when(pl.program_id(1) == 0)
    def _():
        k_ref[...] = jnp.zeros(k_ref.shape, BF16)
        vt_ref[...] = jnp.zeros(vt_ref.shape, BF16)

    @pl.when(pl.program_id(1) > 0)
    def _():
        h = _rms(x_ref[...], g_ref[...]).astype(BF16)
        k_ref[0] = _dot(h, wk_ref[...]).astype(BF16)
        vt_ref[0] = _dot_nt(wvt_ref[...], h).astype(BF16)


def _kv_proj(x, g, wk, wvt, bsz, seq):
    tm = WINDOW
    nst = seq // tm
    d = x.shape[1]
    nk = wk.shape[1]
    nv = wvt.shape[0]
    return pl.pallas_call(
        _kv_proj_kernel,
        out_shape=(jax.ShapeDtypeStruct((bsz, seq + tm, nk), BF16),
                   jax.ShapeDtypeStruct((bsz, nv, seq + tm), BF16)),
        grid=(bsz, nst + 1),
        in_specs=[pl.BlockSpec((tm, d), lambda b, s: (b * nst + jnp.maximum(s - 1, 0), 0)),
                  pl.BlockSpec((1, d), lambda b, s: (0, 0)),
                  pl.BlockSpec((d, nk), lambda b, s: (0, 0)),
                  pl.BlockSpec((nv, d), lambda b, s: (0, 0))],
        out_specs=(pl.BlockSpec((1, tm, nk), lambda b, s: (b, s, 0)),
                   pl.BlockSpec((1, nv, tm), lambda b, s: (b, 0, s))),
        compiler_params=_cparams(("parallel", "arbitrary")),
        name="nsa_kv_proj",
    )(x, g, wk, wvt)


CONV_HALO = 32


CONV_ROWS = 64


def _conv_kernel(a_ref, b_ref, w_ref, cb_ref, lg_ref, lb_ref, o_ref, buf_ref, sh_ref, *, ts):
    @pl.when(pl.program_id(1) == 0)
    def _():
        buf_ref[0:CONV_HALO, :] = jnp.zeros((CONV_HALO, CONV_CH), F32)

    buf_ref[CONV_HALO:CONV_HALO + ts, :] = a_ref[...].astype(F32) * jax.nn.sigmoid(b_ref[...].astype(F32))
    span = ts + CONV_HALO - SUBLANES
    for r in range(1, SUBLANES):
        sh_ref[r - 1, 0:span, :] = buf_ref[r:r + span, :]
    off = CONV_HALO - (CONV_WIDTH - 1)

    def rows(i, carry):
        r0 = pl.multiple_of(i * CONV_ROWS, CONV_ROWS)
        acc = jnp.zeros((CONV_ROWS, CONV_CH), F32) + cb_ref[...]
        for k in range(CONV_WIDTH):
            res, base = (off + k) % SUBLANES, (off + k) // SUBLANES * SUBLANES
            if res == 0:
                tap = buf_ref[pl.ds(r0 + base, CONV_ROWS), :]
            else:
                tap = sh_ref[res - 1, pl.ds(r0 + base, CONV_ROWS), :]
            acc = acc + tap * w_ref[k:k + 1, :]
        mu = jnp.mean(acc, axis=-1, keepdims=True)
        xc = acc - mu
        var = jnp.mean(xc * xc, axis=-1, keepdims=True)
        y = xc * lax.rsqrt(var + EPS) * lg_ref[...] + lb_ref[...]
        o_ref[pl.ds(r0, CONV_ROWS), :] = (y * jax.nn.sigmoid(y)).astype(BF16)
        return carry

    lax.fori_loop(0, ts // CONV_ROWS, rows, 0)
    buf_ref[0:CONV_HALO, :] = buf_ref[ts:ts + CONV_HALO, :]


def _conv_module(z, conv_w, conv_b, ln_g, ln_b, bsz, seq):
    ts = TOK_TILE
    nst = seq // ts
    wpad = jnp.zeros((32, CONV_CH), F32).at[:CONV_WIDTH].set(conv_w)
    return pl.pallas_call(
        functools.partial(_conv_kernel, ts=ts),
        out_shape=jax.ShapeDtypeStruct((bsz * seq, CONV_CH), BF16),
        grid=(bsz, nst),
        in_specs=[pl.BlockSpec((ts, CONV_CH), lambda b, s: (b * nst + s, Z_UA // CONV_CH)),
                  pl.BlockSpec((ts, CONV_CH), lambda b, s: (b * nst + s, Z_UB // CONV_CH)),
                  pl.BlockSpec((32, CONV_CH), lambda b, s: (0, 0)),
                  pl.BlockSpec((1, CONV_CH), lambda b, s: (0, 0)),
                  pl.BlockSpec((1, CONV_CH), lambda b, s: (0, 0)),
                  pl.BlockSpec((1, CONV_CH), lambda b, s: (0, 0))],
        out_specs=pl.BlockSpec((ts, CONV_CH), lambda b, s: (b * nst + s, 0)),
        scratch_shapes=[pltpu.VMEM((ts + CONV_HALO, CONV_CH), F32),
                        pltpu.VMEM((SUBLANES - 1, ts + CONV_HALO - SUBLANES, CONV_CH), F32)],
        compiler_params=_cparams(("arbitrary", "arbitrary")),
        name="conv_module",
    )(z, z, wpad, conv_b[None], ln_g[None], ln_b[None])


def _compress_kernel(xk_ref, xv_ref, pos_ref, w1_ref, b1_ref, w2k_ref, w2v_ref, kc_ref, vct_ref, *, nch):
    for kind, (x_ref, w2_ref) in enumerate(((xk_ref, w2k_ref), (xv_ref, w2v_ref))):
        a = jnp.zeros((nch, NSA_G * CMP_HIDDEN), F32)
        b = jnp.zeros((nch, NSA_G * CMP_HIDDEN), F32)
        for l in range(CMP_STRIDE):
            xs = x_ref[pl.ds(l, nch, stride=CMP_STRIDE), :]
            a = a + _dot((xs + pos_ref[kind, l:l + 1, :]).astype(BF16), w1_ref[kind, l])
            b = b + _dot((xs + pos_ref[kind, CMP_STRIDE + l:CMP_STRIDE + l + 1, :]).astype(BF16),
                         w1_ref[kind, CMP_STRIDE + l])
        pre = a + pltpu.roll(b, nch - 1, 0) + b1_ref[kind]
        out = _dot(jax.nn.gelu(pre).astype(BF16), w2_ref[...])
        if kind == 0:
            kc_ref[0] = out.astype(BF16)
        else:
            vct_ref[0] = out.T.astype(BF16)


def _blockdiag2(w):
    z = jnp.zeros_like(w)
    return jnp.concatenate([jnp.concatenate([w, z], axis=-1), jnp.concatenate([z, w], axis=-1)], axis=-2)


def _compress(z, pos, w1, b1, w2, bsz, seq):
    nch = seq // CMP_STRIDE
    pos2 = jnp.concatenate([pos, pos], axis=-1)
    w1e = _blockdiag2(w1.reshape(2, CMP_BLOCK, NSA_DH, CMP_HIDDEN)).astype(BF16)
    b1e = jnp.concatenate([b1, b1], axis=-1)[:, None]
    w2k = _blockdiag2(jnp.tile(w2[0], (1, NSA_HG))).astype(BF16)
    w2v = _blockdiag2(w2[1]).astype(BF16)
    full = lambda a: pl.BlockSpec(a.shape, lambda b: (0,) * a.ndim)
    return pl.pallas_call(
        functools.partial(_compress_kernel, nch=nch),
        out_shape=(jax.ShapeDtypeStruct((bsz, nch, NSA_G * NSA_HG * NSA_DH), BF16),
                   jax.ShapeDtypeStruct((bsz, GD, nch), BF16)),
        grid=(bsz,),
        in_specs=[pl.BlockSpec((seq, GD), lambda b: (b, 0)),
                  pl.BlockSpec((seq, GD), lambda b: (b, 1)),
                  full(pos2), full(w1e), full(b1e), full(w2k), full(w2v)],
        out_specs=(pl.BlockSpec((1, nch, NSA_G * NSA_HG * NSA_DH), lambda b: (b, 0, 0)),
                   pl.BlockSpec((1, GD, nch), lambda b: (b, 0, 0))),
        compiler_params=_cparams(("parallel",)),
        name="nsa_compress",
    )(z, z, pos2, w1e, b1e, w2k, w2v)


NSA_QP = 2 * NSA_QB
NEAR_KEYS = 256
WIN_KEYS = 640
CMP_TAB_ROWS = 512
CMP_TAB_ZERO = 256


def _t5_bucket_np(d):
    exact = REL_BUCKETS // 2
    d = np.maximum(d, 0)
    ratio = np.log(np.maximum(d, 1).astype(np.float32) / np.float32(exact)) / np.float32(math.log(REL_MAX_DIST / exact))
    large = np.minimum(exact + (ratio * (REL_BUCKETS - exact)).astype(np.int32), REL_BUCKETS - 1)
    return np.where(d < exact, d, large).astype(np.int32)


def _bucket_thresholds():
    exact = REL_BUCKETS // 2
    bk = _t5_bucket_np(np.arange(4 * REL_MAX_DIST))
    assert np.all(np.diff(bk) >= 0) and bk[-1] == REL_BUCKETS - 1
    return [int(np.argmax(bk >= k)) for k in range(exact + 1, REL_BUCKETS)]


def _bias_rows(rel_ref, dist, valid, shift):
    exact = REL_BUCKETS // 2
    bucket = jnp.full(dist.shape, exact, jnp.int32)
    for thr in _bucket_thresholds():
        bucket = bucket + jnp.where(dist >= thr, 1, 0)
    bucket = jnp.where(dist < exact, dist, bucket)
    val = jnp.zeros(dist.shape, F32)
    for bkt in range(REL_BUCKETS):
        val = jnp.where(bucket == bkt, rel_ref[0, bkt:bkt + 1, :], val)
    if shift:
        val = val - rel_ref[0, REL_BUCKETS - 1:REL_BUCKETS, :]
    return jnp.where(valid, val * LOG2E, NEG_INF)


def _nsa_bias_kernel(rel_ref, tc_ref, tn_ref, tw_ref):
    hq = NSA_HG * NSA_QP
    rows = 128

    def dist_of(nrows, r0, fn):
        r = r0 + lax.broadcasted_iota(jnp.int32, (nrows, hq), 0)
        t = lax.bitwise_and(lax.broadcasted_iota(jnp.int32, (nrows, hq), 1), NSA_QP - 1)
        return fn(r, t)

    for r0 in range(0, CMP_TAB_ROWS, rows):
        d = dist_of(rows, r0, lambda r, t: t - CMP_STRIDE * (r - CMP_TAB_ZERO) - (CMP_BLOCK - 1))
        tc_ref[0, r0:r0 + rows, :] = _bias_rows(rel_ref, d, d >= 0, False)
    for r0 in range(0, NEAR_KEYS, rows):
        d = dist_of(rows, r0, lambda r, t: NEAR_KEYS // 2 + t - r)
        tn_ref[0, r0:r0 + rows, :] = _bias_rows(rel_ref, d, d >= 0, True)
    for r0 in range(0, WIN_KEYS, rows):
        d = dist_of(rows, r0, lambda r, t: WINDOW + t - r)
        tw_ref[0, r0:r0 + rows, :] = _bias_rows(rel_ref, d, (d >= 0) & (d < WINDOW), False)


def _nsa_tables(rel_bias):
    hq = NSA_HG * NSA_QP
    rel4 = jnp.repeat(rel_bias.reshape(REL_BUCKETS, NSA_G, NSA_HG).transpose(1, 0, 2), NSA_QP, axis=-1)
    spec = lambda r: pl.BlockSpec((1, r, hq), lambda g: (g, 0, 0))
    return pl.pallas_call(
        _nsa_bias_kernel,
        out_shape=(jax.ShapeDtypeStruct((NSA_G, CMP_TAB_ROWS, hq), F32),
                   jax.ShapeDtypeStruct((NSA_G, NEAR_KEYS, hq), F32),
                   jax.ShapeDtypeStruct((NSA_G, WIN_KEYS, hq), F32)),
        grid=(NSA_G,),
        in_specs=[spec(REL_BUCKETS)],
        out_specs=(spec(CMP_TAB_ROWS), spec(NEAR_KEYS), spec(WIN_KEYS)),
        compiler_params=_cparams(("parallel",)),
        name="nsa_bias_tables",
    )(rel4)


SEL_PAD = 8
FAR_KEYS = 1024
KV_FRONT = WINDOW
KREP = NSA_HG * NSA_DH


def _softmax_cols(s):
    m = jnp.max(s, axis=0, keepdims=True)
    p = jnp.exp2(s - m)
    return m, p, jnp.sum(p, axis=0, keepdims=True)


def _mask_blocks(s, mask_ref, row0, nblk):
    parts = []
    for jj in range(nblk):
        row = mask_ref[pl.ds(row0 + jj, 1), :]
        parts.append(jnp.where(row > 0.0, s[SLC_BLOCK * jj:SLC_BLOCK * (jj + 1)], NEG_INF))
    return jnp.concatenate(parts, axis=0)


def _rank_select(score_ref, n_sb, n_sel):
    groups = n_sb // SUBLANES
    sub = lax.broadcasted_iota(jnp.int32, (SUBLANES, NSA_QP), 0)
    tiles = [score_ref[SUBLANES * v:SUBLANES * (v + 1), :] for v in range(groups)]
    cnts = [jnp.zeros((SUBLANES, NSA_QP), F32) for _ in range(groups)]
    for jp in range(n_sb):
        row = score_ref[jp:jp + 1, :]
        for v in range(groups):
            lo = SUBLANES * v
            if jp < lo:
                beats = row >= tiles[v]
            elif jp >= lo + SUBLANES - 1:
                beats = row > tiles[v]
            else:
                beats = (row > tiles[v]) | ((row == tiles[v]) & (sub > jp - lo))
            cnts[v] = cnts[v] + jnp.where(beats, 1.0, 0.0)
    cnt = jnp.concatenate(cnts, axis=0)
    return jnp.where(cnt < float(n_sel), 1.0, 0.0)


def _nsa_kernel(q_ref, gate_ref, kc_ref, vct_ref, ks_ref, kw_ref, vst_ref, vwt_ref,
                tc_ref, tn_ref, tw_ref, cov_ref, rep_ref, o_ref,
                sel_ref, selfar_ref, score_ref, *, n_sb):
    p2 = pl.program_id(1)
    hq = NSA_HG * NSA_QP
    nch = kc_ref.shape[1]
    groups = range(NSA_G)
    kcol = lambda g: slice(KREP * g, KREP * (g + 1))
    vrow = lambda g: slice(NSA_DH * g, NSA_DH * (g + 1))

    qs = []
    for g in groups:
        qb = (q_ref[:, kcol(g)].astype(F32) * (NSA_DH ** -0.5 * LOG2E)).astype(BF16)
        lane_head = lax.shift_right_logical(lax.broadcasted_iota(jnp.int32, qb.shape, 1), 6)
        qs.append(jnp.concatenate([jnp.where(lane_head == h, qb, jnp.zeros_like(qb)) for h in range(NSA_HG)], axis=0))

    start_c = pl.multiple_of(CMP_TAB_ZERO - (NSA_QP // CMP_STRIDE) * p2, SUBLANES)
    lane = lax.broadcasted_iota(jnp.int32, (1, hq), 1)
    tq = NSA_QP * p2 + lax.bitwise_and(lane, NSA_QP - 1)
    anyv = jnp.where(tq >= CMP_BLOCK - 1, 1.0, 0.0)
    jrow = lax.broadcasted_iota(jnp.int32, (n_sb, NSA_QP), 0)
    tok = lax.broadcasted_iota(jnp.int32, (n_sb, NSA_QP), 1)
    cur = 2 * p2 + lax.shift_right_logical(tok, 6)
    forced = (jrow == 0) | (jrow == cur) | (jrow == cur - 1)
    o_cmp = []
    for g in groups:
        sc = _dot_nt(kc_ref[0, :, kcol(g)], qs[g]) + tc_ref[g, pl.ds(start_c, nch), :]
        _, pc, lc = _softmax_cols(sc)
        pc = pc * (anyv / lc)
        o_cmp.append(_dot(vct_ref[0, vrow(g), :], pc.astype(BF16)))
        psum = pc[:, 0:NSA_QP]
        for h in range(1, NSA_HG):
            psum = psum + pc[:, NSA_QP * h:NSA_QP * (h + 1)]
        p_hi = psum.astype(BF16)
        p_lo = (psum - p_hi.astype(F32)).astype(BF16)
        imp = _dot(cov_ref[...], p_hi) + _dot(cov_ref[...], p_lo)
        score_ref[g] = jnp.where(forced, FORCE_SCORE, jnp.where(jrow <= cur, imp, -1.0))

    zeros8 = jnp.zeros((SEL_PAD, hq), F32)
    jrow4 = lax.broadcasted_iota(jnp.int32, (n_sb, hq), 0)
    for g in groups:
        sel = _rank_select(score_ref.at[g], n_sb, min(N_SELECT, n_sb)).astype(BF16)
        sel4 = _dot(sel, rep_ref[...])
        sel_ref[g, 0:SEL_PAD, :] = zeros8
        sel_ref[g, SEL_PAD + n_sb:2 * SEL_PAD + n_sb, :] = zeros8
        sel_ref[g, SEL_PAD:SEL_PAD + n_sb, :] = sel4
        selfar_ref[g, 0:SEL_PAD, :] = zeros8
        selfar_ref[g, SEL_PAD + n_sb:2 * SEL_PAD + n_sb, :] = zeros8
        selfar_ref[g, SEL_PAD:SEL_PAD + n_sb, :] = jnp.where(jrow4 < 2 * p2 - 2, sel4, 0.0)

    win0 = pl.multiple_of(NSA_QP * p2, LANES)
    near0 = pl.multiple_of(win0 + KV_FRONT - NEAR_KEYS // 2, LANES)
    state = []
    for g in groups:
        s = _dot_nt(ks_ref[0, pl.ds(near0, NEAR_KEYS), kcol(g)], qs[g]) + tn_ref[g]
        s = _mask_blocks(s, sel_ref.at[g], 2 * p2 - 2 + SEL_PAD, NEAR_KEYS // SLC_BLOCK)
        m_s, p_s, l_s = _softmax_cols(s)
        state += [m_s, l_s, _dot(vst_ref[0, vrow(g), pl.ds(near0, NEAR_KEYS)], p_s.astype(BF16))]

    window = []
    for g in groups:
        sw = _dot_nt(kw_ref[0, pl.ds(win0, WIN_KEYS), kcol(g)], qs[g]) + tw_ref[g]
        slabs = [sw[NSA_QP * j:NSA_QP * (j + 1)] for j in range(WIN_KEYS // NSA_QP)]
        for j in range(KV_FRONT // NSA_QP):
            slabs[j] = jnp.where(NSA_QP * j + win0 >= KV_FRONT, slabs[j], NEG_INF)
        _, p_w, l_w = _softmax_cols(jnp.concatenate(slabs, axis=0))
        window.append((_dot(vwt_ref[0, vrow(g), pl.ds(win0, WIN_KEYS)], p_w.astype(BF16)), l_w))

    def far_body(c, carry):
        k0 = pl.multiple_of(FAR_KEYS * c + KV_FRONT, LANES)
        sfs = [_dot_nt(ks_ref[0, pl.ds(k0, FAR_KEYS), kcol(g)], qs[g]) for g in groups]
        out = []
        for g in groups:
            m_old, l_old, acc_old = carry[3 * g:3 * g + 3]
            sf = _mask_blocks(sfs[g], selfar_ref.at[g], (FAR_KEYS // SLC_BLOCK) * c + SEL_PAD, FAR_KEYS // SLC_BLOCK)
            m_new = jnp.maximum(m_old, jnp.max(sf, axis=0, keepdims=True))
            alpha = jnp.exp2(m_old - m_new)
            pf = jnp.exp2(sf - m_new)
            l_new = alpha * l_old + jnp.sum(pf, axis=0, keepdims=True)
            acc_new = alpha * acc_old + _dot(vst_ref[0, vrow(g), pl.ds(k0, FAR_KEYS)], pf.astype(BF16))
            out += [m_new, l_new, acc_new]
        return tuple(out)

    n_far = lax.div(jnp.maximum(p2 - 1, 0) * NSA_QP + FAR_KEYS - 1, FAR_KEYS)
    state = lax.fori_loop(0, n_far, far_body, tuple(state))

    gt = gate_ref[...].astype(F32).T
    r = lax.broadcasted_iota(jnp.int32, (NSA_QP, NSA_QP), 0)
    c = lax.broadcasted_iota(jnp.int32, (NSA_QP, NSA_QP), 1)
    eye = jnp.where(r == c, 1.0, 0.0).astype(BF16)
    for g in groups:
        _, l_s, acc_s = state[3 * g:3 * g + 3]
        acc_w, l_w = window[g]
        gsel = jax.nn.sigmoid(gt[3 * NSA_HG * g:3 * NSA_HG * (g + 1)])
        gate = lambda b: jnp.concatenate([gsel[3 * h + b:3 * h + b + 1] for h in range(NSA_HG)], axis=1)
        out_t = (gate(0) * o_cmp[g] + (gate(1) / l_s) * acc_s + (gate(2) / l_w) * acc_w).astype(BF16)
        stacked = jnp.concatenate([out_t[:, NSA_QP * h:NSA_QP * (h + 1)] for h in range(NSA_HG)], axis=0)
        o_ref[:, kcol(g)] = _dot_nt(eye, stacked).astype(BF16)


def _nsa_attention(z, kc, vct, kk, vvt, tables, bsz, seq):
    g, hg, dh, qp = NSA_G, NSA_HG, NSA_DH, NSA_QP
    nstep = seq // qp
    n_sb = seq // SLC_BLOCK
    nch = kc.shape[1]
    hq = hg * qp
    sp = kk.shape[1]
    tc, tn, tw = tables
    c_start = CMP_STRIDE * np.arange(nch)
    s_start = SLC_BLOCK * np.arange(n_sb)
    cover_t = ((c_start[None, :] < s_start[:, None] + SLC_BLOCK)
               & (c_start[None, :] + CMP_BLOCK > s_start[:, None])
               & (np.arange(nch)[None, :] < (seq - CMP_BLOCK) // CMP_STRIDE + 1))
    cover_t = jnp.asarray(cover_t.astype(np.float32), BF16)
    rep = jnp.asarray(np.tile(np.eye(qp, dtype=np.float32), (1, hg)), BF16)
    full = lambda a: pl.BlockSpec(a.shape, lambda b, i: (0,) * a.ndim, pipeline_mode=pl.Buffered(1))
    qd = g * hg * dh
    return pl.pallas_call(
        functools.partial(_nsa_kernel, n_sb=n_sb),
        out_shape=jax.ShapeDtypeStruct((bsz * seq, qd), BF16),
        grid=(bsz, nstep),
        in_specs=[pl.BlockSpec((qp, qd), lambda b, i: (b * nstep + i, Z_Q // qd)),
                  pl.BlockSpec((qp, LANES), lambda b, i: (b * nstep + i, Z_GN // LANES)),
                  pl.BlockSpec((1, nch, g * KREP), lambda b, i: (b, 0, 0)),
                  pl.BlockSpec((1, g * dh, nch), lambda b, i: (b, 0, 0)),
                  pl.BlockSpec((1, sp, g * KREP), lambda b, i: (b, 0, 0)),
                  pl.BlockSpec((1, sp, g * KREP), lambda b, i: (b, 0, 1)),
                  pl.BlockSpec((1, g * dh, sp), lambda b, i: (b, 0, 0)),
                  pl.BlockSpec((1, g * dh, sp), lambda b, i: (b, 1, 0)),
                  full(tc), full(tn), full(tw), full(cover_t), full(rep)],
        out_specs=pl.BlockSpec((qp, qd), lambda b, i: (b * nstep + i, 0)),
        scratch_shapes=[pltpu.VMEM((g, n_sb + 2 * SEL_PAD, hq), F32),
                        pltpu.VMEM((g, n_sb + 2 * SEL_PAD, hq), F32),
                        pltpu.VMEM((g, n_sb, qp), F32)],
        compiler_params=_cparams(("parallel", "arbitrary")),
        name="nsa_attention",
    )(z, z, kc, vct, kk, kk, vvt, vvt, tc, tn, tw, cover_t, rep)


def _rope_table_kernel(pos_ref, inv_ref, o_ref):
    ang = inv_ref[...] * pos_ref[0].astype(F32)
    c, s = jnp.cos(ang), jnp.sin(ang)
    o_ref[...] = jnp.concatenate([c, c, -s, s], axis=0).T


def _rope_table(positions):
    tm = TOK_TILE
    t = positions.size
    half = QK_ROPE // 2
    inv = (ROPE_THETA ** (-jnp.arange(half, dtype=F32) / half))[:, None]
    return pl.pallas_call(
        _rope_table_kernel,
        out_shape=jax.ShapeDtypeStruct((t, 2 * QK_ROPE), F32),
        grid=(t // tm,),
        in_specs=[pl.BlockSpec((1, 1, tm), lambda i: (i, 0, 0)),
                  pl.BlockSpec((half, 1), lambda i: (0, 0))],
        out_specs=pl.BlockSpec((tm, 2 * QK_ROPE), lambda i: (i, 0)),
        compiler_params=_cparams(("parallel",)),
        name="rope_table",
    )(positions.reshape(t // tm, 1, tm), inv)


MLA_HW = 256


def _mla_proj_kernel(cq_ref, ckv_ref, kr_ref, rope_ref, nq_ref, nkv_ref, wq_ref, wkn_ref, wvt_ref,
                     q_ref, k_ref, vt_ref):
    scale = (QK_NOPE + QK_ROPE) ** -0.5 * LOG2E
    rope = rope_ref[...]
    yq = _dot(_rms(cq_ref[...].astype(F32), nq_ref[...]).astype(BF16), wq_ref[...])
    ckv = _rms(ckv_ref[...].astype(F32), nkv_ref[...]).astype(BF16)
    ykn = _dot(ckv, wkn_ref[...])
    vt_ref[0] = _dot_nt(wvt_ref[...], ckv).astype(BF16)
    kp = kr_ref[...].astype(F32) * rope
    kp = kp + pltpu.roll(kp, QK_ROPE, 1)
    lane = lax.broadcasted_iota(jnp.int32, kp.shape, 1)
    kp = jnp.where(lane < QK_ROPE, kp, 0.0).astype(BF16)
    for h in range(MLA_HEADS):
        base = MLA_HW * h
        q_ref[:, base:base + QK_NOPE] = (yq[:, base:base + QK_NOPE] * scale).astype(BF16)
        qp = yq[:, base + QK_NOPE:base + MLA_HW] * rope
        qp = qp + pltpu.roll(qp, QK_ROPE, 1)
        q_ref[:, base + QK_NOPE:base + MLA_HW] = (qp * scale).astype(BF16)
        k_ref[:, base:base + QK_NOPE] = ykn[:, QK_NOPE * h:QK_NOPE * (h + 1)].astype(BF16)
        k_ref[:, base + QK_NOPE:base + MLA_HW] = kp


def _swap_halves(w):
    half = QK_ROPE // 2
    return jnp.concatenate([w[..., half:], w[..., :half]], axis=-1)


def _mla_proj(z, rope_tab, norm_q, norm_kv, w_uq, w_ukv, bsz, seq):
    tm = TOK_TILE
    t = z.shape[0]
    nst = seq // tm
    wq = w_uq.reshape(Q_RANK, MLA_HEADS, QK_NOPE + QK_ROPE)
    wq = jnp.concatenate([wq, _swap_halves(wq[..., QK_NOPE:])], axis=-1)
    wq = wq.reshape(Q_RANK, MLA_HEADS * MLA_HW).astype(BF16)
    wkv = w_ukv.reshape(KV_RANK, MLA_HEADS, QK_NOPE + V_DIM)
    wkn = wkv[..., :QK_NOPE].reshape(KV_RANK, MLA_HEADS * QK_NOPE).astype(BF16)
    wvt = wkv[..., QK_NOPE:].reshape(KV_RANK, MLA_HEADS * V_DIM).T.astype(BF16)
    hw = MLA_HEADS * MLA_HW
    hv = MLA_HEADS * V_DIM
    row = lambda b, s: b * nst + s
    return pl.pallas_call(
        _mla_proj_kernel,
        out_shape=(jax.ShapeDtypeStruct((t, hw), BF16),
                   jax.ShapeDtypeStruct((t, hw), BF16),
                   jax.ShapeDtypeStruct((bsz, hv, seq), BF16)),
        grid=(bsz, nst),
        in_specs=[pl.BlockSpec((tm, Q_RANK), lambda b, s: (row(b, s), Z_CQ // Q_RANK)),
                  pl.BlockSpec((tm, KV_RANK), lambda b, s: (row(b, s), Z_CKV // KV_RANK)),
                  pl.BlockSpec((tm, 2 * QK_ROPE), lambda b, s: (row(b, s), Z_KR // (2 * QK_ROPE))),
                  pl.BlockSpec((tm, 2 * QK_ROPE), lambda b, s: (row(b, s), 0)),
                  pl.BlockSpec((1, Q_RANK), lambda b, s: (0, 0)),
                  pl.BlockSpec((1, KV_RANK), lambda b, s: (0, 0)),
                  pl.BlockSpec((Q_RANK, hw), lambda b, s: (0, 0)),
                  pl.BlockSpec((KV_RANK, hv), lambda b, s: (0, 0)),
                  pl.BlockSpec((hv, KV_RANK), lambda b, s: (0, 0))],
        out_specs=(pl.BlockSpec((tm, hw), lambda b, s: (row(b, s), 0)),
                   pl.BlockSpec((tm, hw), lambda b, s: (row(b, s), 0)),
                   pl.BlockSpec((1, hv, tm), lambda b, s: (b, 0, s))),
        compiler_params=_cparams(("parallel", "parallel")),
        name="mla_proj",
    )(z, z, z, rope_tab, norm_q[None], norm_kv[None], wq, wkn, wvt)


def _mla_attn_kernel(q_ref, k_ref, vt_ref, o_ref, *, tq, tk, nh):
    iq = pl.program_id(2)
    cd = lax.div(iq * tq, tk)
    heads = range(nh)
    hcol = lambda h: slice(MLA_HW * h, MLA_HW * (h + 1))
    vrow = lambda h: slice(V_DIM * h, V_DIM * (h + 1))
    qs = [q_ref[:, hcol(h)] for h in heads]

    def scores(c, h):
        k0 = pl.multiple_of(c * tk, tk)
        return _dot_nt(k_ref[0, pl.ds(k0, tk), hcol(h)], qs[h])

    def diagonal(nk):
        k0 = pl.multiple_of((iq + 1) * tq - nk, tq)
        kpos = k0 + lax.broadcasted_iota(jnp.int32, (nk, tq), 0)
        qpos = iq * tq + lax.broadcasted_iota(jnp.int32, (nk, tq), 1)
        st = []
        for h in heads:
            s = jnp.where(kpos <= qpos, _dot_nt(k_ref[0, pl.ds(k0, nk), hcol(h)], qs[h]), NEG_INF)
            m0, p0, l0 = _softmax_cols(s)
            st += [m0, l0, _dot(vt_ref[0, vrow(h), pl.ds(k0, nk)], p0.astype(BF16))]
        return tuple(st)

    assert tk == 2 * tq
    state = lax.cond(lax.rem(iq, 2) == 0, lambda: diagonal(tq), lambda: diagonal(tk))

    def body(c, carry):
        k0 = pl.multiple_of(c * tk, tk)
        ss = [scores(c, h) for h in heads]
        out = []
        for h in heads:
            m_old, l_old, acc_old = carry[3 * h:3 * h + 3]
            m_new = jnp.maximum(m_old, jnp.max(ss[h], axis=0, keepdims=True))
            alpha = jnp.exp2(m_old - m_new)
            p = jnp.exp2(ss[h] - m_new)
            l_new = alpha * l_old + jnp.sum(p, axis=0, keepdims=True)
            acc_new = alpha * acc_old + _dot(vt_ref[0, vrow(h), pl.ds(k0, tk)], p.astype(BF16))
            out += [m_new, l_new, acc_new]
        return tuple(out)

    state = lax.fori_loop(0, cd, body, tuple(state))
    for h in heads:
        _, l, acc = state[3 * h:3 * h + 3]
        o_ref[:, vrow(h)] = (acc / l).T.astype(BF16)


def _mla_attention(qf, kf, vt, bsz, seq):
    tq, tk, nh = MLA_TQ, MLA_TK, MLA_HEADS_PER_STEP
    h = MLA_HEADS
    nq = seq // tq
    k3 = kf.reshape(bsz, seq, h * MLA_HW)
    return pl.pallas_call(
        functools.partial(_mla_attn_kernel, tq=tq, tk=tk, nh=nh),
        out_shape=jax.ShapeDtypeStruct((bsz * seq, h * V_DIM), BF16),
        grid=(bsz, h // nh, nq),
        in_specs=[pl.BlockSpec((tq, nh * MLA_HW), lambda b, hh, i: (b * nq + i, hh)),
                  pl.BlockSpec((1, seq, nh * MLA_HW), lambda b, hh, i: (b, 0, hh)),
                  pl.BlockSpec((1, nh * V_DIM, seq), lambda b, hh, i: (b, hh, 0))],
        out_specs=pl.BlockSpec((tq, nh * V_DIM), lambda b, hh, i: (b * nq + i, hh)),
        compiler_params=_cparams(("parallel", "parallel", "arbitrary")),
        name="mla_attention",
    )(qf, k3, vt)


def _merge_xattn_kernel(ya_ref, yb_ref, yc_ref, ga_ref, gb_ref, gc_ref, x_ref,
                        wa_ref, wb_ref, wc_ref, wo_ref,
                        gx_ref, wq_ref, kv_ref, wxo_ref, o_ref):
    sig = lambda ref: jax.nn.sigmoid(ref[...].astype(F32))
    y = (sig(ga_ref) * _dot(ya_ref[...], wa_ref[...])
         + sig(gb_ref) * _dot(yb_ref[...], wb_ref[...])
         + sig(gc_ref) * _dot(yc_ref[...], wc_ref[...]))
    x = x_ref[...] + _dot(y.astype(BF16), wo_ref[...])
    h = _rms(x, gx_ref[...]).astype(BF16)
    q = _dot(h, wq_ref[...]) * XATTN_DH ** -0.5
    hd = XATTN_HEADS * XATTN_DH
    outs = []
    for hh in range(XATTN_HEADS):
        qh = q[:, XATTN_DH * hh:XATTN_DH * (hh + 1)].astype(BF16)
        kh = kv_ref[0, :, XATTN_DH * hh:XATTN_DH * (hh + 1)]
        vh = kv_ref[0, :, hd + XATTN_DH * hh:hd + XATTN_DH * (hh + 1)]
        s = _dot_nt(qh, kh)
        m = jnp.max(s, axis=-1, keepdims=True)
        p = jnp.exp(s - m)
        p = p / jnp.sum(p, axis=-1, keepdims=True)
        outs.append(_dot(p.astype(BF16), vh))
    o = jnp.concatenate(outs, axis=-1).astype(BF16)
    o_ref[...] = x + _dot(o, wxo_ref[...])


def _merge_xattn(ya, yb, yc, z, x, wa, wb, wc, wo, gx, wq, kv, wxo, bsz, seq):
    tm = TOK_TILE
    t, d = x.shape
    nst = seq // tm
    m_len = kv.shape[1]
    hd = XATTN_HEADS * XATTN_DH
    row = lambda b, s: b * nst + s
    act = pl.BlockSpec((tm, ya.shape[1]), lambda b, s: (row(b, s), 0))
    gate = lambda k: pl.BlockSpec((tm, d), lambda b, s: (row(b, s), Z_GM // d + k))
    const = lambda shape: pl.BlockSpec(shape, lambda b, s: (0, 0))
    bf = lambda w: w.astype(BF16)
    return pl.pallas_call(
        _merge_xattn_kernel,
        out_shape=jax.ShapeDtypeStruct((t, d), F32),
        grid=(bsz, nst),
        in_specs=[act, act, act, gate(0), gate(1), gate(2),
                  pl.BlockSpec((tm, d), lambda b, s: (row(b, s), 0)),
                  const((ya.shape[1], d)), const((ya.shape[1], d)), const((ya.shape[1], d)), const((d, d)),
                  const((1, d)), const((d, hd)),
                  pl.BlockSpec((1, m_len, 2 * hd), lambda b, s: (b, 0, 0)),
                  const((hd, d))],
        out_specs=pl.BlockSpec((tm, d), lambda b, s: (row(b, s), 0)),
        compiler_params=_cparams(("parallel", "parallel")),
        name="merge_xattn",
    )(ya, yb, yc, z, z, z, x, bf(wa), bf(wb), bf(wc), bf(wo), gx[None], bf(wq), kv, bf(wxo))


def _ffn_kernel(x_ref, g_ref, wg_ref, wu_ref, wd_ref, gf_ref, o_ref, h_ref, acc_ref, *, final):
    c = pl.program_id(1)

    @pl.when(c == 0)
    def _():
        h_ref[...] = _rms(x_ref[...], g_ref[...]).astype(BF16)
        acc_ref[...] = x_ref[...]

    h = h_ref[...]
    gate = _dot(h, wg_ref[...])
    up = _dot(h, wu_ref[...])
    act = (gate * jax.nn.sigmoid(gate) * up).astype(BF16)
    acc_ref[...] += _dot(act, wd_ref[...])

    @pl.when(c == pl.num_programs(1) - 1)
    def _():
        y = acc_ref[...]
        o_ref[...] = _rms(y, gf_ref[...]) if final else y


def _ffn(x, g, w_gate_up, w_down, g_final, final):
    tm, tc = FFN_TM, FFN_TC
    t, d = x.shape
    nc = FFN_HIDDEN // tc
    wgu = w_gate_up.astype(BF16)
    return pl.pallas_call(
        functools.partial(_ffn_kernel, final=final),
        out_shape=jax.ShapeDtypeStruct((t, d), F32),
        grid=(t // tm, nc),
        in_specs=[pl.BlockSpec((tm, d), lambda i, c: (i, 0)),
                  pl.BlockSpec((1, d), lambda i, c: (0, 0)),
                  pl.BlockSpec((d, tc), lambda i, c: (0, c)),
                  pl.BlockSpec((d, tc), lambda i, c: (0, nc + c)),
                  pl.BlockSpec((tc, d), lambda i, c: (c, 0)),
                  pl.BlockSpec((1, d), lambda i, c: (0, 0))],
        out_specs=pl.BlockSpec((tm, d), lambda i, c: (i, 0)),
        scratch_shapes=[pltpu.VMEM((tm, d), BF16), pltpu.VMEM((tm, d), F32)],
        compiler_params=_cparams(("parallel", "arbitrary")),
        name="ffn",
    )(x, g[None], wgu, wgu, w_down.astype(BF16), g_final[None])


def _split_w_in(w):
    k_rope = w[:, O_KR:O_KR + QK_ROPE]
    kv = lambda kind: w[:, O_KV + GD * kind:O_KV + GD * (kind + 1)]
    pad = jnp.zeros((w.shape[0], Z_COLS - Z_GN - 3 * NSA_HEADS), w.dtype)
    wz = jnp.concatenate([
        w[:, O_GM:O_GM + 3 * D_MODEL],
        w[:, O_GLU:O_GLU + 2 * CONV_CH],
        w[:, O_Q:O_Q + NSA_HEADS * NSA_DH],
        w[:, O_CKV:O_CKV + KV_RANK],
        k_rope, _swap_halves(k_rope),
        w[:, O_CQ:O_CQ + Q_RANK],
        kv(0), kv(1),
        w[:, O_GN:O_GN + 3 * NSA_HEADS], pad], axis=1).astype(BF16)
    per_head = lambda wkind: jnp.tile(wkind.reshape(-1, NSA_G, 1, NSA_DH), (1, 1, NSA_HG, 1)).reshape(-1, NSA_G * KREP)
    wk = jnp.concatenate([per_head(kv(2)), per_head(kv(4))], axis=1).astype(BF16)
    wvt = jnp.concatenate([kv(3), kv(5)], axis=1).T.astype(BF16)
    return wz, wk, wvt


def kernel(x, mem, positions, rel_bias, norm_mix, norm_xattn, norm_mem, norm_ffn, norm_final, w_in, conv_w, conv_b, conv_ln_g, conv_ln_b, w_branch_conv, cmp_pos_k, cmp_w1_k, cmp_b1_k, cmp_w2_k, cmp_pos_v, cmp_w1_v, cmp_b1_v, cmp_w2_v, w_branch_nsa, mla_norm_q, mla_norm_kv, w_uq, w_ukv, w_branch_mla, w_out, w_xq, w_xkv, w_xo, w_gate_up, w_down):
    bsz, seq, d = x.shape
    depth = w_in.shape[0]
    t = bsz * seq
    m_len = mem.shape[1]
    xt = x.reshape(t, d)
    memt = mem.reshape(bsz * m_len, d)
    rope_tab = _rope_table(positions)
    tables = _nsa_tables(rel_bias)
    for l in range(depth):
        wz, wk, wvt = _split_w_in(w_in[l])
        z, zc = _in_proj(xt, norm_mix[l][None], wz)
        kk, vvt = _kv_proj(xt, norm_mix[l][None], wk, wvt, bsz, seq)
        ya = _conv_module(z, conv_w[l], conv_b[l], conv_ln_g[l], conv_ln_b[l], bsz, seq)
        kc, vct = _compress(zc, jnp.stack([cmp_pos_k[l], cmp_pos_v[l]]), jnp.stack([cmp_w1_k[l], cmp_w1_v[l]]),
                            jnp.stack([cmp_b1_k[l], cmp_b1_v[l]]), jnp.stack([cmp_w2_k[l], cmp_w2_v[l]]), bsz, seq)
        yb = _nsa_attention(z, kc, vct, kk, vvt, tables, bsz, seq)
        qf, kf, vt = _mla_proj(z, rope_tab, mla_norm_q[l], mla_norm_kv[l], w_uq[l], w_ukv[l], bsz, seq)
        yc = _mla_attention(qf, kf, vt, bsz, seq)
        mem_kv = _norm_matmul(memt, norm_mem[l][None], w_xkv[l].astype(BF16), 256, 1024, BF16)
        mem_kv = mem_kv.reshape(bsz, m_len, 2 * XATTN_HEADS * XATTN_DH)
        xt = _merge_xattn(ya, yb, yc, z, xt, w_branch_conv[l], w_branch_nsa[l], w_branch_mla[l], w_out[l],
                          norm_xattn[l], w_xq[l], mem_kv, w_xo[l], bsz, seq)
        xt = _ffn(xt, norm_ffn[l], w_gate_up[l], w_down[l], norm_final, l == depth - 1)
    return xt.reshape(bsz, seq, d)
```

```python
import functools
import math

import numpy as np
import jax
import jax.numpy as jnp
from jax import lax
from jax.experimental import pallas as pl
from jax.experimental.pallas import tpu as pltpu

F32 = jnp.float32
BF16 = jnp.bfloat16

EPS = 1e-6
NEG_INF = -1e30
FORCE_SCORE = 1e4

D_MODEL = 1024
CONV_CH = 512
CONV_WIDTH = 31
NSA_HEADS = 8
NSA_G = 2
NSA_HG = NSA_HEADS // NSA_G
NSA_DH = 64
CMP_BLOCK = 32
CMP_STRIDE = 16
CMP_HIDDEN = 256
SLC_BLOCK = 64
N_SELECT = 16
WINDOW = 512
NSA_QB = 64
MLA_HEADS = 4
Q_RANK = 384
KV_RANK = 256
QK_NOPE = 128
QK_ROPE = 64
V_DIM = 128
ROPE_THETA = 10000.0
REL_BUCKETS = 32
REL_MAX_DIST = 128
XATTN_HEADS = 4
XATTN_DH = 128
FFN_HIDDEN = 2816

LANES = 128
SUBLANES = 8

O_GLU, O_Q, O_KV, O_GN, O_CQ, O_CKV, O_KR, O_GM = 0, 1024, 1536, 2304, 2328, 2712, 2968, 3032
GD = NSA_G * NSA_DH

Z_GM = 0
Z_UA = 3072
Z_UB = 3584
Z_Q = 4096
Z_CKV = 4608
Z_KR = 4864
Z_CQ = 4992
Z_CMP = 5376
Z_GN = 5632
Z_COLS = 5760

VMEM_LIMIT = 56 * 1024 * 1024

TOK_TILE = 512
IN_PROJ_TM = 1024
IN_PROJ_TN = 1152
FFN_TM = 1024
FFN_TC = 256
MLA_TQ = 512
MLA_TK = 1024
MLA_HEADS_PER_STEP = 2

LOG2E = math.log2(math.e)


def _cparams(sem):
    return pltpu.CompilerParams(dimension_semantics=sem, vmem_limit_bytes=VMEM_LIMIT)


def _rms(x, g):
    return x * lax.rsqrt(jnp.mean(x * x, axis=-1, keepdims=True) + EPS) * g


def _dot(a, b):
    return jnp.dot(a, b, preferred_element_type=F32)


def _dot_nt(a, b):
    return lax.dot_general(a, b, (((1,), (1,)), ((), ())), preferred_element_type=F32)


def _norm_matmul_kernel(x_ref, g_ref, w_ref, o_ref, h_ref):
    @pl.when(pl.program_id(1) == 0)
    def _():
        h_ref[...] = _rms(x_ref[...], g_ref[...]).astype(BF16)

    o_ref[...] = _dot(h_ref[...], w_ref[...]).astype(o_ref.dtype)


def _norm_matmul(x, g, w, tm, tn, out_dtype):
    m, k = x.shape
    n = w.shape[1]
    return pl.pallas_call(
        _norm_matmul_kernel,
        out_shape=jax.ShapeDtypeStruct((m, n), out_dtype),
        grid=(m // tm, n // tn),
        in_specs=[pl.BlockSpec((tm, k), lambda i, j: (i, 0)),
                  pl.BlockSpec((1, k), lambda i, j: (0, 0)),
                  pl.BlockSpec((k, tn), lambda i, j: (0, j))],
        out_specs=pl.BlockSpec((tm, tn), lambda i, j: (i, j)),
        scratch_shapes=[pltpu.VMEM((tm, k), BF16)],
        compiler_params=_cparams(("parallel", "arbitrary")),
        name="norm_matmul",
    )(x, g, w)


def _in_proj_kernel(x_ref, g_ref, w_ref, z_ref, zc_ref, h_ref, *, cmp_tile, cmp_off):
    @pl.when(pl.program_id(1) == 0)
    def _():
        h_ref[...] = _rms(x_ref[...], g_ref[...]).astype(BF16)

    acc = _dot(h_ref[...], w_ref[...])
    z_ref[...] = acc.astype(BF16)

    @pl.when(pl.program_id(1) == cmp_tile)
    def _():
        zc_ref[...] = acc[:, cmp_off:cmp_off + 2 * GD]


def _in_proj(x, g, w):
    tm, tn = IN_PROJ_TM, IN_PROJ_TN
    m, k = x.shape
    n = w.shape[1]
    return pl.pallas_call(
        functools.partial(_in_proj_kernel, cmp_tile=Z_CMP // tn, cmp_off=Z_CMP % tn),
        out_shape=(jax.ShapeDtypeStruct((m, n), BF16), jax.ShapeDtypeStruct((m, 2 * GD), F32)),
        grid=(m // tm, n // tn),
        in_specs=[pl.BlockSpec((tm, k), lambda i, j: (i, 0)),
                  pl.BlockSpec((1, k), lambda i, j: (0, 0)),
                  pl.BlockSpec((k, tn), lambda i, j: (0, j))],
        out_specs=(pl.BlockSpec((tm, tn), lambda i, j: (i, j)),
                   pl.BlockSpec((tm, 2 * GD), lambda i, j: (i, 0))),
        scratch_shapes=[pltpu.VMEM((tm, k), BF16)],
        compiler_params=_cparams(("parallel", "arbitrary")),
        name="in_proj",
    )(x, g, w)


def _kv_proj_kernel(x_ref, g_ref, wk_ref, wvt_ref, k_ref, vt_ref):
    @pl.when(pl.program_id(1) == 0)
    def _():
        k_ref[...] = jnp.zeros(k_ref.shape, BF16)
        vt_ref[...] = jnp.zeros(vt_ref.shape, BF16)

    @pl.when(pl.program_id(1) > 0)
    def _():
        h = _rms(x_ref[...], g_ref[...]).astype(BF16)
        k_ref[0] = _dot(h, wk_ref[...]).astype(BF16)
        vt_ref[0] = _dot_nt(wvt_ref[...], h).astype(BF16)


def _kv_proj(x, g, wk, wvt, bsz, seq):
    tm = WINDOW
    nst = seq // tm
    d = x.shape[1]
    nk = wk.shape[1]
    nv = wvt.shape[0]
    return pl.pallas_call(
        _kv_proj_kernel,
        out_shape=(jax.ShapeDtypeStruct((bsz, seq + tm, nk), BF16),
                   jax.ShapeDtypeStruct((bsz, nv, seq + tm), BF16)),
        grid=(bsz, nst + 1),
        in_specs=[pl.BlockSpec((tm, d), lambda b, s: (b * nst + jnp.maximum(s - 1, 0), 0)),
                  pl.BlockSpec((1, d), lambda b, s: (0, 0)),
                  pl.BlockSpec((d, nk), lambda b, s: (0, 0)),
                  pl.BlockSpec((nv, d), lambda b, s: (0, 0))],
        out_specs=(pl.BlockSpec((1, tm, nk), lambda b, s: (b, s, 0)),
                   pl.BlockSpec((1, nv, tm), lambda b, s: (b, 0, s))),
        compiler_params=_cparams(("parallel", "arbitrary")),
        name="nsa_kv_proj",
    )(x, g, wk, wvt)


CONV_HALO = 32


CONV_ROWS = 64


def _conv_kernel(a_ref, b_ref, w_ref, cb_ref, lg_ref, lb_ref, o_ref, buf_ref, sh_ref, *, ts):
    @pl.when(pl.program_id(1) == 0)
    def _():
        buf_ref[0:CONV_HALO, :] = jnp.zeros((CONV_HALO, CONV_CH), F32)

    buf_ref[CONV_HALO:CONV_HALO + ts, :] = a_ref[...].astype(F32) * jax.nn.sigmoid(b_ref[...].astype(F32))
    span = ts + CONV_HALO - SUBLANES
    for r in range(1, SUBLANES):
        sh_ref[r - 1, 0:span, :] = buf_ref[r:r + span, :]
    off = CONV_HALO - (CONV_WIDTH - 1)

    def rows(i, carry):
        r0 = pl.multiple_of(i * CONV_ROWS, CONV_ROWS)
        acc = jnp.zeros((CONV_ROWS, CONV_CH), F32) + cb_ref[...]
        for k in range(CONV_WIDTH):
            res, base = (off + k) % SUBLANES, (off + k) // SUBLANES * SUBLANES
            if res == 0:
                tap = buf_ref[pl.ds(r0 + base, CONV_ROWS), :]
            else:
                tap = sh_ref[res - 1, pl.ds(r0 + base, CONV_ROWS), :]
            acc = acc + tap * w_ref[k:k + 1, :]
        mu = jnp.mean(acc, axis=-1, keepdims=True)
        xc = acc - mu
        var = jnp.mean(xc * xc, axis=-1, keepdims=True)
        y = xc * lax.rsqrt(var + EPS) * lg_ref[...] + lb_ref[...]
        o_ref[pl.ds(r0, CONV_ROWS), :] = (y * jax.nn.sigmoid(y)).astype(BF16)
        return carry

    lax.fori_loop(0, ts // CONV_ROWS, rows, 0)
    buf_ref[0:CONV_HALO, :] = buf_ref[ts:ts + CONV_HALO, :]


def _conv_module(z, conv_w, conv_b, ln_g, ln_b, bsz, seq):
    ts = TOK_TILE
    nst = seq // ts
    wpad = jnp.zeros((32, CONV_CH), F32).at[:CONV_WIDTH].set(conv_w)
    return pl.pallas_call(
        functools.partial(_conv_kernel, ts=ts),
        out_shape=jax.ShapeDtypeStruct((bsz * seq, CONV_CH), BF16),
        grid=(bsz, nst),
        in_specs=[pl.BlockSpec((ts, CONV_CH), lambda b, s: (b * nst + s, Z_UA // CONV_CH)),
                  pl.BlockSpec((ts, CONV_CH), lambda b, s: (b * nst + s, Z_UB // CONV_CH)),
                  pl.BlockSpec((32, CONV_CH), lambda b, s: (0, 0)),
                  pl.BlockSpec((1, CONV_CH), lambda b, s: (0, 0)),
                  pl.BlockSpec((1, CONV_CH), lambda b, s: (0, 0)),
                  pl.BlockSpec((1, CONV_CH), lambda b, s: (0, 0))],
        out_specs=pl.BlockSpec((ts, CONV_CH), lambda b, s: (b * nst + s, 0)),
        scratch_shapes=[pltpu.VMEM((ts + CONV_HALO, CONV_CH), F32),
                        pltpu.VMEM((SUBLANES - 1, ts + CONV_HALO - SUBLANES, CONV_CH), F32)],
        compiler_params=_cparams(("arbitrary", "arbitrary")),
        name="conv_module",
    )(z, z, wpad, conv_b[None], ln_g[None], ln_b[None])


def _compress_kernel(xk_ref, xv_ref, pos_ref, w1_ref, b1_ref, w2k_ref, w2v_ref, kc_ref, vct_ref, *, nch):
    for kind, (x_ref, w2_ref) in enumerate(((xk_ref, w2k_ref), (xv_ref, w2v_ref))):
        a = jnp.zeros((nch, NSA_G * CMP_HIDDEN), F32)
        b = jnp.zeros((nch, NSA_G * CMP_HIDDEN), F32)
        for l in range(CMP_STRIDE):
            xs = x_ref[pl.ds(l, nch, stride=CMP_STRIDE), :]
            a = a + _dot((xs + pos_ref[kind, l:l + 1, :]).astype(BF16), w1_ref[kind, l])
            b = b + _dot((xs + pos_ref[kind, CMP_STRIDE + l:CMP_STRIDE + l + 1, :]).astype(BF16),
                         w1_ref[kind, CMP_STRIDE + l])
        pre = a + pltpu.roll(b, nch - 1, 0) + b1_ref[kind]
        out = _dot(jax.nn.gelu(pre).astype(BF16), w2_ref[...])
        if kind == 0:
            kc_ref[0] = out.astype(BF16)
        else:
            vct_ref[0] = out.T.astype(BF16)


def _blockdiag2(w):
    z = jnp.zeros_like(w)
    return jnp.concatenate([jnp.concatenate([w, z], axis=-1), jnp.concatenate([z, w], axis=-1)], axis=-2)


def _compress(z, pos, w1, b1, w2, bsz, seq):
    nch = seq // CMP_STRIDE
    pos2 = jnp.concatenate([pos, pos], axis=-1)
    w1e = _blockdiag2(w1.reshape(2, CMP_BLOCK, NSA_DH, CMP_HIDDEN)).astype(BF16)
    b1e = jnp.concatenate([b1, b1], axis=-1)[:, None]
    w2k = _blockdiag2(w2[0]).astype(BF16)
    w2v = _blockdiag2(w2[1]).astype(BF16)
    full = lambda a: pl.BlockSpec(a.shape, lambda b: (0,) * a.ndim)
    return pl.pallas_call(
        functools.partial(_compress_kernel, nch=nch),
        out_shape=(jax.ShapeDtypeStruct((bsz, nch, GD), BF16),
                   jax.ShapeDtypeStruct((bsz, GD, nch), BF16)),
        grid=(bsz,),
        in_specs=[pl.BlockSpec((seq, GD), lambda b: (b, 0)),
                  pl.BlockSpec((seq, GD), lambda b: (b, 1)),
                  full(pos2), full(w1e), full(b1e), full(w2k), full(w2v)],
        out_specs=(pl.BlockSpec((1, nch, GD), lambda b: (b, 0, 0)),
                   pl.BlockSpec((1, GD, nch), lambda b: (b, 0, 0))),
        compiler_params=_cparams(("parallel",)),
        name="nsa_compress",
    )(z, z, pos2, w1e, b1e, w2k, w2v)


NSA_QP = 2 * NSA_QB
NEAR_KEYS = 256
WIN_KEYS = 640
CMP_TAB_ROWS = 512
CMP_TAB_ZERO = 256


def _t5_bucket_np(d):
    exact = REL_BUCKETS // 2
    d = np.maximum(d, 0)
    ratio = np.log(np.maximum(d, 1).astype(np.float32) / np.float32(exact)) / np.float32(math.log(REL_MAX_DIST / exact))
    large = np.minimum(exact + (ratio * (REL_BUCKETS - exact)).astype(np.int32), REL_BUCKETS - 1)
    return np.where(d < exact, d, large).astype(np.int32)


def _bucket_thresholds():
    exact = REL_BUCKETS // 2
    bk = _t5_bucket_np(np.arange(4 * REL_MAX_DIST))
    assert np.all(np.diff(bk) >= 0) and bk[-1] == REL_BUCKETS - 1
    return [int(np.argmax(bk >= k)) for k in range(exact + 1, REL_BUCKETS)]


def _bias_rows(rel_ref, dist, valid, shift):
    exact = REL_BUCKETS // 2
    bucket = jnp.full(dist.shape, exact, jnp.int32)
    for thr in _bucket_thresholds():
        bucket = bucket + jnp.where(dist >= thr, 1, 0)
    bucket = jnp.where(dist < exact, dist, bucket)
    val = jnp.zeros(dist.shape, F32)
    for bkt in range(REL_BUCKETS):
        val = jnp.where(bucket == bkt, rel_ref[0, bkt:bkt + 1, :], val)
    if shift:
        val = val - rel_ref[0, REL_BUCKETS - 1:REL_BUCKETS, :]
    return jnp.where(valid, val * LOG2E, NEG_INF)


def _nsa_bias_kernel(rel_ref, tc_ref, tn_ref, tw_ref):
    hq = NSA_HG * NSA_QP
    rows = 128

    def dist_of(nrows, r0, fn):
        r = r0 + lax.broadcasted_iota(jnp.int32, (nrows, hq), 0)
        t = lax.bitwise_and(lax.broadcasted_iota(jnp.int32, (nrows, hq), 1), NSA_QP - 1)
        return fn(r, t)

    for r0 in range(0, CMP_TAB_ROWS, rows):
        d = dist_of(rows, r0, lambda r, t: t - CMP_STRIDE * (r - CMP_TAB_ZERO) - (CMP_BLOCK - 1))
        tc_ref[0, r0:r0 + rows, :] = _bias_rows(rel_ref, d, d >= 0, False)
    for r0 in range(0, NEAR_KEYS, rows):
        d = dist_of(rows, r0, lambda r, t: NEAR_KEYS // 2 + t - r)
        tn_ref[0, r0:r0 + rows, :] = _bias_rows(rel_ref, d, d >= 0, True)
    for r0 in range(0, WIN_KEYS, rows):
        d = dist_of(rows, r0, lambda r, t: WINDOW + t - r)
        tw_ref[0, r0:r0 + rows, :] = _bias_rows(rel_ref, d, (d >= 0) & (d < WINDOW), False)


def _nsa_tables(rel_bias):
    hq = NSA_HG * NSA_QP
    rel4 = jnp.repeat(rel_bias.reshape(REL_BUCKETS, NSA_G, NSA_HG).transpose(1, 0, 2), NSA_QP, axis=-1)
    spec = lambda r: pl.BlockSpec((1, r, hq), lambda g: (g, 0, 0))
    return pl.pallas_call(
        _nsa_bias_kernel,
        out_shape=(jax.ShapeDtypeStruct((NSA_G, CMP_TAB_ROWS, hq), F32),
                   jax.ShapeDtypeStruct((NSA_G, NEAR_KEYS, hq), F32),
                   jax.ShapeDtypeStruct((NSA_G, WIN_KEYS, hq), F32)),
        grid=(NSA_G,),
        in_specs=[spec(REL_BUCKETS)],
        out_specs=(spec(CMP_TAB_ROWS), spec(NEAR_KEYS), spec(WIN_KEYS)),
        compiler_params=_cparams(("parallel",)),
        name="nsa_bias_tables",
    )(rel4)


SEL_PAD = 8
FAR_KEYS = 1024
KV_FRONT = WINDOW
KREP = NSA_HG * NSA_DH


def _softmax_cols(s):
    m = jnp.max(s, axis=0, keepdims=True)
    p = jnp.exp2(s - m)
    return m, p, jnp.sum(p, axis=0, keepdims=True)


def _mask_blocks(s, mask_ref, row0, nblk):
    parts = []
    for jj in range(nblk):
        row = mask_ref[pl.ds(row0 + jj, 1), :]
        parts.append(jnp.where(row > 0.0, s[SLC_BLOCK * jj:SLC_BLOCK * (jj + 1)], NEG_INF))
    return jnp.concatenate(parts, axis=0)


def _rank_select(score_ref, n_sb, n_sel):
    groups = n_sb // SUBLANES
    sub = lax.broadcasted_iota(jnp.int32, (SUBLANES, NSA_QP), 0)
    tiles = [score_ref[SUBLANES * v:SUBLANES * (v + 1), :] for v in range(groups)]
    cnts = [jnp.zeros((SUBLANES, NSA_QP), F32) for _ in range(groups)]
    for jp in range(n_sb):
        row = score_ref[jp:jp + 1, :]
        for v in range(groups):
            lo = SUBLANES * v
            if jp < lo:
                beats = row >= tiles[v]
            elif jp >= lo + SUBLANES - 1:
                beats = row > tiles[v]
            else:
                beats = (row > tiles[v]) | ((row == tiles[v]) & (sub > jp - lo))
            cnts[v] = cnts[v] + jnp.where(beats, 1.0, 0.0)
    cnt = jnp.concatenate(cnts, axis=0)
    return jnp.where(cnt < float(n_sel), 1.0, 0.0)


def _nsa_kernel(q_ref, gate_ref, kc_ref, vct_ref, ks_ref, kw_ref, vst_ref, vwt_ref,
                tc_ref, tn_ref, tw_ref, cov_ref, rep_ref, o_ref,
                sel_ref, selfar_ref, score_ref, *, n_sb):
    p2 = pl.program_id(1)
    hq = NSA_HG * NSA_QP
    nch = kc_ref.shape[1]
    groups = range(NSA_G)
    qcol = lambda g: slice(KREP * g, KREP * (g + 1))
    vrow = lambda g: slice(NSA_DH * g, NSA_DH * (g + 1))

    qs = []
    zq = jnp.zeros((hq, NSA_DH), BF16)
    for g in groups:
        qb = (q_ref[:, qcol(g)].astype(F32) * (NSA_DH ** -0.5 * LOG2E)).astype(BF16)
        q64 = jnp.concatenate([qb[:, NSA_DH * h:NSA_DH * (h + 1)] for h in range(NSA_HG)], axis=0)
        qs.append(jnp.concatenate([q64, zq] if g == 0 else [zq, q64], axis=1))

    start_c = pl.multiple_of(CMP_TAB_ZERO - (NSA_QP // CMP_STRIDE) * p2, SUBLANES)
    lane = lax.broadcasted_iota(jnp.int32, (1, hq), 1)
    tq = NSA_QP * p2 + lax.bitwise_and(lane, NSA_QP - 1)
    anyv = jnp.where(tq >= CMP_BLOCK - 1, 1.0, 0.0)
    jrow = lax.broadcasted_iota(jnp.int32, (n_sb, NSA_QP), 0)
    tok = lax.broadcasted_iota(jnp.int32, (n_sb, NSA_QP), 1)
    cur = 2 * p2 + lax.shift_right_logical(tok, 6)
    forced = (jrow == 0) | (jrow == cur) | (jrow == cur - 1)
    o_cmp = []
    for g in groups:
        sc = _dot_nt(kc_ref[0], qs[g]) + tc_ref[g, pl.ds(start_c, nch), :]
        _, pc, lc = _softmax_cols(sc)
        pc = pc * (anyv / lc)
        o_cmp.append(_dot(vct_ref[0, vrow(g), :], pc.astype(BF16)))
        psum = pc[:, 0:NSA_QP]
        for h in range(1, NSA_HG):
            psum = psum + pc[:, NSA_QP * h:NSA_QP * (h + 1)]
        p_hi = psum.astype(BF16)
        p_lo = (psum - p_hi.astype(F32)).astype(BF16)
        imp = _dot(cov_ref[...], p_hi) + _dot(cov_ref[...], p_lo)
        score_ref[g] = jnp.where(forced, FORCE_SCORE, jnp.where(jrow <= cur, imp, -1.0))

    zeros8 = jnp.zeros((SEL_PAD, hq), F32)
    jrow4 = lax.broadcasted_iota(jnp.int32, (n_sb, hq), 0)
    for g in groups:
        sel = _rank_select(score_ref.at[g], n_sb, min(N_SELECT, n_sb)).astype(BF16)
        sel4 = _dot(sel, rep_ref[...])
        sel_ref[g, 0:SEL_PAD, :] = zeros8
        sel_ref[g, SEL_PAD + n_sb:2 * SEL_PAD + n_sb, :] = zeros8
        sel_ref[g, SEL_PAD:SEL_PAD + n_sb, :] = sel4
        selfar_ref[g, 0:SEL_PAD, :] = zeros8
        selfar_ref[g, SEL_PAD + n_sb:2 * SEL_PAD + n_sb, :] = zeros8
        selfar_ref[g, SEL_PAD:SEL_PAD + n_sb, :] = jnp.where(jrow4 < 2 * p2 - 2, sel4, 0.0)

    win0 = pl.multiple_of(NSA_QP * p2, LANES)
    near0 = pl.multiple_of(win0 + KV_FRONT - NEAR_KEYS // 2, LANES)
    state = []
    for g in groups:
        s = _dot_nt(ks_ref[0, pl.ds(near0, NEAR_KEYS), :], qs[g]) + tn_ref[g]
        s = _mask_blocks(s, sel_ref.at[g], 2 * p2 - 2 + SEL_PAD, NEAR_KEYS // SLC_BLOCK)
        m_s, p_s, l_s = _softmax_cols(s)
        state += [m_s, l_s, _dot(vst_ref[0, vrow(g), pl.ds(near0, NEAR_KEYS)], p_s.astype(BF16))]

    window = []
    for g in groups:
        sw = _dot_nt(kw_ref[0, pl.ds(win0, WIN_KEYS), :], qs[g]) + tw_ref[g]
        slabs = [sw[NSA_QP * j:NSA_QP * (j + 1)] for j in range(WIN_KEYS // NSA_QP)]
        for j in range(KV_FRONT // NSA_QP):
            slabs[j] = jnp.where(NSA_QP * j + win0 >= KV_FRONT, slabs[j], NEG_INF)
        _, p_w, l_w = _softmax_cols(jnp.concatenate(slabs, axis=0))
        window.append((_dot(vwt_ref[0, vrow(g), pl.ds(win0, WIN_KEYS)], p_w.astype(BF16)), l_w))

    def far_body(c, carry):
        k0 = pl.multiple_of(FAR_KEYS * c + KV_FRONT, LANES)
        kf = ks_ref[0, pl.ds(k0, FAR_KEYS), :]
        sfs = [_dot_nt(kf, qs[g]) for g in groups]
        out = []
        for g in groups:
            m_old, l_old, acc_old = carry[3 * g:3 * g + 3]
            sf = _mask_blocks(sfs[g], selfar_ref.at[g], (FAR_KEYS // SLC_BLOCK) * c + SEL_PAD, FAR_KEYS // SLC_BLOCK)
            m_new = jnp.maximum(m_old, jnp.max(sf, axis=0, keepdims=True))
            alpha = jnp.exp2(m_old - m_new)
            pf = jnp.exp2(sf - m_new)
            l_new = alpha * l_old + jnp.sum(pf, axis=0, keepdims=True)
            acc_new = alpha * acc_old + _dot(vst_ref[0, vrow(g), pl.ds(k0, FAR_KEYS)], pf.astype(BF16))
            out += [m_new, l_new, acc_new]
        return tuple(out)

    n_far = lax.div(jnp.maximum(p2 - 1, 0) * NSA_QP + FAR_KEYS - 1, FAR_KEYS)
    state = lax.fori_loop(0, n_far, far_body, tuple(state))

    gt = gate_ref[...].astype(F32).T
    r = lax.broadcasted_iota(jnp.int32, (NSA_QP, NSA_QP), 0)
    c = lax.broadcasted_iota(jnp.int32, (NSA_QP, NSA_QP), 1)
    eye = jnp.where(r == c, 1.0, 0.0).astype(BF16)
    for g in groups:
        _, l_s, acc_s = state[3 * g:3 * g + 3]
        acc_w, l_w = window[g]
        gsel = jax.nn.sigmoid(gt[3 * NSA_HG * g:3 * NSA_HG * (g + 1)])
        gate = lambda b: jnp.concatenate([gsel[3 * h + b:3 * h + b + 1] for h in range(NSA_HG)], axis=1)
        out_t = (gate(0) * o_cmp[g] + (gate(1) / l_s) * acc_s + (gate(2) / l_w) * acc_w).astype(BF16)
        stacked = jnp.concatenate([out_t[:, NSA_QP * h:NSA_QP * (h + 1)] for h in range(NSA_HG)], axis=0)
        o_ref[:, qcol(g)] = _dot_nt(eye, stacked).astype(BF16)


def _nsa_attention(z, kc, vct, kk, vvt, tables, bsz, seq):
    g, hg, dh, qp = NSA_G, NSA_HG, NSA_DH, NSA_QP
    nstep = seq // qp
    n_sb = seq // SLC_BLOCK
    nch = kc.shape[1]
    hq = hg * qp
    sp = kk.shape[1]
    tc, tn, tw = tables
    c_start = CMP_STRIDE * np.arange(nch)
    s_start = SLC_BLOCK * np.arange(n_sb)
    cover_t = ((c_start[None, :] < s_start[:, None] + SLC_BLOCK)
               & (c_start[None, :] + CMP_BLOCK > s_start[:, None])
               & (np.arange(nch)[None, :] < (seq - CMP_BLOCK) // CMP_STRIDE + 1))
    cover_t = jnp.asarray(cover_t.astype(np.float32), BF16)
    rep = jnp.asarray(np.tile(np.eye(qp, dtype=np.float32), (1, hg)), BF16)
    full = lambda a: pl.BlockSpec(a.shape, lambda b, i: (0,) * a.ndim, pipeline_mode=pl.Buffered(1))
    qd = g * hg * dh
    return pl.pallas_call(
        functools.partial(_nsa_kernel, n_sb=n_sb),
        out_shape=jax.ShapeDtypeStruct((bsz * seq, qd), BF16),
        grid=(bsz, nstep),
        in_specs=[pl.BlockSpec((qp, qd), lambda b, i: (b * nstep + i, Z_Q // qd)),
                  pl.BlockSpec((qp, LANES), lambda b, i: (b * nstep + i, Z_GN // LANES)),
                  pl.BlockSpec((1, nch, GD), lambda b, i: (b, 0, 0)),
                  pl.BlockSpec((1, g * dh, nch), lambda b, i: (b, 0, 0)),
                  pl.BlockSpec((1, sp, GD), lambda b, i: (b, 0, 0)),
                  pl.BlockSpec((1, sp, GD), lambda b, i: (b, 0, 1)),
                  pl.BlockSpec((1, g * dh, sp), lambda b, i: (b, 0, 0)),
                  pl.BlockSpec((1, g * dh, sp), lambda b, i: (b, 1, 0)),
                  full(tc), full(tn), full(tw), full(cover_t), full(rep)],
        out_specs=pl.BlockSpec((qp, qd), lambda b, i: (b * nstep + i, 0)),
        scratch_shapes=[pltpu.VMEM((g, n_sb + 2 * SEL_PAD, hq), F32),
                        pltpu.VMEM((g, n_sb + 2 * SEL_PAD, hq), F32),
                        pltpu.VMEM((g, n_sb, qp), F32)],
        compiler_params=_cparams(("parallel", "arbitrary")),
        name="nsa_attention",
    )(z, z, kc, vct, kk, kk, vvt, vvt, tc, tn, tw, cover_t, rep)


def _rope_table_kernel(pos_ref, inv_ref, o_ref):
    ang = inv_ref[...] * pos_ref[0].astype(F32)
    c, s = jnp.cos(ang), jnp.sin(ang)
    o_ref[...] = jnp.concatenate([c, c, -s, s], axis=0).T


def _rope_table(positions):
    tm = TOK_TILE
    t = positions.size
    half = QK_ROPE // 2
    inv = (ROPE_THETA ** (-jnp.arange(half, dtype=F32) / half))[:, None]
    return pl.pallas_call(
        _rope_table_kernel,
        out_shape=jax.ShapeDtypeStruct((t, 2 * QK_ROPE), F32),
        grid=(t // tm,),
        in_specs=[pl.BlockSpec((1, 1, tm), lambda i: (i, 0, 0)),
                  pl.BlockSpec((half, 1), lambda i: (0, 0))],
        out_specs=pl.BlockSpec((tm, 2 * QK_ROPE), lambda i: (i, 0)),
        compiler_params=_cparams(("parallel",)),
        name="rope_table",
    )(positions.reshape(t // tm, 1, tm), inv)


MLA_HW = 256


def _mla_proj_kernel(cq_ref, ckv_ref, kr_ref, rope_ref, nq_ref, nkv_ref, wq_ref, wkn_ref, wvt_ref,
                     q_ref, k_ref, vt_ref):
    scale = (QK_NOPE + QK_ROPE) ** -0.5 * LOG2E
    rope = rope_ref[...]
    yq = _dot(_rms(cq_ref[...].astype(F32), nq_ref[...]).astype(BF16), wq_ref[...])
    ckv = _rms(ckv_ref[...].astype(F32), nkv_ref[...]).astype(BF16)
    ykn = _dot(ckv, wkn_ref[...])
    vt_ref[0] = _dot_nt(wvt_ref[...], ckv).astype(BF16)
    kp = kr_ref[...].astype(F32) * rope
    kp = kp + pltpu.roll(kp, QK_ROPE, 1)
    lane = lax.broadcasted_iota(jnp.int32, kp.shape, 1)
    kp = jnp.where(lane < QK_ROPE, kp, 0.0).astype(BF16)
    for h in range(MLA_HEADS):
        base = MLA_HW * h
        q_ref[:, base:base + QK_NOPE] = (yq[:, base:base + QK_NOPE] * scale).astype(BF16)
        qp = yq[:, base + QK_NOPE:base + MLA_HW] * rope
        qp = qp + pltpu.roll(qp, QK_ROPE, 1)
        q_ref[:, base + QK_NOPE:base + MLA_HW] = (qp * scale).astype(BF16)
        k_ref[:, base:base + QK_NOPE] = ykn[:, QK_NOPE * h:QK_NOPE * (h + 1)].astype(BF16)
        k_ref[:, base + QK_NOPE:base + MLA_HW] = kp


def _swap_halves(w):
    half = QK_ROPE // 2
    return jnp.concatenate([w[..., half:], w[..., :half]], axis=-1)


def _mla_proj(z, rope_tab, norm_q, norm_kv, w_uq, w_ukv, bsz, seq):
    tm = TOK_TILE
    t = z.shape[0]
    nst = seq // tm
    wq = w_uq.reshape(Q_RANK, MLA_HEADS, QK_NOPE + QK_ROPE)
    wq = jnp.concatenate([wq, _swap_halves(wq[..., QK_NOPE:])], axis=-1)
    wq = wq.reshape(Q_RANK, MLA_HEADS * MLA_HW).astype(BF16)
    wkv = w_ukv.reshape(KV_RANK, MLA_HEADS, QK_NOPE + V_DIM)
    wkn = wkv[..., :QK_NOPE].reshape(KV_RANK, MLA_HEADS * QK_NOPE).astype(BF16)
    wvt = wkv[..., QK_NOPE:].reshape(KV_RANK, MLA_HEADS * V_DIM).T.astype(BF16)
    hw = MLA_HEADS * MLA_HW
    hv = MLA_HEADS * V_DIM
    row = lambda b, s: b * nst + s
    return pl.pallas_call(
        _mla_proj_kernel,
        out_shape=(jax.ShapeDtypeStruct((t, hw), BF16),
                   jax.ShapeDtypeStruct((t, hw), BF16),
                   jax.ShapeDtypeStruct((bsz, hv, seq), BF16)),
        grid=(bsz, nst),
        in_specs=[pl.BlockSpec((tm, Q_RANK), lambda b, s: (row(b, s), Z_CQ // Q_RANK)),
                  pl.BlockSpec((tm, KV_RANK), lambda b, s: (row(b, s), Z_CKV // KV_RANK)),
                  pl.BlockSpec((tm, 2 * QK_ROPE), lambda b, s: (row(b, s), Z_KR // (2 * QK_ROPE))),
                  pl.BlockSpec((tm, 2 * QK_ROPE), lambda b, s: (row(b, s), 0)),
                  pl.BlockSpec((1, Q_RANK), lambda b, s: (0, 0)),
                  pl.BlockSpec((1, KV_RANK), lambda b, s: (0, 0)),
                  pl.BlockSpec((Q_RANK, hw), lambda b, s: (0, 0)),
                  pl.BlockSpec((KV_RANK, hv), lambda b, s: (0, 0)),
                  pl.BlockSpec((hv, KV_RANK), lambda b, s: (0, 0))],
        out_specs=(pl.BlockSpec((tm, hw), lambda b, s: (row(b, s), 0)),
                   pl.BlockSpec((tm, hw), lambda b, s: (row(b, s), 0)),
                   pl.BlockSpec((1, hv, tm), lambda b, s: (b, 0, s))),
        compiler_params=_cparams(("parallel", "parallel")),
        name="mla_proj",
    )(z, z, z, rope_tab, norm_q[None], norm_kv[None], wq, wkn, wvt)


def _mla_attn_kernel(q_ref, k_ref, vt_ref, o_ref, *, tq, tk, nh):
    iq = pl.program_id(2)
    cd = lax.div(iq * tq, tk)
    heads = range(nh)
    hcol = lambda h: slice(MLA_HW * h, MLA_HW * (h + 1))
    vrow = lambda h: slice(V_DIM * h, V_DIM * (h + 1))
    qs = [q_ref[:, hcol(h)] for h in heads]

    def scores(c, h):
        k0 = pl.multiple_of(c * tk, tk)
        return _dot_nt(k_ref[0, pl.ds(k0, tk), hcol(h)], qs[h])

    def diagonal(nk):
        k0 = pl.multiple_of((iq + 1) * tq - nk, tq)
        kpos = k0 + lax.broadcasted_iota(jnp.int32, (nk, tq), 0)
        qpos = iq * tq + lax.broadcasted_iota(jnp.int32, (nk, tq), 1)
        st = []
        for h in heads:
            s = jnp.where(kpos <= qpos, _dot_nt(k_ref[0, pl.ds(k0, nk), hcol(h)], qs[h]), NEG_INF)
            m0, p0, l0 = _softmax_cols(s)
            st += [m0, l0, _dot(vt_ref[0, vrow(h), pl.ds(k0, nk)], p0.astype(BF16))]
        return tuple(st)

    assert tk == 2 * tq
    state = lax.cond(lax.rem(iq, 2) == 0, lambda: diagonal(tq), lambda: diagonal(tk))

    def body(c, carry):
        k0 = pl.multiple_of(c * tk, tk)
        ss = [scores(c, h) for h in heads]
        out = []
        for h in heads:
            m_old, l_old, acc_old = carry[3 * h:3 * h + 3]
            m_new = jnp.maximum(m_old, jnp.max(ss[h], axis=0, keepdims=True))
            alpha = jnp.exp2(m_old - m_new)
            p = jnp.exp2(ss[h] - m_new)
            l_new = alpha * l_old + jnp.sum(p, axis=0, keepdims=True)
            acc_new = alpha * acc_old + _dot(vt_ref[0, vrow(h), pl.ds(k0, tk)], p.astype(BF16))
            out += [m_new, l_new, acc_new]
        return tuple(out)

    state = lax.fori_loop(0, cd, body, tuple(state))
    for h in heads:
        _, l, acc = state[3 * h:3 * h + 3]
        o_ref[:, vrow(h)] = (acc / l).T.astype(BF16)


def _mla_attention(qf, kf, vt, bsz, seq):
    tq, tk, nh = MLA_TQ, MLA_TK, MLA_HEADS_PER_STEP
    h = MLA_HEADS
    nq = seq // tq
    k3 = kf.reshape(bsz, seq, h * MLA_HW)
    return pl.pallas_call(
        functools.partial(_mla_attn_kernel, tq=tq, tk=tk, nh=nh),
        out_shape=jax.ShapeDtypeStruct((bsz * seq, h * V_DIM), BF16),
        grid=(bsz, h // nh, nq),
        in_specs=[pl.BlockSpec((tq, nh * MLA_HW), lambda b, hh, i: (b * nq + i, hh)),
                  pl.BlockSpec((1, seq, nh * MLA_HW), lambda b, hh, i: (b, 0, hh)),
                  pl.BlockSpec((1, nh * V_DIM, seq), lambda b, hh, i: (b, hh, 0))],
        out_specs=pl.BlockSpec((tq, nh * V_DIM), lambda b, hh, i: (b * nq + i, hh)),
        compiler_params=_cparams(("parallel", "parallel", "arbitrary")),
        name="mla_attention",
    )(qf, k3, vt)


def _merge_xattn_kernel(ya_ref, yb_ref, yc_ref, ga_ref, gb_ref, gc_ref, x_ref,
                        wa_ref, wb_ref, wc_ref, wo_ref,
                        gx_ref, wq_ref, kv_ref, wxo_ref, o_ref):
    sig = lambda ref: jax.nn.sigmoid(ref[...].astype(F32))
    y = (sig(ga_ref) * _dot(ya_ref[...], wa_ref[...])
         + sig(gb_ref) * _dot(yb_ref[...], wb_ref[...])
         + sig(gc_ref) * _dot(yc_ref[...], wc_ref[...]))
    x = x_ref[...] + _dot(y.astype(BF16), wo_ref[...])
    h = _rms(x, gx_ref[...]).astype(BF16)
    q = _dot(h, wq_ref[...]) * XATTN_DH ** -0.5
    hd = XATTN_HEADS * XATTN_DH
    outs = []
    for hh in range(XATTN_HEADS):
        qh = q[:, XATTN_DH * hh:XATTN_DH * (hh + 1)].astype(BF16)
        kh = kv_ref[0, :, XATTN_DH * hh:XATTN_DH * (hh + 1)]
        vh = kv_ref[0, :, hd + XATTN_DH * hh:hd + XATTN_DH * (hh + 1)]
        s = _dot_nt(qh, kh)
        m = jnp.max(s, axis=-1, keepdims=True)
        p = jnp.exp(s - m)
        p = p / jnp.sum(p, axis=-1, keepdims=True)
        outs.append(_dot(p.astype(BF16), vh))
    o = jnp.concatenate(outs, axis=-1).astype(BF16)
    o_ref[...] = x + _dot(o, wxo_ref[...])


def _merge_xattn(ya, yb, yc, z, x, wa, wb, wc, wo, gx, wq, kv, wxo, bsz, seq):
    tm = TOK_TILE
    t, d = x.shape
    nst = seq // tm
    m_len = kv.shape[1]
    hd = XATTN_HEADS * XATTN_DH
    row = lambda b, s: b * nst + s
    act = pl.BlockSpec((tm, ya.shape[1]), lambda b, s: (row(b, s), 0))
    gate = lambda k: pl.BlockSpec((tm, d), lambda b, s: (row(b, s), Z_GM // d + k))
    const = lambda shape: pl.BlockSpec(shape, lambda b, s: (0, 0))
    bf = lambda w: w.astype(BF16)
    return pl.pallas_call(
        _merge_xattn_kernel,
        out_shape=jax.ShapeDtypeStruct((t, d), F32),
        grid=(bsz, nst),
        in_specs=[act, act, act, gate(0), gate(1), gate(2),
                  pl.BlockSpec((tm, d), lambda b, s: (row(b, s), 0)),
                  const((ya.shape[1], d)), const((ya.shape[1], d)), const((ya.shape[1], d)), const((d, d)),
                  const((1, d)), const((d, hd)),
                  pl.BlockSpec((1, m_len, 2 * hd), lambda b, s: (b, 0, 0)),
                  const((hd, d))],
        out_specs=pl.BlockSpec((tm, d), lambda b, s: (row(b, s), 0)),
        compiler_params=_cparams(("parallel", "parallel")),
        name="merge_xattn",
    )(ya, yb, yc, z, z, z, x, bf(wa), bf(wb), bf(wc), bf(wo), gx[None], bf(wq), kv, bf(wxo))


def _ffn_kernel(x_ref, g_ref, wg_ref, wu_ref, wd_ref, gf_ref, o_ref, h_ref, acc_ref, *, final):
    c = pl.program_id(1)

    @pl.when(c == 0)
    def _():
        h_ref[...] = _rms(x_ref[...], g_ref[...]).astype(BF16)
        acc_ref[...] = x_ref[...]

    h = h_ref[...]
    gate = _dot(h, wg_ref[...])
    up = _dot(h, wu_ref[...])
    act = (gate * jax.nn.sigmoid(gate) * up).astype(BF16)
    acc_ref[...] += _dot(act, wd_ref[...])

    @pl.when(c == pl.num_programs(1) - 1)
    def _():
        y = acc_ref[...]
        o_ref[...] = _rms(y, gf_ref[...]) if final else y


def _ffn(x, g, w_gate_up, w_down, g_final, final):
    tm, tc = FFN_TM, FFN_TC
    t, d = x.shape
    nc = FFN_HIDDEN // tc
    wgu = w_gate_up.astype(BF16)
    return pl.pallas_call(
        functools.partial(_ffn_kernel, final=final),
        out_shape=jax.ShapeDtypeStruct((t, d), F32),
        grid=(t // tm, nc),
        in_specs=[pl.BlockSpec((tm, d), lambda i, c: (i, 0)),
                  pl.BlockSpec((1, d), lambda i, c: (0, 0)),
                  pl.BlockSpec((d, tc), lambda i, c: (0, c)),
                  pl.BlockSpec((d, tc), lambda i, c: (0, nc + c)),
                  pl.BlockSpec((tc, d), lambda i, c: (c, 0)),
                  pl.BlockSpec((1, d), lambda i, c: (0, 0))],
        out_specs=pl.BlockSpec((tm, d), lambda i, c: (i, 0)),
        scratch_shapes=[pltpu.VMEM((tm, d), BF16), pltpu.VMEM((tm, d), F32)],
        compiler_params=_cparams(("parallel", "arbitrary")),
        name="ffn",
    )(x, g[None], wgu, wgu, w_down.astype(BF16), g_final[None])


def _split_w_in(w):
    k_rope = w[:, O_KR:O_KR + QK_ROPE]
    kv = lambda kind: w[:, O_KV + GD * kind:O_KV + GD * (kind + 1)]
    pad = jnp.zeros((w.shape[0], Z_COLS - Z_GN - 3 * NSA_HEADS), w.dtype)
    wz = jnp.concatenate([
        w[:, O_GM:O_GM + 3 * D_MODEL],
        w[:, O_GLU:O_GLU + 2 * CONV_CH],
        w[:, O_Q:O_Q + NSA_HEADS * NSA_DH],
        w[:, O_CKV:O_CKV + KV_RANK],
        k_rope, _swap_halves(k_rope),
        w[:, O_CQ:O_CQ + Q_RANK],
        kv(0), kv(1),
        w[:, O_GN:O_GN + 3 * NSA_HEADS], pad], axis=1).astype(BF16)
    wk = jnp.concatenate([kv(2), kv(4)], axis=1).astype(BF16)
    wvt = jnp.concatenate([kv(3), kv(5)], axis=1).T.astype(BF16)
    return wz, wk, wvt


def kernel(x, mem, positions, rel_bias, norm_mix, norm_xattn, norm_mem, norm_ffn, norm_final, w_in, conv_w, conv_b, conv_ln_g, conv_ln_b, w_branch_conv, cmp_pos_k, cmp_w1_k, cmp_b1_k, cmp_w2_k, cmp_pos_v, cmp_w1_v, cmp_b1_v, cmp_w2_v, w_branch_nsa, mla_norm_q, mla_norm_kv, w_uq, w_ukv, w_branch_mla, w_out, w_xq, w_xkv, w_xo, w_gate_up, w_down):
    bsz, seq, d = x.shape
    depth = w_in.shape[0]
    t = bsz * seq
    m_len = mem.shape[1]
    xt = x.reshape(t, d)
    memt = mem.reshape(bsz * m_len, d)
    rope_tab = _rope_table(positions)
    tables = _nsa_tables(rel_bias)
    for l in range(depth):
        wz, wk, wvt = _split_w_in(w_in[l])
        z, zc = _in_proj(xt, norm_mix[l][None], wz)
        kk, vvt = _kv_proj(xt, norm_mix[l][None], wk, wvt, bsz, seq)
        ya = _conv_module(z, conv_w[l], conv_b[l], conv_ln_g[l], conv_ln_b[l], bsz, seq)
        kc, vct = _compress(zc, jnp.stack([cmp_pos_k[l], cmp_pos_v[l]]), jnp.stack([cmp_w1_k[l], cmp_w1_v[l]]),
                            jnp.stack([cmp_b1_k[l], cmp_b1_v[l]]), jnp.stack([cmp_w2_k[l], cmp_w2_v[l]]), bsz, seq)
        yb = _nsa_attention(z, kc, vct, kk, vvt, tables, bsz, seq)
        qf, kf, vt = _mla_proj(z, rope_tab, mla_norm_q[l], mla_norm_kv[l], w_uq[l], w_ukv[l], bsz, seq)
        yc = _mla_attention(qf, kf, vt, bsz, seq)
        mem_kv = _norm_matmul(memt, norm_mem[l][None], w_xkv[l].astype(BF16), 256, 1024, BF16)
        mem_kv = mem_kv.reshape(bsz, m_len, 2 * XATTN_HEADS * XATTN_DH)
        xt = _merge_xattn(ya, yb, yc, z, xt, w_branch_conv[l], w_branch_nsa[l], w_branch_mla[l], w_out[l],
                          norm_xattn[l], w_xq[l], mem_kv, w_xo[l], bsz, seq)
        xt = _ffn(xt, norm_ffn[l], w_gate_up[l], w_down[l], norm_final, l == depth - 1)
    return xt.reshape(bsz, seq, d)
```

```python
import functools
import math

import numpy as np
import jax
import jax.numpy as jnp
from jax import lax
from jax.experimental import pallas as pl
from jax.experimental.pallas import tpu as pltpu

F32 = jnp.float32
BF16 = jnp.bfloat16

EPS = 1e-6
NEG_INF = -1e30
FORCE_SCORE = 1e4

D_MODEL = 1024
CONV_CH = 512
CONV_WIDTH = 31
NSA_HEADS = 8
NSA_G = 2
NSA_HG = NSA_HEADS // NSA_G
NSA_DH = 64
CMP_BLOCK = 32
CMP_STRIDE = 16
CMP_HIDDEN = 256
SLC_BLOCK = 64
N_SELECT = 16
WINDOW = 512
NSA_QB = 64
MLA_HEADS = 4
Q_RANK = 384
KV_RANK = 256
QK_NOPE = 128
QK_ROPE = 64
V_DIM = 128
ROPE_THETA = 10000.0
REL_BUCKETS = 32
REL_MAX_DIST = 128
XATTN_HEADS = 4
XATTN_DH = 128
FFN_HIDDEN = 2816

LANES = 128
SUBLANES = 8

O_GLU, O_Q, O_KV, O_GN, O_CQ, O_CKV, O_KR, O_GM = 0, 1024, 1536, 2304, 2328, 2712, 2968, 3032
GD = NSA_G * NSA_DH

Z_GM = 0
Z_UA = 3072
Z_UB = 3584
Z_Q = 4096
Z_CKV = 4608
Z_KR = 4864
Z_CQ = 4992
Z_CMP = 5376
Z_GN = 5632
Z_COLS = 5760

VMEM_LIMIT = 56 * 1024 * 1024

TOK_TILE = 512
IN_PROJ_TM = 1024
IN_PROJ_TN = 1152
FFN_TM = 1024
FFN_TC = 256
MLA_TQ = 512
MLA_TK = 1024
MLA_HEADS_PER_STEP = 2

LOG2E = math.log2(math.e)


def _cparams(sem):
    return pltpu.CompilerParams(dimension_semantics=sem, vmem_limit_bytes=VMEM_LIMIT)


def _rms(x, g):
    return x * lax.rsqrt(jnp.mean(x * x, axis=-1, keepdims=True) + EPS) * g


def _dot(a, b):
    return jnp.dot(a, b, preferred_element_type=F32)


def _dot_nt(a, b):
    return lax.dot_general(a, b, (((1,), (1,)), ((), ())), preferred_element_type=F32)


def _norm_matmul_kernel(x_ref, g_ref, w_ref, o_ref, h_ref):
    @pl.when(pl.program_id(1) == 0)
    def _():
        h_ref[...] = _rms(x_ref[...], g_ref[...]).astype(BF16)

    o_ref[...] = _dot(h_ref[...], w_ref[...]).astype(o_ref.dtype)


def _norm_matmul(x, g, w, tm, tn, out_dtype):
    m, k = x.shape
    n = w.shape[1]
    return pl.pallas_call(
        _norm_matmul_kernel,
        out_shape=jax.ShapeDtypeStruct((m, n), out_dtype),
        grid=(m // tm, n // tn),
        in_specs=[pl.BlockSpec((tm, k), lambda i, j: (i, 0)),
                  pl.BlockSpec((1, k), lambda i, j: (0, 0)),
                  pl.BlockSpec((k, tn), lambda i, j: (0, j))],
        out_specs=pl.BlockSpec((tm, tn), lambda i, j: (i, j)),
        scratch_shapes=[pltpu.VMEM((tm, k), BF16)],
        compiler_params=_cparams(("parallel", "arbitrary")),
        name="norm_matmul",
    )(x, g, w)


def _in_proj_kernel(x_ref, g_ref, w_ref, z_ref, zc_ref, h_ref, *, cmp_tile, cmp_off):
    @pl.when(pl.program_id(1) == 0)
    def _():
        h_ref[...] = _rms(x_ref[...], g_ref[...]).astype(BF16)

    acc = _dot(h_ref[...], w_ref[...])
    z_ref[...] = acc.astype(BF16)

    @pl.when(pl.program_id(1) == cmp_tile)
    def _():
        zc_ref[...] = acc[:, cmp_off:cmp_off + 2 * GD]


def _in_proj(x, g, w):
    tm, tn = IN_PROJ_TM, IN_PROJ_TN
    m, k = x.shape
    n = w.shape[1]
    return pl.pallas_call(
        functools.partial(_in_proj_kernel, cmp_tile=Z_CMP // tn, cmp_off=Z_CMP % tn),
        out_shape=(jax.ShapeDtypeStruct((m, n), BF16), jax.ShapeDtypeStruct((m, 2 * GD), F32)),
        grid=(m // tm, n // tn),
        in_specs=[pl.BlockSpec((tm, k), lambda i, j: (i, 0)),
                  pl.BlockSpec((1, k), lambda i, j: (0, 0)),
                  pl.BlockSpec((k, tn), lambda i, j: (0, j))],
        out_specs=(pl.BlockSpec((tm, tn), lambda i, j: (i, j)),
                   pl.BlockSpec((tm, 2 * GD), lambda i, j: (i, 0))),
        scratch_shapes=[pltpu.VMEM((tm, k), BF16)],
        compiler_params=_cparams(("parallel", "arbitrary")),
        name="in_proj",
    )(x, g, w)


def _kv_proj_kernel(x_ref, g_ref, wk_ref, wvt_ref, k_ref, vt_ref):
    @pl.when(pl.program_id(1) == 0)
    def _():
        k_ref[...] = jnp.zeros(k_ref.shape, BF16)
        vt_ref[...] = jnp.zeros(vt_ref.shape, BF16)

    @pl.when(pl.program_id(1) > 0)
    def _():
        h = _rms(x_ref[...], g_ref[...]).astype(BF16)
        k = _dot(h, wk_ref[...])
        tm = k.shape[0]
        row = lax.broadcasted_iota(jnp.int32, k.shape, 0)
        lane = lax.broadcasted_iota(jnp.int32, k.shape, 1)
        blk = (pl.program_id(1) - 1) * (tm // SLC_BLOCK) + lax.shift_right_logical(row, 6)
        hot = (lane < NSA_G * GD) & (lax.bitwise_and(lane, GD - 1) == blk + NSA_DH)
        k_ref[0] = jnp.where(hot, 1.0, k).astype(BF16)
        vt_ref[0] = _dot_nt(wvt_ref[...], h).astype(BF16)


def _kv_proj(x, g, wk, wvt, bsz, seq):
    tm = WINDOW
    nst = seq // tm
    d = x.shape[1]
    nk = wk.shape[1]
    nv = wvt.shape[0]
    return pl.pallas_call(
        _kv_proj_kernel,
        out_shape=(jax.ShapeDtypeStruct((bsz, seq + tm, nk), BF16),
                   jax.ShapeDtypeStruct((bsz, nv, seq + tm), BF16)),
        grid=(bsz, nst + 1),
        in_specs=[pl.BlockSpec((tm, d), lambda b, s: (b * nst + jnp.maximum(s - 1, 0), 0)),
                  pl.BlockSpec((1, d), lambda b, s: (0, 0)),
                  pl.BlockSpec((d, nk), lambda b, s: (0, 0)),
                  pl.BlockSpec((nv, d), lambda b, s: (0, 0))],
        out_specs=(pl.BlockSpec((1, tm, nk), lambda b, s: (b, s, 0)),
                   pl.BlockSpec((1, nv, tm), lambda b, s: (b, 0, s))),
        compiler_params=_cparams(("parallel", "arbitrary")),
        name="nsa_kv_proj",
    )(x, g, wk, wvt)


CONV_HALO = 32


CONV_ROWS = 64


def _conv_kernel(a_ref, b_ref, w_ref, cb_ref, lg_ref, lb_ref, o_ref, buf_ref, sh_ref, *, ts):
    @pl.when(pl.program_id(1) == 0)
    def _():
        buf_ref[0:CONV_HALO, :] = jnp.zeros((CONV_HALO, CONV_CH), F32)

    buf_ref[CONV_HALO:CONV_HALO + ts, :] = a_ref[...].astype(F32) * jax.nn.sigmoid(b_ref[...].astype(F32))
    span = ts + CONV_HALO - SUBLANES
    for r in range(1, SUBLANES):
        sh_ref[r - 1, 0:span, :] = buf_ref[r:r + span, :]
    off = CONV_HALO - (CONV_WIDTH - 1)

    def rows(i, carry):
        r0 = pl.multiple_of(i * CONV_ROWS, CONV_ROWS)
        acc = jnp.zeros((CONV_ROWS, CONV_CH), F32) + cb_ref[...]
        for k in range(CONV_WIDTH):
            res, base = (off + k) % SUBLANES, (off + k) // SUBLANES * SUBLANES
            if res == 0:
                tap = buf_ref[pl.ds(r0 + base, CONV_ROWS), :]
            else:
                tap = sh_ref[res - 1, pl.ds(r0 + base, CONV_ROWS), :]
            acc = acc + tap * w_ref[k:k + 1, :]
        mu = jnp.mean(acc, axis=-1, keepdims=True)
        xc = acc - mu
        var = jnp.mean(xc * xc, axis=-1, keepdims=True)
        y = xc * lax.rsqrt(var + EPS) * lg_ref[...] + lb_ref[...]
        o_ref[pl.ds(r0, CONV_ROWS), :] = (y * jax.nn.sigmoid(y)).astype(BF16)
        return carry

    lax.fori_loop(0, ts // CONV_ROWS, rows, 0)
    buf_ref[0:CONV_HALO, :] = buf_ref[ts:ts + CONV_HALO, :]


def _conv_module(z, conv_w, conv_b, ln_g, ln_b, bsz, seq):
    ts = TOK_TILE
    nst = seq // ts
    wpad = jnp.zeros((32, CONV_CH), F32).at[:CONV_WIDTH].set(conv_w)
    return pl.pallas_call(
        functools.partial(_conv_kernel, ts=ts),
        out_shape=jax.ShapeDtypeStruct((bsz * seq, CONV_CH), BF16),
        grid=(bsz, nst),
        in_specs=[pl.BlockSpec((ts, CONV_CH), lambda b, s: (b * nst + s, Z_UA // CONV_CH)),
                  pl.BlockSpec((ts, CONV_CH), lambda b, s: (b * nst + s, Z_UB // CONV_CH)),
                  pl.BlockSpec((32, CONV_CH), lambda b, s: (0, 0)),
                  pl.BlockSpec((1, CONV_CH), lambda b, s: (0, 0)),
                  pl.BlockSpec((1, CONV_CH), lambda b, s: (0, 0)),
                  pl.BlockSpec((1, CONV_CH), lambda b, s: (0, 0))],
        out_specs=pl.BlockSpec((ts, CONV_CH), lambda b, s: (b * nst + s, 0)),
        scratch_shapes=[pltpu.VMEM((ts + CONV_HALO, CONV_CH), F32),
                        pltpu.VMEM((SUBLANES - 1, ts + CONV_HALO - SUBLANES, CONV_CH), F32)],
        compiler_params=_cparams(("arbitrary", "arbitrary")),
        name="conv_module",
    )(z, z, wpad, conv_b[None], ln_g[None], ln_b[None])


def _compress_kernel(xk_ref, xv_ref, pos_ref, w1_ref, b1_ref, w2k_ref, w2v_ref, kc_ref, vct_ref, *, nch):
    for kind, (x_ref, w2_ref) in enumerate(((xk_ref, w2k_ref), (xv_ref, w2v_ref))):
        a = jnp.zeros((nch, NSA_G * CMP_HIDDEN), F32)
        b = jnp.zeros((nch, NSA_G * CMP_HIDDEN), F32)
        for l in range(CMP_STRIDE):
            xs = x_ref[pl.ds(l, nch, stride=CMP_STRIDE), :]
            a = a + _dot((xs + pos_ref[kind, l:l + 1, :]).astype(BF16), w1_ref[kind, l])
            b = b + _dot((xs + pos_ref[kind, CMP_STRIDE + l:CMP_STRIDE + l + 1, :]).astype(BF16),
                         w1_ref[kind, CMP_STRIDE + l])
        pre = a + pltpu.roll(b, nch - 1, 0) + b1_ref[kind]
        out = _dot(jax.nn.gelu(pre).astype(BF16), w2_ref[...])
        if kind == 0:
            kc_ref[0] = out.astype(BF16)
        else:
            vct_ref[0] = out.T.astype(BF16)


def _blockdiag2(w):
    z = jnp.zeros_like(w)
    return jnp.concatenate([jnp.concatenate([w, z], axis=-1), jnp.concatenate([z, w], axis=-1)], axis=-2)


def _compress(z, pos, w1, b1, w2, bsz, seq):
    nch = seq // CMP_STRIDE
    pos2 = jnp.concatenate([pos, pos], axis=-1)
    w1e = _blockdiag2(w1.reshape(2, CMP_BLOCK, NSA_DH, CMP_HIDDEN)).astype(BF16)
    b1e = jnp.concatenate([b1, b1], axis=-1)[:, None]
    w2k = _blockdiag2(w2[0]).astype(BF16)
    w2v = _blockdiag2(w2[1]).astype(BF16)
    full = lambda a: pl.BlockSpec(a.shape, lambda b: (0,) * a.ndim)
    return pl.pallas_call(
        functools.partial(_compress_kernel, nch=nch),
        out_shape=(jax.ShapeDtypeStruct((bsz, nch, GD), BF16),
                   jax.ShapeDtypeStruct((bsz, GD, nch), BF16)),
        grid=(bsz,),
        in_specs=[pl.BlockSpec((seq, GD), lambda b: (b, 0)),
                  pl.BlockSpec((seq, GD), lambda b: (b, 1)),
                  full(pos2), full(w1e), full(b1e), full(w2k), full(w2v)],
        out_specs=(pl.BlockSpec((1, nch, GD), lambda b: (b, 0, 0)),
                   pl.BlockSpec((1, GD, nch), lambda b: (b, 0, 0))),
        compiler_params=_cparams(("parallel",)),
        name="nsa_compress",
    )(z, z, pos2, w1e, b1e, w2k, w2v)


NSA_QP = 2 * NSA_QB
NEAR_KEYS = 256
WIN_KEYS = 640
CMP_TAB_ROWS = 512
CMP_TAB_ZERO = 256


def _t5_bucket_np(d):
    exact = REL_BUCKETS // 2
    d = np.maximum(d, 0)
    ratio = np.log(np.maximum(d, 1).astype(np.float32) / np.float32(exact)) / np.float32(math.log(REL_MAX_DIST / exact))
    large = np.minimum(exact + (ratio * (REL_BUCKETS - exact)).astype(np.int32), REL_BUCKETS - 1)
    return np.where(d < exact, d, large).astype(np.int32)


def _bucket_thresholds():
    exact = REL_BUCKETS // 2
    bk = _t5_bucket_np(np.arange(4 * REL_MAX_DIST))
    assert np.all(np.diff(bk) >= 0) and bk[-1] == REL_BUCKETS - 1
    return [int(np.argmax(bk >= k)) for k in range(exact + 1, REL_BUCKETS)]


def _bias_rows(rel_ref, dist, valid, shift):
    exact = REL_BUCKETS // 2
    bucket = jnp.full(dist.shape, exact, jnp.int32)
    for thr in _bucket_thresholds():
        bucket = bucket + jnp.where(dist >= thr, 1, 0)
    bucket = jnp.where(dist < exact, dist, bucket)
    val = jnp.zeros(dist.shape, F32)
    for bkt in range(REL_BUCKETS):
        val = jnp.where(bucket == bkt, rel_ref[0, bkt:bkt + 1, :], val)
    if shift:
        val = val - rel_ref[0, REL_BUCKETS - 1:REL_BUCKETS, :]
    return jnp.where(valid, val * LOG2E, NEG_INF)


def _nsa_bias_kernel(rel_ref, tc_ref, tn_ref, tw_ref):
    hq = NSA_HG * NSA_QP
    rows = 128

    def dist_of(nrows, r0, fn):
        r = r0 + lax.broadcasted_iota(jnp.int32, (nrows, hq), 0)
        t = lax.bitwise_and(lax.broadcasted_iota(jnp.int32, (nrows, hq), 1), NSA_QP - 1)
        return fn(r, t)

    for r0 in range(0, CMP_TAB_ROWS, rows):
        d = dist_of(rows, r0, lambda r, t: t - CMP_STRIDE * (r - CMP_TAB_ZERO) - (CMP_BLOCK - 1))
        tc_ref[0, r0:r0 + rows, :] = _bias_rows(rel_ref, d, d >= 0, False)
    for r0 in range(0, NEAR_KEYS, rows):
        d = dist_of(rows, r0, lambda r, t: NEAR_KEYS // 2 + t - r)
        tn_ref[0, r0:r0 + rows, :] = _bias_rows(rel_ref, d, d >= 0, True)
    for r0 in range(0, WIN_KEYS, rows):
        d = dist_of(rows, r0, lambda r, t: WINDOW + t - r)
        tw_ref[0, r0:r0 + rows, :] = _bias_rows(rel_ref, d, (d >= 0) & (d < WINDOW), False)


def _nsa_tables(rel_bias):
    hq = NSA_HG * NSA_QP
    rel4 = jnp.repeat(rel_bias.reshape(REL_BUCKETS, NSA_G, NSA_HG).transpose(1, 0, 2), NSA_QP, axis=-1)
    spec = lambda r: pl.BlockSpec((1, r, hq), lambda g: (g, 0, 0))
    return pl.pallas_call(
        _nsa_bias_kernel,
        out_shape=(jax.ShapeDtypeStruct((NSA_G, CMP_TAB_ROWS, hq), F32),
                   jax.ShapeDtypeStruct((NSA_G, NEAR_KEYS, hq), F32),
                   jax.ShapeDtypeStruct((NSA_G, WIN_KEYS, hq), F32)),
        grid=(NSA_G,),
        in_specs=[spec(REL_BUCKETS)],
        out_specs=(spec(CMP_TAB_ROWS), spec(NEAR_KEYS), spec(WIN_KEYS)),
        compiler_params=_cparams(("parallel",)),
        name="nsa_bias_tables",
    )(rel4)


SEL_PAD = 8
FAR_KEYS = 1024
KV_FRONT = WINDOW
KREP = NSA_HG * NSA_DH


def _softmax_cols(s):
    m = jnp.max(s, axis=0, keepdims=True)
    p = jnp.exp2(s - m)
    return m, p, jnp.sum(p, axis=0, keepdims=True)


def _mask_blocks(s, mask_ref, row0, nblk):
    parts = []
    for jj in range(nblk):
        row = mask_ref[pl.ds(row0 + jj, 1), :]
        parts.append(jnp.where(row > 0.0, s[SLC_BLOCK * jj:SLC_BLOCK * (jj + 1)], NEG_INF))
    return jnp.concatenate(parts, axis=0)


def _rank_select(score_ref, n_sb, n_sel):
    groups = n_sb // SUBLANES
    sub = lax.broadcasted_iota(jnp.int32, (SUBLANES, NSA_QP), 0)
    tiles = [score_ref[SUBLANES * v:SUBLANES * (v + 1), :] for v in range(groups)]
    cnts = [jnp.zeros((SUBLANES, NSA_QP), F32) for _ in range(groups)]
    for jp in range(n_sb):
        row = score_ref[jp:jp + 1, :]
        for v in range(groups):
            lo = SUBLANES * v
            if jp < lo:
                beats = row >= tiles[v]
            elif jp >= lo + SUBLANES - 1:
                beats = row > tiles[v]
            else:
                beats = (row > tiles[v]) | ((row == tiles[v]) & (sub > jp - lo))
            cnts[v] = cnts[v] + jnp.where(beats, 1.0, 0.0)
    cnt = jnp.concatenate(cnts, axis=0)
    return jnp.where(cnt < float(n_sel), 1.0, 0.0)


def _nsa_kernel(q_ref, gate_ref, kc_ref, vct_ref, ks0_ref, ks1_ref, kw_ref, vst_ref, vwt_ref,
                tc_ref, tn_ref, tw_ref, cov_ref, rep_ref, o_ref,
                sel_ref, score_ref, *, n_sb):
    p2 = pl.program_id(1)
    hq = NSA_HG * NSA_QP
    nch = kc_ref.shape[1]
    groups = range(NSA_G)
    qcol = lambda g: slice(KREP * g, KREP * (g + 1))
    vrow = lambda g: slice(NSA_DH * g, NSA_DH * (g + 1))

    ks_refs = (ks0_ref, ks1_ref)
    qs, q64s = [], []
    zq = jnp.zeros((hq, NSA_DH), BF16)
    for g in groups:
        qb = (q_ref[:, qcol(g)].astype(F32) * (NSA_DH ** -0.5 * LOG2E)).astype(BF16)
        q64 = jnp.concatenate([qb[:, NSA_DH * h:NSA_DH * (h + 1)] for h in range(NSA_HG)], axis=0)
        q64s.append(q64)
        qs.append(jnp.concatenate([q64, zq] if g == 0 else [zq, q64], axis=1))

    start_c = pl.multiple_of(CMP_TAB_ZERO - (NSA_QP // CMP_STRIDE) * p2, SUBLANES)
    lane = lax.broadcasted_iota(jnp.int32, (1, hq), 1)
    tq = NSA_QP * p2 + lax.bitwise_and(lane, NSA_QP - 1)
    anyv = jnp.where(tq >= CMP_BLOCK - 1, 1.0, 0.0)
    jrow = lax.broadcasted_iota(jnp.int32, (n_sb, NSA_QP), 0)
    tok = lax.broadcasted_iota(jnp.int32, (n_sb, NSA_QP), 1)
    cur = 2 * p2 + lax.shift_right_logical(tok, 6)
    forced = (jrow == 0) | (jrow == cur) | (jrow == cur - 1)
    o_cmp = []
    for g in groups:
        sc = _dot_nt(kc_ref[0], qs[g]) + tc_ref[g, pl.ds(start_c, nch), :]
        _, pc, lc = _softmax_cols(sc)
        pc = pc * (anyv / lc)
        o_cmp.append(_dot(vct_ref[0, vrow(g), :], pc.astype(BF16)))
        psum = pc[:, 0:NSA_QP]
        for h in range(1, NSA_HG):
            psum = psum + pc[:, NSA_QP * h:NSA_QP * (h + 1)]
        p_hi = psum.astype(BF16)
        p_lo = (psum - p_hi.astype(F32)).astype(BF16)
        imp = _dot(cov_ref[...], p_hi) + _dot(cov_ref[...], p_lo)
        score_ref[g] = jnp.where(forced, FORCE_SCORE, jnp.where(jrow <= cur, imp, -1.0))

    zeros8 = jnp.zeros((SEL_PAD, hq), F32)
    q_far = []
    for g in groups:
        sel = _rank_select(score_ref.at[g], n_sb, min(N_SELECT, n_sb))
        sel_ref[g, 0:SEL_PAD, :] = zeros8
        sel_ref[g, SEL_PAD + n_sb:2 * SEL_PAD + n_sb, :] = zeros8
        sel_ref[g, SEL_PAD:SEL_PAD + n_sb, :] = _dot(sel.astype(BF16), rep_ref[...])
        far_sel = jnp.where(jrow < 2 * p2 - 2, sel, 0.0)
        mb = ((far_sel - 1.0) * -NEG_INF).T.astype(BF16)
        if n_sb < NSA_DH:
            mb = jnp.concatenate([mb, jnp.zeros((NSA_QP, NSA_DH - n_sb), BF16)], axis=1)
        q_far.append(jnp.concatenate([q64s[g], jnp.concatenate([mb] * NSA_HG, axis=0)], axis=1))

    win0 = pl.multiple_of(NSA_QP * p2, LANES)
    near0 = pl.multiple_of(win0 + KV_FRONT - NEAR_KEYS // 2, LANES)
    state = []
    for g in groups:
        s = _dot_nt(ks_refs[g][0, pl.ds(near0, NEAR_KEYS), :], jnp.concatenate([q64s[g], zq], axis=1)) + tn_ref[g]
        s = _mask_blocks(s, sel_ref.at[g], 2 * p2 - 2 + SEL_PAD, NEAR_KEYS // SLC_BLOCK)
        m_s, p_s, l_s = _softmax_cols(s)
        state += [m_s, l_s, _dot(vst_ref[0, vrow(g), pl.ds(near0, NEAR_KEYS)], p_s.astype(BF16))]

    window = []
    for g in groups:
        sw = _dot_nt(kw_ref[0, pl.ds(win0, WIN_KEYS), :], qs[g]) + tw_ref[g]
        slabs = [sw[NSA_QP * j:NSA_QP * (j + 1)] for j in range(WIN_KEYS // NSA_QP)]
        for j in range(KV_FRONT // NSA_QP):
            slabs[j] = jnp.where(NSA_QP * j + win0 >= KV_FRONT, slabs[j], NEG_INF)
        _, p_w, l_w = _softmax_cols(jnp.concatenate(slabs, axis=0))
        window.append((_dot(vwt_ref[0, vrow(g), pl.ds(win0, WIN_KEYS)], p_w.astype(BF16)), l_w))

    def far_body(c, carry):
        k0 = pl.multiple_of(FAR_KEYS * c + KV_FRONT, LANES)
        sfs = [_dot_nt(ks_refs[g][0, pl.ds(k0, FAR_KEYS), :], q_far[g]) for g in groups]
        out = []
        for g in groups:
            m_old, l_old, acc_old = carry[3 * g:3 * g + 3]
            sf = sfs[g]
            m_new = jnp.maximum(m_old, jnp.max(sf, axis=0, keepdims=True))
            alpha = jnp.exp2(m_old - m_new)
            pf = jnp.exp2(sf - m_new)
            l_new = alpha * l_old + jnp.sum(pf, axis=0, keepdims=True)
            acc_new = alpha * acc_old + _dot(vst_ref[0, vrow(g), pl.ds(k0, FAR_KEYS)], pf.astype(BF16))
            out += [m_new, l_new, acc_new]
        return tuple(out)

    n_far = lax.div(jnp.maximum(p2 - 1, 0) * NSA_QP + FAR_KEYS - 1, FAR_KEYS)
    state = lax.fori_loop(0, n_far, far_body, tuple(state))

    gt = gate_ref[...].astype(F32).T
    r = lax.broadcasted_iota(jnp.int32, (NSA_QP, NSA_QP), 0)
    c = lax.broadcasted_iota(jnp.int32, (NSA_QP, NSA_QP), 1)
    eye = jnp.where(r == c, 1.0, 0.0).astype(BF16)
    for g in groups:
        _, l_s, acc_s = state[3 * g:3 * g + 3]
        acc_w, l_w = window[g]
        gsel = jax.nn.sigmoid(gt[3 * NSA_HG * g:3 * NSA_HG * (g + 1)])
        gate = lambda b: jnp.concatenate([gsel[3 * h + b:3 * h + b + 1] for h in range(NSA_HG)], axis=1)
        out_t = (gate(0) * o_cmp[g] + (gate(1) / l_s) * acc_s + (gate(2) / l_w) * acc_w).astype(BF16)
        stacked = jnp.concatenate([out_t[:, NSA_QP * h:NSA_QP * (h + 1)] for h in range(NSA_HG)], axis=0)
        o_ref[:, qcol(g)] = _dot_nt(eye, stacked).astype(BF16)


def _nsa_attention(z, kc, vct, kk, vvt, tables, bsz, seq):
    g, hg, dh, qp = NSA_G, NSA_HG, NSA_DH, NSA_QP
    nstep = seq // qp
    n_sb = seq // SLC_BLOCK
    nch = kc.shape[1]
    hq = hg * qp
    sp = kk.shape[1]
    tc, tn, tw = tables
    c_start = CMP_STRIDE * np.arange(nch)
    s_start = SLC_BLOCK * np.arange(n_sb)
    cover_t = ((c_start[None, :] < s_start[:, None] + SLC_BLOCK)
               & (c_start[None, :] + CMP_BLOCK > s_start[:, None])
               & (np.arange(nch)[None, :] < (seq - CMP_BLOCK) // CMP_STRIDE + 1))
    cover_t = jnp.asarray(cover_t.astype(np.float32), BF16)
    rep = jnp.asarray(np.tile(np.eye(qp, dtype=np.float32), (1, hg)), BF16)
    full = lambda a: pl.BlockSpec(a.shape, lambda b, i: (0,) * a.ndim, pipeline_mode=pl.Buffered(1))
    qd = g * hg * dh
    return pl.pallas_call(
        functools.partial(_nsa_kernel, n_sb=n_sb),
        out_shape=jax.ShapeDtypeStruct((bsz * seq, qd), BF16),
        grid=(bsz, nstep),
        in_specs=[pl.BlockSpec((qp, qd), lambda b, i: (b * nstep + i, Z_Q // qd)),
                  pl.BlockSpec((qp, LANES), lambda b, i: (b * nstep + i, Z_GN // LANES)),
                  pl.BlockSpec((1, nch, GD), lambda b, i: (b, 0, 0)),
                  pl.BlockSpec((1, g * dh, nch), lambda b, i: (b, 0, 0)),
                  pl.BlockSpec((1, sp, GD), lambda b, i: (b, 0, 0)),
                  pl.BlockSpec((1, sp, GD), lambda b, i: (b, 0, 1)),
                  pl.BlockSpec((1, sp, GD), lambda b, i: (b, 0, 2)),
                  pl.BlockSpec((1, g * dh, sp), lambda b, i: (b, 0, 0)),
                  pl.BlockSpec((1, g * dh, sp), lambda b, i: (b, 1, 0)),
                  full(tc), full(tn), full(tw), full(cover_t), full(rep)],
        out_specs=pl.BlockSpec((qp, qd), lambda b, i: (b * nstep + i, 0)),
        scratch_shapes=[pltpu.VMEM((g, n_sb + 2 * SEL_PAD, hq), F32),
                        pltpu.VMEM((g, n_sb, qp), F32)],
        compiler_params=_cparams(("parallel", "arbitrary")),
        name="nsa_attention",
    )(z, z, kc, vct, kk, kk, kk, vvt, vvt, tc, tn, tw, cover_t, rep)


def _rope_table_kernel(pos_ref, inv_ref, o_ref):
    ang = inv_ref[...] * pos_ref[0].astype(F32)
    c, s = jnp.cos(ang), jnp.sin(ang)
    o_ref[...] = jnp.concatenate([c, c, -s, s], axis=0).T


def _rope_table(positions):
    tm = TOK_TILE
    t = positions.size
    half = QK_ROPE // 2
    inv = (ROPE_THETA ** (-jnp.arange(half, dtype=F32) / half))[:, None]
    return pl.pallas_call(
        _rope_table_kernel,
        out_shape=jax.ShapeDtypeStruct((t, 2 * QK_ROPE), F32),
        grid=(t // tm,),
        in_specs=[pl.BlockSpec((1, 1, tm), lambda i: (i, 0, 0)),
                  pl.BlockSpec((half, 1), lambda i: (0, 0))],
        out_specs=pl.BlockSpec((tm, 2 * QK_ROPE), lambda i: (i, 0)),
        compiler_params=_cparams(("parallel",)),
        name="rope_table",
    )(positions.reshape(t // tm, 1, tm), inv)


MLA_HW = 256


def _mla_proj_kernel(cq_ref, ckv_ref, kr_ref, rope_ref, nq_ref, nkv_ref, wq_ref, wkn_ref, wvt_ref,
                     q_ref, k_ref, vt_ref):
    scale = (QK_NOPE + QK_ROPE) ** -0.5 * LOG2E
    rope = rope_ref[...]
    yq = _dot(_rms(cq_ref[...].astype(F32), nq_ref[...]).astype(BF16), wq_ref[...])
    ckv = _rms(ckv_ref[...].astype(F32), nkv_ref[...]).astype(BF16)
    ykn = _dot(ckv, wkn_ref[...])
    vt_ref[0] = _dot_nt(wvt_ref[...], ckv).astype(BF16)
    kp = kr_ref[...].astype(F32) * rope
    kp = kp + pltpu.roll(kp, QK_ROPE, 1)
    lane = lax.broadcasted_iota(jnp.int32, kp.shape, 1)
    kp = jnp.where(lane < QK_ROPE, kp, 0.0).astype(BF16)
    for h in range(MLA_HEADS):
        base = MLA_HW * h
        q_ref[:, base:base + QK_NOPE] = (yq[:, base:base + QK_NOPE] * scale).astype(BF16)
        qp = yq[:, base + QK_NOPE:base + MLA_HW] * rope
        qp = qp + pltpu.roll(qp, QK_ROPE, 1)
        q_ref[:, base + QK_NOPE:base + MLA_HW] = (qp * scale).astype(BF16)
        k_ref[:, base:base + QK_NOPE] = ykn[:, QK_NOPE * h:QK_NOPE * (h + 1)].astype(BF16)
        k_ref[:, base + QK_NOPE:base + MLA_HW] = kp


def _swap_halves(w):
    half = QK_ROPE // 2
    return jnp.concatenate([w[..., half:], w[..., :half]], axis=-1)


def _mla_proj(z, rope_tab, norm_q, norm_kv, w_uq, w_ukv, bsz, seq):
    tm = TOK_TILE
    t = z.shape[0]
    nst = seq // tm
    wq = w_uq.reshape(Q_RANK, MLA_HEADS, QK_NOPE + QK_ROPE)
    wq = jnp.concatenate([wq, _swap_halves(wq[..., QK_NOPE:])], axis=-1)
    wq = wq.reshape(Q_RANK, MLA_HEADS * MLA_HW).astype(BF16)
    wkv = w_ukv.reshape(KV_RANK, MLA_HEADS, QK_NOPE + V_DIM)
    wkn = wkv[..., :QK_NOPE].reshape(KV_RANK, MLA_HEADS * QK_NOPE).astype(BF16)
    wvt = wkv[..., QK_NOPE:].reshape(KV_RANK, MLA_HEADS * V_DIM).T.astype(BF16)
    hw = MLA_HEADS * MLA_HW
    hv = MLA_HEADS * V_DIM
    row = lambda b, s: b * nst + s
    return pl.pallas_call(
        _mla_proj_kernel,
        out_shape=(jax.ShapeDtypeStruct((t, hw), BF16),
                   jax.ShapeDtypeStruct((t, hw), BF16),
                   jax.ShapeDtypeStruct((bsz, hv, seq), BF16)),
        grid=(bsz, nst),
        in_specs=[pl.BlockSpec((tm, Q_RANK), lambda b, s: (row(b, s), Z_CQ // Q_RANK)),
                  pl.BlockSpec((tm, KV_RANK), lambda b, s: (row(b, s), Z_CKV // KV_RANK)),
                  pl.BlockSpec((tm, 2 * QK_ROPE), lambda b, s: (row(b, s), Z_KR // (2 * QK_ROPE))),
                  pl.BlockSpec((tm, 2 * QK_ROPE), lambda b, s: (row(b, s), 0)),
                  pl.BlockSpec((1, Q_RANK), lambda b, s: (0, 0)),
                  pl.BlockSpec((1, KV_RANK), lambda b, s: (0, 0)),
                  pl.BlockSpec((Q_RANK, hw), lambda b, s: (0, 0)),
                  pl.BlockSpec((KV_RANK, hv), lambda b, s: (0, 0)),
                  pl.BlockSpec((hv, KV_RANK), lambda b, s: (0, 0))],
        out_specs=(pl.BlockSpec((tm, hw), lambda b, s: (row(b, s), 0)),
                   pl.BlockSpec((tm, hw), lambda b, s: (row(b, s), 0)),
                   pl.BlockSpec((1, hv, tm), lambda b, s: (b, 0, s))),
        compiler_params=_cparams(("parallel", "parallel")),
        name="mla_proj",
    )(z, z, z, rope_tab, norm_q[None], norm_kv[None], wq, wkn, wvt)


def _mla_attn_kernel(q_ref, k_ref, vt_ref, o_ref, *, tq, tk, nh):
    iq = pl.program_id(2)
    cd = lax.div(iq * tq, tk)
    heads = range(nh)
    hcol = lambda h: slice(MLA_HW * h, MLA_HW * (h + 1))
    vrow = lambda h: slice(V_DIM * h, V_DIM * (h + 1))
    qs = [q_ref[:, hcol(h)] for h in heads]

    def scores(c, h):
        k0 = pl.multiple_of(c * tk, tk)
        return _dot_nt(k_ref[0, pl.ds(k0, tk), hcol(h)], qs[h])

    def diagonal(nk):
        k0 = pl.multiple_of((iq + 1) * tq - nk, tq)
        kpos = k0 + lax.broadcasted_iota(jnp.int32, (nk, tq), 0)
        qpos = iq * tq + lax.broadcasted_iota(jnp.int32, (nk, tq), 1)
        st = []
        for h in heads:
            s = jnp.where(kpos <= qpos, _dot_nt(k_ref[0, pl.ds(k0, nk), hcol(h)], qs[h]), NEG_INF)
            m0, p0, l0 = _softmax_cols(s)
            st += [m0, l0, _dot(vt_ref[0, vrow(h), pl.ds(k0, nk)], p0.astype(BF16))]
        return tuple(st)

    assert tk == 2 * tq
    state = lax.cond(lax.rem(iq, 2) == 0, lambda: diagonal(tq), lambda: diagonal(tk))

    def body(c, carry):
        k0 = pl.multiple_of(c * tk, tk)
        ss = [scores(c, h) for h in heads]
        out = []
        for h in heads:
            m_old, l_old, acc_old = carry[3 * h:3 * h + 3]
            m_new = jnp.maximum(m_old, jnp.max(ss[h], axis=0, keepdims=True))
            alpha = jnp.exp2(m_old - m_new)
            p = jnp.exp2(ss[h] - m_new)
            l_new = alpha * l_old + jnp.sum(p, axis=0, keepdims=True)
            acc_new = alpha * acc_old + _dot(vt_ref[0, vrow(h), pl.ds(k0, tk)], p.astype(BF16))
            out += [m_new, l_new, acc_new]
        return tuple(out)

    state = lax.fori_loop(0, cd, body, tuple(state))
    for h in heads:
        _, l, acc = state[3 * h:3 * h + 3]
        o_ref[:, vrow(h)] = (acc / l).T.astype(BF16)


def _mla_attention(qf, kf, vt, bsz, seq):
    tq, tk, nh = MLA_TQ, MLA_TK, MLA_HEADS_PER_STEP
    h = MLA_HEADS
    nq = seq // tq
    k3 = kf.reshape(bsz, seq, h * MLA_HW)
    return pl.pallas_call(
        functools.partial(_mla_attn_kernel, tq=tq, tk=tk, nh=nh),
        out_shape=jax.ShapeDtypeStruct((bsz * seq, h * V_DIM), BF16),
        grid=(bsz, h // nh, nq),
        in_specs=[pl.BlockSpec((tq, nh * MLA_HW), lambda b, hh, i: (b * nq + i, hh)),
                  pl.BlockSpec((1, seq, nh * MLA_HW), lambda b, hh, i: (b, 0, hh)),
                  pl.BlockSpec((1, nh * V_DIM, seq), lambda b, hh, i: (b, hh, 0))],
        out_specs=pl.BlockSpec((tq, nh * V_DIM), lambda b, hh, i: (b * nq + i, hh)),
        compiler_params=_cparams(("parallel", "parallel", "arbitrary")),
        name="mla_attention",
    )(qf, k3, vt)


def _merge_xattn_kernel(ya_ref, yb_ref, yc_ref, ga_ref, gb_ref, gc_ref, x_ref,
                        wa_ref, wb_ref, wc_ref, wo_ref,
                        gx_ref, wq_ref, kv_ref, wxo_ref, o_ref):
    sig = lambda ref: jax.nn.sigmoid(ref[...].astype(F32))
    y = (sig(ga_ref) * _dot(ya_ref[...], wa_ref[...])
         + sig(gb_ref) * _dot(yb_ref[...], wb_ref[...])
         + sig(gc_ref) * _dot(yc_ref[...], wc_ref[...]))
    x = x_ref[...] + _dot(y.astype(BF16), wo_ref[...])
    h = _rms(x, gx_ref[...]).astype(BF16)
    q = _dot(h, wq_ref[...]) * XATTN_DH ** -0.5
    hd = XATTN_HEADS * XATTN_DH
    outs = []
    for hh in range(XATTN_HEADS):
        qh = q[:, XATTN_DH * hh:XATTN_DH * (hh + 1)].astype(BF16)
        kh = kv_ref[0, :, XATTN_DH * hh:XATTN_DH * (hh + 1)]
        vh = kv_ref[0, :, hd + XATTN_DH * hh:hd + XATTN_DH * (hh + 1)]
        s = _dot_nt(qh, kh)
        m = jnp.max(s, axis=-1, keepdims=True)
        p = jnp.exp(s - m)
        p = p / jnp.sum(p, axis=-1, keepdims=True)
        outs.append(_dot(p.astype(BF16), vh))
    o = jnp.concatenate(outs, axis=-1).astype(BF16)
    o_ref[...] = x + _dot(o, wxo_ref[...])


def _merge_xattn(ya, yb, yc, z, x, wa, wb, wc, wo, gx, wq, kv, wxo, bsz, seq):
    tm = TOK_TILE
    t, d = x.shape
    nst = seq // tm
    m_len = kv.shape[1]
    hd = XATTN_HEADS * XATTN_DH
    row = lambda b, s: b * nst + s
    act = pl.BlockSpec((tm, ya.shape[1]), lambda b, s: (row(b, s), 0))
    gate = lambda k: pl.BlockSpec((tm, d), lambda b, s: (row(b, s), Z_GM // d + k))
    const = lambda shape: pl.BlockSpec(shape, lambda b, s: (0, 0))
    bf = lambda w: w.astype(BF16)
    return pl.pallas_call(
        _merge_xattn_kernel,
        out_shape=jax.ShapeDtypeStruct((t, d), F32),
        grid=(bsz, nst),
        in_specs=[act, act, act, gate(0), gate(1), gate(2),
                  pl.BlockSpec((tm, d), lambda b, s: (row(b, s), 0)),
                  const((ya.shape[1], d)), const((ya.shape[1], d)), const((ya.shape[1], d)), const((d, d)),
                  const((1, d)), const((d, hd)),
                  pl.BlockSpec((1, m_len, 2 * hd), lambda b, s: (b, 0, 0)),
                  const((hd, d))],
        out_specs=pl.BlockSpec((tm, d), lambda b, s: (row(b, s), 0)),
        compiler_params=_cparams(("parallel", "parallel")),
        name="merge_xattn",
    )(ya, yb, yc, z, z, z, x, bf(wa), bf(wb), bf(wc), bf(wo), gx[None], bf(wq), kv, bf(wxo))


def _ffn_kernel(x_ref, g_ref, wg_ref, wu_ref, wd_ref, gf_ref, o_ref, h_ref, acc_ref, *, final):
    c = pl.program_id(1)

    @pl.when(c == 0)
    def _():
        h_ref[...] = _rms(x_ref[...], g_ref[...]).astype(BF16)
        acc_ref[...] = x_ref[...]

    h = h_ref[...]
    gate = _dot(h, wg_ref[...])
    up = _dot(h, wu_ref[...])
    act = (gate * jax.nn.sigmoid(gate) * up).astype(BF16)
    acc_ref[...] += _dot(act, wd_ref[...])

    @pl.when(c == pl.num_programs(1) - 1)
    def _():
        y = acc_ref[...]
        o_ref[...] = _rms(y, gf_ref[...]) if final else y


def _ffn(x, g, w_gate_up, w_down, g_final, final):
    tm, tc = FFN_TM, FFN_TC
    t, d = x.shape
    nc = FFN_HIDDEN // tc
    wgu = w_gate_up.astype(BF16)
    return pl.pallas_call(
        functools.partial(_ffn_kernel, final=final),
        out_shape=jax.ShapeDtypeStruct((t, d), F32),
        grid=(t // tm, nc),
        in_specs=[pl.BlockSpec((tm, d), lambda i, c: (i, 0)),
                  pl.BlockSpec((1, d), lambda i, c: (0, 0)),
                  pl.BlockSpec((d, tc), lambda i, c: (0, c)),
                  pl.BlockSpec((d, tc), lambda i, c: (0, nc + c)),
                  pl.BlockSpec((tc, d), lambda i, c: (c, 0)),
                  pl.BlockSpec((1, d), lambda i, c: (0, 0))],
        out_specs=pl.BlockSpec((tm, d), lambda i, c: (i, 0)),
        scratch_shapes=[pltpu.VMEM((tm, d), BF16), pltpu.VMEM((tm, d), F32)],
        compiler_params=_cparams(("parallel", "arbitrary")),
        name="ffn",
    )(x, g[None], wgu, wgu, w_down.astype(BF16), g_final[None])


def _split_w_in(w):
    k_rope = w[:, O_KR:O_KR + QK_ROPE]
    kv = lambda kind: w[:, O_KV + GD * kind:O_KV + GD * (kind + 1)]
    pad = jnp.zeros((w.shape[0], Z_COLS - Z_GN - 3 * NSA_HEADS), w.dtype)
    wz = jnp.concatenate([
        w[:, O_GM:O_GM + 3 * D_MODEL],
        w[:, O_GLU:O_GLU + 2 * CONV_CH],
        w[:, O_Q:O_Q + NSA_HEADS * NSA_DH],
        w[:, O_CKV:O_CKV + KV_RANK],
        k_rope, _swap_halves(k_rope),
        w[:, O_CQ:O_CQ + Q_RANK],
        kv(0), kv(1),
        w[:, O_GN:O_GN + 3 * NSA_HEADS], pad], axis=1).astype(BF16)
    z64 = jnp.zeros((w.shape[0], NSA_DH), w.dtype)
    wk = jnp.concatenate([kv(2)[:, :NSA_DH], z64, kv(2)[:, NSA_DH:], z64, kv(4)], axis=1).astype(BF16)
    wvt = jnp.concatenate([kv(3), kv(5)], axis=1).T.astype(BF16)
    return wz, wk, wvt


def kernel(x, mem, positions, rel_bias, norm_mix, norm_xattn, norm_mem, norm_ffn, norm_final, w_in, conv_w, conv_b, conv_ln_g, conv_ln_b, w_branch_conv, cmp_pos_k, cmp_w1_k, cmp_b1_k, cmp_w2_k, cmp_pos_v, cmp_w1_v, cmp_b1_v, cmp_w2_v, w_branch_nsa, mla_norm_q, mla_norm_kv, w_uq, w_ukv, w_branch_mla, w_out, w_xq, w_xkv, w_xo, w_gate_up, w_down):
    bsz, seq, d = x.shape
    depth = w_in.shape[0]
    t = bsz * seq
    m_len = mem.shape[1]
    xt = x.reshape(t, d)
    memt = mem.reshape(bsz * m_len, d)
    rope_tab = _rope_table(positions)
    tables = _nsa_tables(rel_bias)
    for l in range(depth):
        wz, wk, wvt = _split_w_in(w_in[l])
        z, zc = _in_proj(xt, norm_mix[l][None], wz)
        kk, vvt = _kv_proj(xt, norm_mix[l][None], wk, wvt, bsz, seq)
        ya = _conv_module(z, conv_w[l], conv_b[l], conv_ln_g[l], conv_ln_b[l], bsz, seq)
        kc, vct = _compress(zc, jnp.stack([cmp_pos_k[l], cmp_pos_v[l]]), jnp.stack([cmp_w1_k[l], cmp_w1_v[l]]),
                            jnp.stack([cmp_b1_k[l], cmp_b1_v[l]]), jnp.stack([cmp_w2_k[l], cmp_w2_v[l]]), bsz, seq)
        yb = _nsa_attention(z, kc, vct, kk, vvt, tables, bsz, seq)
        qf, kf, vt = _mla_proj(z, rope_tab, mla_norm_q[l], mla_norm_kv[l], w_uq[l], w_ukv[l], bsz, seq)
        yc = _mla_attention(qf, kf, vt, bsz, seq)
        mem_kv = _norm_matmul(memt, norm_mem[l][None], w_xkv[l].astype(BF16), 256, 1024, BF16)
        mem_kv = mem_kv.reshape(bsz, m_len, 2 * XATTN_HEADS * XATTN_DH)
        xt = _merge_xattn(ya, yb, yc, z, xt, w_branch_conv[l], w_branch_nsa[l], w_branch_mla[l], w_out[l],
                          norm_xattn[l], w_xq[l], mem_kv, w_xo[l], bsz, seq)
        xt = _ffn(xt, norm_ffn[l], w_gate_up[l], w_down[l], norm_final, l == depth - 1)
    return xt.reshape(bsz, seq, d)
```

```python
import functools
import math

import numpy as np
import jax
import jax.numpy as jnp
from jax import lax
from jax.experimental import pallas as pl
from jax.experimental.pallas import tpu as pltpu

F32 = jnp.float32
BF16 = jnp.bfloat16

EPS = 1e-6
NEG_INF = -1e30
FORCE_SCORE = 1e4

D_MODEL = 1024
CONV_CH = 512
CONV_WIDTH = 31
NSA_HEADS = 8
NSA_G = 2
NSA_HG = NSA_HEADS // NSA_G
NSA_DH = 64
CMP_BLOCK = 32
CMP_STRIDE = 16
CMP_HIDDEN = 256
SLC_BLOCK = 64
N_SELECT = 16
WINDOW = 512
NSA_QB = 64
MLA_HEADS = 4
Q_RANK = 384
KV_RANK = 256
QK_NOPE = 128
QK_ROPE = 64
V_DIM = 128
ROPE_THETA = 10000.0
REL_BUCKETS = 32
REL_MAX_DIST = 128
XATTN_HEADS = 4
XATTN_DH = 128
FFN_HIDDEN = 2816

LANES = 128
SUBLANES = 8

O_GLU, O_Q, O_KV, O_GN, O_CQ, O_CKV, O_KR, O_GM = 0, 1024, 1536, 2304, 2328, 2712, 2968, 3032
GD = NSA_G * NSA_DH

Z_GM = 0
Z_UA = 3072
Z_UB = 3584
Z_Q = 4096
Z_CKV = 4608
Z_KR = 4864
Z_CQ = 4992
Z_CMP = 5376
Z_GN = 5632
Z_COLS = 5760

VMEM_LIMIT = 56 * 1024 * 1024

TOK_TILE = 512
IN_PROJ_TM = 1024
IN_PROJ_TN = 1152
FFN_TM = 1024
FFN_TC = 256
MLA_TQ = 512
MLA_TK = 1024
MLA_HEADS_PER_STEP = 2

LOG2E = math.log2(math.e)


def _cparams(sem):
    return pltpu.CompilerParams(dimension_semantics=sem, vmem_limit_bytes=VMEM_LIMIT)


def _rms(x, g):
    return x * lax.rsqrt(jnp.mean(x * x, axis=-1, keepdims=True) + EPS) * g


def _dot(a, b):
    return jnp.dot(a, b, preferred_element_type=F32)


def _dot_nt(a, b):
    return lax.dot_general(a, b, (((1,), (1,)), ((), ())), preferred_element_type=F32)


def _norm_matmul_kernel(x_ref, g_ref, w_ref, o_ref, h_ref):
    @pl.when(pl.program_id(1) == 0)
    def _():
        h_ref[...] = _rms(x_ref[...], g_ref[...]).astype(BF16)

    o_ref[...] = _dot(h_ref[...], w_ref[...]).astype(o_ref.dtype)


def _norm_matmul(x, g, w, tm, tn, out_dtype):
    m, k = x.shape
    n = w.shape[1]
    return pl.pallas_call(
        _norm_matmul_kernel,
        out_shape=jax.ShapeDtypeStruct((m, n), out_dtype),
        grid=(m // tm, n // tn),
        in_specs=[pl.BlockSpec((tm, k), lambda i, j: (i, 0)),
                  pl.BlockSpec((1, k), lambda i, j: (0, 0)),
                  pl.BlockSpec((k, tn), lambda i, j: (0, j))],
        out_specs=pl.BlockSpec((tm, tn), lambda i, j: (i, j)),
        scratch_shapes=[pltpu.VMEM((tm, k), BF16)],
        compiler_params=_cparams(("parallel", "arbitrary")),
        name="norm_matmul",
    )(x, g, w)


def _in_proj_kernel(x_ref, g_ref, w_ref, z_ref, zc_ref, h_ref, *, cmp_tile, cmp_off):
    @pl.when(pl.program_id(1) == 0)
    def _():
        h_ref[...] = _rms(x_ref[...], g_ref[...]).astype(BF16)

    acc = _dot(h_ref[...], w_ref[...])
    z_ref[...] = acc.astype(BF16)

    @pl.when(pl.program_id(1) == cmp_tile)
    def _():
        zc_ref[...] = acc[:, cmp_off:cmp_off + 2 * GD]


def _in_proj(x, g, w):
    tm, tn = IN_PROJ_TM, IN_PROJ_TN
    m, k = x.shape
    n = w.shape[1]
    return pl.pallas_call(
        functools.partial(_in_proj_kernel, cmp_tile=Z_CMP // tn, cmp_off=Z_CMP % tn),
        out_shape=(jax.ShapeDtypeStruct((m, n), BF16), jax.ShapeDtypeStruct((m, 2 * GD), F32)),
        grid=(m // tm, n // tn),
        in_specs=[pl.BlockSpec((tm, k), lambda i, j: (i, 0)),
                  pl.BlockSpec((1, k), lambda i, j: (0, 0)),
                  pl.BlockSpec((k, tn), lambda i, j: (0, j))],
        out_specs=(pl.BlockSpec((tm, tn), lambda i, j: (i, j)),
                   pl.BlockSpec((tm, 2 * GD), lambda i, j: (i, 0))),
        scratch_shapes=[pltpu.VMEM((tm, k), BF16)],
        compiler_params=_cparams(("parallel", "arbitrary")),
        name="in_proj",
    )(x, g, w)


def _kv_proj_kernel(x_ref, g_ref, wk_ref, wvt_ref, k_ref, vt_ref):
    @pl.when(pl.program_id(1) == 0)
    def _():
        k_ref[...] = jnp.zeros(k_ref.shape, BF16)
        vt_ref[...] = jnp.zeros(vt_ref.shape, BF16)

    @pl.when(pl.program_id(1) > 0)
    def _():
        h = _rms(x_ref[...], g_ref[...]).astype(BF16)
        k = _dot(h, wk_ref[...])
        tm = k.shape[0]
        row = lax.broadcasted_iota(jnp.int32, k.shape, 0)
        lane = lax.broadcasted_iota(jnp.int32, k.shape, 1)
        blk = (pl.program_id(1) - 1) * (tm // SLC_BLOCK) + lax.shift_right_logical(row, 6)
        hot = (lane < NSA_G * GD) & (lax.bitwise_and(lane, GD - 1) == blk + NSA_DH)
        k_ref[0] = jnp.where(hot, 1.0, k).astype(BF16)
        vt_ref[0] = _dot_nt(wvt_ref[...], h).astype(BF16)


def _kv_proj(x, g, wk, wvt, bsz, seq):
    tm = WINDOW
    nst = seq // tm
    d = x.shape[1]
    nk = wk.shape[1]
    nv = wvt.shape[0]
    return pl.pallas_call(
        _kv_proj_kernel,
        out_shape=(jax.ShapeDtypeStruct((bsz, seq + tm, nk), BF16),
                   jax.ShapeDtypeStruct((bsz, nv, seq + tm), BF16)),
        grid=(bsz, nst + 1),
        in_specs=[pl.BlockSpec((tm, d), lambda b, s: (b * nst + jnp.maximum(s - 1, 0), 0)),
                  pl.BlockSpec((1, d), lambda b, s: (0, 0)),
                  pl.BlockSpec((d, nk), lambda b, s: (0, 0)),
                  pl.BlockSpec((nv, d), lambda b, s: (0, 0))],
        out_specs=(pl.BlockSpec((1, tm, nk), lambda b, s: (b, s, 0)),
                   pl.BlockSpec((1, nv, tm), lambda b, s: (b, 0, s))),
        compiler_params=_cparams(("parallel", "arbitrary")),
        name="nsa_kv_proj",
    )(x, g, wk, wvt)


CONV_HALO = 32


CONV_ROWS = 64


def _conv_kernel(a_ref, b_ref, w_ref, cb_ref, lg_ref, lb_ref, o_ref, buf_ref, sh_ref, *, ts):
    @pl.when(pl.program_id(1) == 0)
    def _():
        buf_ref[0:CONV_HALO, :] = jnp.zeros((CONV_HALO, CONV_CH), F32)

    buf_ref[CONV_HALO:CONV_HALO + ts, :] = a_ref[...].astype(F32) * jax.nn.sigmoid(b_ref[...].astype(F32))
    span = ts + CONV_HALO - SUBLANES
    for r in range(1, SUBLANES):
        sh_ref[r - 1, 0:span, :] = buf_ref[r:r + span, :]
    off = CONV_HALO - (CONV_WIDTH - 1)

    def rows(i, carry):
        r0 = pl.multiple_of(i * CONV_ROWS, CONV_ROWS)
        acc = jnp.zeros((CONV_ROWS, CONV_CH), F32) + cb_ref[...]
        for k in range(CONV_WIDTH):
            res, base = (off + k) % SUBLANES, (off + k) // SUBLANES * SUBLANES
            if res == 0:
                tap = buf_ref[pl.ds(r0 + base, CONV_ROWS), :]
            else:
                tap = sh_ref[res - 1, pl.ds(r0 + base, CONV_ROWS), :]
            acc = acc + tap * w_ref[k:k + 1, :]
        mu = jnp.mean(acc, axis=-1, keepdims=True)
        xc = acc - mu
        var = jnp.mean(xc * xc, axis=-1, keepdims=True)
        y = xc * lax.rsqrt(var + EPS) * lg_ref[...] + lb_ref[...]
        o_ref[pl.ds(r0, CONV_ROWS), :] = (y * jax.nn.sigmoid(y)).astype(BF16)
        return carry

    lax.fori_loop(0, ts // CONV_ROWS, rows, 0)
    buf_ref[0:CONV_HALO, :] = buf_ref[ts:ts + CONV_HALO, :]


def _conv_module(z, conv_w, conv_b, ln_g, ln_b, bsz, seq):
    ts = TOK_TILE
    nst = seq // ts
    wpad = jnp.zeros((32, CONV_CH), F32).at[:CONV_WIDTH].set(conv_w)
    return pl.pallas_call(
        functools.partial(_conv_kernel, ts=ts),
        out_shape=jax.ShapeDtypeStruct((bsz * seq, CONV_CH), BF16),
        grid=(bsz, nst),
        in_specs=[pl.BlockSpec((ts, CONV_CH), lambda b, s: (b * nst + s, Z_UA // CONV_CH)),
                  pl.BlockSpec((ts, CONV_CH), lambda b, s: (b * nst + s, Z_UB // CONV_CH)),
                  pl.BlockSpec((32, CONV_CH), lambda b, s: (0, 0)),
                  pl.BlockSpec((1, CONV_CH), lambda b, s: (0, 0)),
                  pl.BlockSpec((1, CONV_CH), lambda b, s: (0, 0)),
                  pl.BlockSpec((1, CONV_CH), lambda b, s: (0, 0))],
        out_specs=pl.BlockSpec((ts, CONV_CH), lambda b, s: (b * nst + s, 0)),
        scratch_shapes=[pltpu.VMEM((ts + CONV_HALO, CONV_CH), F32),
                        pltpu.VMEM((SUBLANES - 1, ts + CONV_HALO - SUBLANES, CONV_CH), F32)],
        compiler_params=_cparams(("arbitrary", "arbitrary")),
        name="conv_module",
    )(z, z, wpad, conv_b[None], ln_g[None], ln_b[None])


def _compress_kernel(xk_ref, xv_ref, pos_ref, w1_ref, b1_ref, w2k_ref, w2v_ref, kc_ref, vct_ref, *, nch):
    for kind, (x_ref, w2_ref) in enumerate(((xk_ref, w2k_ref), (xv_ref, w2v_ref))):
        a = jnp.zeros((nch, NSA_G * CMP_HIDDEN), F32)
        b = jnp.zeros((nch, NSA_G * CMP_HIDDEN), F32)
        for l in range(CMP_STRIDE):
            xs = x_ref[pl.ds(l, nch, stride=CMP_STRIDE), :]
            a = a + _dot((xs + pos_ref[kind, l:l + 1, :]).astype(BF16), w1_ref[kind, l])
            b = b + _dot((xs + pos_ref[kind, CMP_STRIDE + l:CMP_STRIDE + l + 1, :]).astype(BF16),
                         w1_ref[kind, CMP_STRIDE + l])
        pre = a + pltpu.roll(b, nch - 1, 0) + b1_ref[kind]
        out = _dot(jax.nn.gelu(pre).astype(BF16), w2_ref[...])
        if kind == 0:
            kc_ref[0] = out.astype(BF16)
        else:
            vct_ref[0] = out.T.astype(BF16)


def _blockdiag2(w):
    z = jnp.zeros_like(w)
    return jnp.concatenate([jnp.concatenate([w, z], axis=-1), jnp.concatenate([z, w], axis=-1)], axis=-2)


def _compress(z, pos, w1, b1, w2, bsz, seq):
    nch = seq // CMP_STRIDE
    pos2 = jnp.concatenate([pos, pos], axis=-1)
    w1e = _blockdiag2(w1.reshape(2, CMP_BLOCK, NSA_DH, CMP_HIDDEN)).astype(BF16)
    b1e = jnp.concatenate([b1, b1], axis=-1)[:, None]
    w2k = _blockdiag2(w2[0]).astype(BF16)
    w2v = _blockdiag2(w2[1]).astype(BF16)
    full = lambda a: pl.BlockSpec(a.shape, lambda b: (0,) * a.ndim)
    return pl.pallas_call(
        functools.partial(_compress_kernel, nch=nch),
        out_shape=(jax.ShapeDtypeStruct((bsz, nch, GD), BF16),
                   jax.ShapeDtypeStruct((bsz, GD, nch), BF16)),
        grid=(bsz,),
        in_specs=[pl.BlockSpec((seq, GD), lambda b: (b, 0)),
                  pl.BlockSpec((seq, GD), lambda b: (b, 1)),
                  full(pos2), full(w1e), full(b1e), full(w2k), full(w2v)],
        out_specs=(pl.BlockSpec((1, nch, GD), lambda b: (b, 0, 0)),
                   pl.BlockSpec((1, GD, nch), lambda b: (b, 0, 0))),
        compiler_params=_cparams(("parallel",)),
        name="nsa_compress",
    )(z, z, pos2, w1e, b1e, w2k, w2v)


NSA_QP = 2 * NSA_QB
NEAR_KEYS = 256
WIN_KEYS = 640
CMP_TAB_ROWS = 512
CMP_TAB_ZERO = 256


def _t5_bucket_np(d):
    exact = REL_BUCKETS // 2
    d = np.maximum(d, 0)
    ratio = np.log(np.maximum(d, 1).astype(np.float32) / np.float32(exact)) / np.float32(math.log(REL_MAX_DIST / exact))
    large = np.minimum(exact + (ratio * (REL_BUCKETS - exact)).astype(np.int32), REL_BUCKETS - 1)
    return np.where(d < exact, d, large).astype(np.int32)


def _bucket_thresholds():
    exact = REL_BUCKETS // 2
    bk = _t5_bucket_np(np.arange(4 * REL_MAX_DIST))
    assert np.all(np.diff(bk) >= 0) and bk[-1] == REL_BUCKETS - 1
    return [int(np.argmax(bk >= k)) for k in range(exact + 1, REL_BUCKETS)]


def _bias_rows(rel_ref, dist, valid, shift):
    exact = REL_BUCKETS // 2
    bucket = jnp.full(dist.shape, exact, jnp.int32)
    for thr in _bucket_thresholds():
        bucket = bucket + jnp.where(dist >= thr, 1, 0)
    bucket = jnp.where(dist < exact, dist, bucket)
    val = jnp.zeros(dist.shape, F32)
    for bkt in range(REL_BUCKETS):
        val = jnp.where(bucket == bkt, rel_ref[0, bkt:bkt + 1, :], val)
    if shift:
        val = val - rel_ref[0, REL_BUCKETS - 1:REL_BUCKETS, :]
    return jnp.where(valid, val * LOG2E, NEG_INF)


def _nsa_bias_kernel(rel_ref, tc_ref, tn_ref, tw_ref):
    hq = NSA_HG * NSA_QP
    rows = 128

    def dist_of(nrows, r0, fn):
        r = r0 + lax.broadcasted_iota(jnp.int32, (nrows, hq), 0)
        t = lax.bitwise_and(lax.broadcasted_iota(jnp.int32, (nrows, hq), 1), NSA_QP - 1)
        return fn(r, t)

    for r0 in range(0, CMP_TAB_ROWS, rows):
        d = dist_of(rows, r0, lambda r, t: t - CMP_STRIDE * (r - CMP_TAB_ZERO) - (CMP_BLOCK - 1))
        tc_ref[0, r0:r0 + rows, :] = _bias_rows(rel_ref, d, d >= 0, False)
    for r0 in range(0, NEAR_KEYS, rows):
        d = dist_of(rows, r0, lambda r, t: NEAR_KEYS // 2 + t - r)
        tn_ref[0, r0:r0 + rows, :] = _bias_rows(rel_ref, d, d >= 0, True)
    for r0 in range(0, WIN_KEYS, rows):
        d = dist_of(rows, r0, lambda r, t: WINDOW + t - r)
        tw_ref[0, r0:r0 + rows, :] = _bias_rows(rel_ref, d, (d >= 0) & (d < WINDOW), False)


def _nsa_tables(rel_bias):
    hq = NSA_HG * NSA_QP
    rel4 = jnp.repeat(rel_bias.reshape(REL_BUCKETS, NSA_G, NSA_HG).transpose(1, 0, 2), NSA_QP, axis=-1)
    spec = lambda r: pl.BlockSpec((1, r, hq), lambda g: (g, 0, 0))
    return pl.pallas_call(
        _nsa_bias_kernel,
        out_shape=(jax.ShapeDtypeStruct((NSA_G, CMP_TAB_ROWS, hq), F32),
                   jax.ShapeDtypeStruct((NSA_G, NEAR_KEYS, hq), F32),
                   jax.ShapeDtypeStruct((NSA_G, WIN_KEYS, hq), F32)),
        grid=(NSA_G,),
        in_specs=[spec(REL_BUCKETS)],
        out_specs=(spec(CMP_TAB_ROWS), spec(NEAR_KEYS), spec(WIN_KEYS)),
        compiler_params=_cparams(("parallel",)),
        name="nsa_bias_tables",
    )(rel4)


FAR_KEYS = 1024
KV_FRONT = WINDOW
KREP = NSA_HG * NSA_DH


def _softmax_cols(s):
    m = jnp.max(s, axis=0, keepdims=True)
    p = jnp.exp2(s - m)
    return m, p, jnp.sum(p, axis=0, keepdims=True)


def _rank_select(score_ref, n_sb, n_sel):
    groups = n_sb // SUBLANES
    sub = lax.broadcasted_iota(jnp.int32, (SUBLANES, NSA_QP), 0)
    tiles = [score_ref[SUBLANES * v:SUBLANES * (v + 1), :] for v in range(groups)]
    cnts = [jnp.zeros((SUBLANES, NSA_QP), F32) for _ in range(groups)]
    for jp in range(n_sb):
        row = score_ref[jp:jp + 1, :]
        for v in range(groups):
            lo = SUBLANES * v
            if jp < lo:
                beats = row >= tiles[v]
            elif jp >= lo + SUBLANES - 1:
                beats = row > tiles[v]
            else:
                beats = (row > tiles[v]) | ((row == tiles[v]) & (sub > jp - lo))
            cnts[v] = cnts[v] + jnp.where(beats, 1.0, 0.0)
    cnt = jnp.concatenate(cnts, axis=0)
    return jnp.where(cnt < float(n_sel), 1.0, 0.0)


def _nsa_kernel(q_ref, gate_ref, kc_ref, vct_ref, ks0_ref, ks1_ref, kw_ref, vst_ref, vwt_ref,
                tc_ref, tn_ref, tw_ref, cov_ref, o_ref, score_ref, *, n_sb):
    p2 = pl.program_id(1)
    hq = NSA_HG * NSA_QP
    nch = kc_ref.shape[1]
    groups = range(NSA_G)
    qcol = lambda g: slice(KREP * g, KREP * (g + 1))
    vrow = lambda g: slice(NSA_DH * g, NSA_DH * (g + 1))

    ks_refs = (ks0_ref, ks1_ref)
    qs, q64s = [], []
    zq = jnp.zeros((hq, NSA_DH), BF16)
    for g in groups:
        qb = (q_ref[:, qcol(g)].astype(F32) * (NSA_DH ** -0.5 * LOG2E)).astype(BF16)
        q64 = jnp.concatenate([qb[:, NSA_DH * h:NSA_DH * (h + 1)] for h in range(NSA_HG)], axis=0)
        q64s.append(q64)
        qs.append(jnp.concatenate([q64, zq] if g == 0 else [zq, q64], axis=1))

    start_c = pl.multiple_of(CMP_TAB_ZERO - (NSA_QP // CMP_STRIDE) * p2, SUBLANES)
    lane = lax.broadcasted_iota(jnp.int32, (1, hq), 1)
    tq = NSA_QP * p2 + lax.bitwise_and(lane, NSA_QP - 1)
    anyv = jnp.where(tq >= CMP_BLOCK - 1, 1.0, 0.0)
    jrow = lax.broadcasted_iota(jnp.int32, (n_sb, NSA_QP), 0)
    tok = lax.broadcasted_iota(jnp.int32, (n_sb, NSA_QP), 1)
    cur = 2 * p2 + lax.shift_right_logical(tok, 6)
    forced = (jrow == 0) | (jrow == cur) | (jrow == cur - 1)
    o_cmp = []
    for g in groups:
        sc = _dot_nt(kc_ref[0], qs[g]) + tc_ref[g, pl.ds(start_c, nch), :]
        _, pc, lc = _softmax_cols(sc)
        pc = pc * (anyv / lc)
        o_cmp.append(_dot(vct_ref[0, vrow(g), :], pc.astype(BF16)))
        psum = pc[:, 0:NSA_QP]
        for h in range(1, NSA_HG):
            psum = psum + pc[:, NSA_QP * h:NSA_QP * (h + 1)]
        p_hi = psum.astype(BF16)
        p_lo = (psum - p_hi.astype(F32)).astype(BF16)
        imp = _dot(cov_ref[...], p_hi) + _dot(cov_ref[...], p_lo)
        score_ref[g] = jnp.where(forced, FORCE_SCORE, jnp.where(jrow <= cur, imp, -1.0))

    def mask_operand(g, keep):
        mb = ((keep - 1.0) * -NEG_INF).T.astype(BF16)
        if n_sb < NSA_DH:
            mb = jnp.concatenate([mb, jnp.zeros((NSA_QP, NSA_DH - n_sb), BF16)], axis=1)
        return jnp.concatenate([q64s[g], jnp.concatenate([mb] * NSA_HG, axis=0)], axis=1)

    q_near, q_far = [], []
    for g in groups:
        sel = _rank_select(score_ref.at[g], n_sb, min(N_SELECT, n_sb))
        q_near.append(mask_operand(g, sel))
        q_far.append(mask_operand(g, jnp.where(jrow < 2 * p2 - 2, sel, 0.0)))

    win0 = pl.multiple_of(NSA_QP * p2, LANES)
    near0 = pl.multiple_of(win0 + KV_FRONT - NEAR_KEYS // 2, LANES)
    state = []
    for g in groups:
        s = _dot_nt(ks_refs[g][0, pl.ds(near0, NEAR_KEYS), :], q_near[g]) + tn_ref[g]
        s = jnp.concatenate([jnp.where(p2 > 0, s[0:NEAR_KEYS // 2], NEG_INF), s[NEAR_KEYS // 2:]], axis=0)
        m_s, p_s, l_s = _softmax_cols(s)
        state += [m_s, l_s, _dot(vst_ref[0, vrow(g), pl.ds(near0, NEAR_KEYS)], p_s.astype(BF16))]

    window = []
    for g in groups:
        sw = _dot_nt(kw_ref[0, pl.ds(win0, WIN_KEYS), :], qs[g]) + tw_ref[g]
        slabs = [sw[NSA_QP * j:NSA_QP * (j + 1)] for j in range(WIN_KEYS // NSA_QP)]
        for j in range(KV_FRONT // NSA_QP):
            slabs[j] = jnp.where(NSA_QP * j + win0 >= KV_FRONT, slabs[j], NEG_INF)
        _, p_w, l_w = _softmax_cols(jnp.concatenate(slabs, axis=0))
        window.append((_dot(vwt_ref[0, vrow(g), pl.ds(win0, WIN_KEYS)], p_w.astype(BF16)), l_w))

    def far_body(c, carry):
        k0 = pl.multiple_of(FAR_KEYS * c + KV_FRONT, LANES)
        sfs = [_dot_nt(ks_refs[g][0, pl.ds(k0, FAR_KEYS), :], q_far[g]) for g in groups]
        out = []
        for g in groups:
            m_old, l_old, acc_old = carry[3 * g:3 * g + 3]
            sf = sfs[g]
            m_new = jnp.maximum(m_old, jnp.max(sf, axis=0, keepdims=True))
            alpha = jnp.exp2(m_old - m_new)
            pf = jnp.exp2(sf - m_new)
            l_new = alpha * l_old + jnp.sum(pf, axis=0, keepdims=True)
            acc_new = alpha * acc_old + _dot(vst_ref[0, vrow(g), pl.ds(k0, FAR_KEYS)], pf.astype(BF16))
            out += [m_new, l_new, acc_new]
        return tuple(out)

    n_far = lax.div(jnp.maximum(p2 - 1, 0) * NSA_QP + FAR_KEYS - 1, FAR_KEYS)
    state = lax.fori_loop(0, n_far, far_body, tuple(state))

    gt = gate_ref[...].astype(F32).T
    r = lax.broadcasted_iota(jnp.int32, (NSA_QP, NSA_QP), 0)
    c = lax.broadcasted_iota(jnp.int32, (NSA_QP, NSA_QP), 1)
    eye = jnp.where(r == c, 1.0, 0.0).astype(BF16)
    for g in groups:
        _, l_s, acc_s = state[3 * g:3 * g + 3]
        acc_w, l_w = window[g]
        gsel = jax.nn.sigmoid(gt[3 * NSA_HG * g:3 * NSA_HG * (g + 1)])
        gate = lambda b: jnp.concatenate([gsel[3 * h + b:3 * h + b + 1] for h in range(NSA_HG)], axis=1)
        out_t = (gate(0) * o_cmp[g] + (gate(1) / l_s) * acc_s + (gate(2) / l_w) * acc_w).astype(BF16)
        stacked = jnp.concatenate([out_t[:, NSA_QP * h:NSA_QP * (h + 1)] for h in range(NSA_HG)], axis=0)
        o_ref[:, qcol(g)] = _dot_nt(eye, stacked).astype(BF16)


def _nsa_attention(z, kc, vct, kk, vvt, tables, bsz, seq):
    g, hg, dh, qp = NSA_G, NSA_HG, NSA_DH, NSA_QP
    nstep = seq // qp
    n_sb = seq // SLC_BLOCK
    nch = kc.shape[1]
    hq = hg * qp
    sp = kk.shape[1]
    tc, tn, tw = tables
    c_start = CMP_STRIDE * np.arange(nch)
    s_start = SLC_BLOCK * np.arange(n_sb)
    cover_t = ((c_start[None, :] < s_start[:, None] + SLC_BLOCK)
               & (c_start[None, :] + CMP_BLOCK > s_start[:, None])
               & (np.arange(nch)[None, :] < (seq - CMP_BLOCK) // CMP_STRIDE + 1))
    cover_t = jnp.asarray(cover_t.astype(np.float32), BF16)
    full = lambda a: pl.BlockSpec(a.shape, lambda b, i: (0,) * a.ndim, pipeline_mode=pl.Buffered(1))
    qd = g * hg * dh
    return pl.pallas_call(
        functools.partial(_nsa_kernel, n_sb=n_sb),
        out_shape=jax.ShapeDtypeStruct((bsz * seq, qd), BF16),
        grid=(bsz, nstep),
        in_specs=[pl.BlockSpec((qp, qd), lambda b, i: (b * nstep + i, Z_Q // qd)),
                  pl.BlockSpec((qp, LANES), lambda b, i: (b * nstep + i, Z_GN // LANES)),
                  pl.BlockSpec((1, nch, GD), lambda b, i: (b, 0, 0)),
                  pl.BlockSpec((1, g * dh, nch), lambda b, i: (b, 0, 0)),
                  pl.BlockSpec((1, sp, GD), lambda b, i: (b, 0, 0)),
                  pl.BlockSpec((1, sp, GD), lambda b, i: (b, 0, 1)),
                  pl.BlockSpec((1, sp, GD), lambda b, i: (b, 0, 2)),
                  pl.BlockSpec((1, g * dh, sp), lambda b, i: (b, 0, 0)),
                  pl.BlockSpec((1, g * dh, sp), lambda b, i: (b, 1, 0)),
                  full(tc), full(tn), full(tw), full(cover_t)],
        out_specs=pl.BlockSpec((qp, qd), lambda b, i: (b * nstep + i, 0)),
        scratch_shapes=[pltpu.VMEM((g, n_sb, qp), F32)],
        compiler_params=_cparams(("parallel", "arbitrary")),
        name="nsa_attention",
    )(z, z, kc, vct, kk, kk, kk, vvt, vvt, tc, tn, tw, cover_t)


def _rope_table_kernel(pos_ref, inv_ref, o_ref):
    ang = inv_ref[...] * pos_ref[0].astype(F32)
    c, s = jnp.cos(ang), jnp.sin(ang)
    o_ref[...] = jnp.concatenate([c, c, -s, s], axis=0).T


def _rope_table(positions):
    tm = TOK_TILE
    t = positions.size
    half = QK_ROPE // 2
    inv = (ROPE_THETA ** (-jnp.arange(half, dtype=F32) / half))[:, None]
    return pl.pallas_call(
        _rope_table_kernel,
        out_shape=jax.ShapeDtypeStruct((t, 2 * QK_ROPE), F32),
        grid=(t // tm,),
        in_specs=[pl.BlockSpec((1, 1, tm), lambda i: (i, 0, 0)),
                  pl.BlockSpec((half, 1), lambda i: (0, 0))],
        out_specs=pl.BlockSpec((tm, 2 * QK_ROPE), lambda i: (i, 0)),
        compiler_params=_cparams(("parallel",)),
        name="rope_table",
    )(positions.reshape(t // tm, 1, tm), inv)


MLA_HW = 256


def _mla_proj_kernel(cq_ref, ckv_ref, kr_ref, rope_ref, nq_ref, nkv_ref, wq_ref, wkn_ref, wvt_ref,
                     q_ref, k_ref, vt_ref):
    scale = (QK_NOPE + QK_ROPE) ** -0.5 * LOG2E
    rope = rope_ref[...]
    yq = _dot(_rms(cq_ref[...].astype(F32), nq_ref[...]).astype(BF16), wq_ref[...])
    ckv = _rms(ckv_ref[...].astype(F32), nkv_ref[...]).astype(BF16)
    ykn = _dot(ckv, wkn_ref[...])
    vt_ref[0] = _dot_nt(wvt_ref[...], ckv).astype(BF16)
    kp = kr_ref[...].astype(F32) * rope
    kp = kp + pltpu.roll(kp, QK_ROPE, 1)
    lane = lax.broadcasted_iota(jnp.int32, kp.shape, 1)
    kp = jnp.where(lane < QK_ROPE, kp, 0.0).astype(BF16)
    for h in range(MLA_HEADS):
        base = MLA_HW * h
        q_ref[:, base:base + QK_NOPE] = (yq[:, base:base + QK_NOPE] * scale).astype(BF16)
        qp = yq[:, base + QK_NOPE:base + MLA_HW] * rope
        qp = qp + pltpu.roll(qp, QK_ROPE, 1)
        q_ref[:, base + QK_NOPE:base + MLA_HW] = (qp * scale).astype(BF16)
        k_ref[:, base:base + QK_NOPE] = ykn[:, QK_NOPE * h:QK_NOPE * (h + 1)].astype(BF16)
        k_ref[:, base + QK_NOPE:base + MLA_HW] = kp


def _swap_halves(w):
    half = QK_ROPE // 2
    return jnp.concatenate([w[..., half:], w[..., :half]], axis=-1)


def _mla_proj(z, rope_tab, norm_q, norm_kv, w_uq, w_ukv, bsz, seq):
    tm = TOK_TILE
    t = z.shape[0]
    nst = seq // tm
    wq = w_uq.reshape(Q_RANK, MLA_HEADS, QK_NOPE + QK_ROPE)
    wq = jnp.concatenate([wq, _swap_halves(wq[..., QK_NOPE:])], axis=-1)
    wq = wq.reshape(Q_RANK, MLA_HEADS * MLA_HW).astype(BF16)
    wkv = w_ukv.reshape(KV_RANK, MLA_HEADS, QK_NOPE + V_DIM)
    wkn = wkv[..., :QK_NOPE].reshape(KV_RANK, MLA_HEADS * QK_NOPE).astype(BF16)
    wvt = wkv[..., QK_NOPE:].reshape(KV_RANK, MLA_HEADS * V_DIM).T.astype(BF16)
    hw = MLA_HEADS * MLA_HW
    hv = MLA_HEADS * V_DIM
    row = lambda b, s: b * nst + s
    return pl.pallas_call(
        _mla_proj_kernel,
        out_shape=(jax.ShapeDtypeStruct((t, hw), BF16),
                   jax.ShapeDtypeStruct((t, hw), BF16),
                   jax.ShapeDtypeStruct((bsz, hv, seq), BF16)),
        grid=(bsz, nst),
        in_specs=[pl.BlockSpec((tm, Q_RANK), lambda b, s: (row(b, s), Z_CQ // Q_RANK)),
                  pl.BlockSpec((tm, KV_RANK), lambda b, s: (row(b, s), Z_CKV // KV_RANK)),
                  pl.BlockSpec((tm, 2 * QK_ROPE), lambda b, s: (row(b, s), Z_KR // (2 * QK_ROPE))),
                  pl.BlockSpec((tm, 2 * QK_ROPE), lambda b, s: (row(b, s), 0)),
                  pl.BlockSpec((1, Q_RANK), lambda b, s: (0, 0)),
                  pl.BlockSpec((1, KV_RANK), lambda b, s: (0, 0)),
                  pl.BlockSpec((Q_RANK, hw), lambda b, s: (0, 0)),
                  pl.BlockSpec((KV_RANK, hv), lambda b, s: (0, 0)),
                  pl.BlockSpec((hv, KV_RANK), lambda b, s: (0, 0))],
        out_specs=(pl.BlockSpec((tm, hw), lambda b, s: (row(b, s), 0)),
                   pl.BlockSpec((tm, hw), lambda b, s: (row(b, s), 0)),
                   pl.BlockSpec((1, hv, tm), lambda b, s: (b, 0, s))),
        compiler_params=_cparams(("parallel", "parallel")),
        name="mla_proj",
    )(z, z, z, rope_tab, norm_q[None], norm_kv[None], wq, wkn, wvt)


def _mla_attn_kernel(q_ref, k_ref, vt_ref, o_ref, *, tq, tk, nh):
    iq = pl.program_id(2)
    cd = lax.div(iq * tq, tk)
    heads = range(nh)
    hcol = lambda h: slice(MLA_HW * h, MLA_HW * (h + 1))
    vrow = lambda h: slice(V_DIM * h, V_DIM * (h + 1))
    qs = [q_ref[:, hcol(h)] for h in heads]

    def scores(c, h):
        k0 = pl.multiple_of(c * tk, tk)
        return _dot_nt(k_ref[0, pl.ds(k0, tk), hcol(h)], qs[h])

    def diagonal(nk):
        k0 = pl.multiple_of((iq + 1) * tq - nk, tq)
        kpos = k0 + lax.broadcasted_iota(jnp.int32, (nk, tq), 0)
        qpos = iq * tq + lax.broadcasted_iota(jnp.int32, (nk, tq), 1)
        st = []
        for h in heads:
            s = jnp.where(kpos <= qpos, _dot_nt(k_ref[0, pl.ds(k0, nk), hcol(h)], qs[h]), NEG_INF)
            m0, p0, l0 = _softmax_cols(s)
            st += [m0, l0, _dot(vt_ref[0, vrow(h), pl.ds(k0, nk)], p0.astype(BF16))]
        return tuple(st)

    assert tk == 2 * tq
    state = lax.cond(lax.rem(iq, 2) == 0, lambda: diagonal(tq), lambda: diagonal(tk))

    def body(c, carry):
        k0 = pl.multiple_of(c * tk, tk)
        ss = [scores(c, h) for h in heads]
        out = []
        for h in heads:
            m_old, l_old, acc_old = carry[3 * h:3 * h + 3]
            m_new = jnp.maximum(m_old, jnp.max(ss[h], axis=0, keepdims=True))
            alpha = jnp.exp2(m_old - m_new)
            p = jnp.exp2(ss[h] - m_new)
            l_new = alpha * l_old + jnp.sum(p, axis=0, keepdims=True)
            acc_new = alpha * acc_old + _dot(vt_ref[0, vrow(h), pl.ds(k0, tk)], p.astype(BF16))
            out += [m_new, l_new, acc_new]
        return tuple(out)

    state = lax.fori_loop(0, cd, body, tuple(state))
    for h in heads:
        _, l, acc = state[3 * h:3 * h + 3]
        o_ref[:, vrow(h)] = (acc / l).T.astype(BF16)


def _mla_attention(qf, kf, vt, bsz, seq):
    tq, tk, nh = MLA_TQ, MLA_TK, MLA_HEADS_PER_STEP
    h = MLA_HEADS
    nq = seq // tq
    k3 = kf.reshape(bsz, seq, h * MLA_HW)
    return pl.pallas_call(
        functools.partial(_mla_attn_kernel, tq=tq, tk=tk, nh=nh),
        out_shape=jax.ShapeDtypeStruct((bsz * seq, h * V_DIM), BF16),
        grid=(bsz, h // nh, nq),
        in_specs=[pl.BlockSpec((tq, nh * MLA_HW), lambda b, hh, i: (b * nq + i, hh)),
                  pl.BlockSpec((1, seq, nh * MLA_HW), lambda b, hh, i: (b, 0, hh)),
                  pl.BlockSpec((1, nh * V_DIM, seq), lambda b, hh, i: (b, hh, 0))],
        out_specs=pl.BlockSpec((tq, nh * V_DIM), lambda b, hh, i: (b * nq + i, hh)),
        compiler_params=_cparams(("parallel", "parallel", "arbitrary")),
        name="mla_attention",
    )(qf, k3, vt)


def _merge_xattn_kernel(ya_ref, yb_ref, yc_ref, ga_ref, gb_ref, gc_ref, x_ref,
                        wa_ref, wb_ref, wc_ref, wo_ref,
                        gx_ref, wq_ref, kv_ref, wxo_ref, o_ref):
    sig = lambda ref: jax.nn.sigmoid(ref[...].astype(F32))
    y = (sig(ga_ref) * _dot(ya_ref[...], wa_ref[...])
         + sig(gb_ref) * _dot(yb_ref[...], wb_ref[...])
         + sig(gc_ref) * _dot(yc_ref[...], wc_ref[...]))
    x = x_ref[...] + _dot(y.astype(BF16), wo_ref[...])
    h = _rms(x, gx_ref[...]).astype(BF16)
    q = _dot(h, wq_ref[...]) * XATTN_DH ** -0.5
    hd = XATTN_HEADS * XATTN_DH
    outs = []
    for hh in range(XATTN_HEADS):
        qh = q[:, XATTN_DH * hh:XATTN_DH * (hh + 1)].astype(BF16)
        kh = kv_ref[0, :, XATTN_DH * hh:XATTN_DH * (hh + 1)]
        vh = kv_ref[0, :, hd + XATTN_DH * hh:hd + XATTN_DH * (hh + 1)]
        s = _dot_nt(qh, kh)
        m = jnp.max(s, axis=-1, keepdims=True)
        p = jnp.exp(s - m)
        p = p / jnp.sum(p, axis=-1, keepdims=True)
        outs.append(_dot(p.astype(BF16), vh))
    o = jnp.concatenate(outs, axis=-1).astype(BF16)
    o_ref[...] = x + _dot(o, wxo_ref[...])


def _merge_xattn(ya, yb, yc, z, x, wa, wb, wc, wo, gx, wq, kv, wxo, bsz, seq):
    tm = TOK_TILE
    t, d = x.shape
    nst = seq // tm
    m_len = kv.shape[1]
    hd = XATTN_HEADS * XATTN_DH
    row = lambda b, s: b * nst + s
    act = pl.BlockSpec((tm, ya.shape[1]), lambda b, s: (row(b, s), 0))
    gate = lambda k: pl.BlockSpec((tm, d), lambda b, s: (row(b, s), Z_GM // d + k))
    const = lambda shape: pl.BlockSpec(shape, lambda b, s: (0, 0))
    bf = lambda w: w.astype(BF16)
    return pl.pallas_call(
        _merge_xattn_kernel,
        out_shape=jax.ShapeDtypeStruct((t, d), F32),
        grid=(bsz, nst),
        in_specs=[act, act, act, gate(0), gate(1), gate(2),
                  pl.BlockSpec((tm, d), lambda b, s: (row(b, s), 0)),
                  const((ya.shape[1], d)), const((ya.shape[1], d)), const((ya.shape[1], d)), const((d, d)),
                  const((1, d)), const((d, hd)),
                  pl.BlockSpec((1, m_len, 2 * hd), lambda b, s: (b, 0, 0)),
                  const((hd, d))],
        out_specs=pl.BlockSpec((tm, d), lambda b, s: (row(b, s), 0)),
        compiler_params=_cparams(("parallel", "parallel")),
        name="merge_xattn",
    )(ya, yb, yc, z, z, z, x, bf(wa), bf(wb), bf(wc), bf(wo), gx[None], bf(wq), kv, bf(wxo))


def _ffn_kernel(x_ref, g_ref, wg_ref, wu_ref, wd_ref, gf_ref, o_ref, h_ref, acc_ref, *, final):
    c = pl.program_id(1)

    @pl.when(c == 0)
    def _():
        h_ref[...] = _rms(x_ref[...], g_ref[...]).astype(BF16)
        acc_ref[...] = x_ref[...]

    h = h_ref[...]
    gate = _dot(h, wg_ref[...])
    up = _dot(h, wu_ref[...])
    act = (gate * jax.nn.sigmoid(gate) * up).astype(BF16)
    acc_ref[...] += _dot(act, wd_ref[...])

    @pl.when(c == pl.num_programs(1) - 1)
    def _():
        y = acc_ref[...]
        o_ref[...] = _rms(y, gf_ref[...]) if final else y


def _ffn(x, g, w_gate_up, w_down, g_final, final):
    tm, tc = FFN_TM, FFN_TC
    t, d = x.shape
    nc = FFN_HIDDEN // tc
    wgu = w_gate_up.astype(BF16)
    return pl.pallas_call(
        functools.partial(_ffn_kernel, final=final),
        out_shape=jax.ShapeDtypeStruct((t, d), F32),
        grid=(t // tm, nc),
        in_specs=[pl.BlockSpec((tm, d), lambda i, c: (i, 0)),
                  pl.BlockSpec((1, d), lambda i, c: (0, 0)),
                  pl.BlockSpec((d, tc), lambda i, c: (0, c)),
                  pl.BlockSpec((d, tc), lambda i, c: (0, nc + c)),
                  pl.BlockSpec((tc, d), lambda i, c: (c, 0)),
                  pl.BlockSpec((1, d), lambda i, c: (0, 0))],
        out_specs=pl.BlockSpec((tm, d), lambda i, c: (i, 0)),
        scratch_shapes=[pltpu.VMEM((tm, d), BF16), pltpu.VMEM((tm, d), F32)],
        compiler_params=_cparams(("parallel", "arbitrary")),
        name="ffn",
    )(x, g[None], wgu, wgu, w_down.astype(BF16), g_final[None])


def _split_w_in(w):
    k_rope = w[:, O_KR:O_KR + QK_ROPE]
    kv = lambda kind: w[:, O_KV + GD * kind:O_KV + GD * (kind + 1)]
    pad = jnp.zeros((w.shape[0], Z_COLS - Z_GN - 3 * NSA_HEADS), w.dtype)
    wz = jnp.concatenate([
        w[:, O_GM:O_GM + 3 * D_MODEL],
        w[:, O_GLU:O_GLU + 2 * CONV_CH],
        w[:, O_Q:O_Q + NSA_HEADS * NSA_DH],
        w[:, O_CKV:O_CKV + KV_RANK],
        k_rope, _swap_halves(k_rope),
        w[:, O_CQ:O_CQ + Q_RANK],
        kv(0), kv(1),
        w[:, O_GN:O_GN + 3 * NSA_HEADS], pad], axis=1).astype(BF16)
    z64 = jnp.zeros((w.shape[0], NSA_DH), w.dtype)
    wk = jnp.concatenate([kv(2)[:, :NSA_DH], z64, kv(2)[:, NSA_DH:], z64, kv(4)], axis=1).astype(BF16)
    wvt = jnp.concatenate([kv(3), kv(5)], axis=1).T.astype(BF16)
    return wz, wk, wvt


def kernel(x, mem, positions, rel_bias, norm_mix, norm_xattn, norm_mem, norm_ffn, norm_final, w_in, conv_w, conv_b, conv_ln_g, conv_ln_b, w_branch_conv, cmp_pos_k, cmp_w1_k, cmp_b1_k, cmp_w2_k, cmp_pos_v, cmp_w1_v, cmp_b1_v, cmp_w2_v, w_branch_nsa, mla_norm_q, mla_norm_kv, w_uq, w_ukv, w_branch_mla, w_out, w_xq, w_xkv, w_xo, w_gate_up, w_down):
    bsz, seq, d = x.shape
    depth = w_in.shape[0]
    t = bsz * seq
    m_len = mem.shape[1]
    xt = x.reshape(t, d)
    memt = mem.reshape(bsz * m_len, d)
    rope_tab = _rope_table(positions)
    tables = _nsa_tables(rel_bias)
    for l in range(depth):
        wz, wk, wvt = _split_w_in(w_in[l])
        z, zc = _in_proj(xt, norm_mix[l][None], wz)
        kk, vvt = _kv_proj(xt, norm_mix[l][None], wk, wvt, bsz, seq)
        ya = _conv_module(z, conv_w[l], conv_b[l], conv_ln_g[l], conv_ln_b[l], bsz, seq)
        kc, vct = _compress(zc, jnp.stack([cmp_pos_k[l], cmp_pos_v[l]]), jnp.stack([cmp_w1_k[l], cmp_w1_v[l]]),
                            jnp.stack([cmp_b1_k[l], cmp_b1_v[l]]), jnp.stack([cmp_w2_k[l], cmp_w2_v[l]]), bsz, seq)
        yb = _nsa_attention(z, kc, vct, kk, vvt, tables, bsz, seq)
        qf, kf, vt = _mla_proj(z, rope_tab, mla_norm_q[l], mla_norm_kv[l], w_uq[l], w_ukv[l], bsz, seq)
        yc = _mla_attention(qf, kf, vt, bsz, seq)
        mem_kv = _norm_matmul(memt, norm_mem[l][None], w_xkv[l].astype(BF16), 256, 1024, BF16)
        mem_kv = mem_kv.reshape(bsz, m_len, 2 * XATTN_HEADS * XATTN_DH)
        xt = _merge_xattn(ya, yb, yc, z, xt, w_branch_conv[l], w_branch_nsa[l], w_branch_mla[l], w_out[l],
                          norm_xattn[l], w_xq[l], mem_kv, w_xo[l], bsz, seq)
        xt = _ffn(xt, norm_ffn[l], w_gate_up[l], w_down[l], norm_final, l == depth - 1)
    return xt.reshape(bsz, seq, d)
```

```python
import functools
import math

import numpy as np
import jax
import jax.numpy as jnp
from jax import lax
from jax.experimental import pallas as pl
from jax.experimental.pallas import tpu as pltpu

F32 = jnp.float32
BF16 = jnp.bfloat16

EPS = 1e-6
NEG_INF = -1e30
FORCE_SCORE = 1e4

D_MODEL = 1024
CONV_CH = 512
CONV_WIDTH = 31
NSA_HEADS = 8
NSA_G = 2
NSA_HG = NSA_HEADS // NSA_G
NSA_DH = 64
CMP_BLOCK = 32
CMP_STRIDE = 16
CMP_HIDDEN = 256
SLC_BLOCK = 64
N_SELECT = 16
WINDOW = 512
NSA_QB = 64
MLA_HEADS = 4
Q_RANK = 384
KV_RANK = 256
QK_NOPE = 128
QK_ROPE = 64
V_DIM = 128
ROPE_THETA = 10000.0
REL_BUCKETS = 32
REL_MAX_DIST = 128
XATTN_HEADS = 4
XATTN_DH = 128
FFN_HIDDEN = 2816

LANES = 128
SUBLANES = 8

O_GLU, O_Q, O_KV, O_GN, O_CQ, O_CKV, O_KR, O_GM = 0, 1024, 1536, 2304, 2328, 2712, 2968, 3032
GD = NSA_G * NSA_DH

Z_GM = 0
Z_UA = 3072
Z_UB = 3584
Z_Q = 4096
Z_CKV = 4608
Z_KR = 4864
Z_CQ = 4992
Z_CMP = 5376
Z_GN = 5632
Z_COLS = 5760

VMEM_LIMIT = 56 * 1024 * 1024

TOK_TILE = 512
IN_PROJ_TM = 1024
IN_PROJ_TN = 1152
FFN_TM = 1024
FFN_TC = 256
MLA_TQ = 512
MLA_TK = 1024
MLA_HEADS_PER_STEP = 2

LOG2E = math.log2(math.e)


def _cparams(sem):
    return pltpu.CompilerParams(dimension_semantics=sem, vmem_limit_bytes=VMEM_LIMIT)


def _rms(x, g):
    return x * lax.rsqrt(jnp.mean(x * x, axis=-1, keepdims=True) + EPS) * g


def _dot(a, b):
    return jnp.dot(a, b, preferred_element_type=F32)


def _dot_nt(a, b):
    return lax.dot_general(a, b, (((1,), (1,)), ((), ())), preferred_element_type=F32)


def _norm_matmul_kernel(x_ref, g_ref, w_ref, o_ref, h_ref):
    @pl.when(pl.program_id(1) == 0)
    def _():
        h_ref[...] = _rms(x_ref[...], g_ref[...]).astype(BF16)

    o_ref[...] = _dot(h_ref[...], w_ref[...]).astype(o_ref.dtype)


def _norm_matmul(x, g, w, tm, tn, out_dtype):
    m, k = x.shape
    n = w.shape[1]
    return pl.pallas_call(
        _norm_matmul_kernel,
        out_shape=jax.ShapeDtypeStruct((m, n), out_dtype),
        grid=(m // tm, n // tn),
        in_specs=[pl.BlockSpec((tm, k), lambda i, j: (i, 0)),
                  pl.BlockSpec((1, k), lambda i, j: (0, 0)),
                  pl.BlockSpec((k, tn), lambda i, j: (0, j))],
        out_specs=pl.BlockSpec((tm, tn), lambda i, j: (i, j)),
        scratch_shapes=[pltpu.VMEM((tm, k), BF16)],
        compiler_params=_cparams(("parallel", "arbitrary")),
        name="norm_matmul",
    )(x, g, w)


def _in_proj_kernel(x_ref, g_ref, w_ref, z_ref, zc_ref, h_ref, *, cmp_tile, cmp_off):
    @pl.when(pl.program_id(1) == 0)
    def _():
        h_ref[...] = _rms(x_ref[...], g_ref[...]).astype(BF16)

    acc = _dot(h_ref[...], w_ref[...])
    z_ref[...] = acc.astype(BF16)

    @pl.when(pl.program_id(1) == cmp_tile)
    def _():
        zc_ref[...] = acc[:, cmp_off:cmp_off + 2 * GD]


def _in_proj(x, g, w):
    tm, tn = IN_PROJ_TM, IN_PROJ_TN
    m, k = x.shape
    n = w.shape[1]
    return pl.pallas_call(
        functools.partial(_in_proj_kernel, cmp_tile=Z_CMP // tn, cmp_off=Z_CMP % tn),
        out_shape=(jax.ShapeDtypeStruct((m, n), BF16), jax.ShapeDtypeStruct((m, 2 * GD), F32)),
        grid=(m // tm, n // tn),
        in_specs=[pl.BlockSpec((tm, k), lambda i, j: (i, 0)),
                  pl.BlockSpec((1, k), lambda i, j: (0, 0)),
                  pl.BlockSpec((k, tn), lambda i, j: (0, j))],
        out_specs=(pl.BlockSpec((tm, tn), lambda i, j: (i, j)),
                   pl.BlockSpec((tm, 2 * GD), lambda i, j: (i, 0))),
        scratch_shapes=[pltpu.VMEM((tm, k), BF16)],
        compiler_params=_cparams(("parallel", "arbitrary")),
        name="in_proj",
    )(x, g, w)


def _kv_proj_kernel(x_ref, g_ref, wk_ref, wvt_ref, k_ref, vt_ref):
    @pl.when(pl.program_id(1) == 0)
    def _():
        k_ref[...] = jnp.zeros(k_ref.shape, BF16)
        vt_ref[...] = jnp.zeros(vt_ref.shape, BF16)

    @pl.when(pl.program_id(1) > 0)
    def _():
        h = _rms(x_ref[...], g_ref[...]).astype(BF16)
        k = _dot(h, wk_ref[...])
        tm = k.shape[0]
        row = lax.broadcasted_iota(jnp.int32, k.shape, 0)
        lane = lax.broadcasted_iota(jnp.int32, k.shape, 1)
        blk = (pl.program_id(1) - 1) * (tm // SLC_BLOCK) + lax.shift_right_logical(row, 6)
        hot = (lane < NSA_G * GD) & (lax.bitwise_and(lane, GD - 1) == blk + NSA_DH)
        k_ref[0] = jnp.where(hot, 1.0, k).astype(BF16)
        vt_ref[0] = _dot_nt(wvt_ref[...], h).astype(BF16)


def _kv_proj(x, g, wk, wvt, bsz, seq):
    tm = WINDOW
    nst = seq // tm
    d = x.shape[1]
    nk = wk.shape[1]
    nv = wvt.shape[0]
    return pl.pallas_call(
        _kv_proj_kernel,
        out_shape=(jax.ShapeDtypeStruct((bsz, seq + tm, nk), BF16),
                   jax.ShapeDtypeStruct((bsz, nv, seq + tm), BF16)),
        grid=(bsz, nst + 1),
        in_specs=[pl.BlockSpec((tm, d), lambda b, s: (b * nst + jnp.maximum(s - 1, 0), 0)),
                  pl.BlockSpec((1, d), lambda b, s: (0, 0)),
                  pl.BlockSpec((d, nk), lambda b, s: (0, 0)),
                  pl.BlockSpec((nv, d), lambda b, s: (0, 0))],
        out_specs=(pl.BlockSpec((1, tm, nk), lambda b, s: (b, s, 0)),
                   pl.BlockSpec((1, nv, tm), lambda b, s: (b, 0, s))),
        compiler_params=_cparams(("parallel", "arbitrary")),
        name="nsa_kv_proj",
    )(x, g, wk, wvt)


CONV_HALO = 32


CONV_ROWS = 64


def _conv_kernel(a_ref, b_ref, w_ref, cb_ref, lg_ref, lb_ref, o_ref, buf_ref, sh_ref, *, ts):
    @pl.when(pl.program_id(1) == 0)
    def _():
        buf_ref[0:CONV_HALO, :] = jnp.zeros((CONV_HALO, CONV_CH), F32)

    buf_ref[CONV_HALO:CONV_HALO + ts, :] = a_ref[...].astype(F32) * jax.nn.sigmoid(b_ref[...].astype(F32))
    span = ts + CONV_HALO - SUBLANES
    for r in range(1, SUBLANES):
        sh_ref[r - 1, 0:span, :] = buf_ref[r:r + span, :]
    off = CONV_HALO - (CONV_WIDTH - 1)

    def rows(i, carry):
        r0 = pl.multiple_of(i * CONV_ROWS, CONV_ROWS)
        acc = jnp.zeros((CONV_ROWS, CONV_CH), F32) + cb_ref[...]
        for k in range(CONV_WIDTH):
            res, base = (off + k) % SUBLANES, (off + k) // SUBLANES * SUBLANES
            if res == 0:
                tap = buf_ref[pl.ds(r0 + base, CONV_ROWS), :]
            else:
                tap = sh_ref[res - 1, pl.ds(r0 + base, CONV_ROWS), :]
            acc = acc + tap * w_ref[k:k + 1, :]
        mu = jnp.mean(acc, axis=-1, keepdims=True)
        xc = acc - mu
        var = jnp.mean(xc * xc, axis=-1, keepdims=True)
        y = xc * lax.rsqrt(var + EPS) * lg_ref[...] + lb_ref[...]
        o_ref[pl.ds(r0, CONV_ROWS), :] = (y * jax.nn.sigmoid(y)).astype(BF16)
        return carry

    lax.fori_loop(0, ts // CONV_ROWS, rows, 0)
    buf_ref[0:CONV_HALO, :] = buf_ref[ts:ts + CONV_HALO, :]


def _conv_module(z, conv_w, conv_b, ln_g, ln_b, bsz, seq):
    ts = TOK_TILE
    nst = seq // ts
    wpad = jnp.zeros((32, CONV_CH), F32).at[:CONV_WIDTH].set(conv_w)
    return pl.pallas_call(
        functools.partial(_conv_kernel, ts=ts),
        out_shape=jax.ShapeDtypeStruct((bsz * seq, CONV_CH), BF16),
        grid=(bsz, nst),
        in_specs=[pl.BlockSpec((ts, CONV_CH), lambda b, s: (b * nst + s, Z_UA // CONV_CH)),
                  pl.BlockSpec((ts, CONV_CH), lambda b, s: (b * nst + s, Z_UB // CONV_CH)),
                  pl.BlockSpec((32, CONV_CH), lambda b, s: (0, 0)),
                  pl.BlockSpec((1, CONV_CH), lambda b, s: (0, 0)),
                  pl.BlockSpec((1, CONV_CH), lambda b, s: (0, 0)),
                  pl.BlockSpec((1, CONV_CH), lambda b, s: (0, 0))],
        out_specs=pl.BlockSpec((ts, CONV_CH), lambda b, s: (b * nst + s, 0)),
        scratch_shapes=[pltpu.VMEM((ts + CONV_HALO, CONV_CH), F32),
                        pltpu.VMEM((SUBLANES - 1, ts + CONV_HALO - SUBLANES, CONV_CH), F32)],
        compiler_params=_cparams(("arbitrary", "arbitrary")),
        name="conv_module",
    )(z, z, wpad, conv_b[None], ln_g[None], ln_b[None])


def _compress_kernel(xk_ref, xv_ref, pos_ref, w1_ref, b1_ref, w2k_ref, w2v_ref, kc_ref, vct_ref, *, nch):
    for kind, (x_ref, w2_ref) in enumerate(((xk_ref, w2k_ref), (xv_ref, w2v_ref))):
        a = jnp.zeros((nch, NSA_G * CMP_HIDDEN), F32)
        b = jnp.zeros((nch, NSA_G * CMP_HIDDEN), F32)
        for l in range(CMP_STRIDE):
            xs = x_ref[pl.ds(l, nch, stride=CMP_STRIDE), :]
            a = a + _dot((xs + pos_ref[kind, l:l + 1, :]).astype(BF16), w1_ref[kind, l])
            b = b + _dot((xs + pos_ref[kind, CMP_STRIDE + l:CMP_STRIDE + l + 1, :]).astype(BF16),
                         w1_ref[kind, CMP_STRIDE + l])
        pre = a + pltpu.roll(b, nch - 1, 0) + b1_ref[kind]
        out = _dot(jax.nn.gelu(pre).astype(BF16), w2_ref[...])
        if kind == 0:
            kc_ref[0] = out.astype(BF16)
        else:
            vct_ref[0] = out.T.astype(BF16)


def _blockdiag2(w):
    z = jnp.zeros_like(w)
    return jnp.concatenate([jnp.concatenate([w, z], axis=-1), jnp.concatenate([z, w], axis=-1)], axis=-2)


def _compress(z, pos, w1, b1, w2, bsz, seq):
    nch = seq // CMP_STRIDE
    pos2 = jnp.concatenate([pos, pos], axis=-1)
    w1e = _blockdiag2(w1.reshape(2, CMP_BLOCK, NSA_DH, CMP_HIDDEN)).astype(BF16)
    b1e = jnp.concatenate([b1, b1], axis=-1)[:, None]
    w2k = _blockdiag2(w2[0]).astype(BF16)
    w2v = _blockdiag2(w2[1]).astype(BF16)
    full = lambda a: pl.BlockSpec(a.shape, lambda b: (0,) * a.ndim)
    return pl.pallas_call(
        functools.partial(_compress_kernel, nch=nch),
        out_shape=(jax.ShapeDtypeStruct((bsz, nch, GD), BF16),
                   jax.ShapeDtypeStruct((bsz, GD, nch), BF16)),
        grid=(bsz,),
        in_specs=[pl.BlockSpec((seq, GD), lambda b: (b, 0)),
                  pl.BlockSpec((seq, GD), lambda b: (b, 1)),
                  full(pos2), full(w1e), full(b1e), full(w2k), full(w2v)],
        out_specs=(pl.BlockSpec((1, nch, GD), lambda b: (b, 0, 0)),
                   pl.BlockSpec((1, GD, nch), lambda b: (b, 0, 0))),
        compiler_params=_cparams(("parallel",)),
        name="nsa_compress",
    )(z, z, pos2, w1e, b1e, w2k, w2v)


NSA_QP = 4 * NSA_QB
STEP_BLOCKS = NSA_QP // SLC_BLOCK
NEAR_BACK = 2 * SLC_BLOCK
NEAR_KEYS = NSA_QP + NEAR_BACK
WIN_KEYS = WINDOW + NSA_QP
CMP_TAB_ROWS = 512
CMP_TAB_ZERO = 256


def _t5_bucket_np(d):
    exact = REL_BUCKETS // 2
    d = np.maximum(d, 0)
    ratio = np.log(np.maximum(d, 1).astype(np.float32) / np.float32(exact)) / np.float32(math.log(REL_MAX_DIST / exact))
    large = np.minimum(exact + (ratio * (REL_BUCKETS - exact)).astype(np.int32), REL_BUCKETS - 1)
    return np.where(d < exact, d, large).astype(np.int32)


def _bucket_thresholds():
    exact = REL_BUCKETS // 2
    bk = _t5_bucket_np(np.arange(4 * REL_MAX_DIST))
    assert np.all(np.diff(bk) >= 0) and bk[-1] == REL_BUCKETS - 1
    return [int(np.argmax(bk >= k)) for k in range(exact + 1, REL_BUCKETS)]


def _bias_rows(rel_ref, dist, valid, shift):
    exact = REL_BUCKETS // 2
    bucket = jnp.full(dist.shape, exact, jnp.int32)
    for thr in _bucket_thresholds():
        bucket = bucket + jnp.where(dist >= thr, 1, 0)
    bucket = jnp.where(dist < exact, dist, bucket)
    val = jnp.zeros(dist.shape, F32)
    for bkt in range(REL_BUCKETS):
        val = jnp.where(bucket == bkt, rel_ref[0, bkt:bkt + 1, :], val)
    if shift:
        val = val - rel_ref[0, REL_BUCKETS - 1:REL_BUCKETS, :]
    return jnp.where(valid, val * LOG2E, NEG_INF)


def _nsa_bias_kernel(rel_ref, tc_ref, tn_ref, tw_ref):
    hq = NSA_HG * NSA_QP
    rows = 128

    def dist_of(nrows, r0, fn):
        r = r0 + lax.broadcasted_iota(jnp.int32, (nrows, hq), 0)
        t = lax.bitwise_and(lax.broadcasted_iota(jnp.int32, (nrows, hq), 1), NSA_QP - 1)
        return fn(r, t)

    for r0 in range(0, CMP_TAB_ROWS, rows):
        d = dist_of(rows, r0, lambda r, t: t - CMP_STRIDE * (r - CMP_TAB_ZERO) - (CMP_BLOCK - 1))
        tc_ref[0, r0:r0 + rows, :] = _bias_rows(rel_ref, d, d >= 0, False)
    for r0 in range(0, NEAR_KEYS, rows):
        d = dist_of(rows, r0, lambda r, t: NEAR_BACK + t - r)
        tn_ref[0, r0:r0 + rows, :] = _bias_rows(rel_ref, d, d >= 0, True)
    for r0 in range(0, WIN_KEYS, rows):
        d = dist_of(rows, r0, lambda r, t: WINDOW + t - r)
        tw_ref[0, r0:r0 + rows, :] = _bias_rows(rel_ref, d, (d >= 0) & (d < WINDOW), False)


def _nsa_tables(rel_bias):
    hq = NSA_HG * NSA_QP
    rel4 = jnp.repeat(rel_bias.reshape(REL_BUCKETS, NSA_G, NSA_HG).transpose(1, 0, 2), NSA_QP, axis=-1)
    spec = lambda r: pl.BlockSpec((1, r, hq), lambda g: (g, 0, 0))
    return pl.pallas_call(
        _nsa_bias_kernel,
        out_shape=(jax.ShapeDtypeStruct((NSA_G, CMP_TAB_ROWS, hq), F32),
                   jax.ShapeDtypeStruct((NSA_G, NEAR_KEYS, hq), F32),
                   jax.ShapeDtypeStruct((NSA_G, WIN_KEYS, hq), F32)),
        grid=(NSA_G,),
        in_specs=[spec(REL_BUCKETS)],
        out_specs=(spec(CMP_TAB_ROWS), spec(NEAR_KEYS), spec(WIN_KEYS)),
        compiler_params=_cparams(("parallel",)),
        name="nsa_bias_tables",
    )(rel4)


FAR_KEYS = 1024
KV_FRONT = WINDOW
KREP = NSA_HG * NSA_DH


def _softmax_cols(s):
    m = jnp.max(s, axis=0, keepdims=True)
    p = jnp.exp2(s - m)
    return m, p, jnp.sum(p, axis=0, keepdims=True)


def _rank_select(score_ref, n_sb, n_sel):
    groups = n_sb // SUBLANES
    sub = lax.broadcasted_iota(jnp.int32, (SUBLANES, NSA_QP), 0)
    tiles = [score_ref[SUBLANES * v:SUBLANES * (v + 1), :] for v in range(groups)]
    cnts = [jnp.zeros((SUBLANES, NSA_QP), F32) for _ in range(groups)]
    for jp in range(n_sb):
        row = score_ref[jp:jp + 1, :]
        for v in range(groups):
            lo = SUBLANES * v
            if jp < lo:
                beats = row >= tiles[v]
            elif jp >= lo + SUBLANES - 1:
                beats = row > tiles[v]
            else:
                beats = (row > tiles[v]) | ((row == tiles[v]) & (sub > jp - lo))
            cnts[v] = cnts[v] + jnp.where(beats, 1.0, 0.0)
    cnt = jnp.concatenate(cnts, axis=0)
    return jnp.where(cnt < float(n_sel), 1.0, 0.0)


def _nsa_kernel(q_ref, gate_ref, kc_ref, vct_ref, ks0_ref, ks1_ref, kw_ref, vst_ref, vwt_ref,
                tc_ref, tn_ref, tw_ref, cov_ref, o_ref, score_ref, *, n_sb):
    p2 = pl.program_id(1)
    hq = NSA_HG * NSA_QP
    nch = kc_ref.shape[1]
    groups = range(NSA_G)
    qcol = lambda g: slice(KREP * g, KREP * (g + 1))
    vrow = lambda g: slice(NSA_DH * g, NSA_DH * (g + 1))

    ks_refs = (ks0_ref, ks1_ref)
    qs, q64s = [], []
    zq = jnp.zeros((hq, NSA_DH), BF16)
    for g in groups:
        qb = (q_ref[:, qcol(g)].astype(F32) * (NSA_DH ** -0.5 * LOG2E)).astype(BF16)
        q64 = jnp.concatenate([qb[:, NSA_DH * h:NSA_DH * (h + 1)] for h in range(NSA_HG)], axis=0)
        q64s.append(q64)
        qs.append(jnp.concatenate([q64, zq] if g == 0 else [zq, q64], axis=1))

    start_c = pl.multiple_of(CMP_TAB_ZERO - (NSA_QP // CMP_STRIDE) * p2, SUBLANES)
    lane = lax.broadcasted_iota(jnp.int32, (1, hq), 1)
    tq = NSA_QP * p2 + lax.bitwise_and(lane, NSA_QP - 1)
    anyv = jnp.where(tq >= CMP_BLOCK - 1, 1.0, 0.0)
    jrow = lax.broadcasted_iota(jnp.int32, (n_sb, NSA_QP), 0)
    tok = lax.broadcasted_iota(jnp.int32, (n_sb, NSA_QP), 1)
    cur = STEP_BLOCKS * p2 + lax.shift_right_logical(tok, 6)
    forced = (jrow == 0) | (jrow == cur) | (jrow == cur - 1)
    o_cmp = []
    for g in groups:
        sc = _dot_nt(kc_ref[0], qs[g]) + tc_ref[g, pl.ds(start_c, nch), :]
        _, pc, lc = _softmax_cols(sc)
        pc = pc * (anyv / lc)
        o_cmp.append(_dot(vct_ref[0, vrow(g), :], pc.astype(BF16)))
        psum = pc[:, 0:NSA_QP]
        for h in range(1, NSA_HG):
            psum = psum + pc[:, NSA_QP * h:NSA_QP * (h + 1)]
        p_hi = psum.astype(BF16)
        p_lo = (psum - p_hi.astype(F32)).astype(BF16)
        imp = _dot(cov_ref[...], p_hi) + _dot(cov_ref[...], p_lo)
        score_ref[g] = jnp.where(forced, FORCE_SCORE, jnp.where(jrow <= cur, imp, -1.0))

    def mask_operand(g, keep):
        mb = ((keep - 1.0) * -NEG_INF).T.astype(BF16)
        if n_sb < NSA_DH:
            mb = jnp.concatenate([mb, jnp.zeros((NSA_QP, NSA_DH - n_sb), BF16)], axis=1)
        return jnp.concatenate([q64s[g], jnp.concatenate([mb] * NSA_HG, axis=0)], axis=1)

    q_near, q_far = [], []
    for g in groups:
        sel = _rank_select(score_ref.at[g], n_sb, min(N_SELECT, n_sb))
        q_near.append(mask_operand(g, sel))
        far_blocks = STEP_BLOCKS * p2 - NEAR_BACK // SLC_BLOCK
        q_far.append(mask_operand(g, jnp.where(jrow < far_blocks, sel, 0.0)))

    win0 = pl.multiple_of(NSA_QP * p2, LANES)
    near0 = pl.multiple_of(win0 + KV_FRONT - NEAR_BACK, LANES)
    state = []
    for g in groups:
        s = _dot_nt(ks_refs[g][0, pl.ds(near0, NEAR_KEYS), :], q_near[g]) + tn_ref[g]
        s = jnp.concatenate([jnp.where(p2 > 0, s[0:NEAR_BACK], NEG_INF), s[NEAR_BACK:]], axis=0)
        m_s, p_s, l_s = _softmax_cols(s)
        state += [m_s, l_s, _dot(vst_ref[0, vrow(g), pl.ds(near0, NEAR_KEYS)], p_s.astype(BF16))]

    window = []
    for g in groups:
        sw = _dot_nt(kw_ref[0, pl.ds(win0, WIN_KEYS), :], qs[g]) + tw_ref[g]
        slabs = [sw[NSA_QP * j:NSA_QP * (j + 1)] for j in range(WIN_KEYS // NSA_QP)]
        for j in range(KV_FRONT // NSA_QP):
            slabs[j] = jnp.where(NSA_QP * j + win0 >= KV_FRONT, slabs[j], NEG_INF)
        _, p_w, l_w = _softmax_cols(jnp.concatenate(slabs, axis=0))
        window.append((_dot(vwt_ref[0, vrow(g), pl.ds(win0, WIN_KEYS)], p_w.astype(BF16)), l_w))

    def far_body(c, carry):
        k0 = pl.multiple_of(FAR_KEYS * c + KV_FRONT, LANES)
        sfs = [_dot_nt(ks_refs[g][0, pl.ds(k0, FAR_KEYS), :], q_far[g]) for g in groups]
        out = []
        for g in groups:
            m_old, l_old, acc_old = carry[3 * g:3 * g + 3]
            sf = sfs[g]
            m_new = jnp.maximum(m_old, jnp.max(sf, axis=0, keepdims=True))
            alpha = jnp.exp2(m_old - m_new)
            pf = jnp.exp2(sf - m_new)
            l_new = alpha * l_old + jnp.sum(pf, axis=0, keepdims=True)
            acc_new = alpha * acc_old + _dot(vst_ref[0, vrow(g), pl.ds(k0, FAR_KEYS)], pf.astype(BF16))
            out += [m_new, l_new, acc_new]
        return tuple(out)

    n_far = lax.div(jnp.maximum(p2 * NSA_QP - NEAR_BACK, 0) + FAR_KEYS - 1, FAR_KEYS)
    state = lax.fori_loop(0, n_far, far_body, tuple(state))

    gt = gate_ref[...].astype(F32).T
    r = lax.broadcasted_iota(jnp.int32, (NSA_QP, NSA_QP), 0)
    c = lax.broadcasted_iota(jnp.int32, (NSA_QP, NSA_QP), 1)
    eye = jnp.where(r == c, 1.0, 0.0).astype(BF16)
    for g in groups:
        _, l_s, acc_s = state[3 * g:3 * g + 3]
        acc_w, l_w = window[g]
        gsel = jax.nn.sigmoid(gt[3 * NSA_HG * g:3 * NSA_HG * (g + 1)])
        gate = lambda b: jnp.concatenate([gsel[3 * h + b:3 * h + b + 1] for h in range(NSA_HG)], axis=1)
        out_t = (gate(0) * o_cmp[g] + (gate(1) / l_s) * acc_s + (gate(2) / l_w) * acc_w).astype(BF16)
        stacked = jnp.concatenate([out_t[:, NSA_QP * h:NSA_QP * (h + 1)] for h in range(NSA_HG)], axis=0)
        o_ref[:, qcol(g)] = _dot_nt(eye, stacked).astype(BF16)


def _nsa_attention(z, kc, vct, kk, vvt, tables, bsz, seq):
    g, hg, dh, qp = NSA_G, NSA_HG, NSA_DH, NSA_QP
    nstep = seq // qp
    n_sb = seq // SLC_BLOCK
    nch = kc.shape[1]
    hq = hg * qp
    sp = kk.shape[1]
    tc, tn, tw = tables
    c_start = CMP_STRIDE * np.arange(nch)
    s_start = SLC_BLOCK * np.arange(n_sb)
    cover_t = ((c_start[None, :] < s_start[:, None] + SLC_BLOCK)
               & (c_start[None, :] + CMP_BLOCK > s_start[:, None])
               & (np.arange(nch)[None, :] < (seq - CMP_BLOCK) // CMP_STRIDE + 1))
    cover_t = jnp.asarray(cover_t.astype(np.float32), BF16)
    full = lambda a: pl.BlockSpec(a.shape, lambda b, i: (0,) * a.ndim, pipeline_mode=pl.Buffered(1))
    qd = g * hg * dh
    return pl.pallas_call(
        functools.partial(_nsa_kernel, n_sb=n_sb),
        out_shape=jax.ShapeDtypeStruct((bsz * seq, qd), BF16),
        grid=(bsz, nstep),
        in_specs=[pl.BlockSpec((qp, qd), lambda b, i: (b * nstep + i, Z_Q // qd)),
                  pl.BlockSpec((qp, LANES), lambda b, i: (b * nstep + i, Z_GN // LANES)),
                  pl.BlockSpec((1, nch, GD), lambda b, i: (b, 0, 0)),
                  pl.BlockSpec((1, g * dh, nch), lambda b, i: (b, 0, 0)),
                  pl.BlockSpec((1, sp, GD), lambda b, i: (b, 0, 0)),
                  pl.BlockSpec((1, sp, GD), lambda b, i: (b, 0, 1)),
                  pl.BlockSpec((1, sp, GD), lambda b, i: (b, 0, 2)),
                  pl.BlockSpec((1, g * dh, sp), lambda b, i: (b, 0, 0)),
                  pl.BlockSpec((1, g * dh, sp), lambda b, i: (b, 1, 0)),
                  full(tc), full(tn), full(tw), full(cover_t)],
        out_specs=pl.BlockSpec((qp, qd), lambda b, i: (b * nstep + i, 0)),
        scratch_shapes=[pltpu.VMEM((g, n_sb, qp), F32)],
        compiler_params=_cparams(("parallel", "arbitrary")),
        name="nsa_attention",
    )(z, z, kc, vct, kk, kk, kk, vvt, vvt, tc, tn, tw, cover_t)


def _rope_table_kernel(pos_ref, inv_ref, o_ref):
    ang = inv_ref[...] * pos_ref[0].astype(F32)
    c, s = jnp.cos(ang), jnp.sin(ang)
    o_ref[...] = jnp.concatenate([c, c, -s, s], axis=0).T


def _rope_table(positions):
    tm = TOK_TILE
    t = positions.size
    half = QK_ROPE // 2
    inv = (ROPE_THETA ** (-jnp.arange(half, dtype=F32) / half))[:, None]
    return pl.pallas_call(
        _rope_table_kernel,
        out_shape=jax.ShapeDtypeStruct((t, 2 * QK_ROPE), F32),
        grid=(t // tm,),
        in_specs=[pl.BlockSpec((1, 1, tm), lambda i: (i, 0, 0)),
                  pl.BlockSpec((half, 1), lambda i: (0, 0))],
        out_specs=pl.BlockSpec((tm, 2 * QK_ROPE), lambda i: (i, 0)),
        compiler_params=_cparams(("parallel",)),
        name="rope_table",
    )(positions.reshape(t // tm, 1, tm), inv)


MLA_HW = 256


def _mla_proj_kernel(cq_ref, ckv_ref, kr_ref, rope_ref, nq_ref, nkv_ref, wq_ref, wkn_ref, wvt_ref,
                     q_ref, k_ref, vt_ref):
    scale = (QK_NOPE + QK_ROPE) ** -0.5 * LOG2E
    rope = rope_ref[...]
    yq = _dot(_rms(cq_ref[...].astype(F32), nq_ref[...]).astype(BF16), wq_ref[...])
    ckv = _rms(ckv_ref[...].astype(F32), nkv_ref[...]).astype(BF16)
    ykn = _dot(ckv, wkn_ref[...])
    vt_ref[0] = _dot_nt(wvt_ref[...], ckv).astype(BF16)
    kp = kr_ref[...].astype(F32) * rope
    kp = kp + pltpu.roll(kp, QK_ROPE, 1)
    lane = lax.broadcasted_iota(jnp.int32, kp.shape, 1)
    kp = jnp.where(lane < QK_ROPE, kp, 0.0).astype(BF16)
    for h in range(MLA_HEADS):
        base = MLA_HW * h
        q_ref[:, base:base + QK_NOPE] = (yq[:, base:base + QK_NOPE] * scale).astype(BF16)
        qp = yq[:, base + QK_NOPE:base + MLA_HW] * rope
        qp = qp + pltpu.roll(qp, QK_ROPE, 1)
        q_ref[:, base + QK_NOPE:base + MLA_HW] = (qp * scale).astype(BF16)
        k_ref[:, base:base + QK_NOPE] = ykn[:, QK_NOPE * h:QK_NOPE * (h + 1)].astype(BF16)
        k_ref[:, base + QK_NOPE:base + MLA_HW] = kp


def _swap_halves(w):
    half = QK_ROPE // 2
    return jnp.concatenate([w[..., half:], w[..., :half]], axis=-1)


def _mla_proj(z, rope_tab, norm_q, norm_kv, w_uq, w_ukv, bsz, seq):
    tm = TOK_TILE
    t = z.shape[0]
    nst = seq // tm
    wq = w_uq.reshape(Q_RANK, MLA_HEADS, QK_NOPE + QK_ROPE)
    wq = jnp.concatenate([wq, _swap_halves(wq[..., QK_NOPE:])], axis=-1)
    wq = wq.reshape(Q_RANK, MLA_HEADS * MLA_HW).astype(BF16)
    wkv = w_ukv.reshape(KV_RANK, MLA_HEADS, QK_NOPE + V_DIM)
    wkn = wkv[..., :QK_NOPE].reshape(KV_RANK, MLA_HEADS * QK_NOPE).astype(BF16)
    wvt = wkv[..., QK_NOPE:].reshape(KV_RANK, MLA_HEADS * V_DIM).T.astype(BF16)
    hw = MLA_HEADS * MLA_HW
    hv = MLA_HEADS * V_DIM
    row = lambda b, s: b * nst + s
    return pl.pallas_call(
        _mla_proj_kernel,
        out_shape=(jax.ShapeDtypeStruct((t, hw), BF16),
                   jax.ShapeDtypeStruct((t, hw), BF16),
                   jax.ShapeDtypeStruct((bsz, hv, seq), BF16)),
        grid=(bsz, nst),
        in_specs=[pl.BlockSpec((tm, Q_RANK), lambda b, s: (row(b, s), Z_CQ // Q_RANK)),
                  pl.BlockSpec((tm, KV_RANK), lambda b, s: (row(b, s), Z_CKV // KV_RANK)),
                  pl.BlockSpec((tm, 2 * QK_ROPE), lambda b, s: (row(b, s), Z_KR // (2 * QK_ROPE))),
                  pl.BlockSpec((tm, 2 * QK_ROPE), lambda b, s: (row(b, s), 0)),
                  pl.BlockSpec((1, Q_RANK), lambda b, s: (0, 0)),
                  pl.BlockSpec((1, KV_RANK), lambda b, s: (0, 0)),
                  pl.BlockSpec((Q_RANK, hw), lambda b, s: (0, 0)),
                  pl.BlockSpec((KV_RANK, hv), lambda b, s: (0, 0)),
                  pl.BlockSpec((hv, KV_RANK), lambda b, s: (0, 0))],
        out_specs=(pl.BlockSpec((tm, hw), lambda b, s: (row(b, s), 0)),
                   pl.BlockSpec((tm, hw), lambda b, s: (row(b, s), 0)),
                   pl.BlockSpec((1, hv, tm), lambda b, s: (b, 0, s))),
        compiler_params=_cparams(("parallel", "parallel")),
        name="mla_proj",
    )(z, z, z, rope_tab, norm_q[None], norm_kv[None], wq, wkn, wvt)


def _mla_attn_kernel(q_ref, k_ref, vt_ref, o_ref, *, tq, tk, nh):
    iq = pl.program_id(2)
    cd = lax.div(iq * tq, tk)
    heads = range(nh)
    hcol = lambda h: slice(MLA_HW * h, MLA_HW * (h + 1))
    vrow = lambda h: slice(V_DIM * h, V_DIM * (h + 1))
    qs = [q_ref[:, hcol(h)] for h in heads]

    def scores(c, h):
        k0 = pl.multiple_of(c * tk, tk)
        return _dot_nt(k_ref[0, pl.ds(k0, tk), hcol(h)], qs[h])

    def diagonal(nk):
        k0 = pl.multiple_of((iq + 1) * tq - nk, tq)
        kpos = k0 + lax.broadcasted_iota(jnp.int32, (nk, tq), 0)
        qpos = iq * tq + lax.broadcasted_iota(jnp.int32, (nk, tq), 1)
        st = []
        for h in heads:
            s = jnp.where(kpos <= qpos, _dot_nt(k_ref[0, pl.ds(k0, nk), hcol(h)], qs[h]), NEG_INF)
            m0, p0, l0 = _softmax_cols(s)
            st += [m0, l0, _dot(vt_ref[0, vrow(h), pl.ds(k0, nk)], p0.astype(BF16))]
        return tuple(st)

    assert tk == 2 * tq
    state = lax.cond(lax.rem(iq, 2) == 0, lambda: diagonal(tq), lambda: diagonal(tk))

    def body(c, carry):
        k0 = pl.multiple_of(c * tk, tk)
        ss = [scores(c, h) for h in heads]
        out = []
        for h in heads:
            m_old, l_old, acc_old = carry[3 * h:3 * h + 3]
            m_new = jnp.maximum(m_old, jnp.max(ss[h], axis=0, keepdims=True))
            alpha = jnp.exp2(m_old - m_new)
            p = jnp.exp2(ss[h] - m_new)
            l_new = alpha * l_old + jnp.sum(p, axis=0, keepdims=True)
            acc_new = alpha * acc_old + _dot(vt_ref[0, vrow(h), pl.ds(k0, tk)], p.astype(BF16))
            out += [m_new, l_new, acc_new]
        return tuple(out)

    state = lax.fori_loop(0, cd, body, tuple(state))
    for h in heads:
        _, l, acc = state[3 * h:3 * h + 3]
        o_ref[:, vrow(h)] = (acc / l).T.astype(BF16)


def _mla_attention(qf, kf, vt, bsz, seq):
    tq, tk, nh = MLA_TQ, MLA_TK, MLA_HEADS_PER_STEP
    h = MLA_HEADS
    nq = seq // tq
    k3 = kf.reshape(bsz, seq, h * MLA_HW)
    return pl.pallas_call(
        functools.partial(_mla_attn_kernel, tq=tq, tk=tk, nh=nh),
        out_shape=jax.ShapeDtypeStruct((bsz * seq, h * V_DIM), BF16),
        grid=(bsz, h // nh, nq),
        in_specs=[pl.BlockSpec((tq, nh * MLA_HW), lambda b, hh, i: (b * nq + i, hh)),
                  pl.BlockSpec((1, seq, nh * MLA_HW), lambda b, hh, i: (b, 0, hh)),
                  pl.BlockSpec((1, nh * V_DIM, seq), lambda b, hh, i: (b, hh, 0))],
        out_specs=pl.BlockSpec((tq, nh * V_DIM), lambda b, hh, i: (b * nq + i, hh)),
        compiler_params=_cparams(("parallel", "parallel", "arbitrary")),
        name="mla_attention",
    )(qf, k3, vt)


def _merge_xattn_kernel(ya_ref, yb_ref, yc_ref, ga_ref, gb_ref, gc_ref, x_ref,
                        wa_ref, wb_ref, wc_ref, wo_ref,
                        gx_ref, wq_ref, kv_ref, wxo_ref, o_ref):
    sig = lambda ref: jax.nn.sigmoid(ref[...].astype(F32))
    y = (sig(ga_ref) * _dot(ya_ref[...], wa_ref[...])
         + sig(gb_ref) * _dot(yb_ref[...], wb_ref[...])
         + sig(gc_ref) * _dot(yc_ref[...], wc_ref[...]))
    x = x_ref[...] + _dot(y.astype(BF16), wo_ref[...])
    h = _rms(x, gx_ref[...]).astype(BF16)
    q = _dot(h, wq_ref[...]) * XATTN_DH ** -0.5
    hd = XATTN_HEADS * XATTN_DH
    outs = []
    for hh in range(XATTN_HEADS):
        qh = q[:, XATTN_DH * hh:XATTN_DH * (hh + 1)].astype(BF16)
        kh = kv_ref[0, :, XATTN_DH * hh:XATTN_DH * (hh + 1)]
        vh = kv_ref[0, :, hd + XATTN_DH * hh:hd + XATTN_DH * (hh + 1)]
        s = _dot_nt(qh, kh)
        m = jnp.max(s, axis=-1, keepdims=True)
        p = jnp.exp(s - m)
        p = p / jnp.sum(p, axis=-1, keepdims=True)
        outs.append(_dot(p.astype(BF16), vh))
    o = jnp.concatenate(outs, axis=-1).astype(BF16)
    o_ref[...] = x + _dot(o, wxo_ref[...])


def _merge_xattn(ya, yb, yc, z, x, wa, wb, wc, wo, gx, wq, kv, wxo, bsz, seq):
    tm = TOK_TILE
    t, d = x.shape
    nst = seq // tm
    m_len = kv.shape[1]
    hd = XATTN_HEADS * XATTN_DH
    row = lambda b, s: b * nst + s
    act = pl.BlockSpec((tm, ya.shape[1]), lambda b, s: (row(b, s), 0))
    gate = lambda k: pl.BlockSpec((tm, d), lambda b, s: (row(b, s), Z_GM // d + k))
    const = lambda shape: pl.BlockSpec(shape, lambda b, s: (0, 0))
    bf = lambda w: w.astype(BF16)
    return pl.pallas_call(
        _merge_xattn_kernel,
        out_shape=jax.ShapeDtypeStruct((t, d), F32),
        grid=(bsz, nst),
        in_specs=[act, act, act, gate(0), gate(1), gate(2),
                  pl.BlockSpec((tm, d), lambda b, s: (row(b, s), 0)),
                  const((ya.shape[1], d)), const((ya.shape[1], d)), const((ya.shape[1], d)), const((d, d)),
                  const((1, d)), const((d, hd)),
                  pl.BlockSpec((1, m_len, 2 * hd), lambda b, s: (b, 0, 0)),
                  const((hd, d))],
        out_specs=pl.BlockSpec((tm, d), lambda b, s: (row(b, s), 0)),
        compiler_params=_cparams(("parallel", "parallel")),
        name="merge_xattn",
    )(ya, yb, yc, z, z, z, x, bf(wa), bf(wb), bf(wc), bf(wo), gx[None], bf(wq), kv, bf(wxo))


def _ffn_kernel(x_ref, g_ref, wg_ref, wu_ref, wd_ref, gf_ref, o_ref, h_ref, acc_ref, *, final):
    c = pl.program_id(1)

    @pl.when(c == 0)
    def _():
        h_ref[...] = _rms(x_ref[...], g_ref[...]).astype(BF16)
        acc_ref[...] = x_ref[...]

    h = h_ref[...]
    gate = _dot(h, wg_ref[...])
    up = _dot(h, wu_ref[...])
    act = (gate * jax.nn.sigmoid(gate) * up).astype(BF16)
    acc_ref[...] += _dot(act, wd_ref[...])

    @pl.when(c == pl.num_programs(1) - 1)
    def _():
        y = acc_ref[...]
        o_ref[...] = _rms(y, gf_ref[...]) if final else y


def _ffn(x, g, w_gate_up, w_down, g_final, final):
    tm, tc = FFN_TM, FFN_TC
    t, d = x.shape
    nc = FFN_HIDDEN // tc
    wgu = w_gate_up.astype(BF16)
    return pl.pallas_call(
        functools.partial(_ffn_kernel, final=final),
        out_shape=jax.ShapeDtypeStruct((t, d), F32),
        grid=(t // tm, nc),
        in_specs=[pl.BlockSpec((tm, d), lambda i, c: (i, 0)),
                  pl.BlockSpec((1, d), lambda i, c: (0, 0)),
                  pl.BlockSpec((d, tc), lambda i, c: (0, c)),
                  pl.BlockSpec((d, tc), lambda i, c: (0, nc + c)),
                  pl.BlockSpec((tc, d), lambda i, c: (c, 0)),
                  pl.BlockSpec((1, d), lambda i, c: (0, 0))],
        out_specs=pl.BlockSpec((tm, d), lambda i, c: (i, 0)),
        scratch_shapes=[pltpu.VMEM((tm, d), BF16), pltpu.VMEM((tm, d), F32)],
        compiler_params=_cparams(("parallel", "arbitrary")),
        name="ffn",
    )(x, g[None], wgu, wgu, w_down.astype(BF16), g_final[None])


def _split_w_in(w):
    k_rope = w[:, O_KR:O_KR + QK_ROPE]
    kv = lambda kind: w[:, O_KV + GD * kind:O_KV + GD * (kind + 1)]
    pad = jnp.zeros((w.shape[0], Z_COLS - Z_GN - 3 * NSA_HEADS), w.dtype)
    wz = jnp.concatenate([
        w[:, O_GM:O_GM + 3 * D_MODEL],
        w[:, O_GLU:O_GLU + 2 * CONV_CH],
        w[:, O_Q:O_Q + NSA_HEADS * NSA_DH],
        w[:, O_CKV:O_CKV + KV_RANK],
        k_rope, _swap_halves(k_rope),
        w[:, O_CQ:O_CQ + Q_RANK],
        kv(0), kv(1),
        w[:, O_GN:O_GN + 3 * NSA_HEADS], pad], axis=1).astype(BF16)
    z64 = jnp.zeros((w.shape[0], NSA_DH), w.dtype)
    wk = jnp.concatenate([kv(2)[:, :NSA_DH], z64, kv(2)[:, NSA_DH:], z64, kv(4)], axis=1).astype(BF16)
    wvt = jnp.concatenate([kv(3), kv(5)], axis=1).T.astype(BF16)
    return wz, wk, wvt


def kernel(x, mem, positions, rel_bias, norm_mix, norm_xattn, norm_mem, norm_ffn, norm_final, w_in, conv_w, conv_b, conv_ln_g, conv_ln_b, w_branch_conv, cmp_pos_k, cmp_w1_k, cmp_b1_k, cmp_w2_k, cmp_pos_v, cmp_w1_v, cmp_b1_v, cmp_w2_v, w_branch_nsa, mla_norm_q, mla_norm_kv, w_uq, w_ukv, w_branch_mla, w_out, w_xq, w_xkv, w_xo, w_gate_up, w_down):
    bsz, seq, d = x.shape
    depth = w_in.shape[0]
    t = bsz * seq
    m_len = mem.shape[1]
    xt = x.reshape(t, d)
    memt = mem.reshape(bsz * m_len, d)
    rope_tab = _rope_table(positions)
    tables = _nsa_tables(rel_bias)
    for l in range(depth):
        wz, wk, wvt = _split_w_in(w_in[l])
        z, zc = _in_proj(xt, norm_mix[l][None], wz)
        kk, vvt = _kv_proj(xt, norm_mix[l][None], wk, wvt, bsz, seq)
        ya = _conv_module(z, conv_w[l], conv_b[l], conv_ln_g[l], conv_ln_b[l], bsz, seq)
        kc, vct = _compress(zc, jnp.stack([cmp_pos_k[l], cmp_pos_v[l]]), jnp.stack([cmp_w1_k[l], cmp_w1_v[l]]),
                            jnp.stack([cmp_b1_k[l], cmp_b1_v[l]]), jnp.stack([cmp_w2_k[l], cmp_w2_v[l]]), bsz, seq)
        yb = _nsa_attention(z, kc, vct, kk, vvt, tables, bsz, seq)
        qf, kf, vt = _mla_proj(z, rope_tab, mla_norm_q[l], mla_norm_kv[l], w_uq[l], w_ukv[l], bsz, seq)
        yc = _mla_attention(qf, kf, vt, bsz, seq)
        mem_kv = _norm_matmul(memt, norm_mem[l][None], w_xkv[l].astype(BF16), 256, 1024, BF16)
        mem_kv = mem_kv.reshape(bsz, m_len, 2 * XATTN_HEADS * XATTN_DH)
        xt = _merge_xattn(ya, yb, yc, z, xt, w_branch_conv[l], w_branch_nsa[l], w_branch_mla[l], w_out[l],
                          norm_xattn[l], w_xq[l], mem_kv, w_xo[l], bsz, seq)
        xt = _ffn(xt, norm_ffn[l], w_gate_up[l], w_down[l], norm_final, l == depth - 1)
    return xt.reshape(bsz, seq, d)
```

```python
import functools
import math

import numpy as np
import jax
import jax.numpy as jnp
from jax import lax
from jax.experimental import pallas as pl
from jax.experimental.pallas import tpu as pltpu

F32 = jnp.float32
BF16 = jnp.bfloat16

EPS = 1e-6
NEG_INF = -1e30
FORCE_SCORE = 1e4

D_MODEL = 1024
CONV_CH = 512
CONV_WIDTH = 31
NSA_HEADS = 8
NSA_G = 2
NSA_HG = NSA_HEADS // NSA_G
NSA_DH = 64
CMP_BLOCK = 32
CMP_STRIDE = 16
CMP_HIDDEN = 256
SLC_BLOCK = 64
N_SELECT = 16
WINDOW = 512
NSA_QB = 64
MLA_HEADS = 4
Q_RANK = 384
KV_RANK = 256
QK_NOPE = 128
QK_ROPE = 64
V_DIM = 128
ROPE_THETA = 10000.0
REL_BUCKETS = 32
REL_MAX_DIST = 128
XATTN_HEADS = 4
XATTN_DH = 128
FFN_HIDDEN = 2816

LANES = 128
SUBLANES = 8

O_GLU, O_Q, O_KV, O_GN, O_CQ, O_CKV, O_KR, O_GM = 0, 1024, 1536, 2304, 2328, 2712, 2968, 3032
GD = NSA_G * NSA_DH

Z_GM = 0
Z_UA = 3072
Z_UB = 3584
Z_Q = 4096
Z_CKV = 4608
Z_KR = 4864
Z_CQ = 4992
Z_CMP = 5376
Z_GN = 5632
Z_COLS = 5760

VMEM_LIMIT = 56 * 1024 * 1024

TOK_TILE = 512
IN_PROJ_TM = 1024
IN_PROJ_TN = 1152
FFN_TM = 1024
FFN_TC = 256
MLA_TQ = 512
MLA_TK = 1024
MLA_HEADS_PER_STEP = 2

LOG2E = math.log2(math.e)


def _cparams(sem):
    return pltpu.CompilerParams(dimension_semantics=sem, vmem_limit_bytes=VMEM_LIMIT)


def _rms(x, g):
    return x * lax.rsqrt(jnp.mean(x * x, axis=-1, keepdims=True) + EPS) * g


def _dot(a, b):
    return jnp.dot(a, b, preferred_element_type=F32)


def _dot_nt(a, b):
    return lax.dot_general(a, b, (((1,), (1,)), ((), ())), preferred_element_type=F32)


def _norm_matmul_kernel(x_ref, g_ref, w_ref, o_ref, h_ref):
    @pl.when(pl.program_id(1) == 0)
    def _():
        h_ref[...] = _rms(x_ref[...], g_ref[...]).astype(BF16)

    o_ref[...] = _dot(h_ref[...], w_ref[...]).astype(o_ref.dtype)


def _norm_matmul(x, g, w, tm, tn, out_dtype):
    m, k = x.shape
    n = w.shape[1]
    return pl.pallas_call(
        _norm_matmul_kernel,
        out_shape=jax.ShapeDtypeStruct((m, n), out_dtype),
        grid=(m // tm, n // tn),
        in_specs=[pl.BlockSpec((tm, k), lambda i, j: (i, 0)),
                  pl.BlockSpec((1, k), lambda i, j: (0, 0)),
                  pl.BlockSpec((k, tn), lambda i, j: (0, j))],
        out_specs=pl.BlockSpec((tm, tn), lambda i, j: (i, j)),
        scratch_shapes=[pltpu.VMEM((tm, k), BF16)],
        compiler_params=_cparams(("parallel", "arbitrary")),
        name="norm_matmul",
    )(x, g, w)


def _in_proj_kernel(x_ref, g_ref, w_ref, z_ref, zc_ref, h_ref, *, cmp_tile, cmp_off):
    @pl.when(pl.program_id(1) == 0)
    def _():
        h_ref[...] = _rms(x_ref[...], g_ref[...]).astype(BF16)

    acc = _dot(h_ref[...], w_ref[...])
    z_ref[...] = acc.astype(BF16)

    @pl.when(pl.program_id(1) == cmp_tile)
    def _():
        zc_ref[...] = acc[:, cmp_off:cmp_off + 2 * GD]


def _in_proj(x, g, w):
    tm, tn = IN_PROJ_TM, IN_PROJ_TN
    m, k = x.shape
    n = w.shape[1]
    return pl.pallas_call(
        functools.partial(_in_proj_kernel, cmp_tile=Z_CMP // tn, cmp_off=Z_CMP % tn),
        out_shape=(jax.ShapeDtypeStruct((m, n), BF16), jax.ShapeDtypeStruct((m, 2 * GD), F32)),
        grid=(m // tm, n // tn),
        in_specs=[pl.BlockSpec((tm, k), lambda i, j: (i, 0)),
                  pl.BlockSpec((1, k), lambda i, j: (0, 0)),
                  pl.BlockSpec((k, tn), lambda i, j: (0, j))],
        out_specs=(pl.BlockSpec((tm, tn), lambda i, j: (i, j)),
                   pl.BlockSpec((tm, 2 * GD), lambda i, j: (i, 0))),
        scratch_shapes=[pltpu.VMEM((tm, k), BF16)],
        compiler_params=_cparams(("parallel", "arbitrary")),
        name="in_proj",
    )(x, g, w)


def _kv_proj_kernel(x_ref, g_ref, wk_ref, wvt_ref, k_ref, vt_ref):
    @pl.when(pl.program_id(1) == 0)
    def _():
        k_ref[...] = jnp.zeros(k_ref.shape, BF16)
        vt_ref[...] = jnp.zeros(vt_ref.shape, BF16)

    @pl.when(pl.program_id(1) > 0)
    def _():
        h = _rms(x_ref[...], g_ref[...]).astype(BF16)
        k = _dot(h, wk_ref[...])
        tm = k.shape[0]
        row = lax.broadcasted_iota(jnp.int32, k.shape, 0)
        lane = lax.broadcasted_iota(jnp.int32, k.shape, 1)
        blk = (pl.program_id(1) - 1) * (tm // SLC_BLOCK) + lax.shift_right_logical(row, 6)
        hot = (lane < NSA_G * GD) & (lax.bitwise_and(lane, GD - 1) == blk + NSA_DH)
        k_ref[0] = jnp.where(hot, 1.0, k).astype(BF16)
        vt_ref[0] = _dot_nt(wvt_ref[...], h).astype(BF16)


def _kv_proj(x, g, wk, wvt, bsz, seq):
    tm = WINDOW
    nst = seq // tm
    d = x.shape[1]
    nk = wk.shape[1]
    nv = wvt.shape[0]
    return pl.pallas_call(
        _kv_proj_kernel,
        out_shape=(jax.ShapeDtypeStruct((bsz, seq + tm, nk), BF16),
                   jax.ShapeDtypeStruct((bsz, nv, seq + tm), BF16)),
        grid=(bsz, nst + 1),
        in_specs=[pl.BlockSpec((tm, d), lambda b, s: (b * nst + jnp.maximum(s - 1, 0), 0)),
                  pl.BlockSpec((1, d), lambda b, s: (0, 0)),
                  pl.BlockSpec((d, nk), lambda b, s: (0, 0)),
                  pl.BlockSpec((nv, d), lambda b, s: (0, 0))],
        out_specs=(pl.BlockSpec((1, tm, nk), lambda b, s: (b, s, 0)),
                   pl.BlockSpec((1, nv, tm), lambda b, s: (b, 0, s))),
        compiler_params=_cparams(("parallel", "arbitrary")),
        name="nsa_kv_proj",
    )(x, g, wk, wvt)


CONV_HALO = 32


CONV_ROWS = 64


def _conv_kernel(a_ref, b_ref, w_ref, cb_ref, lg_ref, lb_ref, o_ref, buf_ref, sh_ref, *, ts):
    @pl.when(pl.program_id(1) == 0)
    def _():
        buf_ref[0:CONV_HALO, :] = jnp.zeros((CONV_HALO, CONV_CH), F32)

    buf_ref[CONV_HALO:CONV_HALO + ts, :] = a_ref[...].astype(F32) * jax.nn.sigmoid(b_ref[...].astype(F32))
    span = ts + CONV_HALO - SUBLANES
    for r in range(1, SUBLANES):
        sh_ref[r - 1, 0:span, :] = buf_ref[r:r + span, :]
    off = CONV_HALO - (CONV_WIDTH - 1)

    def rows(i, carry):
        r0 = pl.multiple_of(i * CONV_ROWS, CONV_ROWS)
        acc = jnp.zeros((CONV_ROWS, CONV_CH), F32) + cb_ref[...]
        for k in range(CONV_WIDTH):
            res, base = (off + k) % SUBLANES, (off + k) // SUBLANES * SUBLANES
            if res == 0:
                tap = buf_ref[pl.ds(r0 + base, CONV_ROWS), :]
            else:
                tap = sh_ref[res - 1, pl.ds(r0 + base, CONV_ROWS), :]
            acc = acc + tap * w_ref[k:k + 1, :]
        mu = jnp.mean(acc, axis=-1, keepdims=True)
        xc = acc - mu
        var = jnp.mean(xc * xc, axis=-1, keepdims=True)
        y = xc * lax.rsqrt(var + EPS) * lg_ref[...] + lb_ref[...]
        o_ref[pl.ds(r0, CONV_ROWS), :] = (y * jax.nn.sigmoid(y)).astype(BF16)
        return carry

    lax.fori_loop(0, ts // CONV_ROWS, rows, 0)
    buf_ref[0:CONV_HALO, :] = buf_ref[ts:ts + CONV_HALO, :]


def _conv_module(z, conv_w, conv_b, ln_g, ln_b, bsz, seq):
    ts = TOK_TILE
    nst = seq // ts
    wpad = jnp.zeros((32, CONV_CH), F32).at[:CONV_WIDTH].set(conv_w)
    return pl.pallas_call(
        functools.partial(_conv_kernel, ts=ts),
        out_shape=jax.ShapeDtypeStruct((bsz * seq, CONV_CH), BF16),
        grid=(bsz, nst),
        in_specs=[pl.BlockSpec((ts, CONV_CH), lambda b, s: (b * nst + s, Z_UA // CONV_CH)),
                  pl.BlockSpec((ts, CONV_CH), lambda b, s: (b * nst + s, Z_UB // CONV_CH)),
                  pl.BlockSpec((32, CONV_CH), lambda b, s: (0, 0)),
                  pl.BlockSpec((1, CONV_CH), lambda b, s: (0, 0)),
                  pl.BlockSpec((1, CONV_CH), lambda b, s: (0, 0)),
                  pl.BlockSpec((1, CONV_CH), lambda b, s: (0, 0))],
        out_specs=pl.BlockSpec((ts, CONV_CH), lambda b, s: (b * nst + s, 0)),
        scratch_shapes=[pltpu.VMEM((ts + CONV_HALO, CONV_CH), F32),
                        pltpu.VMEM((SUBLANES - 1, ts + CONV_HALO - SUBLANES, CONV_CH), F32)],
        compiler_params=_cparams(("arbitrary", "arbitrary")),
        name="conv_module",
    )(z, z, wpad, conv_b[None], ln_g[None], ln_b[None])


def _compress_kernel(xk_ref, xv_ref, pos_ref, w1_ref, b1_ref, w2k_ref, w2v_ref, kc_ref, vct_ref, *, nch):
    for kind, (x_ref, w2_ref) in enumerate(((xk_ref, w2k_ref), (xv_ref, w2v_ref))):
        a = jnp.zeros((nch, NSA_G * CMP_HIDDEN), F32)
        b = jnp.zeros((nch, NSA_G * CMP_HIDDEN), F32)
        for l in range(CMP_STRIDE):
            xs = x_ref[pl.ds(l, nch, stride=CMP_STRIDE), :]
            a = a + _dot((xs + pos_ref[kind, l:l + 1, :]).astype(BF16), w1_ref[kind, l])
            b = b + _dot((xs + pos_ref[kind, CMP_STRIDE + l:CMP_STRIDE + l + 1, :]).astype(BF16),
                         w1_ref[kind, CMP_STRIDE + l])
        pre = a + pltpu.roll(b, nch - 1, 0) + b1_ref[kind]
        out = _dot(jax.nn.gelu(pre).astype(BF16), w2_ref[...])
        if kind == 0:
            kc_ref[0] = out.astype(BF16)
        else:
            vct_ref[0] = out.T.astype(BF16)


def _blockdiag2(w):
    z = jnp.zeros_like(w)
    return jnp.concatenate([jnp.concatenate([w, z], axis=-1), jnp.concatenate([z, w], axis=-1)], axis=-2)


def _compress(z, pos, w1, b1, w2, bsz, seq):
    nch = seq // CMP_STRIDE
    pos2 = jnp.concatenate([pos, pos], axis=-1)
    w1e = _blockdiag2(w1.reshape(2, CMP_BLOCK, NSA_DH, CMP_HIDDEN)).astype(BF16)
    b1e = jnp.concatenate([b1, b1], axis=-1)[:, None]
    w2k = _blockdiag2(w2[0]).astype(BF16)
    w2v = _blockdiag2(w2[1]).astype(BF16)
    full = lambda a: pl.BlockSpec(a.shape, lambda b: (0,) * a.ndim)
    return pl.pallas_call(
        functools.partial(_compress_kernel, nch=nch),
        out_shape=(jax.ShapeDtypeStruct((bsz, nch, GD), BF16),
                   jax.ShapeDtypeStruct((bsz, GD, nch), BF16)),
        grid=(bsz,),
        in_specs=[pl.BlockSpec((seq, GD), lambda b: (b, 0)),
                  pl.BlockSpec((seq, GD), lambda b: (b, 1)),
                  full(pos2), full(w1e), full(b1e), full(w2k), full(w2v)],
        out_specs=(pl.BlockSpec((1, nch, GD), lambda b: (b, 0, 0)),
                   pl.BlockSpec((1, GD, nch), lambda b: (b, 0, 0))),
        compiler_params=_cparams(("parallel",)),
        name="nsa_compress",
    )(z, z, pos2, w1e, b1e, w2k, w2v)


NSA_QP = 4 * NSA_QB
STEP_BLOCKS = NSA_QP // SLC_BLOCK
NEAR_BACK = 2 * SLC_BLOCK
NEAR_KEYS = NSA_QP + NEAR_BACK
WIN_KEYS = WINDOW + NSA_QP
CMP_TAB_ROWS = 512
CMP_TAB_ZERO = 256


def _t5_bucket_np(d):
    exact = REL_BUCKETS // 2
    d = np.maximum(d, 0)
    ratio = np.log(np.maximum(d, 1).astype(np.float32) / np.float32(exact)) / np.float32(math.log(REL_MAX_DIST / exact))
    large = np.minimum(exact + (ratio * (REL_BUCKETS - exact)).astype(np.int32), REL_BUCKETS - 1)
    return np.where(d < exact, d, large).astype(np.int32)


def _bucket_thresholds():
    exact = REL_BUCKETS // 2
    bk = _t5_bucket_np(np.arange(4 * REL_MAX_DIST))
    assert np.all(np.diff(bk) >= 0) and bk[-1] == REL_BUCKETS - 1
    return [int(np.argmax(bk >= k)) for k in range(exact + 1, REL_BUCKETS)]


def _bias_rows(rel_ref, dist, valid, shift):
    exact = REL_BUCKETS // 2
    bucket = jnp.full(dist.shape, exact, jnp.int32)
    for thr in _bucket_thresholds():
        bucket = bucket + jnp.where(dist >= thr, 1, 0)
    bucket = jnp.where(dist < exact, dist, bucket)
    val = jnp.zeros(dist.shape, F32)
    for bkt in range(REL_BUCKETS):
        val = jnp.where(bucket == bkt, rel_ref[0, bkt:bkt + 1, :], val)
    if shift:
        val = val - rel_ref[0, REL_BUCKETS - 1:REL_BUCKETS, :]
    return jnp.where(valid, val * LOG2E, NEG_INF)


def _nsa_bias_kernel(rel_ref, tc_ref, tn_ref, tw_ref):
    hq = NSA_HG * NSA_QP
    rows = 128

    def dist_of(nrows, r0, fn):
        r = r0 + lax.broadcasted_iota(jnp.int32, (nrows, hq), 0)
        t = lax.bitwise_and(lax.broadcasted_iota(jnp.int32, (nrows, hq), 1), NSA_QP - 1)
        return fn(r, t)

    for r0 in range(0, CMP_TAB_ROWS, rows):
        d = dist_of(rows, r0, lambda r, t: t - CMP_STRIDE * (r - CMP_TAB_ZERO) - (CMP_BLOCK - 1))
        tc_ref[0, r0:r0 + rows, :] = _bias_rows(rel_ref, d, d >= 0, False)
    for r0 in range(0, NEAR_KEYS, rows):
        d = dist_of(rows, r0, lambda r, t: NEAR_BACK + t - r)
        tn_ref[0, r0:r0 + rows, :] = _bias_rows(rel_ref, d, d >= 0, True)
    for r0 in range(0, WIN_KEYS, rows):
        d = dist_of(rows, r0, lambda r, t: WINDOW + t - r)
        tw_ref[0, r0:r0 + rows, :] = _bias_rows(rel_ref, d, (d >= 0) & (d < WINDOW), False)


def _nsa_tables(rel_bias):
    hq = NSA_HG * NSA_QP
    rel4 = jnp.repeat(rel_bias.reshape(REL_BUCKETS, NSA_G, NSA_HG).transpose(1, 0, 2), NSA_QP, axis=-1)
    spec = lambda r: pl.BlockSpec((1, r, hq), lambda g: (g, 0, 0))
    return pl.pallas_call(
        _nsa_bias_kernel,
        out_shape=(jax.ShapeDtypeStruct((NSA_G, CMP_TAB_ROWS, hq), F32),
                   jax.ShapeDtypeStruct((NSA_G, NEAR_KEYS, hq), F32),
                   jax.ShapeDtypeStruct((NSA_G, WIN_KEYS, hq), F32)),
        grid=(NSA_G,),
        in_specs=[spec(REL_BUCKETS)],
        out_specs=(spec(CMP_TAB_ROWS), spec(NEAR_KEYS), spec(WIN_KEYS)),
        compiler_params=_cparams(("parallel",)),
        name="nsa_bias_tables",
    )(rel4)


FAR_KEYS = 512
KV_FRONT = WINDOW
KREP = NSA_HG * NSA_DH


def _softmax_cols(s):
    m = jnp.max(s, axis=0, keepdims=True)
    p = jnp.exp2(s - m)
    return m, p, jnp.sum(p, axis=0, keepdims=True)


def _rank_select(score_ref, n_sb, n_sel):
    groups = n_sb // SUBLANES
    sub = lax.broadcasted_iota(jnp.int32, (SUBLANES, NSA_QP), 0)
    tiles = [score_ref[SUBLANES * v:SUBLANES * (v + 1), :] for v in range(groups)]
    cnts = [jnp.zeros((SUBLANES, NSA_QP), F32) for _ in range(groups)]
    for jp in range(n_sb):
        row = score_ref[jp:jp + 1, :]
        for v in range(groups):
            lo = SUBLANES * v
            if jp < lo:
                beats = row >= tiles[v]
            elif jp >= lo + SUBLANES - 1:
                beats = row > tiles[v]
            else:
                beats = (row > tiles[v]) | ((row == tiles[v]) & (sub > jp - lo))
            cnts[v] = cnts[v] + jnp.where(beats, 1.0, 0.0)
    cnt = jnp.concatenate(cnts, axis=0)
    return jnp.where(cnt < float(n_sel), 1.0, 0.0)


def _nsa_kernel(q_ref, gate_ref, kc_ref, vct_ref, ks0_ref, ks1_ref, kw_ref, vst_ref, vwt_ref,
                tc_ref, tn_ref, tw_ref, cov_ref, o_ref, score_ref, *, n_sb):
    p2 = pl.program_id(1)
    hq = NSA_HG * NSA_QP
    nch = kc_ref.shape[1]
    groups = range(NSA_G)
    qcol = lambda g: slice(KREP * g, KREP * (g + 1))
    vrow = lambda g: slice(NSA_DH * g, NSA_DH * (g + 1))

    ks_refs = (ks0_ref, ks1_ref)
    qs, q64s = [], []
    zq = jnp.zeros((hq, NSA_DH), BF16)
    for g in groups:
        qb = (q_ref[:, qcol(g)].astype(F32) * (NSA_DH ** -0.5 * LOG2E)).astype(BF16)
        q64 = jnp.concatenate([qb[:, NSA_DH * h:NSA_DH * (h + 1)] for h in range(NSA_HG)], axis=0)
        q64s.append(q64)
        qs.append(jnp.concatenate([q64, zq] if g == 0 else [zq, q64], axis=1))

    start_c = pl.multiple_of(CMP_TAB_ZERO - (NSA_QP // CMP_STRIDE) * p2, SUBLANES)
    lane = lax.broadcasted_iota(jnp.int32, (1, hq), 1)
    tq = NSA_QP * p2 + lax.bitwise_and(lane, NSA_QP - 1)
    anyv = jnp.where(tq >= CMP_BLOCK - 1, 1.0, 0.0)
    jrow = lax.broadcasted_iota(jnp.int32, (n_sb, NSA_QP), 0)
    tok = lax.broadcasted_iota(jnp.int32, (n_sb, NSA_QP), 1)
    cur = STEP_BLOCKS * p2 + lax.shift_right_logical(tok, 6)
    forced = (jrow == 0) | (jrow == cur) | (jrow == cur - 1)
    o_cmp = []
    for g in groups:
        sc = _dot_nt(kc_ref[0], qs[g]) + tc_ref[g, pl.ds(start_c, nch), :]
        _, pc, lc = _softmax_cols(sc)
        pc = pc * (anyv / lc)
        o_cmp.append(_dot(vct_ref[0, vrow(g), :], pc.astype(BF16)))
        psum = pc[:, 0:NSA_QP]
        for h in range(1, NSA_HG):
            psum = psum + pc[:, NSA_QP * h:NSA_QP * (h + 1)]
        p_hi = psum.astype(BF16)
        p_lo = (psum - p_hi.astype(F32)).astype(BF16)
        imp = _dot(cov_ref[...], p_hi) + _dot(cov_ref[...], p_lo)
        score_ref[g] = jnp.where(forced, FORCE_SCORE, jnp.where(jrow <= cur, imp, -1.0))

    def mask_operand(g, keep):
        mb = ((keep - 1.0) * -NEG_INF).T.astype(BF16)
        if n_sb < NSA_DH:
            mb = jnp.concatenate([mb, jnp.zeros((NSA_QP, NSA_DH - n_sb), BF16)], axis=1)
        return jnp.concatenate([q64s[g], jnp.concatenate([mb] * NSA_HG, axis=0)], axis=1)

    q_near, q_far = [], []
    for g in groups:
        sel = _rank_select(score_ref.at[g], n_sb, min(N_SELECT, n_sb))
        q_near.append(mask_operand(g, sel))
        far_blocks = STEP_BLOCKS * p2 - NEAR_BACK // SLC_BLOCK
        q_far.append(mask_operand(g, jnp.where(jrow < far_blocks, sel, 0.0)))

    win0 = pl.multiple_of(NSA_QP * p2, LANES)
    near0 = pl.multiple_of(win0 + KV_FRONT - NEAR_BACK, LANES)
    state = []
    for g in groups:
        s = _dot_nt(ks_refs[g][0, pl.ds(near0, NEAR_KEYS), :], q_near[g]) + tn_ref[g]
        s = jnp.concatenate([jnp.where(p2 > 0, s[0:NEAR_BACK], NEG_INF), s[NEAR_BACK:]], axis=0)
        m_s, p_s, l_s = _softmax_cols(s)
        state += [m_s, l_s, _dot(vst_ref[0, vrow(g), pl.ds(near0, NEAR_KEYS)], p_s.astype(BF16))]

    window = []
    for g in groups:
        sw = _dot_nt(kw_ref[0, pl.ds(win0, WIN_KEYS), :], qs[g]) + tw_ref[g]
        slabs = [sw[NSA_QP * j:NSA_QP * (j + 1)] for j in range(WIN_KEYS // NSA_QP)]
        for j in range(KV_FRONT // NSA_QP):
            slabs[j] = jnp.where(NSA_QP * j + win0 >= KV_FRONT, slabs[j], NEG_INF)
        _, p_w, l_w = _softmax_cols(jnp.concatenate(slabs, axis=0))
        window.append((_dot(vwt_ref[0, vrow(g), pl.ds(win0, WIN_KEYS)], p_w.astype(BF16)), l_w))

    def far_body(c, carry):
        k0 = pl.multiple_of(FAR_KEYS * c + KV_FRONT, LANES)
        sfs = [_dot_nt(ks_refs[g][0, pl.ds(k0, FAR_KEYS), :], q_far[g]) for g in groups]
        out = []
        for g in groups:
            m_old, l_old, acc_old = carry[3 * g:3 * g + 3]
            sf = sfs[g]
            m_new = jnp.maximum(m_old, jnp.max(sf, axis=0, keepdims=True))
            alpha = jnp.exp2(m_old - m_new)
            pf = jnp.exp2(sf - m_new)
            l_new = alpha * l_old + jnp.sum(pf, axis=0, keepdims=True)
            acc_new = alpha * acc_old + _dot(vst_ref[0, vrow(g), pl.ds(k0, FAR_KEYS)], pf.astype(BF16))
            out += [m_new, l_new, acc_new]
        return tuple(out)

    n_far = lax.div(jnp.maximum(p2 * NSA_QP - NEAR_BACK, 0) + FAR_KEYS - 1, FAR_KEYS)
    state = lax.fori_loop(0, n_far, far_body, tuple(state))

    gt = gate_ref[...].astype(F32).T
    r = lax.broadcasted_iota(jnp.int32, (NSA_QP, NSA_QP), 0)
    c = lax.broadcasted_iota(jnp.int32, (NSA_QP, NSA_QP), 1)
    eye = jnp.where(r == c, 1.0, 0.0).astype(BF16)
    for g in groups:
        _, l_s, acc_s = state[3 * g:3 * g + 3]
        acc_w, l_w = window[g]
        gsel = jax.nn.sigmoid(gt[3 * NSA_HG * g:3 * NSA_HG * (g + 1)])
        gate = lambda b: jnp.concatenate([gsel[3 * h + b:3 * h + b + 1] for h in range(NSA_HG)], axis=1)
        out_t = (gate(0) * o_cmp[g] + (gate(1) / l_s) * acc_s + (gate(2) / l_w) * acc_w).astype(BF16)
        stacked = jnp.concatenate([out_t[:, NSA_QP * h:NSA_QP * (h + 1)] for h in range(NSA_HG)], axis=0)
        o_ref[:, qcol(g)] = _dot_nt(eye, stacked).astype(BF16)


def _nsa_attention(z, kc, vct, kk, vvt, tables, bsz, seq):
    g, hg, dh, qp = NSA_G, NSA_HG, NSA_DH, NSA_QP
    nstep = seq // qp
    n_sb = seq // SLC_BLOCK
    nch = kc.shape[1]
    hq = hg * qp
    sp = kk.shape[1]
    tc, tn, tw = tables
    c_start = CMP_STRIDE * np.arange(nch)
    s_start = SLC_BLOCK * np.arange(n_sb)
    cover_t = ((c_start[None, :] < s_start[:, None] + SLC_BLOCK)
               & (c_start[None, :] + CMP_BLOCK > s_start[:, None])
               & (np.arange(nch)[None, :] < (seq - CMP_BLOCK) // CMP_STRIDE + 1))
    cover_t = jnp.asarray(cover_t.astype(np.float32), BF16)
    full = lambda a: pl.BlockSpec(a.shape, lambda b, i: (0,) * a.ndim, pipeline_mode=pl.Buffered(1))
    qd = g * hg * dh
    return pl.pallas_call(
        functools.partial(_nsa_kernel, n_sb=n_sb),
        out_shape=jax.ShapeDtypeStruct((bsz * seq, qd), BF16),
        grid=(bsz, nstep),
        in_specs=[pl.BlockSpec((qp, qd), lambda b, i: (b * nstep + i, Z_Q // qd)),
                  pl.BlockSpec((qp, LANES), lambda b, i: (b * nstep + i, Z_GN // LANES)),
                  pl.BlockSpec((1, nch, GD), lambda b, i: (b, 0, 0)),
                  pl.BlockSpec((1, g * dh, nch), lambda b, i: (b, 0, 0)),
                  pl.BlockSpec((1, sp, GD), lambda b, i: (b, 0, 0)),
                  pl.BlockSpec((1, sp, GD), lambda b, i: (b, 0, 1)),
                  pl.BlockSpec((1, sp, GD), lambda b, i: (b, 0, 2)),
                  pl.BlockSpec((1, g * dh, sp), lambda b, i: (b, 0, 0)),
                  pl.BlockSpec((1, g * dh, sp), lambda b, i: (b, 1, 0)),
                  full(tc), full(tn), full(tw), full(cover_t)],
        out_specs=pl.BlockSpec((qp, qd), lambda b, i: (b * nstep + i, 0)),
        scratch_shapes=[pltpu.VMEM((g, n_sb, qp), F32)],
        compiler_params=_cparams(("parallel", "arbitrary")),
        name="nsa_attention",
    )(z, z, kc, vct, kk, kk, kk, vvt, vvt, tc, tn, tw, cover_t)


def _rope_table_kernel(pos_ref, inv_ref, o_ref):
    ang = inv_ref[...] * pos_ref[0].astype(F32)
    c, s = jnp.cos(ang), jnp.sin(ang)
    o_ref[...] = jnp.concatenate([c, c, -s, s], axis=0).T


def _rope_table(positions):
    tm = TOK_TILE
    t = positions.size
    half = QK_ROPE // 2
    inv = (ROPE_THETA ** (-jnp.arange(half, dtype=F32) / half))[:, None]
    return pl.pallas_call(
        _rope_table_kernel,
        out_shape=jax.ShapeDtypeStruct((t, 2 * QK_ROPE), F32),
        grid=(t // tm,),
        in_specs=[pl.BlockSpec((1, 1, tm), lambda i: (i, 0, 0)),
                  pl.BlockSpec((half, 1), lambda i: (0, 0))],
        out_specs=pl.BlockSpec((tm, 2 * QK_ROPE), lambda i: (i, 0)),
        compiler_params=_cparams(("parallel",)),
        name="rope_table",
    )(positions.reshape(t // tm, 1, tm), inv)


MLA_HW = 256


def _mla_proj_kernel(cq_ref, ckv_ref, kr_ref, rope_ref, nq_ref, nkv_ref, wq_ref, wkn_ref, wvt_ref,
                     q_ref, k_ref, vt_ref):
    scale = (QK_NOPE + QK_ROPE) ** -0.5 * LOG2E
    rope = rope_ref[...]
    yq = _dot(_rms(cq_ref[...].astype(F32), nq_ref[...]).astype(BF16), wq_ref[...])
    ckv = _rms(ckv_ref[...].astype(F32), nkv_ref[...]).astype(BF16)
    ykn = _dot(ckv, wkn_ref[...])
    vt_ref[0] = _dot_nt(wvt_ref[...], ckv).astype(BF16)
    kp = kr_ref[...].astype(F32) * rope
    kp = kp + pltpu.roll(kp, QK_ROPE, 1)
    lane = lax.broadcasted_iota(jnp.int32, kp.shape, 1)
    kp = jnp.where(lane < QK_ROPE, kp, 0.0).astype(BF16)
    for h in range(MLA_HEADS):
        base = MLA_HW * h
        q_ref[:, base:base + QK_NOPE] = (yq[:, base:base + QK_NOPE] * scale).astype(BF16)
        qp = yq[:, base + QK_NOPE:base + MLA_HW] * rope
        qp = qp + pltpu.roll(qp, QK_ROPE, 1)
        q_ref[:, base + QK_NOPE:base + MLA_HW] = (qp * scale).astype(BF16)
        k_ref[:, base:base + QK_NOPE] = ykn[:, QK_NOPE * h:QK_NOPE * (h + 1)].astype(BF16)
        k_ref[:, base + QK_NOPE:base + MLA_HW] = kp


def _swap_halves(w):
    half = QK_ROPE // 2
    return jnp.concatenate([w[..., half:], w[..., :half]], axis=-1)


def _mla_proj(z, rope_tab, norm_q, norm_kv, w_uq, w_ukv, bsz, seq):
    tm = TOK_TILE
    t = z.shape[0]
    nst = seq // tm
    wq = w_uq.reshape(Q_RANK, MLA_HEADS, QK_NOPE + QK_ROPE)
    wq = jnp.concatenate([wq, _swap_halves(wq[..., QK_NOPE:])], axis=-1)
    wq = wq.reshape(Q_RANK, MLA_HEADS * MLA_HW).astype(BF16)
    wkv = w_ukv.reshape(KV_RANK, MLA_HEADS, QK_NOPE + V_DIM)
    wkn = wkv[..., :QK_NOPE].reshape(KV_RANK, MLA_HEADS * QK_NOPE).astype(BF16)
    wvt = wkv[..., QK_NOPE:].reshape(KV_RANK, MLA_HEADS * V_DIM).T.astype(BF16)
    hw = MLA_HEADS * MLA_HW
    hv = MLA_HEADS * V_DIM
    row = lambda b, s: b * nst + s
    return pl.pallas_call(
        _mla_proj_kernel,
        out_shape=(jax.ShapeDtypeStruct((t, hw), BF16),
                   jax.ShapeDtypeStruct((t, hw), BF16),
                   jax.ShapeDtypeStruct((bsz, hv, seq), BF16)),
        grid=(bsz, nst),
        in_specs=[pl.BlockSpec((tm, Q_RANK), lambda b, s: (row(b, s), Z_CQ // Q_RANK)),
                  pl.BlockSpec((tm, KV_RANK), lambda b, s: (row(b, s), Z_CKV // KV_RANK)),
                  pl.BlockSpec((tm, 2 * QK_ROPE), lambda b, s: (row(b, s), Z_KR // (2 * QK_ROPE))),
                  pl.BlockSpec((tm, 2 * QK_ROPE), lambda b, s: (row(b, s), 0)),
                  pl.BlockSpec((1, Q_RANK), lambda b, s: (0, 0)),
                  pl.BlockSpec((1, KV_RANK), lambda b, s: (0, 0)),
                  pl.BlockSpec((Q_RANK, hw), lambda b, s: (0, 0)),
                  pl.BlockSpec((KV_RANK, hv), lambda b, s: (0, 0)),
                  pl.BlockSpec((hv, KV_RANK), lambda b, s: (0, 0))],
        out_specs=(pl.BlockSpec((tm, hw), lambda b, s: (row(b, s), 0)),
                   pl.BlockSpec((tm, hw), lambda b, s: (row(b, s), 0)),
                   pl.BlockSpec((1, hv, tm), lambda b, s: (b, 0, s))),
        compiler_params=_cparams(("parallel", "parallel")),
        name="mla_proj",
    )(z, z, z, rope_tab, norm_q[None], norm_kv[None], wq, wkn, wvt)


def _mla_attn_kernel(q_ref, k_ref, vt_ref, o_ref, *, tq, tk, nh):
    iq = pl.program_id(2)
    cd = lax.div(iq * tq, tk)
    heads = range(nh)
    hcol = lambda h: slice(MLA_HW * h, MLA_HW * (h + 1))
    vrow = lambda h: slice(V_DIM * h, V_DIM * (h + 1))
    qs = [q_ref[:, hcol(h)] for h in heads]

    def scores(c, h):
        k0 = pl.multiple_of(c * tk, tk)
        return _dot_nt(k_ref[0, pl.ds(k0, tk), hcol(h)], qs[h])

    def diagonal(nk):
        k0 = pl.multiple_of((iq + 1) * tq - nk, tq)
        kpos = k0 + lax.broadcasted_iota(jnp.int32, (nk, tq), 0)
        qpos = iq * tq + lax.broadcasted_iota(jnp.int32, (nk, tq), 1)
        st = []
        for h in heads:
            s = jnp.where(kpos <= qpos, _dot_nt(k_ref[0, pl.ds(k0, nk), hcol(h)], qs[h]), NEG_INF)
            m0, p0, l0 = _softmax_cols(s)
            st += [m0, l0, _dot(vt_ref[0, vrow(h), pl.ds(k0, nk)], p0.astype(BF16))]
        return tuple(st)

    assert tk == 2 * tq
    state = lax.cond(lax.rem(iq, 2) == 0, lambda: diagonal(tq), lambda: diagonal(tk))

    def body(c, carry):
        k0 = pl.multiple_of(c * tk, tk)
        ss = [scores(c, h) for h in heads]
        out = []
        for h in heads:
            m_old, l_old, acc_old = carry[3 * h:3 * h + 3]
            m_new = jnp.maximum(m_old, jnp.max(ss[h], axis=0, keepdims=True))
            alpha = jnp.exp2(m_old - m_new)
            p = jnp.exp2(ss[h] - m_new)
            l_new = alpha * l_old + jnp.sum(p, axis=0, keepdims=True)
            acc_new = alpha * acc_old + _dot(vt_ref[0, vrow(h), pl.ds(k0, tk)], p.astype(BF16))
            out += [m_new, l_new, acc_new]
        return tuple(out)

    state = lax.fori_loop(0, cd, body, tuple(state))
    for h in heads:
        _, l, acc = state[3 * h:3 * h + 3]
        o_ref[:, vrow(h)] = (acc / l).T.astype(BF16)


def _mla_attention(qf, kf, vt, bsz, seq):
    tq, tk, nh = MLA_TQ, MLA_TK, MLA_HEADS_PER_STEP
    h = MLA_HEADS
    nq = seq // tq
    k3 = kf.reshape(bsz, seq, h * MLA_HW)
    return pl.pallas_call(
        functools.partial(_mla_attn_kernel, tq=tq, tk=tk, nh=nh),
        out_shape=jax.ShapeDtypeStruct((bsz * seq, h * V_DIM), BF16),
        grid=(bsz, h // nh, nq),
        in_specs=[pl.BlockSpec((tq, nh * MLA_HW), lambda b, hh, i: (b * nq + i, hh)),
                  pl.BlockSpec((1, seq, nh * MLA_HW), lambda b, hh, i: (b, 0, hh)),
                  pl.BlockSpec((1, nh * V_DIM, seq), lambda b, hh, i: (b, hh, 0))],
        out_specs=pl.BlockSpec((tq, nh * V_DIM), lambda b, hh, i: (b * nq + i, hh)),
        compiler_params=_cparams(("parallel", "parallel", "arbitrary")),
        name="mla_attention",
    )(qf, k3, vt)


def _merge_xattn_kernel(ya_ref, yb_ref, yc_ref, ga_ref, gb_ref, gc_ref, x_ref,
                        wa_ref, wb_ref, wc_ref, wo_ref,
                        gx_ref, wq_ref, kv_ref, wxo_ref, o_ref):
    sig = lambda ref: jax.nn.sigmoid(ref[...].astype(F32))
    y = (sig(ga_ref) * _dot(ya_ref[...], wa_ref[...])
         + sig(gb_ref) * _dot(yb_ref[...], wb_ref[...])
         + sig(gc_ref) * _dot(yc_ref[...], wc_ref[...]))
    x = x_ref[...] + _dot(y.astype(BF16), wo_ref[...])
    h = _rms(x, gx_ref[...]).astype(BF16)
    q = _dot(h, wq_ref[...]) * XATTN_DH ** -0.5
    hd = XATTN_HEADS * XATTN_DH
    outs = []
    for hh in range(XATTN_HEADS):
        qh = q[:, XATTN_DH * hh:XATTN_DH * (hh + 1)].astype(BF16)
        kh = kv_ref[0, :, XATTN_DH * hh:XATTN_DH * (hh + 1)]
        vh = kv_ref[0, :, hd + XATTN_DH * hh:hd + XATTN_DH * (hh + 1)]
        s = _dot_nt(qh, kh)
        m = jnp.max(s, axis=-1, keepdims=True)
        p = jnp.exp(s - m)
        p = p / jnp.sum(p, axis=-1, keepdims=True)
        outs.append(_dot(p.astype(BF16), vh))
    o = jnp.concatenate(outs, axis=-1).astype(BF16)
    o_ref[...] = x + _dot(o, wxo_ref[...])


def _merge_xattn(ya, yb, yc, z, x, wa, wb, wc, wo, gx, wq, kv, wxo, bsz, seq):
    tm = TOK_TILE
    t, d = x.shape
    nst = seq // tm
    m_len = kv.shape[1]
    hd = XATTN_HEADS * XATTN_DH
    row = lambda b, s: b * nst + s
    act = pl.BlockSpec((tm, ya.shape[1]), lambda b, s: (row(b, s), 0))
    gate = lambda k: pl.BlockSpec((tm, d), lambda b, s: (row(b, s), Z_GM // d + k))
    const = lambda shape: pl.BlockSpec(shape, lambda b, s: (0, 0))
    bf = lambda w: w.astype(BF16)
    return pl.pallas_call(
        _merge_xattn_kernel,
        out_shape=jax.ShapeDtypeStruct((t, d), F32),
        grid=(bsz, nst),
        in_specs=[act, act, act, gate(0), gate(1), gate(2),
                  pl.BlockSpec((tm, d), lambda b, s: (row(b, s), 0)),
                  const((ya.shape[1], d)), const((ya.shape[1], d)), const((ya.shape[1], d)), const((d, d)),
                  const((1, d)), const((d, hd)),
                  pl.BlockSpec((1, m_len, 2 * hd), lambda b, s: (b, 0, 0)),
                  const((hd, d))],
        out_specs=pl.BlockSpec((tm, d), lambda b, s: (row(b, s), 0)),
        compiler_params=_cparams(("parallel", "parallel")),
        name="merge_xattn",
    )(ya, yb, yc, z, z, z, x, bf(wa), bf(wb), bf(wc), bf(wo), gx[None], bf(wq), kv, bf(wxo))


def _ffn_kernel(x_ref, g_ref, wg_ref, wu_ref, wd_ref, gf_ref, o_ref, h_ref, acc_ref, *, final):
    c = pl.program_id(1)

    @pl.when(c == 0)
    def _():
        h_ref[...] = _rms(x_ref[...], g_ref[...]).astype(BF16)
        acc_ref[...] = x_ref[...]

    h = h_ref[...]
    gate = _dot(h, wg_ref[...])
    up = _dot(h, wu_ref[...])
    act = (gate * jax.nn.sigmoid(gate) * up).astype(BF16)
    acc_ref[...] += _dot(act, wd_ref[...])

    @pl.when(c == pl.num_programs(1) - 1)
    def _():
        y = acc_ref[...]
        o_ref[...] = _rms(y, gf_ref[...]) if final else y


def _ffn(x, g, w_gate_up, w_down, g_final, final):
    tm, tc = FFN_TM, FFN_TC
    t, d = x.shape
    nc = FFN_HIDDEN // tc
    wgu = w_gate_up.astype(BF16)
    return pl.pallas_call(
        functools.partial(_ffn_kernel, final=final),
        out_shape=jax.ShapeDtypeStruct((t, d), F32),
        grid=(t // tm, nc),
        in_specs=[pl.BlockSpec((tm, d), lambda i, c: (i, 0)),
                  pl.BlockSpec((1, d), lambda i, c: (0, 0)),
                  pl.BlockSpec((d, tc), lambda i, c: (0, c)),
                  pl.BlockSpec((d, tc), lambda i, c: (0, nc + c)),
                  pl.BlockSpec((tc, d), lambda i, c: (c, 0)),
                  pl.BlockSpec((1, d), lambda i, c: (0, 0))],
        out_specs=pl.BlockSpec((tm, d), lambda i, c: (i, 0)),
        scratch_shapes=[pltpu.VMEM((tm, d), BF16), pltpu.VMEM((tm, d), F32)],
        compiler_params=_cparams(("parallel", "arbitrary")),
        name="ffn",
    )(x, g[None], wgu, wgu, w_down.astype(BF16), g_final[None])


def _split_w_in(w):
    k_rope = w[:, O_KR:O_KR + QK_ROPE]
    kv = lambda kind: w[:, O_KV + GD * kind:O_KV + GD * (kind + 1)]
    pad = jnp.zeros((w.shape[0], Z_COLS - Z_GN - 3 * NSA_HEADS), w.dtype)
    wz = jnp.concatenate([
        w[:, O_GM:O_GM + 3 * D_MODEL],
        w[:, O_GLU:O_GLU + 2 * CONV_CH],
        w[:, O_Q:O_Q + NSA_HEADS * NSA_DH],
        w[:, O_CKV:O_CKV + KV_RANK],
        k_rope, _swap_halves(k_rope),
        w[:, O_CQ:O_CQ + Q_RANK],
        kv(0), kv(1),
        w[:, O_GN:O_GN + 3 * NSA_HEADS], pad], axis=1).astype(BF16)
    z64 = jnp.zeros((w.shape[0], NSA_DH), w.dtype)
    wk = jnp.concatenate([kv(2)[:, :NSA_DH], z64, kv(2)[:, NSA_DH:], z64, kv(4)], axis=1).astype(BF16)
    wvt = jnp.concatenate([kv(3), kv(5)], axis=1).T.astype(BF16)
    return wz, wk, wvt


def kernel(x, mem, positions, rel_bias, norm_mix, norm_xattn, norm_mem, norm_ffn, norm_final, w_in, conv_w, conv_b, conv_ln_g, conv_ln_b, w_branch_conv, cmp_pos_k, cmp_w1_k, cmp_b1_k, cmp_w2_k, cmp_pos_v, cmp_w1_v, cmp_b1_v, cmp_w2_v, w_branch_nsa, mla_norm_q, mla_norm_kv, w_uq, w_ukv, w_branch_mla, w_out, w_xq, w_xkv, w_xo, w_gate_up, w_down):
    bsz, seq, d = x.shape
    depth = w_in.shape[0]
    t = bsz * seq
    m_len = mem.shape[1]
    xt = x.reshape(t, d)
    memt = mem.reshape(bsz * m_len, d)
    rope_tab = _rope_table(positions)
    tables = _nsa_tables(rel_bias)
    for l in range(depth):
        wz, wk, wvt = _split_w_in(w_in[l])
        z, zc = _in_proj(xt, norm_mix[l][None], wz)
        kk, vvt = _kv_proj(xt, norm_mix[l][None], wk, wvt, bsz, seq)
        ya = _conv_module(z, conv_w[l], conv_b[l], conv_ln_g[l], conv_ln_b[l], bsz, seq)
        kc, vct = _compress(zc, jnp.stack([cmp_pos_k[l], cmp_pos_v[l]]), jnp.stack([cmp_w1_k[l], cmp_w1_v[l]]),
                            jnp.stack([cmp_b1_k[l], cmp_b1_v[l]]), jnp.stack([cmp_w2_k[l], cmp_w2_v[l]]), bsz, seq)
        yb = _nsa_attention(z, kc, vct, kk, vvt, tables, bsz, seq)
        qf, kf, vt = _mla_proj(z, rope_tab, mla_norm_q[l], mla_norm_kv[l], w_uq[l], w_ukv[l], bsz, seq)
        yc = _mla_attention(qf, kf, vt, bsz, seq)
        mem_kv = _norm_matmul(memt, norm_mem[l][None], w_xkv[l].astype(BF16), 256, 1024, BF16)
        mem_kv = mem_kv.reshape(bsz, m_len, 2 * XATTN_HEADS * XATTN_DH)
        xt = _merge_xattn(ya, yb, yc, z, xt, w_branch_conv[l], w_branch_nsa[l], w_branch_mla[l], w_out[l],
                          norm_xattn[l], w_xq[l], mem_kv, w_xo[l], bsz, seq)
        xt = _ffn(xt, norm_ffn[l], w_gate_up[l], w_down[l], norm_final, l == depth - 1)
    return xt.reshape(bsz, seq, d)
```

```python
import functools
import math

import numpy as np
import jax
import jax.numpy as jnp
from jax import lax
from jax.experimental import pallas as pl
from jax.experimental.pallas import tpu as pltpu

F32 = jnp.float32
BF16 = jnp.bfloat16

EPS = 1e-6
NEG_INF = -1e30
FORCE_SCORE = 1e4

D_MODEL = 1024
CONV_CH = 512
CONV_WIDTH = 31
NSA_HEADS = 8
NSA_G = 2
NSA_HG = NSA_HEADS // NSA_G
NSA_DH = 64
CMP_BLOCK = 32
CMP_STRIDE = 16
CMP_HIDDEN = 256
SLC_BLOCK = 64
N_SELECT = 16
WINDOW = 512
NSA_QB = 64
MLA_HEADS = 4
Q_RANK = 384
KV_RANK = 256
QK_NOPE = 128
QK_ROPE = 64
V_DIM = 128
ROPE_THETA = 10000.0
REL_BUCKETS = 32
REL_MAX_DIST = 128
XATTN_HEADS = 4
XATTN_DH = 128
FFN_HIDDEN = 2816

LANES = 128
SUBLANES = 8

O_GLU, O_Q, O_KV, O_GN, O_CQ, O_CKV, O_KR, O_GM = 0, 1024, 1536, 2304, 2328, 2712, 2968, 3032
GD = NSA_G * NSA_DH

Z_GM = 0
Z_UA = 3072
Z_UB = 3584
Z_Q = 4096
Z_CKV = 4608
Z_KR = 4864
Z_CQ = 4992
Z_CMP = 5376
Z_GN = 5632
Z_COLS = 5760

VMEM_LIMIT = 56 * 1024 * 1024

TOK_TILE = 512
IN_PROJ_TM = 1024
IN_PROJ_TN = 1152
FFN_TM = 1024
FFN_TC = 256
MLA_TQ = 512
MLA_TK = 1024
MLA_HEADS_PER_STEP = 2

LOG2E = math.log2(math.e)


def _cparams(sem):
    return pltpu.CompilerParams(dimension_semantics=sem, vmem_limit_bytes=VMEM_LIMIT)


def _rms(x, g):
    return x * lax.rsqrt(jnp.mean(x * x, axis=-1, keepdims=True) + EPS) * g


def _dot(a, b):
    return jnp.dot(a, b, preferred_element_type=F32)


def _dot_nt(a, b):
    return lax.dot_general(a, b, (((1,), (1,)), ((), ())), preferred_element_type=F32)


def _norm_matmul_kernel(x_ref, g_ref, w_ref, o_ref, h_ref):
    @pl.when(pl.program_id(1) == 0)
    def _():
        h_ref[...] = _rms(x_ref[...], g_ref[...]).astype(BF16)

    o_ref[...] = _dot(h_ref[...], w_ref[...]).astype(o_ref.dtype)


def _norm_matmul(x, g, w, tm, tn, out_dtype):
    m, k = x.shape
    n = w.shape[1]
    return pl.pallas_call(
        _norm_matmul_kernel,
        out_shape=jax.ShapeDtypeStruct((m, n), out_dtype),
        grid=(m // tm, n // tn),
        in_specs=[pl.BlockSpec((tm, k), lambda i, j: (i, 0)),
                  pl.BlockSpec((1, k), lambda i, j: (0, 0)),
                  pl.BlockSpec((k, tn), lambda i, j: (0, j))],
        out_specs=pl.BlockSpec((tm, tn), lambda i, j: (i, j)),
        scratch_shapes=[pltpu.VMEM((tm, k), BF16)],
        compiler_params=_cparams(("parallel", "arbitrary")),
        name="norm_matmul",
    )(x, g, w)


def _in_proj_kernel(x_ref, g_ref, w_ref, z_ref, zc_ref, h_ref, *, cmp_tile, cmp_off):
    @pl.when(pl.program_id(1) == 0)
    def _():
        h_ref[...] = _rms(x_ref[...], g_ref[...]).astype(BF16)

    acc = _dot(h_ref[...], w_ref[...])
    z_ref[...] = acc.astype(BF16)

    @pl.when(pl.program_id(1) == cmp_tile)
    def _():
        zc_ref[...] = acc[:, cmp_off:cmp_off + 2 * GD]


def _in_proj(x, g, w):
    tm, tn = IN_PROJ_TM, IN_PROJ_TN
    m, k = x.shape
    n = w.shape[1]
    return pl.pallas_call(
        functools.partial(_in_proj_kernel, cmp_tile=Z_CMP // tn, cmp_off=Z_CMP % tn),
        out_shape=(jax.ShapeDtypeStruct((m, n), BF16), jax.ShapeDtypeStruct((m, 2 * GD), F32)),
        grid=(m // tm, n // tn),
        in_specs=[pl.BlockSpec((tm, k), lambda i, j: (i, 0)),
                  pl.BlockSpec((1, k), lambda i, j: (0, 0)),
                  pl.BlockSpec((k, tn), lambda i, j: (0, j))],
        out_specs=(pl.BlockSpec((tm, tn), lambda i, j: (i, j)),
                   pl.BlockSpec((tm, 2 * GD), lambda i, j: (i, 0))),
        scratch_shapes=[pltpu.VMEM((tm, k), BF16)],
        compiler_params=_cparams(("parallel", "arbitrary")),
        name="in_proj",
    )(x, g, w)


VT_SLAB = NSA_DH + 16


def _kv_proj_kernel(x_ref, g_ref, wk_ref, wvt_ref, k_ref, vt_ref):
    @pl.when(pl.program_id(1) == 0)
    def _():
        k_ref[...] = jnp.zeros(k_ref.shape, BF16)
        vt_ref[...] = jnp.zeros(vt_ref.shape, BF16)

    @pl.when(pl.program_id(1) > 0)
    def _():
        h = _rms(x_ref[...], g_ref[...]).astype(BF16)
        k = _dot(h, wk_ref[...])
        tm = k.shape[0]
        row = lax.broadcasted_iota(jnp.int32, k.shape, 0)
        lane = lax.broadcasted_iota(jnp.int32, k.shape, 1)
        blk = (pl.program_id(1) - 1) * (tm // SLC_BLOCK) + lax.shift_right_logical(row, 6)
        hot = (lane < NSA_G * GD) & (lax.bitwise_and(lane, GD - 1) == blk + NSA_DH)
        k_ref[0] = jnp.where(hot, 1.0, k).astype(BF16)
        v = _dot_nt(wvt_ref[...], h).astype(BF16)
        ones = jnp.ones((VT_SLAB - NSA_DH, tm), BF16)
        vt_ref[0] = jnp.concatenate([x for i in range(v.shape[0] // NSA_DH)
                                     for x in (v[NSA_DH * i:NSA_DH * (i + 1)], ones)], axis=0)


def _kv_proj(x, g, wk, wvt, bsz, seq):
    tm = WINDOW
    nst = seq // tm
    d = x.shape[1]
    nk = wk.shape[1]
    nv = wvt.shape[0] // NSA_DH * VT_SLAB
    return pl.pallas_call(
        _kv_proj_kernel,
        out_shape=(jax.ShapeDtypeStruct((bsz, seq + tm, nk), BF16),
                   jax.ShapeDtypeStruct((bsz, nv, seq + tm), BF16)),
        grid=(bsz, nst + 1),
        in_specs=[pl.BlockSpec((tm, d), lambda b, s: (b * nst + jnp.maximum(s - 1, 0), 0)),
                  pl.BlockSpec((1, d), lambda b, s: (0, 0)),
                  pl.BlockSpec((d, nk), lambda b, s: (0, 0)),
                  pl.BlockSpec((wvt.shape[0], d), lambda b, s: (0, 0))],
        out_specs=(pl.BlockSpec((1, tm, nk), lambda b, s: (b, s, 0)),
                   pl.BlockSpec((1, nv, tm), lambda b, s: (b, 0, s))),
        compiler_params=_cparams(("parallel", "arbitrary")),
        name="nsa_kv_proj",
    )(x, g, wk, wvt)


CONV_HALO = 32


CONV_ROWS = 64


def _conv_kernel(a_ref, b_ref, w_ref, cb_ref, lg_ref, lb_ref, o_ref, buf_ref, sh_ref, *, ts):
    @pl.when(pl.program_id(1) == 0)
    def _():
        buf_ref[0:CONV_HALO, :] = jnp.zeros((CONV_HALO, CONV_CH), F32)

    buf_ref[CONV_HALO:CONV_HALO + ts, :] = a_ref[...].astype(F32) * jax.nn.sigmoid(b_ref[...].astype(F32))
    span = ts + CONV_HALO - SUBLANES
    for r in range(1, SUBLANES):
        sh_ref[r - 1, 0:span, :] = buf_ref[r:r + span, :]
    off = CONV_HALO - (CONV_WIDTH - 1)

    def rows(i, carry):
        r0 = pl.multiple_of(i * CONV_ROWS, CONV_ROWS)
        acc = jnp.zeros((CONV_ROWS, CONV_CH), F32) + cb_ref[...]
        for k in range(CONV_WIDTH):
            res, base = (off + k) % SUBLANES, (off + k) // SUBLANES * SUBLANES
            if res == 0:
                tap = buf_ref[pl.ds(r0 + base, CONV_ROWS), :]
            else:
                tap = sh_ref[res - 1, pl.ds(r0 + base, CONV_ROWS), :]
            acc = acc + tap * w_ref[k:k + 1, :]
        mu = jnp.mean(acc, axis=-1, keepdims=True)
        xc = acc - mu
        var = jnp.mean(xc * xc, axis=-1, keepdims=True)
        y = xc * lax.rsqrt(var + EPS) * lg_ref[...] + lb_ref[...]
        o_ref[pl.ds(r0, CONV_ROWS), :] = (y * jax.nn.sigmoid(y)).astype(BF16)
        return carry

    lax.fori_loop(0, ts // CONV_ROWS, rows, 0)
    buf_ref[0:CONV_HALO, :] = buf_ref[ts:ts + CONV_HALO, :]


def _conv_module(z, conv_w, conv_b, ln_g, ln_b, bsz, seq):
    ts = TOK_TILE
    nst = seq // ts
    wpad = jnp.zeros((32, CONV_CH), F32).at[:CONV_WIDTH].set(conv_w)
    return pl.pallas_call(
        functools.partial(_conv_kernel, ts=ts),
        out_shape=jax.ShapeDtypeStruct((bsz * seq, CONV_CH), BF16),
        grid=(bsz, nst),
        in_specs=[pl.BlockSpec((ts, CONV_CH), lambda b, s: (b * nst + s, Z_UA // CONV_CH)),
                  pl.BlockSpec((ts, CONV_CH), lambda b, s: (b * nst + s, Z_UB // CONV_CH)),
                  pl.BlockSpec((32, CONV_CH), lambda b, s: (0, 0)),
                  pl.BlockSpec((1, CONV_CH), lambda b, s: (0, 0)),
                  pl.BlockSpec((1, CONV_CH), lambda b, s: (0, 0)),
                  pl.BlockSpec((1, CONV_CH), lambda b, s: (0, 0))],
        out_specs=pl.BlockSpec((ts, CONV_CH), lambda b, s: (b * nst + s, 0)),
        scratch_shapes=[pltpu.VMEM((ts + CONV_HALO, CONV_CH), F32),
                        pltpu.VMEM((SUBLANES - 1, ts + CONV_HALO - SUBLANES, CONV_CH), F32)],
        compiler_params=_cparams(("arbitrary", "arbitrary")),
        name="conv_module",
    )(z, z, wpad, conv_b[None], ln_g[None], ln_b[None])


def _compress_kernel(xk_ref, xv_ref, pos_ref, w1_ref, b1_ref, w2k_ref, w2v_ref, kc_ref, vct_ref, *, nch):
    for kind, (x_ref, w2_ref) in enumerate(((xk_ref, w2k_ref), (xv_ref, w2v_ref))):
        a = jnp.zeros((nch, NSA_G * CMP_HIDDEN), F32)
        b = jnp.zeros((nch, NSA_G * CMP_HIDDEN), F32)
        for l in range(CMP_STRIDE):
            xs = x_ref[pl.ds(l, nch, stride=CMP_STRIDE), :]
            a = a + _dot((xs + pos_ref[kind, l:l + 1, :]).astype(BF16), w1_ref[kind, l])
            b = b + _dot((xs + pos_ref[kind, CMP_STRIDE + l:CMP_STRIDE + l + 1, :]).astype(BF16),
                         w1_ref[kind, CMP_STRIDE + l])
        pre = a + pltpu.roll(b, nch - 1, 0) + b1_ref[kind]
        out = _dot(jax.nn.gelu(pre).astype(BF16), w2_ref[...])
        if kind == 0:
            kc_ref[0] = out.astype(BF16)
        else:
            vct_ref[0] = out.T.astype(BF16)


def _blockdiag2(w):
    z = jnp.zeros_like(w)
    return jnp.concatenate([jnp.concatenate([w, z], axis=-1), jnp.concatenate([z, w], axis=-1)], axis=-2)


def _compress(z, pos, w1, b1, w2, bsz, seq):
    nch = seq // CMP_STRIDE
    pos2 = jnp.concatenate([pos, pos], axis=-1)
    w1e = _blockdiag2(w1.reshape(2, CMP_BLOCK, NSA_DH, CMP_HIDDEN)).astype(BF16)
    b1e = jnp.concatenate([b1, b1], axis=-1)[:, None]
    w2k = _blockdiag2(w2[0]).astype(BF16)
    w2v = _blockdiag2(w2[1]).astype(BF16)
    full = lambda a: pl.BlockSpec(a.shape, lambda b: (0,) * a.ndim)
    return pl.pallas_call(
        functools.partial(_compress_kernel, nch=nch),
        out_shape=(jax.ShapeDtypeStruct((bsz, nch, GD), BF16),
                   jax.ShapeDtypeStruct((bsz, GD, nch), BF16)),
        grid=(bsz,),
        in_specs=[pl.BlockSpec((seq, GD), lambda b: (b, 0)),
                  pl.BlockSpec((seq, GD), lambda b: (b, 1)),
                  full(pos2), full(w1e), full(b1e), full(w2k), full(w2v)],
        out_specs=(pl.BlockSpec((1, nch, GD), lambda b: (b, 0, 0)),
                   pl.BlockSpec((1, GD, nch), lambda b: (b, 0, 0))),
        compiler_params=_cparams(("parallel",)),
        name="nsa_compress",
    )(z, z, pos2, w1e, b1e, w2k, w2v)


NSA_QP = 4 * NSA_QB
STEP_BLOCKS = NSA_QP // SLC_BLOCK
NEAR_BACK = 2 * SLC_BLOCK
NEAR_KEYS = NSA_QP + NEAR_BACK
WIN_KEYS = WINDOW + NSA_QP
CMP_TAB_ROWS = 512
CMP_TAB_ZERO = 256


def _t5_bucket_np(d):
    exact = REL_BUCKETS // 2
    d = np.maximum(d, 0)
    ratio = np.log(np.maximum(d, 1).astype(np.float32) / np.float32(exact)) / np.float32(math.log(REL_MAX_DIST / exact))
    large = np.minimum(exact + (ratio * (REL_BUCKETS - exact)).astype(np.int32), REL_BUCKETS - 1)
    return np.where(d < exact, d, large).astype(np.int32)


def _bucket_thresholds():
    exact = REL_BUCKETS // 2
    bk = _t5_bucket_np(np.arange(4 * REL_MAX_DIST))
    assert np.all(np.diff(bk) >= 0) and bk[-1] == REL_BUCKETS - 1
    return [int(np.argmax(bk >= k)) for k in range(exact + 1, REL_BUCKETS)]


def _bias_rows(rel_ref, dist, valid, shift):
    exact = REL_BUCKETS // 2
    bucket = jnp.full(dist.shape, exact, jnp.int32)
    for thr in _bucket_thresholds():
        bucket = bucket + jnp.where(dist >= thr, 1, 0)
    bucket = jnp.where(dist < exact, dist, bucket)
    val = jnp.zeros(dist.shape, F32)
    for bkt in range(REL_BUCKETS):
        val = jnp.where(bucket == bkt, rel_ref[0, bkt:bkt + 1, :], val)
    if shift:
        val = val - rel_ref[0, REL_BUCKETS - 1:REL_BUCKETS, :]
    return jnp.where(valid, val * LOG2E, NEG_INF)


def _nsa_bias_kernel(rel_ref, tc_ref, tn_ref, tw_ref):
    hq = NSA_HG * NSA_QP
    rows = 128

    def dist_of(nrows, r0, fn):
        r = r0 + lax.broadcasted_iota(jnp.int32, (nrows, hq), 0)
        t = lax.bitwise_and(lax.broadcasted_iota(jnp.int32, (nrows, hq), 1), NSA_QP - 1)
        return fn(r, t)

    for r0 in range(0, CMP_TAB_ROWS, rows):
        d = dist_of(rows, r0, lambda r, t: t - CMP_STRIDE * (r - CMP_TAB_ZERO) - (CMP_BLOCK - 1))
        tc_ref[0, r0:r0 + rows, :] = _bias_rows(rel_ref, d, d >= 0, False)
    for r0 in range(0, NEAR_KEYS, rows):
        d = dist_of(rows, r0, lambda r, t: NEAR_BACK + t - r)
        tn_ref[0, r0:r0 + rows, :] = _bias_rows(rel_ref, d, d >= 0, True)
    for r0 in range(0, WIN_KEYS, rows):
        d = dist_of(rows, r0, lambda r, t: WINDOW + t - r)
        tw_ref[0, r0:r0 + rows, :] = _bias_rows(rel_ref, d, (d >= 0) & (d < WINDOW), False)


def _nsa_tables(rel_bias):
    hq = NSA_HG * NSA_QP
    rel4 = jnp.repeat(rel_bias.reshape(REL_BUCKETS, NSA_G, NSA_HG).transpose(1, 0, 2), NSA_QP, axis=-1)
    spec = lambda r: pl.BlockSpec((1, r, hq), lambda g: (g, 0, 0))
    return pl.pallas_call(
        _nsa_bias_kernel,
        out_shape=(jax.ShapeDtypeStruct((NSA_G, CMP_TAB_ROWS, hq), F32),
                   jax.ShapeDtypeStruct((NSA_G, NEAR_KEYS, hq), F32),
                   jax.ShapeDtypeStruct((NSA_G, WIN_KEYS, hq), F32)),
        grid=(NSA_G,),
        in_specs=[spec(REL_BUCKETS)],
        out_specs=(spec(CMP_TAB_ROWS), spec(NEAR_KEYS), spec(WIN_KEYS)),
        compiler_params=_cparams(("parallel",)),
        name="nsa_bias_tables",
    )(rel4)


FAR_KEYS = 512
KV_FRONT = WINDOW
KREP = NSA_HG * NSA_DH


def _softmax_cols(s):
    m = jnp.max(s, axis=0, keepdims=True)
    p = jnp.exp2(s - m)
    return m, p, jnp.sum(p, axis=0, keepdims=True)


def _rank_select(score_ref, n_sb, n_sel):
    groups = n_sb // SUBLANES
    sub = lax.broadcasted_iota(jnp.int32, (SUBLANES, NSA_QP), 0)
    tiles = [score_ref[SUBLANES * v:SUBLANES * (v + 1), :] for v in range(groups)]
    cnts = [jnp.zeros((SUBLANES, NSA_QP), F32) for _ in range(groups)]
    for jp in range(n_sb):
        row = score_ref[jp:jp + 1, :]
        for v in range(groups):
            lo = SUBLANES * v
            if jp < lo:
                beats = row >= tiles[v]
            elif jp >= lo + SUBLANES - 1:
                beats = row > tiles[v]
            else:
                beats = (row > tiles[v]) | ((row == tiles[v]) & (sub > jp - lo))
            cnts[v] = cnts[v] + jnp.where(beats, 1.0, 0.0)
    cnt = jnp.concatenate(cnts, axis=0)
    return jnp.where(cnt < float(n_sel), 1.0, 0.0)


def _nsa_kernel(q_ref, gate_ref, kc_ref, vct_ref, ks0_ref, ks1_ref, kw_ref, vst_ref, vwt_ref,
                tc_ref, tn_ref, tw_ref, cov_ref, o_ref, score_ref, *, n_sb):
    p2 = pl.program_id(1)
    hq = NSA_HG * NSA_QP
    nch = kc_ref.shape[1]
    groups = range(NSA_G)
    qcol = lambda g: slice(KREP * g, KREP * (g + 1))
    vrow = lambda g: slice(NSA_DH * g, NSA_DH * (g + 1))
    vslab = lambda g: slice(VT_SLAB * g, VT_SLAB * (g + 1))

    ks_refs = (ks0_ref, ks1_ref)
    qs, q64s = [], []
    zq = jnp.zeros((hq, NSA_DH), BF16)
    for g in groups:
        qb = (q_ref[:, qcol(g)].astype(F32) * (NSA_DH ** -0.5 * LOG2E)).astype(BF16)
        q64 = jnp.concatenate([qb[:, NSA_DH * h:NSA_DH * (h + 1)] for h in range(NSA_HG)], axis=0)
        q64s.append(q64)
        qs.append(jnp.concatenate([q64, zq] if g == 0 else [zq, q64], axis=1))

    start_c = pl.multiple_of(CMP_TAB_ZERO - (NSA_QP // CMP_STRIDE) * p2, SUBLANES)
    lane = lax.broadcasted_iota(jnp.int32, (1, hq), 1)
    tq = NSA_QP * p2 + lax.bitwise_and(lane, NSA_QP - 1)
    anyv = jnp.where(tq >= CMP_BLOCK - 1, 1.0, 0.0)
    jrow = lax.broadcasted_iota(jnp.int32, (n_sb, NSA_QP), 0)
    tok = lax.broadcasted_iota(jnp.int32, (n_sb, NSA_QP), 1)
    cur = STEP_BLOCKS * p2 + lax.shift_right_logical(tok, 6)
    forced = (jrow == 0) | (jrow == cur) | (jrow == cur - 1)
    o_cmp = []
    for g in groups:
        sc = _dot_nt(kc_ref[0], qs[g]) + tc_ref[g, pl.ds(start_c, nch), :]
        _, pc, lc = _softmax_cols(sc)
        pc = pc * (anyv / lc)
        o_cmp.append(_dot(vct_ref[0, vrow(g), :], pc.astype(BF16)))
        psum = pc[:, 0:NSA_QP]
        for h in range(1, NSA_HG):
            psum = psum + pc[:, NSA_QP * h:NSA_QP * (h + 1)]
        p_hi = psum.astype(BF16)
        p_lo = (psum - p_hi.astype(F32)).astype(BF16)
        imp = _dot(cov_ref[...], p_hi) + _dot(cov_ref[...], p_lo)
        score_ref[g] = jnp.where(forced, FORCE_SCORE, jnp.where(jrow <= cur, imp, -1.0))

    def mask_operand(g, keep):
        mb = ((keep - 1.0) * -NEG_INF).T.astype(BF16)
        if n_sb < NSA_DH:
            mb = jnp.concatenate([mb, jnp.zeros((NSA_QP, NSA_DH - n_sb), BF16)], axis=1)
        return jnp.concatenate([q64s[g], jnp.concatenate([mb] * NSA_HG, axis=0)], axis=1)

    q_near, q_far = [], []
    for g in groups:
        sel = _rank_select(score_ref.at[g], n_sb, min(N_SELECT, n_sb))
        q_near.append(mask_operand(g, sel))
        far_blocks = STEP_BLOCKS * p2 - NEAR_BACK // SLC_BLOCK
        q_far.append(mask_operand(g, jnp.where(jrow < far_blocks, sel, 0.0)))

    win0 = pl.multiple_of(NSA_QP * p2, LANES)
    near0 = pl.multiple_of(win0 + KV_FRONT - NEAR_BACK, LANES)
    state = []
    for g in groups:
        s = _dot_nt(ks_refs[g][0, pl.ds(near0, NEAR_KEYS), :], q_near[g]) + tn_ref[g]
        s = jnp.concatenate([jnp.where(p2 > 0, s[0:NEAR_BACK], NEG_INF), s[NEAR_BACK:]], axis=0)
        m_s = jnp.max(s, axis=0, keepdims=True)
        pv = _dot(vst_ref[0, vslab(g), pl.ds(near0, NEAR_KEYS)], jnp.exp2(s - m_s).astype(BF16))
        state += [m_s, pv[NSA_DH:NSA_DH + 1], pv[0:NSA_DH]]

    window = []
    for g in groups:
        sw = _dot_nt(kw_ref[0, pl.ds(win0, WIN_KEYS), :], qs[g]) + tw_ref[g]
        slabs = [sw[NSA_QP * j:NSA_QP * (j + 1)] for j in range(WIN_KEYS // NSA_QP)]
        for j in range(KV_FRONT // NSA_QP):
            slabs[j] = jnp.where(NSA_QP * j + win0 >= KV_FRONT, slabs[j], NEG_INF)
        sw = jnp.concatenate(slabs, axis=0)
        p_w = jnp.exp2(sw - jnp.max(sw, axis=0, keepdims=True)).astype(BF16)
        pv = _dot(vwt_ref[0, vslab(g), pl.ds(win0, WIN_KEYS)], p_w)
        window.append((pv[0:NSA_DH], pv[NSA_DH:NSA_DH + 1]))

    def far_body(c, carry):
        k0 = pl.multiple_of(FAR_KEYS * c + KV_FRONT, LANES)
        sfs = [_dot_nt(ks_refs[g][0, pl.ds(k0, FAR_KEYS), :], q_far[g]) for g in groups]
        out = []
        for g in groups:
            m_old, l_old, acc_old = carry[3 * g:3 * g + 3]
            sf = sfs[g]
            m_new = jnp.maximum(m_old, jnp.max(sf, axis=0, keepdims=True))
            alpha = jnp.exp2(m_old - m_new)
            pv = _dot(vst_ref[0, vslab(g), pl.ds(k0, FAR_KEYS)], jnp.exp2(sf - m_new).astype(BF16))
            out += [m_new, alpha * l_old + pv[NSA_DH:NSA_DH + 1], alpha * acc_old + pv[0:NSA_DH]]
        return tuple(out)

    n_far = lax.div(jnp.maximum(p2 * NSA_QP - NEAR_BACK, 0) + FAR_KEYS - 1, FAR_KEYS)
    state = lax.fori_loop(0, n_far, far_body, tuple(state))

    gt = gate_ref[...].astype(F32).T
    r = lax.broadcasted_iota(jnp.int32, (NSA_QP, NSA_QP), 0)
    c = lax.broadcasted_iota(jnp.int32, (NSA_QP, NSA_QP), 1)
    eye = jnp.where(r == c, 1.0, 0.0).astype(BF16)
    for g in groups:
        _, l_s, acc_s = state[3 * g:3 * g + 3]
        acc_w, l_w = window[g]
        gsel = jax.nn.sigmoid(gt[3 * NSA_HG * g:3 * NSA_HG * (g + 1)])
        gate = lambda b: jnp.concatenate([gsel[3 * h + b:3 * h + b + 1] for h in range(NSA_HG)], axis=1)
        out_t = (gate(0) * o_cmp[g] + (gate(1) / l_s) * acc_s + (gate(2) / l_w) * acc_w).astype(BF16)
        stacked = jnp.concatenate([out_t[:, NSA_QP * h:NSA_QP * (h + 1)] for h in range(NSA_HG)], axis=0)
        o_ref[:, qcol(g)] = _dot_nt(eye, stacked).astype(BF16)


def _nsa_attention(z, kc, vct, kk, vvt, tables, bsz, seq):
    g, hg, dh, qp = NSA_G, NSA_HG, NSA_DH, NSA_QP
    nstep = seq // qp
    n_sb = seq // SLC_BLOCK
    nch = kc.shape[1]
    hq = hg * qp
    sp = kk.shape[1]
    tc, tn, tw = tables
    c_start = CMP_STRIDE * np.arange(nch)
    s_start = SLC_BLOCK * np.arange(n_sb)
    cover_t = ((c_start[None, :] < s_start[:, None] + SLC_BLOCK)
               & (c_start[None, :] + CMP_BLOCK > s_start[:, None])
               & (np.arange(nch)[None, :] < (seq - CMP_BLOCK) // CMP_STRIDE + 1))
    cover_t = jnp.asarray(cover_t.astype(np.float32), BF16)
    full = lambda a: pl.BlockSpec(a.shape, lambda b, i: (0,) * a.ndim, pipeline_mode=pl.Buffered(1))
    qd = g * hg * dh
    return pl.pallas_call(
        functools.partial(_nsa_kernel, n_sb=n_sb),
        out_shape=jax.ShapeDtypeStruct((bsz * seq, qd), BF16),
        grid=(bsz, nstep),
        in_specs=[pl.BlockSpec((qp, qd), lambda b, i: (b * nstep + i, Z_Q // qd)),
                  pl.BlockSpec((qp, LANES), lambda b, i: (b * nstep + i, Z_GN // LANES)),
                  pl.BlockSpec((1, nch, GD), lambda b, i: (b, 0, 0)),
                  pl.BlockSpec((1, g * dh, nch), lambda b, i: (b, 0, 0)),
                  pl.BlockSpec((1, sp, GD), lambda b, i: (b, 0, 0)),
                  pl.BlockSpec((1, sp, GD), lambda b, i: (b, 0, 1)),
                  pl.BlockSpec((1, sp, GD), lambda b, i: (b, 0, 2)),
                  pl.BlockSpec((1, g * VT_SLAB, sp), lambda b, i: (b, 0, 0)),
                  pl.BlockSpec((1, g * VT_SLAB, sp), lambda b, i: (b, 1, 0)),
                  full(tc), full(tn), full(tw), full(cover_t)],
        out_specs=pl.BlockSpec((qp, qd), lambda b, i: (b * nstep + i, 0)),
        scratch_shapes=[pltpu.VMEM((g, n_sb, qp), F32)],
        compiler_params=_cparams(("parallel", "arbitrary")),
        name="nsa_attention",
    )(z, z, kc, vct, kk, kk, kk, vvt, vvt, tc, tn, tw, cover_t)


def _rope_table_kernel(pos_ref, inv_ref, o_ref):
    ang = inv_ref[...] * pos_ref[0].astype(F32)
    c, s = jnp.cos(ang), jnp.sin(ang)
    o_ref[...] = jnp.concatenate([c, c, -s, s], axis=0).T


def _rope_table(positions):
    tm = TOK_TILE
    t = positions.size
    half = QK_ROPE // 2
    inv = (ROPE_THETA ** (-jnp.arange(half, dtype=F32) / half))[:, None]
    return pl.pallas_call(
        _rope_table_kernel,
        out_shape=jax.ShapeDtypeStruct((t, 2 * QK_ROPE), F32),
        grid=(t // tm,),
        in_specs=[pl.BlockSpec((1, 1, tm), lambda i: (i, 0, 0)),
                  pl.BlockSpec((half, 1), lambda i: (0, 0))],
        out_specs=pl.BlockSpec((tm, 2 * QK_ROPE), lambda i: (i, 0)),
        compiler_params=_cparams(("parallel",)),
        name="rope_table",
    )(positions.reshape(t // tm, 1, tm), inv)


MLA_HW = 256


def _mla_proj_kernel(cq_ref, ckv_ref, kr_ref, rope_ref, nq_ref, nkv_ref, wq_ref, wkn_ref, wvt_ref,
                     q_ref, k_ref, vt_ref):
    scale = (QK_NOPE + QK_ROPE) ** -0.5 * LOG2E
    rope = rope_ref[...]
    yq = _dot(_rms(cq_ref[...].astype(F32), nq_ref[...]).astype(BF16), wq_ref[...])
    ckv = _rms(ckv_ref[...].astype(F32), nkv_ref[...]).astype(BF16)
    ykn = _dot(ckv, wkn_ref[...])
    vt_ref[0] = _dot_nt(wvt_ref[...], ckv).astype(BF16)
    kp = kr_ref[...].astype(F32) * rope
    kp = kp + pltpu.roll(kp, QK_ROPE, 1)
    lane = lax.broadcasted_iota(jnp.int32, kp.shape, 1)
    kp = jnp.where(lane < QK_ROPE, kp, 0.0).astype(BF16)
    for h in range(MLA_HEADS):
        base = MLA_HW * h
        q_ref[:, base:base + QK_NOPE] = (yq[:, base:base + QK_NOPE] * scale).astype(BF16)
        qp = yq[:, base + QK_NOPE:base + MLA_HW] * rope
        qp = qp + pltpu.roll(qp, QK_ROPE, 1)
        q_ref[:, base + QK_NOPE:base + MLA_HW] = (qp * scale).astype(BF16)
        k_ref[:, base:base + QK_NOPE] = ykn[:, QK_NOPE * h:QK_NOPE * (h + 1)].astype(BF16)
        k_ref[:, base + QK_NOPE:base + MLA_HW] = kp


def _swap_halves(w):
    half = QK_ROPE // 2
    return jnp.concatenate([w[..., half:], w[..., :half]], axis=-1)


def _mla_proj(z, rope_tab, norm_q, norm_kv, w_uq, w_ukv, bsz, seq):
    tm = TOK_TILE
    t = z.shape[0]
    nst = seq // tm
    wq = w_uq.reshape(Q_RANK, MLA_HEADS, QK_NOPE + QK_ROPE)
    wq = jnp.concatenate([wq, _swap_halves(wq[..., QK_NOPE:])], axis=-1)
    wq = wq.reshape(Q_RANK, MLA_HEADS * MLA_HW).astype(BF16)
    wkv = w_ukv.reshape(KV_RANK, MLA_HEADS, QK_NOPE + V_DIM)
    wkn = wkv[..., :QK_NOPE].reshape(KV_RANK, MLA_HEADS * QK_NOPE).astype(BF16)
    wvt = wkv[..., QK_NOPE:].reshape(KV_RANK, MLA_HEADS * V_DIM).T.astype(BF16)
    hw = MLA_HEADS * MLA_HW
    hv = MLA_HEADS * V_DIM
    row = lambda b, s: b * nst + s
    return pl.pallas_call(
        _mla_proj_kernel,
        out_shape=(jax.ShapeDtypeStruct((t, hw), BF16),
                   jax.ShapeDtypeStruct((t, hw), BF16),
                   jax.ShapeDtypeStruct((bsz, hv, seq), BF16)),
        grid=(bsz, nst),
        in_specs=[pl.BlockSpec((tm, Q_RANK), lambda b, s: (row(b, s), Z_CQ // Q_RANK)),
                  pl.BlockSpec((tm, KV_RANK), lambda b, s: (row(b, s), Z_CKV // KV_RANK)),
                  pl.BlockSpec((tm, 2 * QK_ROPE), lambda b, s: (row(b, s), Z_KR // (2 * QK_ROPE))),
                  pl.BlockSpec((tm, 2 * QK_ROPE), lambda b, s: (row(b, s), 0)),
                  pl.BlockSpec((1, Q_RANK), lambda b, s: (0, 0)),
                  pl.BlockSpec((1, KV_RANK), lambda b, s: (0, 0)),
                  pl.BlockSpec((Q_RANK, hw), lambda b, s: (0, 0)),
                  pl.BlockSpec((KV_RANK, hv), lambda b, s: (0, 0)),
                  pl.BlockSpec((hv, KV_RANK), lambda b, s: (0, 0))],
        out_specs=(pl.BlockSpec((tm, hw), lambda b, s: (row(b, s), 0)),
                   pl.BlockSpec((tm, hw), lambda b, s: (row(b, s), 0)),
                   pl.BlockSpec((1, hv, tm), lambda b, s: (b, 0, s))),
        compiler_params=_cparams(("parallel", "parallel")),
        name="mla_proj",
    )(z, z, z, rope_tab, norm_q[None], norm_kv[None], wq, wkn, wvt)


def _mla_attn_kernel(q_ref, k_ref, vt_ref, o_ref, *, tq, tk, nh):
    iq = pl.program_id(2)
    cd = lax.div(iq * tq, tk)
    heads = range(nh)
    hcol = lambda h: slice(MLA_HW * h, MLA_HW * (h + 1))
    vrow = lambda h: slice(V_DIM * h, V_DIM * (h + 1))
    qs = [q_ref[:, hcol(h)] for h in heads]

    def scores(c, h):
        k0 = pl.multiple_of(c * tk, tk)
        return _dot_nt(k_ref[0, pl.ds(k0, tk), hcol(h)], qs[h])

    def diagonal(nk):
        k0 = pl.multiple_of((iq + 1) * tq - nk, tq)
        kpos = k0 + lax.broadcasted_iota(jnp.int32, (nk, tq), 0)
        qpos = iq * tq + lax.broadcasted_iota(jnp.int32, (nk, tq), 1)
        st = []
        for h in heads:
            s = jnp.where(kpos <= qpos, _dot_nt(k_ref[0, pl.ds(k0, nk), hcol(h)], qs[h]), NEG_INF)
            m0, p0, l0 = _softmax_cols(s)
            st += [m0, l0, _dot(vt_ref[0, vrow(h), pl.ds(k0, nk)], p0.astype(BF16))]
        return tuple(st)

    assert tk == 2 * tq
    state = lax.cond(lax.rem(iq, 2) == 0, lambda: diagonal(tq), lambda: diagonal(tk))

    def body(c, carry):
        k0 = pl.multiple_of(c * tk, tk)
        ss = [scores(c, h) for h in heads]
        out = []
        for h in heads:
            m_old, l_old, acc_old = carry[3 * h:3 * h + 3]
            m_new = jnp.maximum(m_old, jnp.max(ss[h], axis=0, keepdims=True))
            alpha = jnp.exp2(m_old - m_new)
            p = jnp.exp2(ss[h] - m_new)
            l_new = alpha * l_old + jnp.sum(p, axis=0, keepdims=True)
            acc_new = alpha * acc_old + _dot(vt_ref[0, vrow(h), pl.ds(k0, tk)], p.astype(BF16))
            out += [m_new, l_new, acc_new]
        return tuple(out)

    state = lax.fori_loop(0, cd, body, tuple(state))
    for h in heads:
        _, l, acc = state[3 * h:3 * h + 3]
        o_ref[:, vrow(h)] = (acc / l).T.astype(BF16)


def _mla_attention(qf, kf, vt, bsz, seq):
    tq, tk, nh = MLA_TQ, MLA_TK, MLA_HEADS_PER_STEP
    h = MLA_HEADS
    nq = seq // tq
    k3 = kf.reshape(bsz, seq, h * MLA_HW)
    return pl.pallas_call(
        functools.partial(_mla_attn_kernel, tq=tq, tk=tk, nh=nh),
        out_shape=jax.ShapeDtypeStruct((bsz * seq, h * V_DIM), BF16),
        grid=(bsz, h // nh, nq),
        in_specs=[pl.BlockSpec((tq, nh * MLA_HW), lambda b, hh, i: (b * nq + i, hh)),
                  pl.BlockSpec((1, seq, nh * MLA_HW), lambda b, hh, i: (b, 0, hh)),
                  pl.BlockSpec((1, nh * V_DIM, seq), lambda b, hh, i: (b, hh, 0))],
        out_specs=pl.BlockSpec((tq, nh * V_DIM), lambda b, hh, i: (b * nq + i, hh)),
        compiler_params=_cparams(("parallel", "parallel", "arbitrary")),
        name="mla_attention",
    )(qf, k3, vt)


def _merge_xattn_kernel(ya_ref, yb_ref, yc_ref, ga_ref, gb_ref, gc_ref, x_ref,
                        wa_ref, wb_ref, wc_ref, wo_ref,
                        gx_ref, wq_ref, kv_ref, wxo_ref, o_ref):
    sig = lambda ref: jax.nn.sigmoid(ref[...].astype(F32))
    y = (sig(ga_ref) * _dot(ya_ref[...], wa_ref[...])
         + sig(gb_ref) * _dot(yb_ref[...], wb_ref[...])
         + sig(gc_ref) * _dot(yc_ref[...], wc_ref[...]))
    x = x_ref[...] + _dot(y.astype(BF16), wo_ref[...])
    h = _rms(x, gx_ref[...]).astype(BF16)
    q = _dot(h, wq_ref[...]) * XATTN_DH ** -0.5
    hd = XATTN_HEADS * XATTN_DH
    outs = []
    for hh in range(XATTN_HEADS):
        qh = q[:, XATTN_DH * hh:XATTN_DH * (hh + 1)].astype(BF16)
        kh = kv_ref[0, :, XATTN_DH * hh:XATTN_DH * (hh + 1)]
        vh = kv_ref[0, :, hd + XATTN_DH * hh:hd + XATTN_DH * (hh + 1)]
        s = _dot_nt(qh, kh)
        m = jnp.max(s, axis=-1, keepdims=True)
        p = jnp.exp(s - m)
        p = p / jnp.sum(p, axis=-1, keepdims=True)
        outs.append(_dot(p.astype(BF16), vh))
    o = jnp.concatenate(outs, axis=-1).astype(BF16)
    o_ref[...] = x + _dot(o, wxo_ref[...])


def _merge_xattn(ya, yb, yc, z, x, wa, wb, wc, wo, gx, wq, kv, wxo, bsz, seq):
    tm = TOK_TILE
    t, d = x.shape
    nst = seq // tm
    m_len = kv.shape[1]
    hd = XATTN_HEADS * XATTN_DH
    row = lambda b, s: b * nst + s
    act = pl.BlockSpec((tm, ya.shape[1]), lambda b, s: (row(b, s), 0))
    gate = lambda k: pl.BlockSpec((tm, d), lambda b, s: (row(b, s), Z_GM // d + k))
    const = lambda shape: pl.BlockSpec(shape, lambda b, s: (0, 0))
    bf = lambda w: w.astype(BF16)
    return pl.pallas_call(
        _merge_xattn_kernel,
        out_shape=jax.ShapeDtypeStruct((t, d), F32),
        grid=(bsz, nst),
        in_specs=[act, act, act, gate(0), gate(1), gate(2),
                  pl.BlockSpec((tm, d), lambda b, s: (row(b, s), 0)),
                  const((ya.shape[1], d)), const((ya.shape[1], d)), const((ya.shape[1], d)), const((d, d)),
                  const((1, d)), const((d, hd)),
                  pl.BlockSpec((1, m_len, 2 * hd), lambda b, s: (b, 0, 0)),
                  const((hd, d))],
        out_specs=pl.BlockSpec((tm, d), lambda b, s: (row(b, s), 0)),
        compiler_params=_cparams(("parallel", "parallel")),
        name="merge_xattn",
    )(ya, yb, yc, z, z, z, x, bf(wa), bf(wb), bf(wc), bf(wo), gx[None], bf(wq), kv, bf(wxo))


def _ffn_kernel(x_ref, g_ref, wg_ref, wu_ref, wd_ref, gf_ref, o_ref, h_ref, acc_ref, *, final):
    c = pl.program_id(1)

    @pl.when(c == 0)
    def _():
        h_ref[...] = _rms(x_ref[...], g_ref[...]).astype(BF16)
        acc_ref[...] = x_ref[...]

    h = h_ref[...]
    gate = _dot(h, wg_ref[...])
    up = _dot(h, wu_ref[...])
    act = (gate * jax.nn.sigmoid(gate) * up).astype(BF16)
    acc_ref[...] += _dot(act, wd_ref[...])

    @pl.when(c == pl.num_programs(1) - 1)
    def _():
        y = acc_ref[...]
        o_ref[...] = _rms(y, gf_ref[...]) if final else y


def _ffn(x, g, w_gate_up, w_down, g_final, final):
    tm, tc = FFN_TM, FFN_TC
    t, d = x.shape
    nc = FFN_HIDDEN // tc
    wgu = w_gate_up.astype(BF16)
    return pl.pallas_call(
        functools.partial(_ffn_kernel, final=final),
        out_shape=jax.ShapeDtypeStruct((t, d), F32),
        grid=(t // tm, nc),
        in_specs=[pl.BlockSpec((tm, d), lambda i, c: (i, 0)),
                  pl.BlockSpec((1, d), lambda i, c: (0, 0)),
                  pl.BlockSpec((d, tc), lambda i, c: (0, c)),
                  pl.BlockSpec((d, tc), lambda i, c: (0, nc + c)),
                  pl.BlockSpec((tc, d), lambda i, c: (c, 0)),
                  pl.BlockSpec((1, d), lambda i, c: (0, 0))],
        out_specs=pl.BlockSpec((tm, d), lambda i, c: (i, 0)),
        scratch_shapes=[pltpu.VMEM((tm, d), BF16), pltpu.VMEM((tm, d), F32)],
        compiler_params=_cparams(("parallel", "arbitrary")),
        name="ffn",
    )(x, g[None], wgu, wgu, w_down.astype(BF16), g_final[None])


def _split_w_in(w):
    k_rope = w[:, O_KR:O_KR + QK_ROPE]
    kv = lambda kind: w[:, O_KV + GD * kind:O_KV + GD * (kind + 1)]
    pad = jnp.zeros((w.shape[0], Z_COLS - Z_GN - 3 * NSA_HEADS), w.dtype)
    wz = jnp.concatenate([
        w[:, O_GM:O_GM + 3 * D_MODEL],
        w[:, O_GLU:O_GLU + 2 * CONV_CH],
        w[:, O_Q:O_Q + NSA_HEADS * NSA_DH],
        w[:, O_CKV:O_CKV + KV_RANK],
        k_rope, _swap_halves(k_rope),
        w[:, O_CQ:O_CQ + Q_RANK],
        kv(0), kv(1),
        w[:, O_GN:O_GN + 3 * NSA_HEADS], pad], axis=1).astype(BF16)
    z64 = jnp.zeros((w.shape[0], NSA_DH), w.dtype)
    wk = jnp.concatenate([kv(2)[:, :NSA_DH], z64, kv(2)[:, NSA_DH:], z64, kv(4)], axis=1).astype(BF16)
    wvt = jnp.concatenate([kv(3), kv(5)], axis=1).T.astype(BF16)
    return wz, wk, wvt


def kernel(x, mem, positions, rel_bias, norm_mix, norm_xattn, norm_mem, norm_ffn, norm_final, w_in, conv_w, conv_b, conv_ln_g, conv_ln_b, w_branch_conv, cmp_pos_k, cmp_w1_k, cmp_b1_k, cmp_w2_k, cmp_pos_v, cmp_w1_v, cmp_b1_v, cmp_w2_v, w_branch_nsa, mla_norm_q, mla_norm_kv, w_uq, w_ukv, w_branch_mla, w_out, w_xq, w_xkv, w_xo, w_gate_up, w_down):
    bsz, seq, d = x.shape
    depth = w_in.shape[0]
    t = bsz * seq
    m_len = mem.shape[1]
    xt = x.reshape(t, d)
    memt = mem.reshape(bsz * m_len, d)
    rope_tab = _rope_table(positions)
    tables = _nsa_tables(rel_bias)
    for l in range(depth):
        wz, wk, wvt = _split_w_in(w_in[l])
        z, zc = _in_proj(xt, norm_mix[l][None], wz)
        kk, vvt = _kv_proj(xt, norm_mix[l][None], wk, wvt, bsz, seq)
        ya = _conv_module(z, conv_w[l], conv_b[l], conv_ln_g[l], conv_ln_b[l], bsz, seq)
        kc, vct = _compress(zc, jnp.stack([cmp_pos_k[l], cmp_pos_v[l]]), jnp.stack([cmp_w1_k[l], cmp_w1_v[l]]),
                            jnp.stack([cmp_b1_k[l], cmp_b1_v[l]]), jnp.stack([cmp_w2_k[l], cmp_w2_v[l]]), bsz, seq)
        yb = _nsa_attention(z, kc, vct, kk, vvt, tables, bsz, seq)
        qf, kf, vt = _mla_proj(z, rope_tab, mla_norm_q[l], mla_norm_kv[l], w_uq[l], w_ukv[l], bsz, seq)
        yc = _mla_attention(qf, kf, vt, bsz, seq)
        mem_kv = _norm_matmul(memt, norm_mem[l][None], w_xkv[l].astype(BF16), 256, 1024, BF16)
        mem_kv = mem_kv.reshape(bsz, m_len, 2 * XATTN_HEADS * XATTN_DH)
        xt = _merge_xattn(ya, yb, yc, z, xt, w_branch_conv[l], w_branch_nsa[l], w_branch_mla[l], w_out[l],
                          norm_xattn[l], w_xq[l], mem_kv, w_xo[l], bsz, seq)
        xt = _ffn(xt, norm_ffn[l], w_gate_up[l], w_down[l], norm_final, l == depth - 1)
    return xt.reshape(bsz, seq, d)
```

```python
import functools
import math

import numpy as np
import jax
import jax.numpy as jnp
from jax import lax
from jax.experimental import pallas as pl
from jax.experimental.pallas import tpu as pltpu

F32 = jnp.float32
BF16 = jnp.bfloat16

EPS = 1e-6
NEG_INF = -1e30
FORCE_SCORE = 1e4

D_MODEL = 1024
CONV_CH = 512
CONV_WIDTH = 31
NSA_HEADS = 8
NSA_G = 2
NSA_HG = NSA_HEADS // NSA_G
NSA_DH = 64
CMP_BLOCK = 32
CMP_STRIDE = 16
CMP_HIDDEN = 256
SLC_BLOCK = 64
N_SELECT = 16
WINDOW = 512
NSA_QB = 64
MLA_HEADS = 4
Q_RANK = 384
KV_RANK = 256
QK_NOPE = 128
QK_ROPE = 64
V_DIM = 128
ROPE_THETA = 10000.0
REL_BUCKETS = 32
REL_MAX_DIST = 128
XATTN_HEADS = 4
XATTN_DH = 128
FFN_HIDDEN = 2816

LANES = 128
SUBLANES = 8

O_GLU, O_Q, O_KV, O_GN, O_CQ, O_CKV, O_KR, O_GM = 0, 1024, 1536, 2304, 2328, 2712, 2968, 3032
GD = NSA_G * NSA_DH

Z_GM = 0
Z_UA = 3072
Z_UB = 3584
Z_Q = 4096
Z_CKV = 4608
Z_KR = 4864
Z_CQ = 4992
Z_CMP = 5376
Z_GN = 5632
Z_COLS = 5760

VMEM_LIMIT = 56 * 1024 * 1024

TOK_TILE = 512
IN_PROJ_TM = 1024
IN_PROJ_TN = 1152
FFN_TM = 1024
FFN_TC = 256
MLA_TQ = 512
MLA_TK = 1024
MLA_HEADS_PER_STEP = 2

LOG2E = math.log2(math.e)


def _cparams(sem):
    return pltpu.CompilerParams(dimension_semantics=sem, vmem_limit_bytes=VMEM_LIMIT)


def _rms(x, g):
    return x * lax.rsqrt(jnp.mean(x * x, axis=-1, keepdims=True) + EPS) * g


def _dot(a, b):
    return jnp.dot(a, b, preferred_element_type=F32)


def _dot_nt(a, b):
    return lax.dot_general(a, b, (((1,), (1,)), ((), ())), preferred_element_type=F32)


def _norm_matmul_kernel(x_ref, g_ref, w_ref, o_ref, h_ref):
    @pl.when(pl.program_id(1) == 0)
    def _():
        h_ref[...] = _rms(x_ref[...], g_ref[...]).astype(BF16)

    o_ref[...] = _dot(h_ref[...], w_ref[...]).astype(o_ref.dtype)


def _norm_matmul(x, g, w, tm, tn, out_dtype):
    m, k = x.shape
    n = w.shape[1]
    return pl.pallas_call(
        _norm_matmul_kernel,
        out_shape=jax.ShapeDtypeStruct((m, n), out_dtype),
        grid=(m // tm, n // tn),
        in_specs=[pl.BlockSpec((tm, k), lambda i, j: (i, 0)),
                  pl.BlockSpec((1, k), lambda i, j: (0, 0)),
                  pl.BlockSpec((k, tn), lambda i, j: (0, j))],
        out_specs=pl.BlockSpec((tm, tn), lambda i, j: (i, j)),
        scratch_shapes=[pltpu.VMEM((tm, k), BF16)],
        compiler_params=_cparams(("parallel", "arbitrary")),
        name="norm_matmul",
    )(x, g, w)


def _in_proj_kernel(x_ref, g_ref, w_ref, z_ref, zc_ref, h_ref, *, cmp_tile, cmp_off):
    @pl.when(pl.program_id(1) == 0)
    def _():
        h_ref[...] = _rms(x_ref[...], g_ref[...]).astype(BF16)

    acc = _dot(h_ref[...], w_ref[...])
    z_ref[...] = acc.astype(BF16)

    @pl.when(pl.program_id(1) == cmp_tile)
    def _():
        zc_ref[...] = acc[:, cmp_off:cmp_off + 2 * GD]


def _in_proj(x, g, w):
    tm, tn = IN_PROJ_TM, IN_PROJ_TN
    m, k = x.shape
    n = w.shape[1]
    return pl.pallas_call(
        functools.partial(_in_proj_kernel, cmp_tile=Z_CMP // tn, cmp_off=Z_CMP % tn),
        out_shape=(jax.ShapeDtypeStruct((m, n), BF16), jax.ShapeDtypeStruct((m, 2 * GD), F32)),
        grid=(m // tm, n // tn),
        in_specs=[pl.BlockSpec((tm, k), lambda i, j: (i, 0)),
                  pl.BlockSpec((1, k), lambda i, j: (0, 0)),
                  pl.BlockSpec((k, tn), lambda i, j: (0, j))],
        out_specs=(pl.BlockSpec((tm, tn), lambda i, j: (i, j)),
                   pl.BlockSpec((tm, 2 * GD), lambda i, j: (i, 0))),
        scratch_shapes=[pltpu.VMEM((tm, k), BF16)],
        compiler_params=_cparams(("parallel", "arbitrary")),
        name="in_proj",
    )(x, g, w)


VT_SLAB = NSA_DH + 16


def _kv_proj_kernel(x_ref, g_ref, wk_ref, wvt_ref, k_ref, vt_ref):
    @pl.when(pl.program_id(1) == 0)
    def _():
        k_ref[...] = jnp.zeros(k_ref.shape, BF16)
        vt_ref[...] = jnp.zeros(vt_ref.shape, BF16)

    @pl.when(pl.program_id(1) > 0)
    def _():
        h = _rms(x_ref[...], g_ref[...]).astype(BF16)
        k = _dot(h, wk_ref[...])
        tm = k.shape[0]
        row = lax.broadcasted_iota(jnp.int32, k.shape, 0)
        lane = lax.broadcasted_iota(jnp.int32, k.shape, 1)
        blk = (pl.program_id(1) - 1) * (tm // SLC_BLOCK) + lax.shift_right_logical(row, 6)
        hot = (lane < NSA_G * GD) & (lax.bitwise_and(lane, GD - 1) == blk + NSA_DH)
        k_ref[0] = jnp.where(hot, 1.0, k).astype(BF16)
        v = _dot_nt(wvt_ref[...], h).astype(BF16)
        ones = jnp.ones((VT_SLAB - NSA_DH, tm), BF16)
        vt_ref[0] = jnp.concatenate([x for i in range(v.shape[0] // NSA_DH)
                                     for x in (v[NSA_DH * i:NSA_DH * (i + 1)], ones)], axis=0)


def _kv_proj(x, g, wk, wvt, bsz, seq):
    tm = WINDOW
    nst = seq // tm
    d = x.shape[1]
    nk = wk.shape[1]
    nv = wvt.shape[0] // NSA_DH * VT_SLAB
    return pl.pallas_call(
        _kv_proj_kernel,
        out_shape=(jax.ShapeDtypeStruct((bsz, seq + tm, nk), BF16),
                   jax.ShapeDtypeStruct((bsz, nv, seq + tm), BF16)),
        grid=(bsz, nst + 1),
        in_specs=[pl.BlockSpec((tm, d), lambda b, s: (b * nst + jnp.maximum(s - 1, 0), 0)),
                  pl.BlockSpec((1, d), lambda b, s: (0, 0)),
                  pl.BlockSpec((d, nk), lambda b, s: (0, 0)),
                  pl.BlockSpec((wvt.shape[0], d), lambda b, s: (0, 0))],
        out_specs=(pl.BlockSpec((1, tm, nk), lambda b, s: (b, s, 0)),
                   pl.BlockSpec((1, nv, tm), lambda b, s: (b, 0, s))),
        compiler_params=_cparams(("parallel", "arbitrary")),
        name="nsa_kv_proj",
    )(x, g, wk, wvt)


CONV_HALO = 32


CONV_ROWS = 64


def _conv_kernel(a_ref, b_ref, w_ref, cb_ref, lg_ref, lb_ref, o_ref, buf_ref, sh_ref, *, ts):
    @pl.when(pl.program_id(1) == 0)
    def _():
        buf_ref[0:CONV_HALO, :] = jnp.zeros((CONV_HALO, CONV_CH), F32)

    buf_ref[CONV_HALO:CONV_HALO + ts, :] = a_ref[...].astype(F32) * jax.nn.sigmoid(b_ref[...].astype(F32))
    span = ts + CONV_HALO - SUBLANES
    for r in range(1, SUBLANES):
        sh_ref[r - 1, 0:span, :] = buf_ref[r:r + span, :]
    off = CONV_HALO - (CONV_WIDTH - 1)

    def rows(i, carry):
        r0 = pl.multiple_of(i * CONV_ROWS, CONV_ROWS)
        acc = jnp.zeros((CONV_ROWS, CONV_CH), F32) + cb_ref[...]
        for k in range(CONV_WIDTH):
            res, base = (off + k) % SUBLANES, (off + k) // SUBLANES * SUBLANES
            if res == 0:
                tap = buf_ref[pl.ds(r0 + base, CONV_ROWS), :]
            else:
                tap = sh_ref[res - 1, pl.ds(r0 + base, CONV_ROWS), :]
            acc = acc + tap * w_ref[k:k + 1, :]
        mu = jnp.mean(acc, axis=-1, keepdims=True)
        xc = acc - mu
        var = jnp.mean(xc * xc, axis=-1, keepdims=True)
        y = xc * lax.rsqrt(var + EPS) * lg_ref[...] + lb_ref[...]
        o_ref[pl.ds(r0, CONV_ROWS), :] = (y * jax.nn.sigmoid(y)).astype(BF16)
        return carry

    lax.fori_loop(0, ts // CONV_ROWS, rows, 0)
    buf_ref[0:CONV_HALO, :] = buf_ref[ts:ts + CONV_HALO, :]


def _conv_module(z, conv_w, conv_b, ln_g, ln_b, bsz, seq):
    ts = TOK_TILE
    nst = seq // ts
    wpad = jnp.zeros((32, CONV_CH), F32).at[:CONV_WIDTH].set(conv_w)
    return pl.pallas_call(
        functools.partial(_conv_kernel, ts=ts),
        out_shape=jax.ShapeDtypeStruct((bsz * seq, CONV_CH), BF16),
        grid=(bsz, nst),
        in_specs=[pl.BlockSpec((ts, CONV_CH), lambda b, s: (b * nst + s, Z_UA // CONV_CH)),
                  pl.BlockSpec((ts, CONV_CH), lambda b, s: (b * nst + s, Z_UB // CONV_CH)),
                  pl.BlockSpec((32, CONV_CH), lambda b, s: (0, 0)),
                  pl.BlockSpec((1, CONV_CH), lambda b, s: (0, 0)),
                  pl.BlockSpec((1, CONV_CH), lambda b, s: (0, 0)),
                  pl.BlockSpec((1, CONV_CH), lambda b, s: (0, 0))],
        out_specs=pl.BlockSpec((ts, CONV_CH), lambda b, s: (b * nst + s, 0)),
        scratch_shapes=[pltpu.VMEM((ts + CONV_HALO, CONV_CH), F32),
                        pltpu.VMEM((SUBLANES - 1, ts + CONV_HALO - SUBLANES, CONV_CH), F32)],
        compiler_params=_cparams(("arbitrary", "arbitrary")),
        name="conv_module",
    )(z, z, wpad, conv_b[None], ln_g[None], ln_b[None])


def _compress_kernel(xk_ref, xv_ref, pos_ref, w1_ref, b1_ref, w2k_ref, w2v_ref, kc_ref, vct_ref, *, nch):
    for kind, (x_ref, w2_ref) in enumerate(((xk_ref, w2k_ref), (xv_ref, w2v_ref))):
        a = jnp.zeros((nch, NSA_G * CMP_HIDDEN), F32)
        b = jnp.zeros((nch, NSA_G * CMP_HIDDEN), F32)
        for l in range(CMP_STRIDE):
            xs = x_ref[pl.ds(l, nch, stride=CMP_STRIDE), :]
            a = a + _dot((xs + pos_ref[kind, l:l + 1, :]).astype(BF16), w1_ref[kind, l])
            b = b + _dot((xs + pos_ref[kind, CMP_STRIDE + l:CMP_STRIDE + l + 1, :]).astype(BF16),
                         w1_ref[kind, CMP_STRIDE + l])
        pre = a + pltpu.roll(b, nch - 1, 0) + b1_ref[kind]
        out = _dot(jax.nn.gelu(pre).astype(BF16), w2_ref[...])
        if kind == 0:
            kc_ref[0] = out.astype(BF16)
        else:
            vct_ref[0] = out.T.astype(BF16)


def _blockdiag2(w):
    z = jnp.zeros_like(w)
    return jnp.concatenate([jnp.concatenate([w, z], axis=-1), jnp.concatenate([z, w], axis=-1)], axis=-2)


def _compress(z, pos, w1, b1, w2, bsz, seq):
    nch = seq // CMP_STRIDE
    pos2 = jnp.concatenate([pos, pos], axis=-1)
    w1e = _blockdiag2(w1.reshape(2, CMP_BLOCK, NSA_DH, CMP_HIDDEN)).astype(BF16)
    b1e = jnp.concatenate([b1, b1], axis=-1)[:, None]
    w2k = _blockdiag2(w2[0]).astype(BF16)
    w2v = _blockdiag2(w2[1]).astype(BF16)
    full = lambda a: pl.BlockSpec(a.shape, lambda b: (0,) * a.ndim)
    return pl.pallas_call(
        functools.partial(_compress_kernel, nch=nch),
        out_shape=(jax.ShapeDtypeStruct((bsz, nch, GD), BF16),
                   jax.ShapeDtypeStruct((bsz, GD, nch), BF16)),
        grid=(bsz,),
        in_specs=[pl.BlockSpec((seq, GD), lambda b: (b, 0)),
                  pl.BlockSpec((seq, GD), lambda b: (b, 1)),
                  full(pos2), full(w1e), full(b1e), full(w2k), full(w2v)],
        out_specs=(pl.BlockSpec((1, nch, GD), lambda b: (b, 0, 0)),
                   pl.BlockSpec((1, GD, nch), lambda b: (b, 0, 0))),
        compiler_params=_cparams(("parallel",)),
        name="nsa_compress",
    )(z, z, pos2, w1e, b1e, w2k, w2v)


NSA_QP = 4 * NSA_QB
STEP_BLOCKS = NSA_QP // SLC_BLOCK
NEAR_BACK = 2 * SLC_BLOCK
NEAR_KEYS = NSA_QP + NEAR_BACK
WIN_KEYS = WINDOW + NSA_QP
CMP_TAB_ROWS = 512
CMP_TAB_ZERO = 256


def _t5_bucket_np(d):
    exact = REL_BUCKETS // 2
    d = np.maximum(d, 0)
    ratio = np.log(np.maximum(d, 1).astype(np.float32) / np.float32(exact)) / np.float32(math.log(REL_MAX_DIST / exact))
    large = np.minimum(exact + (ratio * (REL_BUCKETS - exact)).astype(np.int32), REL_BUCKETS - 1)
    return np.where(d < exact, d, large).astype(np.int32)


def _bucket_thresholds():
    exact = REL_BUCKETS // 2
    bk = _t5_bucket_np(np.arange(4 * REL_MAX_DIST))
    assert np.all(np.diff(bk) >= 0) and bk[-1] == REL_BUCKETS - 1
    return [int(np.argmax(bk >= k)) for k in range(exact + 1, REL_BUCKETS)]


def _bias_rows(rel_ref, dist, valid, shift):
    exact = REL_BUCKETS // 2
    bucket = jnp.full(dist.shape, exact, jnp.int32)
    for thr in _bucket_thresholds():
        bucket = bucket + jnp.where(dist >= thr, 1, 0)
    bucket = jnp.where(dist < exact, dist, bucket)
    val = jnp.zeros(dist.shape, F32)
    for bkt in range(REL_BUCKETS):
        val = jnp.where(bucket == bkt, rel_ref[0, bkt:bkt + 1, :], val)
    if shift:
        val = val - rel_ref[0, REL_BUCKETS - 1:REL_BUCKETS, :]
    return jnp.where(valid, val * LOG2E, NEG_INF)


def _nsa_bias_kernel(rel_ref, tc_ref, tn_ref, tw_ref):
    hq = NSA_HG * NSA_QP
    rows = 128

    def dist_of(nrows, r0, fn):
        r = r0 + lax.broadcasted_iota(jnp.int32, (nrows, hq), 0)
        t = lax.bitwise_and(lax.broadcasted_iota(jnp.int32, (nrows, hq), 1), NSA_QP - 1)
        return fn(r, t)

    for r0 in range(0, CMP_TAB_ROWS, rows):
        d = dist_of(rows, r0, lambda r, t: t - CMP_STRIDE * (r - CMP_TAB_ZERO) - (CMP_BLOCK - 1))
        tc_ref[0, r0:r0 + rows, :] = _bias_rows(rel_ref, d, d >= 0, False)
    for r0 in range(0, NEAR_KEYS, rows):
        d = dist_of(rows, r0, lambda r, t: NEAR_BACK + t - r)
        tn_ref[0, r0:r0 + rows, :] = _bias_rows(rel_ref, d, d >= 0, True)
    for r0 in range(0, WIN_KEYS, rows):
        d = dist_of(rows, r0, lambda r, t: WINDOW + t - r)
        tw_ref[0, r0:r0 + rows, :] = _bias_rows(rel_ref, d, (d >= 0) & (d < WINDOW), False)


def _nsa_tables(rel_bias):
    hq = NSA_HG * NSA_QP
    rel4 = jnp.repeat(rel_bias.reshape(REL_BUCKETS, NSA_G, NSA_HG).transpose(1, 0, 2), NSA_QP, axis=-1)
    spec = lambda r: pl.BlockSpec((1, r, hq), lambda g: (g, 0, 0))
    return pl.pallas_call(
        _nsa_bias_kernel,
        out_shape=(jax.ShapeDtypeStruct((NSA_G, CMP_TAB_ROWS, hq), F32),
                   jax.ShapeDtypeStruct((NSA_G, NEAR_KEYS, hq), F32),
                   jax.ShapeDtypeStruct((NSA_G, WIN_KEYS, hq), F32)),
        grid=(NSA_G,),
        in_specs=[spec(REL_BUCKETS)],
        out_specs=(spec(CMP_TAB_ROWS), spec(NEAR_KEYS), spec(WIN_KEYS)),
        compiler_params=_cparams(("parallel",)),
        name="nsa_bias_tables",
    )(rel4)


FAR_KEYS = 512
FAR_TAIL = FAR_KEYS - NEAR_BACK
KV_FRONT = WINDOW
KREP = NSA_HG * NSA_DH


def _softmax_cols(s):
    m = jnp.max(s, axis=0, keepdims=True)
    p = jnp.exp2(s - m)
    return m, p, jnp.sum(p, axis=0, keepdims=True)


def _rank_select(score_ref, n_sb, n_sel):
    groups = n_sb // SUBLANES
    sub = lax.broadcasted_iota(jnp.int32, (SUBLANES, NSA_QP), 0)
    tiles = [score_ref[SUBLANES * v:SUBLANES * (v + 1), :] for v in range(groups)]
    cnts = [jnp.zeros((SUBLANES, NSA_QP), F32) for _ in range(groups)]
    for jp in range(n_sb):
        row = score_ref[jp:jp + 1, :]
        for v in range(groups):
            lo = SUBLANES * v
            if jp < lo:
                beats = row >= tiles[v]
            elif jp >= lo + SUBLANES - 1:
                beats = row > tiles[v]
            else:
                beats = (row > tiles[v]) | ((row == tiles[v]) & (sub > jp - lo))
            cnts[v] = cnts[v] + jnp.where(beats, 1.0, 0.0)
    cnt = jnp.concatenate(cnts, axis=0)
    return jnp.where(cnt < float(n_sel), 1.0, 0.0)


def _nsa_kernel(q_ref, gate_ref, kc_ref, vct_ref, ks0_ref, ks1_ref, kw_ref, vst_ref, vwt_ref,
                tc_ref, tn_ref, tw_ref, cov_ref, o_ref, score_ref, *, n_sb):
    p2 = pl.program_id(1)
    hq = NSA_HG * NSA_QP
    nch = kc_ref.shape[1]
    groups = range(NSA_G)
    qcol = lambda g: slice(KREP * g, KREP * (g + 1))
    vrow = lambda g: slice(NSA_DH * g, NSA_DH * (g + 1))
    vslab = lambda g: slice(VT_SLAB * g, VT_SLAB * (g + 1))

    ks_refs = (ks0_ref, ks1_ref)
    qs, q64s = [], []
    zq = jnp.zeros((hq, NSA_DH), BF16)
    for g in groups:
        qb = (q_ref[:, qcol(g)].astype(F32) * (NSA_DH ** -0.5 * LOG2E)).astype(BF16)
        q64 = jnp.concatenate([qb[:, NSA_DH * h:NSA_DH * (h + 1)] for h in range(NSA_HG)], axis=0)
        q64s.append(q64)
        qs.append(jnp.concatenate([q64, zq] if g == 0 else [zq, q64], axis=1))

    start_c = pl.multiple_of(CMP_TAB_ZERO - (NSA_QP // CMP_STRIDE) * p2, SUBLANES)
    lane = lax.broadcasted_iota(jnp.int32, (1, hq), 1)
    tq = NSA_QP * p2 + lax.bitwise_and(lane, NSA_QP - 1)
    anyv = jnp.where(tq >= CMP_BLOCK - 1, 1.0, 0.0)
    jrow = lax.broadcasted_iota(jnp.int32, (n_sb, NSA_QP), 0)
    tok = lax.broadcasted_iota(jnp.int32, (n_sb, NSA_QP), 1)
    cur = STEP_BLOCKS * p2 + lax.shift_right_logical(tok, 6)
    forced = (jrow == 0) | (jrow == cur) | (jrow == cur - 1)
    o_cmp = []
    for g in groups:
        sc = _dot_nt(kc_ref[0], qs[g]) + tc_ref[g, pl.ds(start_c, nch), :]
        _, pc, lc = _softmax_cols(sc)
        pc = pc * (anyv / lc)
        o_cmp.append(_dot(vct_ref[0, vrow(g), :], pc.astype(BF16)))
        psum = pc[:, 0:NSA_QP]
        for h in range(1, NSA_HG):
            psum = psum + pc[:, NSA_QP * h:NSA_QP * (h + 1)]
        p_hi = psum.astype(BF16)
        p_lo = (psum - p_hi.astype(F32)).astype(BF16)
        imp = _dot(cov_ref[...], p_hi) + _dot(cov_ref[...], p_lo)
        score_ref[g] = jnp.where(forced, FORCE_SCORE, jnp.where(jrow <= cur, imp, -1.0))

    def mask_operand(g, keep):
        mb = ((keep - 1.0) * -NEG_INF).T.astype(BF16)
        if n_sb < NSA_DH:
            mb = jnp.concatenate([mb, jnp.zeros((NSA_QP, NSA_DH - n_sb), BF16)], axis=1)
        return jnp.concatenate([q64s[g], jnp.concatenate([mb] * NSA_HG, axis=0)], axis=1)

    q_near, q_far = [], []
    for g in groups:
        sel = _rank_select(score_ref.at[g], n_sb, min(N_SELECT, n_sb))
        q_near.append(mask_operand(g, sel))
        far_blocks = STEP_BLOCKS * p2 - NEAR_BACK // SLC_BLOCK
        q_far.append(mask_operand(g, jnp.where(jrow < far_blocks, sel, 0.0)))

    win0 = pl.multiple_of(NSA_QP * p2, LANES)
    near0 = pl.multiple_of(win0 + KV_FRONT - NEAR_BACK, LANES)
    state = []
    for g in groups:
        s = _dot_nt(ks_refs[g][0, pl.ds(near0, NEAR_KEYS), :], q_near[g]) + tn_ref[g]
        s = jnp.concatenate([jnp.where(p2 > 0, s[0:NEAR_BACK], NEG_INF), s[NEAR_BACK:]], axis=0)
        m_s = jnp.max(s, axis=0, keepdims=True)
        pv = _dot(vst_ref[0, vslab(g), pl.ds(near0, NEAR_KEYS)], jnp.exp2(s - m_s).astype(BF16))
        state += [m_s, pv[NSA_DH:NSA_DH + 1], pv[0:NSA_DH]]

    window = []
    for g in groups:
        sw = _dot_nt(kw_ref[0, pl.ds(win0, WIN_KEYS), :], qs[g]) + tw_ref[g]
        slabs = [sw[NSA_QP * j:NSA_QP * (j + 1)] for j in range(WIN_KEYS // NSA_QP)]
        for j in range(KV_FRONT // NSA_QP):
            slabs[j] = jnp.where(NSA_QP * j + win0 >= KV_FRONT, slabs[j], NEG_INF)
        sw = jnp.concatenate(slabs, axis=0)
        p_w = jnp.exp2(sw - jnp.max(sw, axis=0, keepdims=True)).astype(BF16)
        pv = _dot(vwt_ref[0, vslab(g), pl.ds(win0, WIN_KEYS)], p_w)
        window.append((pv[0:NSA_DH], pv[NSA_DH:NSA_DH + 1]))

    def far_update(k0, nkeys, carry):
        sfs = [_dot_nt(ks_refs[g][0, pl.ds(k0, nkeys), :], q_far[g]) for g in groups]
        out = []
        for g in groups:
            m_old, l_old, acc_old = carry[3 * g:3 * g + 3]
            sf = sfs[g]
            m_new = jnp.maximum(m_old, jnp.max(sf, axis=0, keepdims=True))
            alpha = jnp.exp2(m_old - m_new)
            pv = _dot(vst_ref[0, vslab(g), pl.ds(k0, nkeys)], jnp.exp2(sf - m_new).astype(BF16))
            out += [m_new, alpha * l_old + pv[NSA_DH:NSA_DH + 1], alpha * acc_old + pv[0:NSA_DH]]
        return tuple(out)

    n_full = lax.div(jnp.maximum(p2 * NSA_QP - NEAR_BACK, 0), FAR_KEYS)
    state = lax.fori_loop(
        0, n_full, lambda c, carry: far_update(pl.multiple_of(FAR_KEYS * c + KV_FRONT, LANES), FAR_KEYS, carry),
        tuple(state))
    state = far_update(pl.multiple_of(FAR_KEYS * n_full + KV_FRONT, LANES), FAR_TAIL, state)

    gt = gate_ref[...].astype(F32).T
    r = lax.broadcasted_iota(jnp.int32, (NSA_QP, NSA_QP), 0)
    c = lax.broadcasted_iota(jnp.int32, (NSA_QP, NSA_QP), 1)
    eye = jnp.where(r == c, 1.0, 0.0).astype(BF16)
    for g in groups:
        _, l_s, acc_s = state[3 * g:3 * g + 3]
        acc_w, l_w = window[g]
        gsel = jax.nn.sigmoid(gt[3 * NSA_HG * g:3 * NSA_HG * (g + 1)])
        gate = lambda b: jnp.concatenate([gsel[3 * h + b:3 * h + b + 1] for h in range(NSA_HG)], axis=1)
        out_t = (gate(0) * o_cmp[g] + (gate(1) / l_s) * acc_s + (gate(2) / l_w) * acc_w).astype(BF16)
        stacked = jnp.concatenate([out_t[:, NSA_QP * h:NSA_QP * (h + 1)] for h in range(NSA_HG)], axis=0)
        o_ref[:, qcol(g)] = _dot_nt(eye, stacked).astype(BF16)


def _nsa_attention(z, kc, vct, kk, vvt, tables, bsz, seq):
    g, hg, dh, qp = NSA_G, NSA_HG, NSA_DH, NSA_QP
    nstep = seq // qp
    n_sb = seq // SLC_BLOCK
    nch = kc.shape[1]
    hq = hg * qp
    sp = kk.shape[1]
    tc, tn, tw = tables
    c_start = CMP_STRIDE * np.arange(nch)
    s_start = SLC_BLOCK * np.arange(n_sb)
    cover_t = ((c_start[None, :] < s_start[:, None] + SLC_BLOCK)
               & (c_start[None, :] + CMP_BLOCK > s_start[:, None])
               & (np.arange(nch)[None, :] < (seq - CMP_BLOCK) // CMP_STRIDE + 1))
    cover_t = jnp.asarray(cover_t.astype(np.float32), BF16)
    full = lambda a: pl.BlockSpec(a.shape, lambda b, i: (0,) * a.ndim, pipeline_mode=pl.Buffered(1))
    qd = g * hg * dh
    return pl.pallas_call(
        functools.partial(_nsa_kernel, n_sb=n_sb),
        out_shape=jax.ShapeDtypeStruct((bsz * seq, qd), BF16),
        grid=(bsz, nstep),
        in_specs=[pl.BlockSpec((qp, qd), lambda b, i: (b * nstep + i, Z_Q // qd)),
                  pl.BlockSpec((qp, LANES), lambda b, i: (b * nstep + i, Z_GN // LANES)),
                  pl.BlockSpec((1, nch, GD), lambda b, i: (b, 0, 0)),
                  pl.BlockSpec((1, g * dh, nch), lambda b, i: (b, 0, 0)),
                  pl.BlockSpec((1, sp, GD), lambda b, i: (b, 0, 0)),
                  pl.BlockSpec((1, sp, GD), lambda b, i: (b, 0, 1)),
                  pl.BlockSpec((1, sp, GD), lambda b, i: (b, 0, 2)),
                  pl.BlockSpec((1, g * VT_SLAB, sp), lambda b, i: (b, 0, 0)),
                  pl.BlockSpec((1, g * VT_SLAB, sp), lambda b, i: (b, 1, 0)),
                  full(tc), full(tn), full(tw), full(cover_t)],
        out_specs=pl.BlockSpec((qp, qd), lambda b, i: (b * nstep + i, 0)),
        scratch_shapes=[pltpu.VMEM((g, n_sb, qp), F32)],
        compiler_params=_cparams(("parallel", "arbitrary")),
        name="nsa_attention",
    )(z, z, kc, vct, kk, kk, kk, vvt, vvt, tc, tn, tw, cover_t)


def _rope_table_kernel(pos_ref, inv_ref, o_ref):
    ang = inv_ref[...] * pos_ref[0].astype(F32)
    c, s = jnp.cos(ang), jnp.sin(ang)
    o_ref[...] = jnp.concatenate([c, c, -s, s], axis=0).T


def _rope_table(positions):
    tm = TOK_TILE
    t = positions.size
    half = QK_ROPE // 2
    inv = (ROPE_THETA ** (-jnp.arange(half, dtype=F32) / half))[:, None]
    return pl.pallas_call(
        _rope_table_kernel,
        out_shape=jax.ShapeDtypeStruct((t, 2 * QK_ROPE), F32),
        grid=(t // tm,),
        in_specs=[pl.BlockSpec((1, 1, tm), lambda i: (i, 0, 0)),
                  pl.BlockSpec((half, 1), lambda i: (0, 0))],
        out_specs=pl.BlockSpec((tm, 2 * QK_ROPE), lambda i: (i, 0)),
        compiler_params=_cparams(("parallel",)),
        name="rope_table",
    )(positions.reshape(t // tm, 1, tm), inv)


MLA_HW = 256


def _mla_proj_kernel(cq_ref, ckv_ref, kr_ref, rope_ref, nq_ref, nkv_ref, wq_ref, wkn_ref, wvt_ref,
                     q_ref, k_ref, vt_ref):
    scale = (QK_NOPE + QK_ROPE) ** -0.5 * LOG2E
    rope = rope_ref[...]
    yq = _dot(_rms(cq_ref[...].astype(F32), nq_ref[...]).astype(BF16), wq_ref[...])
    ckv = _rms(ckv_ref[...].astype(F32), nkv_ref[...]).astype(BF16)
    ykn = _dot(ckv, wkn_ref[...])
    vt_ref[0] = _dot_nt(wvt_ref[...], ckv).astype(BF16)
    kp = kr_ref[...].astype(F32) * rope
    kp = kp + pltpu.roll(kp, QK_ROPE, 1)
    lane = lax.broadcasted_iota(jnp.int32, kp.shape, 1)
    kp = jnp.where(lane < QK_ROPE, kp, 0.0).astype(BF16)
    for h in range(MLA_HEADS):
        base = MLA_HW * h
        q_ref[:, base:base + QK_NOPE] = (yq[:, base:base + QK_NOPE] * scale).astype(BF16)
        qp = yq[:, base + QK_NOPE:base + MLA_HW] * rope
        qp = qp + pltpu.roll(qp, QK_ROPE, 1)
        q_ref[:, base + QK_NOPE:base + MLA_HW] = (qp * scale).astype(BF16)
        k_ref[:, base:base + QK_NOPE] = ykn[:, QK_NOPE * h:QK_NOPE * (h + 1)].astype(BF16)
        k_ref[:, base + QK_NOPE:base + MLA_HW] = kp


def _swap_halves(w):
    half = QK_ROPE // 2
    return jnp.concatenate([w[..., half:], w[..., :half]], axis=-1)


def _mla_proj(z, rope_tab, norm_q, norm_kv, w_uq, w_ukv, bsz, seq):
    tm = TOK_TILE
    t = z.shape[0]
    nst = seq // tm
    wq = w_uq.reshape(Q_RANK, MLA_HEADS, QK_NOPE + QK_ROPE)
    wq = jnp.concatenate([wq, _swap_halves(wq[..., QK_NOPE:])], axis=-1)
    wq = wq.reshape(Q_RANK, MLA_HEADS * MLA_HW).astype(BF16)
    wkv = w_ukv.reshape(KV_RANK, MLA_HEADS, QK_NOPE + V_DIM)
    wkn = wkv[..., :QK_NOPE].reshape(KV_RANK, MLA_HEADS * QK_NOPE).astype(BF16)
    wvt = wkv[..., QK_NOPE:].reshape(KV_RANK, MLA_HEADS * V_DIM).T.astype(BF16)
    hw = MLA_HEADS * MLA_HW
    hv = MLA_HEADS * V_DIM
    row = lambda b, s: b * nst + s
    return pl.pallas_call(
        _mla_proj_kernel,
        out_shape=(jax.ShapeDtypeStruct((t, hw), BF16),
                   jax.ShapeDtypeStruct((t, hw), BF16),
                   jax.ShapeDtypeStruct((bsz, hv, seq), BF16)),
        grid=(bsz, nst),
        in_specs=[pl.BlockSpec((tm, Q_RANK), lambda b, s: (row(b, s), Z_CQ // Q_RANK)),
                  pl.BlockSpec((tm, KV_RANK), lambda b, s: (row(b, s), Z_CKV // KV_RANK)),
                  pl.BlockSpec((tm, 2 * QK_ROPE), lambda b, s: (row(b, s), Z_KR // (2 * QK_ROPE))),
                  pl.BlockSpec((tm, 2 * QK_ROPE), lambda b, s: (row(b, s), 0)),
                  pl.BlockSpec((1, Q_RANK), lambda b, s: (0, 0)),
                  pl.BlockSpec((1, KV_RANK), lambda b, s: (0, 0)),
                  pl.BlockSpec((Q_RANK, hw), lambda b, s: (0, 0)),
                  pl.BlockSpec((KV_RANK, hv), lambda b, s: (0, 0)),
                  pl.BlockSpec((hv, KV_RANK), lambda b, s: (0, 0))],
        out_specs=(pl.BlockSpec((tm, hw), lambda b, s: (row(b, s), 0)),
                   pl.BlockSpec((tm, hw), lambda b, s: (row(b, s), 0)),
                   pl.BlockSpec((1, hv, tm), lambda b, s: (b, 0, s))),
        compiler_params=_cparams(("parallel", "parallel")),
        name="mla_proj",
    )(z, z, z, rope_tab, norm_q[None], norm_kv[None], wq, wkn, wvt)


def _mla_attn_kernel(q_ref, k_ref, vt_ref, o_ref, *, tq, tk, nh):
    iq = pl.program_id(2)
    cd = lax.div(iq * tq, tk)
    heads = range(nh)
    hcol = lambda h: slice(MLA_HW * h, MLA_HW * (h + 1))
    vrow = lambda h: slice(V_DIM * h, V_DIM * (h + 1))
    qs = [q_ref[:, hcol(h)] for h in heads]

    def scores(c, h):
        k0 = pl.multiple_of(c * tk, tk)
        return _dot_nt(k_ref[0, pl.ds(k0, tk), hcol(h)], qs[h])

    def diagonal(nk):
        k0 = pl.multiple_of((iq + 1) * tq - nk, tq)
        kpos = k0 + lax.broadcasted_iota(jnp.int32, (nk, tq), 0)
        qpos = iq * tq + lax.broadcasted_iota(jnp.int32, (nk, tq), 1)
        st = []
        for h in heads:
            s = jnp.where(kpos <= qpos, _dot_nt(k_ref[0, pl.ds(k0, nk), hcol(h)], qs[h]), NEG_INF)
            m0, p0, l0 = _softmax_cols(s)
            st += [m0, l0, _dot(vt_ref[0, vrow(h), pl.ds(k0, nk)], p0.astype(BF16))]
        return tuple(st)

    assert tk == 2 * tq
    state = lax.cond(lax.rem(iq, 2) == 0, lambda: diagonal(tq), lambda: diagonal(tk))

    def body(c, carry):
        k0 = pl.multiple_of(c * tk, tk)
        ss = [scores(c, h) for h in heads]
        out = []
        for h in heads:
            m_old, l_old, acc_old = carry[3 * h:3 * h + 3]
            m_new = jnp.maximum(m_old, jnp.max(ss[h], axis=0, keepdims=True))
            alpha = jnp.exp2(m_old - m_new)
            p = jnp.exp2(ss[h] - m_new)
            l_new = alpha * l_old + jnp.sum(p, axis=0, keepdims=True)
            acc_new = alpha * acc_old + _dot(vt_ref[0, vrow(h), pl.ds(k0, tk)], p.astype(BF16))
            out += [m_new, l_new, acc_new]
        return tuple(out)

    state = lax.fori_loop(0, cd, body, tuple(state))
    for h in heads:
        _, l, acc = state[3 * h:3 * h + 3]
        o_ref[:, vrow(h)] = (acc / l).T.astype(BF16)


def _mla_attention(qf, kf, vt, bsz, seq):
    tq, tk, nh = MLA_TQ, MLA_TK, MLA_HEADS_PER_STEP
    h = MLA_HEADS
    nq = seq // tq
    k3 = kf.reshape(bsz, seq, h * MLA_HW)
    return pl.pallas_call(
        functools.partial(_mla_attn_kernel, tq=tq, tk=tk, nh=nh),
        out_shape=jax.ShapeDtypeStruct((bsz * seq, h * V_DIM), BF16),
        grid=(bsz, h // nh, nq),
        in_specs=[pl.BlockSpec((tq, nh * MLA_HW), lambda b, hh, i: (b * nq + i, hh)),
                  pl.BlockSpec((1, seq, nh * MLA_HW), lambda b, hh, i: (b, 0, hh)),
                  pl.BlockSpec((1, nh * V_DIM, seq), lambda b, hh, i: (b, hh, 0))],
        out_specs=pl.BlockSpec((tq, nh * V_DIM), lambda b, hh, i: (b * nq + i, hh)),
        compiler_params=_cparams(("parallel", "parallel", "arbitrary")),
        name="mla_attention",
    )(qf, k3, vt)


def _merge_xattn_kernel(ya_ref, yb_ref, yc_ref, ga_ref, gb_ref, gc_ref, x_ref,
                        wa_ref, wb_ref, wc_ref, wo_ref,
                        gx_ref, wq_ref, kv_ref, wxo_ref, o_ref):
    sig = lambda ref: jax.nn.sigmoid(ref[...].astype(F32))
    y = (sig(ga_ref) * _dot(ya_ref[...], wa_ref[...])
         + sig(gb_ref) * _dot(yb_ref[...], wb_ref[...])
         + sig(gc_ref) * _dot(yc_ref[...], wc_ref[...]))
    x = x_ref[...] + _dot(y.astype(BF16), wo_ref[...])
    h = _rms(x, gx_ref[...]).astype(BF16)
    q = _dot(h, wq_ref[...]) * XATTN_DH ** -0.5
    hd = XATTN_HEADS * XATTN_DH
    outs = []
    for hh in range(XATTN_HEADS):
        qh = q[:, XATTN_DH * hh:XATTN_DH * (hh + 1)].astype(BF16)
        kh = kv_ref[0, :, XATTN_DH * hh:XATTN_DH * (hh + 1)]
        vh = kv_ref[0, :, hd + XATTN_DH * hh:hd + XATTN_DH * (hh + 1)]
        s = _dot_nt(qh, kh)
        m = jnp.max(s, axis=-1, keepdims=True)
        p = jnp.exp(s - m)
        p = p / jnp.sum(p, axis=-1, keepdims=True)
        outs.append(_dot(p.astype(BF16), vh))
    o = jnp.concatenate(outs, axis=-1).astype(BF16)
    o_ref[...] = x + _dot(o, wxo_ref[...])


def _merge_xattn(ya, yb, yc, z, x, wa, wb, wc, wo, gx, wq, kv, wxo, bsz, seq):
    tm = TOK_TILE
    t, d = x.shape
    nst = seq // tm
    m_len = kv.shape[1]
    hd = XATTN_HEADS * XATTN_DH
    row = lambda b, s: b * nst + s
    act = pl.BlockSpec((tm, ya.shape[1]), lambda b, s: (row(b, s), 0))
    gate = lambda k: pl.BlockSpec((tm, d), lambda b, s: (row(b, s), Z_GM // d + k))
    const = lambda shape: pl.BlockSpec(shape, lambda b, s: (0, 0))
    bf = lambda w: w.astype(BF16)
    return pl.pallas_call(
        _merge_xattn_kernel,
        out_shape=jax.ShapeDtypeStruct((t, d), F32),
        grid=(bsz, nst),
        in_specs=[act, act, act, gate(0), gate(1), gate(2),
                  pl.BlockSpec((tm, d), lambda b, s: (row(b, s), 0)),
                  const((ya.shape[1], d)), const((ya.shape[1], d)), const((ya.shape[1], d)), const((d, d)),
                  const((1, d)), const((d, hd)),
                  pl.BlockSpec((1, m_len, 2 * hd), lambda b, s: (b, 0, 0)),
                  const((hd, d))],
        out_specs=pl.BlockSpec((tm, d), lambda b, s: (row(b, s), 0)),
        compiler_params=_cparams(("parallel", "parallel")),
        name="merge_xattn",
    )(ya, yb, yc, z, z, z, x, bf(wa), bf(wb), bf(wc), bf(wo), gx[None], bf(wq), kv, bf(wxo))


def _ffn_kernel(x_ref, g_ref, wg_ref, wu_ref, wd_ref, gf_ref, o_ref, h_ref, acc_ref, *, final):
    c = pl.program_id(1)

    @pl.when(c == 0)
    def _():
        h_ref[...] = _rms(x_ref[...], g_ref[...]).astype(BF16)
        acc_ref[...] = x_ref[...]

    h = h_ref[...]
    gate = _dot(h, wg_ref[...])
    up = _dot(h, wu_ref[...])
    act = (gate * jax.nn.sigmoid(gate) * up).astype(BF16)
    acc_ref[...] += _dot(act, wd_ref[...])

    @pl.when(c == pl.num_programs(1) - 1)
    def _():
        y = acc_ref[...]
        o_ref[...] = _rms(y, gf_ref[...]) if final else y


def _ffn(x, g, w_gate_up, w_down, g_final, final):
    tm, tc = FFN_TM, FFN_TC
    t, d = x.shape
    nc = FFN_HIDDEN // tc
    wgu = w_gate_up.astype(BF16)
    return pl.pallas_call(
        functools.partial(_ffn_kernel, final=final),
        out_shape=jax.ShapeDtypeStruct((t, d), F32),
        grid=(t // tm, nc),
        in_specs=[pl.BlockSpec((tm, d), lambda i, c: (i, 0)),
                  pl.BlockSpec((1, d), lambda i, c: (0, 0)),
                  pl.BlockSpec((d, tc), lambda i, c: (0, c)),
                  pl.BlockSpec((d, tc), lambda i, c: (0, nc + c)),
                  pl.BlockSpec((tc, d), lambda i, c: (c, 0)),
                  pl.BlockSpec((1, d), lambda i, c: (0, 0))],
        out_specs=pl.BlockSpec((tm, d), lambda i, c: (i, 0)),
        scratch_shapes=[pltpu.VMEM((tm, d), BF16), pltpu.VMEM((tm, d), F32)],
        compiler_params=_cparams(("parallel", "arbitrary")),
        name="ffn",
    )(x, g[None], wgu, wgu, w_down.astype(BF16), g_final[None])


def _split_w_in(w):
    k_rope = w[:, O_KR:O_KR + QK_ROPE]
    kv = lambda kind: w[:, O_KV + GD * kind:O_KV + GD * (kind + 1)]
    pad = jnp.zeros((w.shape[0], Z_COLS - Z_GN - 3 * NSA_HEADS), w.dtype)
    wz = jnp.concatenate([
        w[:, O_GM:O_GM + 3 * D_MODEL],
        w[:, O_GLU:O_GLU + 2 * CONV_CH],
        w[:, O_Q:O_Q + NSA_HEADS * NSA_DH],
        w[:, O_CKV:O_CKV + KV_RANK],
        k_rope, _swap_halves(k_rope),
        w[:, O_CQ:O_CQ + Q_RANK],
        kv(0), kv(1),
        w[:, O_GN:O_GN + 3 * NSA_HEADS], pad], axis=1).astype(BF16)
    z64 = jnp.zeros((w.shape[0], NSA_DH), w.dtype)
    wk = jnp.concatenate([kv(2)[:, :NSA_DH], z64, kv(2)[:, NSA_DH:], z64, kv(4)], axis=1).astype(BF16)
    wvt = jnp.concatenate([kv(3), kv(5)], axis=1).T.astype(BF16)
    return wz, wk, wvt


def kernel(x, mem, positions, rel_bias, norm_mix, norm_xattn, norm_mem, norm_ffn, norm_final, w_in, conv_w, conv_b, conv_ln_g, conv_ln_b, w_branch_conv, cmp_pos_k, cmp_w1_k, cmp_b1_k, cmp_w2_k, cmp_pos_v, cmp_w1_v, cmp_b1_v, cmp_w2_v, w_branch_nsa, mla_norm_q, mla_norm_kv, w_uq, w_ukv, w_branch_mla, w_out, w_xq, w_xkv, w_xo, w_gate_up, w_down):
    bsz, seq, d = x.shape
    depth = w_in.shape[0]
    t = bsz * seq
    m_len = mem.shape[1]
    xt = x.reshape(t, d)
    memt = mem.reshape(bsz * m_len, d)
    rope_tab = _rope_table(positions)
    tables = _nsa_tables(rel_bias)
    for l in range(depth):
        wz, wk, wvt = _split_w_in(w_in[l])
        z, zc = _in_proj(xt, norm_mix[l][None], wz)
        kk, vvt = _kv_proj(xt, norm_mix[l][None], wk, wvt, bsz, seq)
        ya = _conv_module(z, conv_w[l], conv_b[l], conv_ln_g[l], conv_ln_b[l], bsz, seq)
        kc, vct = _compress(zc, jnp.stack([cmp_pos_k[l], cmp_pos_v[l]]), jnp.stack([cmp_w1_k[l], cmp_w1_v[l]]),
                            jnp.stack([cmp_b1_k[l], cmp_b1_v[l]]), jnp.stack([cmp_w2_k[l], cmp_w2_v[l]]), bsz, seq)
        yb = _nsa_attention(z, kc, vct, kk, vvt, tables, bsz, seq)
        qf, kf, vt = _mla_proj(z, rope_tab, mla_norm_q[l], mla_norm_kv[l], w_uq[l], w_ukv[l], bsz, seq)
        yc = _mla_attention(qf, kf, vt, bsz, seq)
        mem_kv = _norm_matmul(memt, norm_mem[l][None], w_xkv[l].astype(BF16), 256, 1024, BF16)
        mem_kv = mem_kv.reshape(bsz, m_len, 2 * XATTN_HEADS * XATTN_DH)
        xt = _merge_xattn(ya, yb, yc, z, xt, w_branch_conv[l], w_branch_nsa[l], w_branch_mla[l], w_out[l],
                          norm_xattn[l], w_xq[l], mem_kv, w_xo[l], bsz, seq)
        xt = _ffn(xt, norm_ffn[l], w_gate_up[l], w_down[l], norm_final, l == depth - 1)
    return xt.reshape(bsz, seq, d)
```

```python
import functools
import math

import numpy as np
import jax
import jax.numpy as jnp
from jax import lax
from jax.experimental import pallas as pl
from jax.experimental.pallas import tpu as pltpu

F32 = jnp.float32
BF16 = jnp.bfloat16

EPS = 1e-6
NEG_INF = -1e30
FORCE_SCORE = 1e4

D_MODEL = 1024
CONV_CH = 512
CONV_WIDTH = 31
NSA_HEADS = 8
NSA_G = 2
NSA_HG = NSA_HEADS // NSA_G
NSA_DH = 64
CMP_BLOCK = 32
CMP_STRIDE = 16
CMP_HIDDEN = 256
SLC_BLOCK = 64
N_SELECT = 16
WINDOW = 512
NSA_QB = 64
MLA_HEADS = 4
Q_RANK = 384
KV_RANK = 256
QK_NOPE = 128
QK_ROPE = 64
V_DIM = 128
ROPE_THETA = 10000.0
REL_BUCKETS = 32
REL_MAX_DIST = 128
XATTN_HEADS = 4
XATTN_DH = 128
FFN_HIDDEN = 2816

LANES = 128
SUBLANES = 8

O_GLU, O_Q, O_KV, O_GN, O_CQ, O_CKV, O_KR, O_GM = 0, 1024, 1536, 2304, 2328, 2712, 2968, 3032
GD = NSA_G * NSA_DH

Z_GM = 0
Z_UA = 3072
Z_UB = 3584
Z_Q = 4096
Z_CKV = 4608
Z_KR = 4864
Z_CQ = 4992
Z_CMP = 5376
Z_GN = 5632
Z_COLS = 5760

VMEM_LIMIT = 56 * 1024 * 1024

TOK_TILE = 512
IN_PROJ_TM = 1024
IN_PROJ_TN = 1152
FFN_TM = 1024
FFN_TC = 256
MLA_TQ = 512
MLA_TK = 1024
MLA_HEADS_PER_STEP = 2

LOG2E = math.log2(math.e)


def _cparams(sem):
    return pltpu.CompilerParams(dimension_semantics=sem, vmem_limit_bytes=VMEM_LIMIT)


def _rms(x, g):
    return x * lax.rsqrt(jnp.mean(x * x, axis=-1, keepdims=True) + EPS) * g


def _dot(a, b):
    return jnp.dot(a, b, preferred_element_type=F32)


def _dot_nt(a, b):
    return lax.dot_general(a, b, (((1,), (1,)), ((), ())), preferred_element_type=F32)


def _norm_matmul_kernel(x_ref, g_ref, w_ref, o_ref, h_ref):
    @pl.when(pl.program_id(1) == 0)
    def _():
        h_ref[...] = _rms(x_ref[...], g_ref[...]).astype(BF16)

    o_ref[...] = _dot(h_ref[...], w_ref[...]).astype(o_ref.dtype)


def _norm_matmul(x, g, w, tm, tn, out_dtype):
    m, k = x.shape
    n = w.shape[1]
    return pl.pallas_call(
        _norm_matmul_kernel,
        out_shape=jax.ShapeDtypeStruct((m, n), out_dtype),
        grid=(m // tm, n // tn),
        in_specs=[pl.BlockSpec((tm, k), lambda i, j: (i, 0)),
                  pl.BlockSpec((1, k), lambda i, j: (0, 0)),
                  pl.BlockSpec((k, tn), lambda i, j: (0, j))],
        out_specs=pl.BlockSpec((tm, tn), lambda i, j: (i, j)),
        scratch_shapes=[pltpu.VMEM((tm, k), BF16)],
        compiler_params=_cparams(("parallel", "arbitrary")),
        name="norm_matmul",
    )(x, g, w)


def _in_proj_kernel(x_ref, g_ref, w_ref, z_ref, zc_ref, h_ref, *, cmp_tile, cmp_off):
    @pl.when(pl.program_id(1) == 0)
    def _():
        h_ref[...] = _rms(x_ref[...], g_ref[...]).astype(BF16)

    acc = _dot(h_ref[...], w_ref[...])
    z_ref[...] = acc.astype(BF16)

    @pl.when(pl.program_id(1) == cmp_tile)
    def _():
        zc_ref[...] = acc[:, cmp_off:cmp_off + 2 * GD]


def _in_proj(x, g, w):
    tm, tn = IN_PROJ_TM, IN_PROJ_TN
    m, k = x.shape
    n = w.shape[1]
    return pl.pallas_call(
        functools.partial(_in_proj_kernel, cmp_tile=Z_CMP // tn, cmp_off=Z_CMP % tn),
        out_shape=(jax.ShapeDtypeStruct((m, n), BF16), jax.ShapeDtypeStruct((m, 2 * GD), F32)),
        grid=(m // tm, n // tn),
        in_specs=[pl.BlockSpec((tm, k), lambda i, j: (i, 0)),
                  pl.BlockSpec((1, k), lambda i, j: (0, 0)),
                  pl.BlockSpec((k, tn), lambda i, j: (0, j))],
        out_specs=(pl.BlockSpec((tm, tn), lambda i, j: (i, j)),
                   pl.BlockSpec((tm, 2 * GD), lambda i, j: (i, 0))),
        scratch_shapes=[pltpu.VMEM((tm, k), BF16)],
        compiler_params=_cparams(("parallel", "arbitrary")),
        name="in_proj",
    )(x, g, w)


VT_SLAB = NSA_DH + 16


def _kv_proj_kernel(x_ref, g_ref, wk_ref, wvt_ref, k_ref, vt_ref):
    @pl.when(pl.program_id(1) == 0)
    def _():
        k_ref[...] = jnp.zeros(k_ref.shape, BF16)
        vt_ref[...] = jnp.zeros(vt_ref.shape, BF16)

    @pl.when(pl.program_id(1) > 0)
    def _():
        h = _rms(x_ref[...], g_ref[...]).astype(BF16)
        k = _dot(h, wk_ref[...])
        tm = k.shape[0]
        row = lax.broadcasted_iota(jnp.int32, k.shape, 0)
        lane = lax.broadcasted_iota(jnp.int32, k.shape, 1)
        blk = (pl.program_id(1) - 1) * (tm // SLC_BLOCK) + lax.shift_right_logical(row, 6)
        hot = (lane < NSA_G * GD) & (lax.bitwise_and(lane, GD - 1) == blk + NSA_DH)
        k_ref[0] = jnp.where(hot, 1.0, k).astype(BF16)
        v = _dot_nt(wvt_ref[...], h).astype(BF16)
        ones = jnp.ones((VT_SLAB - NSA_DH, tm), BF16)
        vt_ref[0] = jnp.concatenate([x for i in range(v.shape[0] // NSA_DH)
                                     for x in (v[NSA_DH * i:NSA_DH * (i + 1)], ones)], axis=0)


def _kv_proj(x, g, wk, wvt, bsz, seq):
    tm = WINDOW
    nst = seq // tm
    d = x.shape[1]
    nk = wk.shape[1]
    nv = wvt.shape[0] // NSA_DH * VT_SLAB
    return pl.pallas_call(
        _kv_proj_kernel,
        out_shape=(jax.ShapeDtypeStruct((bsz, seq + tm, nk), BF16),
                   jax.ShapeDtypeStruct((bsz, nv, seq + tm), BF16)),
        grid=(bsz, nst + 1),
        in_specs=[pl.BlockSpec((tm, d), lambda b, s: (b * nst + jnp.maximum(s - 1, 0), 0)),
                  pl.BlockSpec((1, d), lambda b, s: (0, 0)),
                  pl.BlockSpec((d, nk), lambda b, s: (0, 0)),
                  pl.BlockSpec((wvt.shape[0], d), lambda b, s: (0, 0))],
        out_specs=(pl.BlockSpec((1, tm, nk), lambda b, s: (b, s, 0)),
                   pl.BlockSpec((1, nv, tm), lambda b, s: (b, 0, s))),
        compiler_params=_cparams(("parallel", "arbitrary")),
        name="nsa_kv_proj",
    )(x, g, wk, wvt)


CONV_HALO = 32


CONV_ROWS = 64


def _conv_kernel(a_ref, b_ref, w_ref, cb_ref, lg_ref, lb_ref, o_ref, buf_ref, sh_ref, *, ts):
    @pl.when(pl.program_id(1) == 0)
    def _():
        buf_ref[0:CONV_HALO, :] = jnp.zeros((CONV_HALO, CONV_CH), F32)

    buf_ref[CONV_HALO:CONV_HALO + ts, :] = a_ref[...].astype(F32) * jax.nn.sigmoid(b_ref[...].astype(F32))
    span = ts + CONV_HALO - SUBLANES
    for r in range(1, SUBLANES):
        sh_ref[r - 1, 0:span, :] = buf_ref[r:r + span, :]
    off = CONV_HALO - (CONV_WIDTH - 1)

    def rows(i, carry):
        r0 = pl.multiple_of(i * CONV_ROWS, CONV_ROWS)
        acc = jnp.zeros((CONV_ROWS, CONV_CH), F32) + cb_ref[...]
        for k in range(CONV_WIDTH):
            res, base = (off + k) % SUBLANES, (off + k) // SUBLANES * SUBLANES
            if res == 0:
                tap = buf_ref[pl.ds(r0 + base, CONV_ROWS), :]
            else:
                tap = sh_ref[res - 1, pl.ds(r0 + base, CONV_ROWS), :]
            acc = acc + tap * w_ref[k:k + 1, :]
        mu = jnp.mean(acc, axis=-1, keepdims=True)
        xc = acc - mu
        var = jnp.mean(xc * xc, axis=-1, keepdims=True)
        y = xc * lax.rsqrt(var + EPS) * lg_ref[...] + lb_ref[...]
        o_ref[pl.ds(r0, CONV_ROWS), :] = (y * jax.nn.sigmoid(y)).astype(BF16)
        return carry

    lax.fori_loop(0, ts // CONV_ROWS, rows, 0, unroll=4)
    buf_ref[0:CONV_HALO, :] = buf_ref[ts:ts + CONV_HALO, :]


def _conv_module(z, conv_w, conv_b, ln_g, ln_b, bsz, seq):
    ts = TOK_TILE
    nst = seq // ts
    wpad = jnp.zeros((32, CONV_CH), F32).at[:CONV_WIDTH].set(conv_w)
    return pl.pallas_call(
        functools.partial(_conv_kernel, ts=ts),
        out_shape=jax.ShapeDtypeStruct((bsz * seq, CONV_CH), BF16),
        grid=(bsz, nst),
        in_specs=[pl.BlockSpec((ts, CONV_CH), lambda b, s: (b * nst + s, Z_UA // CONV_CH)),
                  pl.BlockSpec((ts, CONV_CH), lambda b, s: (b * nst + s, Z_UB // CONV_CH)),
                  pl.BlockSpec((32, CONV_CH), lambda b, s: (0, 0)),
                  pl.BlockSpec((1, CONV_CH), lambda b, s: (0, 0)),
                  pl.BlockSpec((1, CONV_CH), lambda b, s: (0, 0)),
                  pl.BlockSpec((1, CONV_CH), lambda b, s: (0, 0))],
        out_specs=pl.BlockSpec((ts, CONV_CH), lambda b, s: (b * nst + s, 0)),
        scratch_shapes=[pltpu.VMEM((ts + CONV_HALO, CONV_CH), F32),
                        pltpu.VMEM((SUBLANES - 1, ts + CONV_HALO - SUBLANES, CONV_CH), F32)],
        compiler_params=_cparams(("arbitrary", "arbitrary")),
        name="conv_module",
    )(z, z, wpad, conv_b[None], ln_g[None], ln_b[None])


def _compress_kernel(xk_ref, xv_ref, pos_ref, w1_ref, b1_ref, w2k_ref, w2v_ref, kc_ref, vct_ref, *, nch):
    for kind, (x_ref, w2_ref) in enumerate(((xk_ref, w2k_ref), (xv_ref, w2v_ref))):
        a = jnp.zeros((nch, NSA_G * CMP_HIDDEN), F32)
        b = jnp.zeros((nch, NSA_G * CMP_HIDDEN), F32)
        for l in range(CMP_STRIDE):
            xs = x_ref[pl.ds(l, nch, stride=CMP_STRIDE), :]
            a = a + _dot((xs + pos_ref[kind, l:l + 1, :]).astype(BF16), w1_ref[kind, l])
            b = b + _dot((xs + pos_ref[kind, CMP_STRIDE + l:CMP_STRIDE + l + 1, :]).astype(BF16),
                         w1_ref[kind, CMP_STRIDE + l])
        pre = a + pltpu.roll(b, nch - 1, 0) + b1_ref[kind]
        out = _dot(jax.nn.gelu(pre).astype(BF16), w2_ref[...])
        if kind == 0:
            kc_ref[0] = out.astype(BF16)
        else:
            vct_ref[0] = out.T.astype(BF16)


def _blockdiag2(w):
    z = jnp.zeros_like(w)
    return jnp.concatenate([jnp.concatenate([w, z], axis=-1), jnp.concatenate([z, w], axis=-1)], axis=-2)


def _compress(z, pos, w1, b1, w2, bsz, seq):
    nch = seq // CMP_STRIDE
    pos2 = jnp.concatenate([pos, pos], axis=-1)
    w1e = _blockdiag2(w1.reshape(2, CMP_BLOCK, NSA_DH, CMP_HIDDEN)).astype(BF16)
    b1e = jnp.concatenate([b1, b1], axis=-1)[:, None]
    w2k = _blockdiag2(w2[0]).astype(BF16)
    w2v = _blockdiag2(w2[1]).astype(BF16)
    full = lambda a: pl.BlockSpec(a.shape, lambda b: (0,) * a.ndim)
    return pl.pallas_call(
        functools.partial(_compress_kernel, nch=nch),
        out_shape=(jax.ShapeDtypeStruct((bsz, nch, GD), BF16),
                   jax.ShapeDtypeStruct((bsz, GD, nch), BF16)),
        grid=(bsz,),
        in_specs=[pl.BlockSpec((seq, GD), lambda b: (b, 0)),
                  pl.BlockSpec((seq, GD), lambda b: (b, 1)),
                  full(pos2), full(w1e), full(b1e), full(w2k), full(w2v)],
        out_specs=(pl.BlockSpec((1, nch, GD), lambda b: (b, 0, 0)),
                   pl.BlockSpec((1, GD, nch), lambda b: (b, 0, 0))),
        compiler_params=_cparams(("parallel",)),
        name="nsa_compress",
    )(z, z, pos2, w1e, b1e, w2k, w2v)


NSA_QP = 4 * NSA_QB
STEP_BLOCKS = NSA_QP // SLC_BLOCK
NEAR_BACK = 2 * SLC_BLOCK
NEAR_KEYS = NSA_QP + NEAR_BACK
WIN_KEYS = WINDOW + NSA_QP
CMP_TAB_ROWS = 512
CMP_TAB_ZERO = 256


def _t5_bucket_np(d):
    exact = REL_BUCKETS // 2
    d = np.maximum(d, 0)
    ratio = np.log(np.maximum(d, 1).astype(np.float32) / np.float32(exact)) / np.float32(math.log(REL_MAX_DIST / exact))
    large = np.minimum(exact + (ratio * (REL_BUCKETS - exact)).astype(np.int32), REL_BUCKETS - 1)
    return np.where(d < exact, d, large).astype(np.int32)


def _bucket_thresholds():
    exact = REL_BUCKETS // 2
    bk = _t5_bucket_np(np.arange(4 * REL_MAX_DIST))
    assert np.all(np.diff(bk) >= 0) and bk[-1] == REL_BUCKETS - 1
    return [int(np.argmax(bk >= k)) for k in range(exact + 1, REL_BUCKETS)]


def _bias_rows(rel_ref, dist, valid, shift):
    exact = REL_BUCKETS // 2
    bucket = jnp.full(dist.shape, exact, jnp.int32)
    for thr in _bucket_thresholds():
        bucket = bucket + jnp.where(dist >= thr, 1, 0)
    bucket = jnp.where(dist < exact, dist, bucket)
    val = jnp.zeros(dist.shape, F32)
    for bkt in range(REL_BUCKETS):
        val = jnp.where(bucket == bkt, rel_ref[0, bkt:bkt + 1, :], val)
    if shift:
        val = val - rel_ref[0, REL_BUCKETS - 1:REL_BUCKETS, :]
    return jnp.where(valid, val * LOG2E, NEG_INF)


def _nsa_bias_kernel(rel_ref, tc_ref, tn_ref, tw_ref):
    hq = NSA_HG * NSA_QP
    rows = 128

    def dist_of(nrows, r0, fn):
        r = r0 + lax.broadcasted_iota(jnp.int32, (nrows, hq), 0)
        t = lax.bitwise_and(lax.broadcasted_iota(jnp.int32, (nrows, hq), 1), NSA_QP - 1)
        return fn(r, t)

    for r0 in range(0, CMP_TAB_ROWS, rows):
        d = dist_of(rows, r0, lambda r, t: t - CMP_STRIDE * (r - CMP_TAB_ZERO) - (CMP_BLOCK - 1))
        tc_ref[0, r0:r0 + rows, :] = _bias_rows(rel_ref, d, d >= 0, False)
    for r0 in range(0, NEAR_KEYS, rows):
        d = dist_of(rows, r0, lambda r, t: NEAR_BACK + t - r)
        tn_ref[0, r0:r0 + rows, :] = _bias_rows(rel_ref, d, d >= 0, True)
    for r0 in range(0, WIN_KEYS, rows):
        d = dist_of(rows, r0, lambda r, t: WINDOW + t - r)
        tw_ref[0, r0:r0 + rows, :] = _bias_rows(rel_ref, d, (d >= 0) & (d < WINDOW), False)


def _nsa_tables(rel_bias):
    hq = NSA_HG * NSA_QP
    rel4 = jnp.repeat(rel_bias.reshape(REL_BUCKETS, NSA_G, NSA_HG).transpose(1, 0, 2), NSA_QP, axis=-1)
    spec = lambda r: pl.BlockSpec((1, r, hq), lambda g: (g, 0, 0))
    return pl.pallas_call(
        _nsa_bias_kernel,
        out_shape=(jax.ShapeDtypeStruct((NSA_G, CMP_TAB_ROWS, hq), F32),
                   jax.ShapeDtypeStruct((NSA_G, NEAR_KEYS, hq), F32),
                   jax.ShapeDtypeStruct((NSA_G, WIN_KEYS, hq), F32)),
        grid=(NSA_G,),
        in_specs=[spec(REL_BUCKETS)],
        out_specs=(spec(CMP_TAB_ROWS), spec(NEAR_KEYS), spec(WIN_KEYS)),
        compiler_params=_cparams(("parallel",)),
        name="nsa_bias_tables",
    )(rel4)


FAR_KEYS = 512
FAR_TAIL = FAR_KEYS - NEAR_BACK
KV_FRONT = WINDOW
KREP = NSA_HG * NSA_DH


def _softmax_cols(s):
    m = jnp.max(s, axis=0, keepdims=True)
    p = jnp.exp2(s - m)
    return m, p, jnp.sum(p, axis=0, keepdims=True)


def _rank_select(score_ref, n_sb, n_sel):
    groups = n_sb // SUBLANES
    sub = lax.broadcasted_iota(jnp.int32, (SUBLANES, NSA_QP), 0)
    tiles = [score_ref[SUBLANES * v:SUBLANES * (v + 1), :] for v in range(groups)]
    cnts = [jnp.zeros((SUBLANES, NSA_QP), F32) for _ in range(groups)]
    for jp in range(n_sb):
        row = score_ref[jp:jp + 1, :]
        for v in range(groups):
            lo = SUBLANES * v
            if jp < lo:
                beats = row >= tiles[v]
            elif jp >= lo + SUBLANES - 1:
                beats = row > tiles[v]
            else:
                beats = (row > tiles[v]) | ((row == tiles[v]) & (sub > jp - lo))
            cnts[v] = cnts[v] + jnp.where(beats, 1.0, 0.0)
    cnt = jnp.concatenate(cnts, axis=0)
    return jnp.where(cnt < float(n_sel), 1.0, 0.0)


def _nsa_kernel(q_ref, gate_ref, kc_ref, vct_ref, ks0_ref, ks1_ref, kw_ref, vst_ref, vwt_ref,
                tc_ref, tn_ref, tw_ref, cov_ref, o_ref, score_ref, *, n_sb):
    p2 = pl.program_id(1)
    hq = NSA_HG * NSA_QP
    nch = kc_ref.shape[1]
    groups = range(NSA_G)
    qcol = lambda g: slice(KREP * g, KREP * (g + 1))
    vrow = lambda g: slice(NSA_DH * g, NSA_DH * (g + 1))
    vslab = lambda g: slice(VT_SLAB * g, VT_SLAB * (g + 1))

    ks_refs = (ks0_ref, ks1_ref)
    qs, q64s = [], []
    zq = jnp.zeros((hq, NSA_DH), BF16)
    for g in groups:
        qb = (q_ref[:, qcol(g)].astype(F32) * (NSA_DH ** -0.5 * LOG2E)).astype(BF16)
        q64 = jnp.concatenate([qb[:, NSA_DH * h:NSA_DH * (h + 1)] for h in range(NSA_HG)], axis=0)
        q64s.append(q64)
        qs.append(jnp.concatenate([q64, zq] if g == 0 else [zq, q64], axis=1))

    start_c = pl.multiple_of(CMP_TAB_ZERO - (NSA_QP // CMP_STRIDE) * p2, SUBLANES)
    lane = lax.broadcasted_iota(jnp.int32, (1, hq), 1)
    tq = NSA_QP * p2 + lax.bitwise_and(lane, NSA_QP - 1)
    anyv = jnp.where(tq >= CMP_BLOCK - 1, 1.0, 0.0)
    jrow = lax.broadcasted_iota(jnp.int32, (n_sb, NSA_QP), 0)
    tok = lax.broadcasted_iota(jnp.int32, (n_sb, NSA_QP), 1)
    cur = STEP_BLOCKS * p2 + lax.shift_right_logical(tok, 6)
    forced = (jrow == 0) | (jrow == cur) | (jrow == cur - 1)
    o_cmp = []
    for g in groups:
        sc = _dot_nt(kc_ref[0], qs[g]) + tc_ref[g, pl.ds(start_c, nch), :]
        _, pc, lc = _softmax_cols(sc)
        pc = pc * (anyv / lc)
        o_cmp.append(_dot(vct_ref[0, vrow(g), :], pc.astype(BF16)))
        psum = pc[:, 0:NSA_QP]
        for h in range(1, NSA_HG):
            psum = psum + pc[:, NSA_QP * h:NSA_QP * (h + 1)]
        p_hi = psum.astype(BF16)
        p_lo = (psum - p_hi.astype(F32)).astype(BF16)
        imp = _dot(cov_ref[...], p_hi) + _dot(cov_ref[...], p_lo)
        score_ref[g] = jnp.where(forced, FORCE_SCORE, jnp.where(jrow <= cur, imp, -1.0))

    def mask_operand(g, keep):
        mb = ((keep - 1.0) * -NEG_INF).T.astype(BF16)
        if n_sb < NSA_DH:
            mb = jnp.concatenate([mb, jnp.zeros((NSA_QP, NSA_DH - n_sb), BF16)], axis=1)
        return jnp.concatenate([q64s[g], jnp.concatenate([mb] * NSA_HG, axis=0)], axis=1)

    q_near, q_far = [], []
    for g in groups:
        sel = _rank_select(score_ref.at[g], n_sb, min(N_SELECT, n_sb))
        q_near.append(mask_operand(g, sel))
        far_blocks = STEP_BLOCKS * p2 - NEAR_BACK // SLC_BLOCK
        q_far.append(mask_operand(g, jnp.where(jrow < far_blocks, sel, 0.0)))

    win0 = pl.multiple_of(NSA_QP * p2, LANES)
    near0 = pl.multiple_of(win0 + KV_FRONT - NEAR_BACK, LANES)
    state = []
    for g in groups:
        s = _dot_nt(ks_refs[g][0, pl.ds(near0, NEAR_KEYS), :], q_near[g]) + tn_ref[g]
        s = jnp.concatenate([jnp.where(p2 > 0, s[0:NEAR_BACK], NEG_INF), s[NEAR_BACK:]], axis=0)
        m_s = jnp.max(s, axis=0, keepdims=True)
        pv = _dot(vst_ref[0, vslab(g), pl.ds(near0, NEAR_KEYS)], jnp.exp2(s - m_s).astype(BF16))
        state += [m_s, pv[NSA_DH:NSA_DH + 1], pv[0:NSA_DH]]

    window = []
    for g in groups:
        sw = _dot_nt(kw_ref[0, pl.ds(win0, WIN_KEYS), :], qs[g]) + tw_ref[g]
        slabs = [sw[NSA_QP * j:NSA_QP * (j + 1)] for j in range(WIN_KEYS // NSA_QP)]
        for j in range(KV_FRONT // NSA_QP):
            slabs[j] = jnp.where(NSA_QP * j + win0 >= KV_FRONT, slabs[j], NEG_INF)
        sw = jnp.concatenate(slabs, axis=0)
        p_w = jnp.exp2(sw - jnp.max(sw, axis=0, keepdims=True)).astype(BF16)
        pv = _dot(vwt_ref[0, vslab(g), pl.ds(win0, WIN_KEYS)], p_w)
        window.append((pv[0:NSA_DH], pv[NSA_DH:NSA_DH + 1]))

    def far_update(k0, nkeys, carry):
        sfs = [_dot_nt(ks_refs[g][0, pl.ds(k0, nkeys), :], q_far[g]) for g in groups]
        out = []
        for g in groups:
            m_old, l_old, acc_old = carry[3 * g:3 * g + 3]
            sf = sfs[g]
            m_new = jnp.maximum(m_old, jnp.max(sf, axis=0, keepdims=True))
            alpha = jnp.exp2(m_old - m_new)
            pv = _dot(vst_ref[0, vslab(g), pl.ds(k0, nkeys)], jnp.exp2(sf - m_new).astype(BF16))
            out += [m_new, alpha * l_old + pv[NSA_DH:NSA_DH + 1], alpha * acc_old + pv[0:NSA_DH]]
        return tuple(out)

    n_full = lax.div(jnp.maximum(p2 * NSA_QP - NEAR_BACK, 0), FAR_KEYS)
    state = lax.fori_loop(
        0, n_full, lambda c, carry: far_update(pl.multiple_of(FAR_KEYS * c + KV_FRONT, LANES), FAR_KEYS, carry),
        tuple(state))
    state = far_update(pl.multiple_of(FAR_KEYS * n_full + KV_FRONT, LANES), FAR_TAIL, state)

    gt = gate_ref[...].astype(F32).T
    r = lax.broadcasted_iota(jnp.int32, (NSA_QP, NSA_QP), 0)
    c = lax.broadcasted_iota(jnp.int32, (NSA_QP, NSA_QP), 1)
    eye = jnp.where(r == c, 1.0, 0.0).astype(BF16)
    for g in groups:
        _, l_s, acc_s = state[3 * g:3 * g + 3]
        acc_w, l_w = window[g]
        gsel = jax.nn.sigmoid(gt[3 * NSA_HG * g:3 * NSA_HG * (g + 1)])
        gate = lambda b: jnp.concatenate([gsel[3 * h + b:3 * h + b + 1] for h in range(NSA_HG)], axis=1)
        out_t = (gate(0) * o_cmp[g] + (gate(1) / l_s) * acc_s + (gate(2) / l_w) * acc_w).astype(BF16)
        stacked = jnp.concatenate([out_t[:, NSA_QP * h:NSA_QP * (h + 1)] for h in range(NSA_HG)], axis=0)
        o_ref[:, qcol(g)] = _dot_nt(eye, stacked).astype(BF16)


def _nsa_attention(z, kc, vct, kk, vvt, tables, bsz, seq):
    g, hg, dh, qp = NSA_G, NSA_HG, NSA_DH, NSA_QP
    nstep = seq // qp
    n_sb = seq // SLC_BLOCK
    nch = kc.shape[1]
    hq = hg * qp
    sp = kk.shape[1]
    tc, tn, tw = tables
    c_start = CMP_STRIDE * np.arange(nch)
    s_start = SLC_BLOCK * np.arange(n_sb)
    cover_t = ((c_start[None, :] < s_start[:, None] + SLC_BLOCK)
               & (c_start[None, :] + CMP_BLOCK > s_start[:, None])
               & (np.arange(nch)[None, :] < (seq - CMP_BLOCK) // CMP_STRIDE + 1))
    cover_t = jnp.asarray(cover_t.astype(np.float32), BF16)
    full = lambda a: pl.BlockSpec(a.shape, lambda b, i: (0,) * a.ndim, pipeline_mode=pl.Buffered(1))
    qd = g * hg * dh
    return pl.pallas_call(
        functools.partial(_nsa_kernel, n_sb=n_sb),
        out_shape=jax.ShapeDtypeStruct((bsz * seq, qd), BF16),
        grid=(bsz, nstep),
        in_specs=[pl.BlockSpec((qp, qd), lambda b, i: (b * nstep + i, Z_Q // qd)),
                  pl.BlockSpec((qp, LANES), lambda b, i: (b * nstep + i, Z_GN // LANES)),
                  pl.BlockSpec((1, nch, GD), lambda b, i: (b, 0, 0)),
                  pl.BlockSpec((1, g * dh, nch), lambda b, i: (b, 0, 0)),
                  pl.BlockSpec((1, sp, GD), lambda b, i: (b, 0, 0)),
                  pl.BlockSpec((1, sp, GD), lambda b, i: (b, 0, 1)),
                  pl.BlockSpec((1, sp, GD), lambda b, i: (b, 0, 2)),
                  pl.BlockSpec((1, g * VT_SLAB, sp), lambda b, i: (b, 0, 0)),
                  pl.BlockSpec((1, g * VT_SLAB, sp), lambda b, i: (b, 1, 0)),
                  full(tc), full(tn), full(tw), full(cover_t)],
        out_specs=pl.BlockSpec((qp, qd), lambda b, i: (b * nstep + i, 0)),
        scratch_shapes=[pltpu.VMEM((g, n_sb, qp), F32)],
        compiler_params=_cparams(("parallel", "arbitrary")),
        name="nsa_attention",
    )(z, z, kc, vct, kk, kk, kk, vvt, vvt, tc, tn, tw, cover_t)


def _rope_table_kernel(pos_ref, inv_ref, o_ref):
    ang = inv_ref[...] * pos_ref[0].astype(F32)
    c, s = jnp.cos(ang), jnp.sin(ang)
    o_ref[...] = jnp.concatenate([c, c, -s, s], axis=0).T


def _rope_table(positions):
    tm = TOK_TILE
    t = positions.size
    half = QK_ROPE // 2
    inv = (ROPE_THETA ** (-jnp.arange(half, dtype=F32) / half))[:, None]
    return pl.pallas_call(
        _rope_table_kernel,
        out_shape=jax.ShapeDtypeStruct((t, 2 * QK_ROPE), F32),
        grid=(t // tm,),
        in_specs=[pl.BlockSpec((1, 1, tm), lambda i: (i, 0, 0)),
                  pl.BlockSpec((half, 1), lambda i: (0, 0))],
        out_specs=pl.BlockSpec((tm, 2 * QK_ROPE), lambda i: (i, 0)),
        compiler_params=_cparams(("parallel",)),
        name="rope_table",
    )(positions.reshape(t // tm, 1, tm), inv)


MLA_HW = 256


def _mla_proj_kernel(cq_ref, ckv_ref, kr_ref, rope_ref, nq_ref, nkv_ref, wq_ref, wkn_ref, wvt_ref,
                     q_ref, k_ref, vt_ref):
    scale = (QK_NOPE + QK_ROPE) ** -0.5 * LOG2E
    rope = rope_ref[...]
    yq = _dot(_rms(cq_ref[...].astype(F32), nq_ref[...]).astype(BF16), wq_ref[...])
    ckv = _rms(ckv_ref[...].astype(F32), nkv_ref[...]).astype(BF16)
    ykn = _dot(ckv, wkn_ref[...])
    vt_ref[0] = _dot_nt(wvt_ref[...], ckv).astype(BF16)
    kp = kr_ref[...].astype(F32) * rope
    kp = kp + pltpu.roll(kp, QK_ROPE, 1)
    lane = lax.broadcasted_iota(jnp.int32, kp.shape, 1)
    kp = jnp.where(lane < QK_ROPE, kp, 0.0).astype(BF16)
    for h in range(MLA_HEADS):
        base = MLA_HW * h
        q_ref[:, base:base + QK_NOPE] = (yq[:, base:base + QK_NOPE] * scale).astype(BF16)
        qp = yq[:, base + QK_NOPE:base + MLA_HW] * rope
        qp = qp + pltpu.roll(qp, QK_ROPE, 1)
        q_ref[:, base + QK_NOPE:base + MLA_HW] = (qp * scale).astype(BF16)
        k_ref[:, base:base + QK_NOPE] = ykn[:, QK_NOPE * h:QK_NOPE * (h + 1)].astype(BF16)
        k_ref[:, base + QK_NOPE:base + MLA_HW] = kp


def _swap_halves(w):
    half = QK_ROPE // 2
    return jnp.concatenate([w[..., half:], w[..., :half]], axis=-1)


def _mla_proj(z, rope_tab, norm_q, norm_kv, w_uq, w_ukv, bsz, seq):
    tm = TOK_TILE
    t = z.shape[0]
    nst = seq // tm
    wq = w_uq.reshape(Q_RANK, MLA_HEADS, QK_NOPE + QK_ROPE)
    wq = jnp.concatenate([wq, _swap_halves(wq[..., QK_NOPE:])], axis=-1)
    wq = wq.reshape(Q_RANK, MLA_HEADS * MLA_HW).astype(BF16)
    wkv = w_ukv.reshape(KV_RANK, MLA_HEADS, QK_NOPE + V_DIM)
    wkn = wkv[..., :QK_NOPE].reshape(KV_RANK, MLA_HEADS * QK_NOPE).astype(BF16)
    wvt = wkv[..., QK_NOPE:].reshape(KV_RANK, MLA_HEADS * V_DIM).T.astype(BF16)
    hw = MLA_HEADS * MLA_HW
    hv = MLA_HEADS * V_DIM
    row = lambda b, s: b * nst + s
    return pl.pallas_call(
        _mla_proj_kernel,
        out_shape=(jax.ShapeDtypeStruct((t, hw), BF16),
                   jax.ShapeDtypeStruct((t, hw), BF16),
                   jax.ShapeDtypeStruct((bsz, hv, seq), BF16)),
        grid=(bsz, nst),
        in_specs=[pl.BlockSpec((tm, Q_RANK), lambda b, s: (row(b, s), Z_CQ // Q_RANK)),
                  pl.BlockSpec((tm, KV_RANK), lambda b, s: (row(b, s), Z_CKV // KV_RANK)),
                  pl.BlockSpec((tm, 2 * QK_ROPE), lambda b, s: (row(b, s), Z_KR // (2 * QK_ROPE))),
                  pl.BlockSpec((tm, 2 * QK_ROPE), lambda b, s: (row(b, s), 0)),
                  pl.BlockSpec((1, Q_RANK), lambda b, s: (0, 0)),
                  pl.BlockSpec((1, KV_RANK), lambda b, s: (0, 0)),
                  pl.BlockSpec((Q_RANK, hw), lambda b, s: (0, 0)),
                  pl.BlockSpec((KV_RANK, hv), lambda b, s: (0, 0)),
                  pl.BlockSpec((hv, KV_RANK), lambda b, s: (0, 0))],
        out_specs=(pl.BlockSpec((tm, hw), lambda b, s: (row(b, s), 0)),
                   pl.BlockSpec((tm, hw), lambda b, s: (row(b, s), 0)),
                   pl.BlockSpec((1, hv, tm), lambda b, s: (b, 0, s))),
        compiler_params=_cparams(("parallel", "parallel")),
        name="mla_proj",
    )(z, z, z, rope_tab, norm_q[None], norm_kv[None], wq, wkn, wvt)


def _mla_attn_kernel(q_ref, k_ref, vt_ref, o_ref, *, tq, tk, nh):
    iq = pl.program_id(2)
    cd = lax.div(iq * tq, tk)
    heads = range(nh)
    hcol = lambda h: slice(MLA_HW * h, MLA_HW * (h + 1))
    vrow = lambda h: slice(V_DIM * h, V_DIM * (h + 1))
    qs = [q_ref[:, hcol(h)] for h in heads]

    def scores(c, h):
        k0 = pl.multiple_of(c * tk, tk)
        return _dot_nt(k_ref[0, pl.ds(k0, tk), hcol(h)], qs[h])

    def diagonal(nk):
        k0 = pl.multiple_of((iq + 1) * tq - nk, tq)
        kpos = k0 + lax.broadcasted_iota(jnp.int32, (nk, tq), 0)
        qpos = iq * tq + lax.broadcasted_iota(jnp.int32, (nk, tq), 1)
        st = []
        for h in heads:
            s = jnp.where(kpos <= qpos, _dot_nt(k_ref[0, pl.ds(k0, nk), hcol(h)], qs[h]), NEG_INF)
            m0, p0, l0 = _softmax_cols(s)
            st += [m0, l0, _dot(vt_ref[0, vrow(h), pl.ds(k0, nk)], p0.astype(BF16))]
        return tuple(st)

    assert tk == 2 * tq
    state = lax.cond(lax.rem(iq, 2) == 0, lambda: diagonal(tq), lambda: diagonal(tk))

    def body(c, carry):
        k0 = pl.multiple_of(c * tk, tk)
        ss = [scores(c, h) for h in heads]
        out = []
        for h in heads:
            m_old, l_old, acc_old = carry[3 * h:3 * h + 3]
            m_new = jnp.maximum(m_old, jnp.max(ss[h], axis=0, keepdims=True))
            alpha = jnp.exp2(m_old - m_new)
            p = jnp.exp2(ss[h] - m_new)
            l_new = alpha * l_old + jnp.sum(p, axis=0, keepdims=True)
            acc_new = alpha * acc_old + _dot(vt_ref[0, vrow(h), pl.ds(k0, tk)], p.astype(BF16))
            out += [m_new, l_new, acc_new]
        return tuple(out)

    state = lax.fori_loop(0, cd, body, tuple(state))
    for h in heads:
        _, l, acc = state[3 * h:3 * h + 3]
        o_ref[:, vrow(h)] = (acc / l).T.astype(BF16)


def _mla_attention(qf, kf, vt, bsz, seq):
    tq, tk, nh = MLA_TQ, MLA_TK, MLA_HEADS_PER_STEP
    h = MLA_HEADS
    nq = seq // tq
    k3 = kf.reshape(bsz, seq, h * MLA_HW)
    return pl.pallas_call(
        functools.partial(_mla_attn_kernel, tq=tq, tk=tk, nh=nh),
        out_shape=jax.ShapeDtypeStruct((bsz * seq, h * V_DIM), BF16),
        grid=(bsz, h // nh, nq),
        in_specs=[pl.BlockSpec((tq, nh * MLA_HW), lambda b, hh, i: (b * nq + i, hh)),
                  pl.BlockSpec((1, seq, nh * MLA_HW), lambda b, hh, i: (b, 0, hh)),
                  pl.BlockSpec((1, nh * V_DIM, seq), lambda b, hh, i: (b, hh, 0))],
        out_specs=pl.BlockSpec((tq, nh * V_DIM), lambda b, hh, i: (b * nq + i, hh)),
        compiler_params=_cparams(("parallel", "parallel", "arbitrary")),
        name="mla_attention",
    )(qf, k3, vt)


def _merge_xattn_kernel(ya_ref, yb_ref, yc_ref, ga_ref, gb_ref, gc_ref, x_ref,
                        wa_ref, wb_ref, wc_ref, wo_ref,
                        gx_ref, wq_ref, kv_ref, wxo_ref, o_ref):
    sig = lambda ref: jax.nn.sigmoid(ref[...].astype(F32))
    y = (sig(ga_ref) * _dot(ya_ref[...], wa_ref[...])
         + sig(gb_ref) * _dot(yb_ref[...], wb_ref[...])
         + sig(gc_ref) * _dot(yc_ref[...], wc_ref[...]))
    x = x_ref[...] + _dot(y.astype(BF16), wo_ref[...])
    h = _rms(x, gx_ref[...]).astype(BF16)
    q = _dot(h, wq_ref[...]) * XATTN_DH ** -0.5
    hd = XATTN_HEADS * XATTN_DH
    outs = []
    for hh in range(XATTN_HEADS):
        qh = q[:, XATTN_DH * hh:XATTN_DH * (hh + 1)].astype(BF16)
        kh = kv_ref[0, :, XATTN_DH * hh:XATTN_DH * (hh + 1)]
        vh = kv_ref[0, :, hd + XATTN_DH * hh:hd + XATTN_DH * (hh + 1)]
        s = _dot_nt(qh, kh)
        m = jnp.max(s, axis=-1, keepdims=True)
        p = jnp.exp(s - m)
        p = p / jnp.sum(p, axis=-1, keepdims=True)
        outs.append(_dot(p.astype(BF16), vh))
    o = jnp.concatenate(outs, axis=-1).astype(BF16)
    o_ref[...] = x + _dot(o, wxo_ref[...])


def _merge_xattn(ya, yb, yc, z, x, wa, wb, wc, wo, gx, wq, kv, wxo, bsz, seq):
    tm = TOK_TILE
    t, d = x.shape
    nst = seq // tm
    m_len = kv.shape[1]
    hd = XATTN_HEADS * XATTN_DH
    row = lambda b, s: b * nst + s
    act = pl.BlockSpec((tm, ya.shape[1]), lambda b, s: (row(b, s), 0))
    gate = lambda k: pl.BlockSpec((tm, d), lambda b, s: (row(b, s), Z_GM // d + k))
    const = lambda shape: pl.BlockSpec(shape, lambda b, s: (0, 0))
    bf = lambda w: w.astype(BF16)
    return pl.pallas_call(
        _merge_xattn_kernel,
        out_shape=jax.ShapeDtypeStruct((t, d), F32),
        grid=(bsz, nst),
        in_specs=[act, act, act, gate(0), gate(1), gate(2),
                  pl.BlockSpec((tm, d), lambda b, s: (row(b, s), 0)),
                  const((ya.shape[1], d)), const((ya.shape[1], d)), const((ya.shape[1], d)), const((d, d)),
                  const((1, d)), const((d, hd)),
                  pl.BlockSpec((1, m_len, 2 * hd), lambda b, s: (b, 0, 0)),
                  const((hd, d))],
        out_specs=pl.BlockSpec((tm, d), lambda b, s: (row(b, s), 0)),
        compiler_params=_cparams(("parallel", "parallel")),
        name="merge_xattn",
    )(ya, yb, yc, z, z, z, x, bf(wa), bf(wb), bf(wc), bf(wo), gx[None], bf(wq), kv, bf(wxo))


def _ffn_kernel(x_ref, g_ref, wg_ref, wu_ref, wd_ref, gf_ref, o_ref, h_ref, acc_ref, *, final):
    c = pl.program_id(1)

    @pl.when(c == 0)
    def _():
        h_ref[...] = _rms(x_ref[...], g_ref[...]).astype(BF16)
        acc_ref[...] = x_ref[...]

    h = h_ref[...]
    gate = _dot(h, wg_ref[...])
    up = _dot(h, wu_ref[...])
    act = (gate * jax.nn.sigmoid(gate) * up).astype(BF16)
    acc_ref[...] += _dot(act, wd_ref[...])

    @pl.when(c == pl.num_programs(1) - 1)
    def _():
        y = acc_ref[...]
        o_ref[...] = _rms(y, gf_ref[...]) if final else y


def _ffn(x, g, w_gate_up, w_down, g_final, final):
    tm, tc = FFN_TM, FFN_TC
    t, d = x.shape
    nc = FFN_HIDDEN // tc
    wgu = w_gate_up.astype(BF16)
    return pl.pallas_call(
        functools.partial(_ffn_kernel, final=final),
        out_shape=jax.ShapeDtypeStruct((t, d), F32),
        grid=(t // tm, nc),
        in_specs=[pl.BlockSpec((tm, d), lambda i, c: (i, 0)),
                  pl.BlockSpec((1, d), lambda i, c: (0, 0)),
                  pl.BlockSpec((d, tc), lambda i, c: (0, c)),
                  pl.BlockSpec((d, tc), lambda i, c: (0, nc + c)),
                  pl.BlockSpec((tc, d), lambda i, c: (c, 0)),
                  pl.BlockSpec((1, d), lambda i, c: (0, 0))],
        out_specs=pl.BlockSpec((tm, d), lambda i, c: (i, 0)),
        scratch_shapes=[pltpu.VMEM((tm, d), BF16), pltpu.VMEM((tm, d), F32)],
        compiler_params=_cparams(("parallel", "arbitrary")),
        name="ffn",
    )(x, g[None], wgu, wgu, w_down.astype(BF16), g_final[None])


def _split_w_in(w):
    k_rope = w[:, O_KR:O_KR + QK_ROPE]
    kv = lambda kind: w[:, O_KV + GD * kind:O_KV + GD * (kind + 1)]
    pad = jnp.zeros((w.shape[0], Z_COLS - Z_GN - 3 * NSA_HEADS), w.dtype)
    wz = jnp.concatenate([
        w[:, O_GM:O_GM + 3 * D_MODEL],
        w[:, O_GLU:O_GLU + 2 * CONV_CH],
        w[:, O_Q:O_Q + NSA_HEADS * NSA_DH],
        w[:, O_CKV:O_CKV + KV_RANK],
        k_rope, _swap_halves(k_rope),
        w[:, O_CQ:O_CQ + Q_RANK],
        kv(0), kv(1),
        w[:, O_GN:O_GN + 3 * NSA_HEADS], pad], axis=1).astype(BF16)
    z64 = jnp.zeros((w.shape[0], NSA_DH), w.dtype)
    wk = jnp.concatenate([kv(2)[:, :NSA_DH], z64, kv(2)[:, NSA_DH:], z64, kv(4)], axis=1).astype(BF16)
    wvt = jnp.concatenate([kv(3), kv(5)], axis=1).T.astype(BF16)
    return wz, wk, wvt


def kernel(x, mem, positions, rel_bias, norm_mix, norm_xattn, norm_mem, norm_ffn, norm_final, w_in, conv_w, conv_b, conv_ln_g, conv_ln_b, w_branch_conv, cmp_pos_k, cmp_w1_k, cmp_b1_k, cmp_w2_k, cmp_pos_v, cmp_w1_v, cmp_b1_v, cmp_w2_v, w_branch_nsa, mla_norm_q, mla_norm_kv, w_uq, w_ukv, w_branch_mla, w_out, w_xq, w_xkv, w_xo, w_gate_up, w_down):
    bsz, seq, d = x.shape
    depth = w_in.shape[0]
    t = bsz * seq
    m_len = mem.shape[1]
    xt = x.reshape(t, d)
    memt = mem.reshape(bsz * m_len, d)
    rope_tab = _rope_table(positions)
    tables = _nsa_tables(rel_bias)
    for l in range(depth):
        wz, wk, wvt = _split_w_in(w_in[l])
        z, zc = _in_proj(xt, norm_mix[l][None], wz)
        kk, vvt = _kv_proj(xt, norm_mix[l][None], wk, wvt, bsz, seq)
        ya = _conv_module(z, conv_w[l], conv_b[l], conv_ln_g[l], conv_ln_b[l], bsz, seq)
        kc, vct = _compress(zc, jnp.stack([cmp_pos_k[l], cmp_pos_v[l]]), jnp.stack([cmp_w1_k[l], cmp_w1_v[l]]),
                            jnp.stack([cmp_b1_k[l], cmp_b1_v[l]]), jnp.stack([cmp_w2_k[l], cmp_w2_v[l]]), bsz, seq)
        yb = _nsa_attention(z, kc, vct, kk, vvt, tables, bsz, seq)
        qf, kf, vt = _mla_proj(z, rope_tab, mla_norm_q[l], mla_norm_kv[l], w_uq[l], w_ukv[l], bsz, seq)
        yc = _mla_attention(qf, kf, vt, bsz, seq)
        mem_kv = _norm_matmul(memt, norm_mem[l][None], w_xkv[l].astype(BF16), 256, 1024, BF16)
        mem_kv = mem_kv.reshape(bsz, m_len, 2 * XATTN_HEADS * XATTN_DH)
        xt = _merge_xattn(ya, yb, yc, z, xt, w_branch_conv[l], w_branch_nsa[l], w_branch_mla[l], w_out[l],
                          norm_xattn[l], w_xq[l], mem_kv, w_xo[l], bsz, seq)
        xt = _ffn(xt, norm_ffn[l], w_gate_up[l], w_down[l], norm_final, l == depth - 1)
    return xt.reshape(bsz, seq, d)
```
